```python
import jax, jax.numpy as jnp
from jax import lax
import numpy as np

D_MODEL = 1024
BATCH = 1
SEQ = 16384
DEPTH = 1

N_META = 16
GLA_HEADS = 4
GLA_DK = 128
GLA_DV = 256
GLA_RANK = 16
GLA_TAU = 16.0
CHUNK = 64
QK_W = GLA_HEADS * GLA_DK
V_W = GLA_HEADS * GLA_DV
CONV_WIDTH = 31
CONV_CH = D_MODEL
N_GROUPS = 8
EXPERTS_PER_GROUP = 8
N_EXPERTS = N_GROUPS * EXPERTS_PER_GROUP
TOP_K = 2
D_EXPERT = 512
MOE_BLOCK = 128
EPS = 1e-6

IN_SPLITS = [QK_W, QK_W, V_W, V_W, GLA_RANK, 2 * CONV_CH, 2 * D_MODEL]
D_IN = int(sum(IN_SPLITS))
IN_OFFSETS = [int(o) for o in np.cumsum(IN_SPLITS)[:-1]]

kernel_name = "gla_conformer_conv_hier_moe_block"


def rmsnorm(x, g):
    xf = x.astype(jnp.float32)
    y = xf * lax.rsqrt(jnp.mean(xf * xf, axis=-1, keepdims=True) + EPS)
    return (y * g.astype(jnp.float32)).astype(x.dtype)


def layernorm(x, g, b):
    xf = x.astype(jnp.float32)
    mu = jnp.mean(xf, axis=-1, keepdims=True)
    xc = xf - mu
    y = xc * lax.rsqrt(jnp.mean(xc * xc, axis=-1, keepdims=True) + EPS)
    return (y * g.astype(jnp.float32) + b.astype(jnp.float32)).astype(x.dtype)


def gla_chunked(q, k, v, log_a):
    B, H, L, DK = q.shape
    DV = v.shape[-1]
    nc = L // CHUNK

    def to_chunks(t):
        return jnp.moveaxis(t.reshape(B, H, nc, CHUNK, t.shape[-1]), 2, 0)

    qc, kc, vc, ac = to_chunks(q), to_chunks(k), to_chunks(v), to_chunks(log_a)
    causal = jnp.tril(jnp.ones((CHUNK, CHUNK), dtype=bool))

    def step(S, inp):
        qi, ki, vi, ai = inp
        b = jnp.cumsum(ai, axis=2)
        diff = b[:, :, :, None, :] - b[:, :, None, :, :]
        decay = jnp.exp(jnp.where(causal[:, :, None], diff, -jnp.inf))
        scores = jnp.einsum('bhtd,bhsd,bhtsd->bhts', qi, ki, decay)
        o = (jnp.einsum('bhts,bhsv->bhtv', scores, vi)
             + jnp.einsum('bhtd,bhdv->bhtv', qi * jnp.exp(b), S))
        b_last = b[:, :, -1:, :]
        k_dec = ki * jnp.exp(b_last - b)
        S = jnp.exp(b_last[:, :, 0, :])[..., None] * S + jnp.einsum('bhsd,bhsv->bhdv', k_dec, vi)
        return S, o

    S0 = jnp.zeros((B, H, DK, DV), jnp.float32)
    _, o = lax.scan(step, S0, (qc, kc, vc, ac))
    return jnp.moveaxis(o, 0, 2).reshape(B, H, L, DV)


def hier_moe(xt, w_rg, b_rg, w_re, b_re, w_gate, w_up, w_down):
    T, D = xt.shape
    xf = xt.astype(jnp.float32)
    g_prob = jax.nn.softmax(xf @ w_rg.astype(jnp.float32) + b_rg.astype(jnp.float32), axis=-1)
    g_w, g_idx = lax.top_k(g_prob, 1)
    e_logits = (xf @ w_re.astype(jnp.float32) + b_re.astype(jnp.float32)).reshape(T, N_GROUPS, EXPERTS_PER_GROUP)
    e_logits = jnp.take_along_axis(e_logits, g_idx[:, :, None], axis=1)[:, 0]
    e_prob = jax.nn.softmax(e_logits, axis=-1)
    top_p, top_i = lax.top_k(e_prob, TOP_K)
    top_p = top_p / jnp.sum(top_p, axis=-1, keepdims=True)
    weights = g_w * top_p
    expert_id = g_idx * EXPERTS_PER_GROUP + top_i

    S = T * TOP_K
    slot_e = expert_id.reshape(S).astype(jnp.int32)
    slot_tok = jnp.repeat(jnp.arange(T, dtype=jnp.int32), TOP_K)
    counts = jax.ops.segment_sum(jnp.ones((S,), jnp.int32), slot_e, num_segments=N_EXPERTS)
    padded = (counts + MOE_BLOCK - 1) // MOE_BLOCK * MOE_BLOCK
    pad_end = jnp.cumsum(padded)
    pad_start = pad_end - padded
    start = jnp.cumsum(counts) - counts
    order = jnp.argsort(slot_e)
    e_sorted = slot_e[order]
    pos = pad_start[e_sorted] + jnp.arange(S, dtype=jnp.int32) - start[e_sorted]
    n_blocks = (S + N_EXPERTS * (MOE_BLOCK - 1) + MOE_BLOCK - 1) // MOE_BLOCK
    P = n_blocks * MOE_BLOCK
    buf_tok = jnp.full((P,), T, dtype=jnp.int32).at[pos].set(slot_tok[order])
    x_pad = jnp.concatenate([xt, jnp.zeros((1, D), xt.dtype)], axis=0)
    xb = x_pad[buf_tok].reshape(n_blocks, MOE_BLOCK, D)
    block_e = jnp.minimum(
        jnp.searchsorted(pad_end, jnp.arange(n_blocks, dtype=jnp.int32) * MOE_BLOCK, side='right'),
        N_EXPERTS - 1)

    def expert_block(args):
        xblk, e = args
        a = xblk @ w_gate[e]
        u = xblk @ w_up[e]
        return (jax.nn.silu(a) * u) @ w_down[e]

    yb = lax.map(expert_block, (xb, block_e)).reshape(P, D)
    y_slot = jnp.zeros((S, D), yb.dtype).at[order].set(yb[pos]).reshape(T, TOP_K, D)
    return jnp.einsum('tk,tkd->td', weights.astype(y_slot.dtype), y_slot)


def setup_inputs(seed: int = 0) -> dict:
    key = jax.random.key(seed)
    ks = jax.random.split(key, 24)
    f32 = jnp.float32
    nrm = lambda k, shape, s: jax.random.normal(k, shape, f32) * s
    Dp = DEPTH
    return {
        "x": nrm(ks[0], (BATCH, SEQ, D_MODEL), 1.0),
        "meta": nrm(ks[1], (N_META, D_MODEL), 1.0),
        "norm1_g": 1.0 + nrm(ks[2], (Dp, D_MODEL), 0.02),
        "w_in": nrm(ks[3], (Dp, D_MODEL, D_IN), D_MODEL ** -0.5),
        "w_decay_up": nrm(ks[4], (Dp, GLA_RANK, QK_W), GLA_RANK ** -0.5),
        "b_decay": nrm(ks[5], (Dp, QK_W), 0.1),
        "gla_norm_g": 1.0 + nrm(ks[6], (Dp, V_W), 0.02),
        "conv_w": nrm(ks[7], (Dp, CONV_WIDTH, CONV_CH), CONV_WIDTH ** -0.5),
        "conv_b": nrm(ks[8], (Dp, CONV_CH), 0.02),
        "conv_ln_g": 1.0 + nrm(ks[9], (Dp, CONV_CH), 0.02),
        "conv_ln_b": nrm(ks[10], (Dp, CONV_CH), 0.02),
        "w_pw2": nrm(ks[11], (Dp, CONV_CH, D_MODEL), CONV_CH ** -0.5),
        "b_gate": nrm(ks[12], (Dp, 2 * D_MODEL), 0.02),
        "w_out": nrm(ks[13], (Dp, D_MODEL, D_MODEL), D_MODEL ** -0.5),
        "norm2_g": 1.0 + nrm(ks[14], (Dp, D_MODEL), 0.02),
        "w_router_group": nrm(ks[15], (Dp, D_MODEL, N_GROUPS), D_MODEL ** -0.5),
        "b_router_group": nrm(ks[16], (Dp, N_GROUPS), 0.01),
        "w_router_expert": nrm(ks[17], (Dp, D_MODEL, N_EXPERTS), D_MODEL ** -0.5),
        "b_router_expert": nrm(ks[18], (Dp, N_EXPERTS), 0.01),
        "w_exp_gate": nrm(ks[19], (Dp, N_EXPERTS, D_MODEL, D_EXPERT), D_MODEL ** -0.5),
        "w_exp_up": nrm(ks[20], (Dp, N_EXPERTS, D_MODEL, D_EXPERT), D_MODEL ** -0.5),
        "w_exp_down": nrm(ks[21], (Dp, N_EXPERTS, D_EXPERT, D_MODEL), D_EXPERT ** -0.5),
        "final_norm_g": 1.0 + nrm(ks[22], (D_MODEL,), 0.02),
    }


def reference(x, meta, norm1_g, w_in, w_decay_up, b_decay, gla_norm_g, conv_w, conv_b,
              conv_ln_g, conv_ln_b, w_pw2, b_gate, w_out, norm2_g, w_router_group,
              b_router_group, w_router_expert, b_router_expert, w_exp_gate, w_exp_up,
              w_exp_down, final_norm_g):
    B = x.shape[0]
    h = jnp.concatenate([jnp.broadcast_to(meta[None].astype(x.dtype), (B, N_META, D_MODEL)), x], axis=1)
    L = h.shape[1]
    pad = (-N_META) % CHUNK

    for i in range(DEPTH):
        hn = rmsnorm(h, norm1_g[i])
        proj = hn @ w_in[i]
        q, k, v, og, dlow, glu, gates = jnp.split(proj, IN_OFFSETS, axis=-1)

        log_a = jax.nn.log_sigmoid((dlow @ w_decay_up[i] + b_decay[i]).astype(jnp.float32)) / GLA_TAU

        def heads(t, d):
            t = t.astype(jnp.float32).reshape(B, L, GLA_HEADS, d).transpose(0, 2, 1, 3)
            return jnp.pad(t, ((0, 0), (0, 0), (pad, 0), (0, 0)))

        o = gla_chunked(heads(q, GLA_DK) * (GLA_DK ** -0.5), heads(k, GLA_DK),
                        heads(v, GLA_DV), heads(log_a, GLA_DK))[:, :, pad:, :]
        o = o * lax.rsqrt(jnp.mean(o * o, axis=-1, keepdims=True) + EPS)
        o = o.transpose(0, 2, 1, 3).reshape(B, L, V_W) * gla_norm_g[i].astype(jnp.float32)
        branch_a = o.astype(h.dtype) * jax.nn.silu(og)

        u_val, u_gate = jnp.split(glu, 2, axis=-1)
        u = u_val * jax.nn.sigmoid(u_gate)
        u = lax.conv_general_dilated(
            u, conv_w[i].reshape(CONV_WIDTH, 1, CONV_CH).astype(u.dtype), window_strides=(1,),
            padding=[(CONV_WIDTH - 1, 0)], dimension_numbers=('NWC', 'WIO', 'NWC'),
            feature_group_count=CONV_CH) + conv_b[i]
        u = jax.nn.silu(layernorm(u, conv_ln_g[i], conv_ln_b[i]))
        branch_b = u @ w_pw2[i]

        g_a, g_b = jnp.split(jax.nn.sigmoid(gates + b_gate[i]), 2, axis=-1)
        h = h + (g_a * branch_a + g_b * branch_b) @ w_out[i]

        hn2 = rmsnorm(h, norm2_g[i]).reshape(B * L, D_MODEL)
        y = hier_moe(hn2, w_router_group[i], b_router_group[i], w_router_expert[i],
                     b_router_expert[i], w_exp_gate[i], w_exp_up[i], w_exp_down[i])
        h = h + y.reshape(B, L, D_MODEL)

    return rmsnorm(h, final_norm_g)[:, N_META:, :]
```

```python
import functools

import jax
import jax.numpy as jnp
from jax import lax
from jax.experimental import pallas as pl
from jax.experimental.pallas import tpu as pltpu

F32 = jnp.float32
BF16 = jnp.bfloat16
HIGHEST = lax.Precision.HIGHEST

D_MODEL = 1024
SEQ = 16384
N_META = 16
GLA_HEADS = 4
GLA_DK = 128
GLA_DV = 256
GLA_RANK = 16
GLA_TAU = 16.0
CHUNK = 64
SUB = 16
QK_W = GLA_HEADS * GLA_DK
V_W = GLA_HEADS * GLA_DV
CONV_WIDTH = 31
N_GROUPS = 8
EXPERTS_PER_GROUP = 8
N_EXPERTS = N_GROUPS * EXPERTS_PER_GROUP
TOP_K = 2
D_EXPERT = 512
EPS = 1e-6

LANES = 128
SUBLANES = 8
ROW_SLABS = D_MODEL // LANES

FRONT = 512
LP = FRONT + SEQ
TM_PROJ = 256
TM_GLA = 512
TM_MIX = 256
HIST = 32
TM_TOK = 256
BM = 256
N_BLOCKS = (SEQ * TOP_K) // BM + N_EXPERTS
P_ROWS = N_BLOCKS * BM
VMEM_LIMIT = 56 * 1024 * 1024


def _sigmoid(x):
    return 1.0 / (1.0 + jnp.exp(-x))


def _silu(x):
    return x * _sigmoid(x)


def _dot(a, b, **kw):
    return jnp.dot(a, b, preferred_element_type=F32, **kw)


def _inproj_kernel(front_ref, x_ref, g_ref, wq, wk, wv, wog, wdl, wglu, wgt, wdu, bd,
                   q_o, k_o, v_o, og_o, la_o, glu_o, gt_o):
    i = pl.program_id(0)
    h = jnp.where(i < FRONT // TM_PROJ, front_ref[...], x_ref[...])
    ms = jnp.mean(h * h, axis=-1, keepdims=True)
    hn = (h * lax.rsqrt(ms + EPS) * g_ref[...]).astype(BF16)
    q_o[...] = _dot(hn, wq[...]).astype(BF16)
    k_o[...] = _dot(hn, wk[...]).astype(BF16)
    v_o[...] = _dot(hn, wv[...]).astype(BF16)
    og_o[...] = _dot(hn, wog[...]).astype(BF16)
    glu_o[...] = _dot(hn, wglu[...]).astype(BF16)
    gt_o[...] = _dot(hn, wgt[...]).astype(BF16)
    dlow = _dot(hn, wdl[...])
    z = _dot(dlow, wdu[...], precision=HIGHEST) + bd[...]
    la_o[...] = (jnp.minimum(z, 0.0) - jnp.log(1.0 + jnp.exp(-jnp.abs(z)))) * (1.0 / GLA_TAU)


def _inproj(front, x2d, g1, wq, wk, wv, wog, wdl, wglu, wgt, wdu, bd):
    n_front = FRONT // TM_PROJ
    grid = (LP // TM_PROJ,)
    row = lambda w: pl.BlockSpec((TM_PROJ, w), lambda i: (i, 0))
    full = lambda a: pl.BlockSpec(a.shape, lambda i: (0, 0))
    return pl.pallas_call(
        _inproj_kernel,
        grid=grid,
        in_specs=[
            pl.BlockSpec((TM_PROJ, D_MODEL), lambda i: (jnp.minimum(i, n_front - 1), 0)),
            pl.BlockSpec((TM_PROJ, D_MODEL), lambda i: (jnp.maximum(i - n_front, 0), 0)),
            full(g1), full(wq), full(wk), full(wv), full(wog), full(wdl), full(wglu), full(wgt),
            full(wdu), full(bd),
        ],
        out_specs=[row(QK_W), row(QK_W), row(V_W), row(D_MODEL), row(QK_W), row(2 * D_MODEL),
                   row(2 * D_MODEL)],
        out_shape=[
            jax.ShapeDtypeStruct((LP, QK_W), BF16), jax.ShapeDtypeStruct((LP, QK_W), BF16),
            jax.ShapeDtypeStruct((LP, V_W), BF16), jax.ShapeDtypeStruct((LP, D_MODEL), BF16),
            jax.ShapeDtypeStruct((LP, QK_W), F32), jax.ShapeDtypeStruct((LP, 2 * D_MODEL), BF16),
            jax.ShapeDtypeStruct((LP, 2 * D_MODEL), BF16),
        ],
        compiler_params=pltpu.CompilerParams(dimension_semantics=("arbitrary",),
                                             vmem_limit_bytes=VMEM_LIMIT),
        name="inproj",
    )(front, x2d, g1, wq, wk, wv, wog, wdl, wglu, wgt, wdu, bd)


def _gla_chunk(q, k, v, la, s_prev, tril, ones_kc, ones_kv):
    b = _dot(tril, la, precision=HIGHEST)
    b_last = b[CHUNK - 1:CHUNK, :]
    vb = v.astype(BF16)

    qe = (q * jnp.exp(b)).astype(BF16)
    o_inter = _dot(qe, s_prev.astype(BF16))
    kd = (k * jnp.exp(b_last - b)).astype(BF16)
    kv = lax.dot_general(kd, vb, (((0,), (0,)), ((), ())), preferred_element_type=F32)
    bl_col = lax.dot_general(la, ones_kc, (((0,), (0,)), ((), ())), precision=HIGHEST,
                             preferred_element_type=F32)
    dec = jnp.exp(bl_col)
    s_new = s_prev * jnp.concatenate([dec] * (GLA_DV // LANES), axis=1) + kv

    def cross(t0, t1, s0, s1):
        ref = b[s1 - 1:s1, :]
        qa = (q[t0:t1] * jnp.exp(b[t0:t1] - ref)).astype(BF16)
        ka = (k[s0:s1] * jnp.exp(ref - b[s0:s1])).astype(BF16)
        sc = lax.dot_general(qa, ka, (((1,), (1,)), ((), ())), preferred_element_type=F32)
        return _dot(sc.astype(BF16), vb[s0:s1])

    half = CHUNK // 2
    o_half = cross(half, CHUNK, 0, half)
    o_q1 = cross(SUB, 2 * SUB, 0, SUB)
    o_q3 = cross(half + SUB, CHUNK, half, half + SUB)

    def diag(r0):
        qb, kb, bb, vv = q[r0:r0 + SUB], k[r0:r0 + SUB], b[r0:r0 + SUB], v[r0:r0 + SUB]
        trow = lax.broadcasted_iota(jnp.int32, (SUB, GLA_DK), 0)
        ps = []
        for s in range(SUB):
            e = jnp.where(trow >= s, jnp.exp(bb - bb[s:s + 1]), 0.0)
            ps.append((qb * kb[s:s + 1] * e).astype(BF16))
        p = jnp.concatenate(ps, axis=0)
        r = _dot(p, ones_kv)
        acc = r[0:SUB] * vv[0:1]
        for s in range(1, SUB):
            acc = acc + r[s * SUB:(s + 1) * SUB] * vv[s:s + 1]
        return acc

    o_intra = jnp.concatenate([
        diag(0),
        diag(SUB) + o_q1,
        diag(2 * SUB) + o_half[0:SUB],
        diag(3 * SUB) + o_half[SUB:2 * SUB] + o_q3,
    ], axis=0)
    return o_inter + o_intra, s_new


def _gla_kernel(q_ref, k_ref, v_ref, la_ref, g_ref, o_ref, s_ref):
    @pl.when(pl.program_id(1) == 0)
    def _():
        s_ref[...] = jnp.zeros_like(s_ref)

    r_i = lax.broadcasted_iota(jnp.int32, (CHUNK, CHUNK), 0)
    c_i = lax.broadcasted_iota(jnp.int32, (CHUNK, CHUNK), 1)
    tril = (c_i <= r_i).astype(F32)
    ones_kc = jnp.ones((CHUNK, LANES), F32)
    ones_kv = jnp.ones((GLA_DK, GLA_DV), BF16)
    g = g_ref[...]

    def body(c, carry):
        r0 = pl.multiple_of(c * CHUNK, CHUNK)
        rows = pl.ds(r0, CHUNK)
        q = q_ref[rows, :].astype(F32) * (GLA_DK ** -0.5)
        k = k_ref[rows, :].astype(F32)
        v = v_ref[rows, :].astype(F32)
        o, s_new = _gla_chunk(q, k, v, la_ref[rows, :], s_ref[...], tril, ones_kc, ones_kv)
        s_ref[...] = s_new
        o = o * lax.rsqrt(jnp.mean(o * o, axis=-1, keepdims=True) + EPS) * g
        o_ref[rows, :] = o.astype(BF16)
        return carry

    lax.fori_loop(0, TM_GLA // CHUNK, body, 0)


def _gla(q, k, v, la, g):
    grid = (GLA_HEADS, LP // TM_GLA)
    qk_spec = pl.BlockSpec((TM_GLA, GLA_DK), lambda h, t: (t, h))
    v_spec = pl.BlockSpec((TM_GLA, GLA_DV), lambda h, t: (t, h))
    return pl.pallas_call(
        _gla_kernel,
        grid=grid,
        in_specs=[qk_spec, qk_spec, v_spec, qk_spec, pl.BlockSpec((1, GLA_DV), lambda h, t: (0, h))],
        out_specs=v_spec,
        out_shape=jax.ShapeDtypeStruct((LP, V_W), BF16),
        scratch_shapes=[pltpu.VMEM((GLA_DK, GLA_DV), F32)],
        compiler_params=pltpu.CompilerParams(dimension_semantics=("arbitrary", "arbitrary"),
                                             vmem_limit_bytes=VMEM_LIMIT),
        name="gla",
    )(q, k, v, la, g)


def _mix_kernel(o_ref, og_ref, glu_ref, hist_ref, gt_ref, x_ref, cw_ref, cb_ref, lg_ref, lb_ref,
                wpw_ref, bg_ref, wo_ref, g2_ref, wr_ref, br_ref,
                h_o, hn_o, ri_o, rw_o, cnt_o, ubuf, base):
    i = pl.program_id(0)

    @pl.when(i == 0)
    def _():
        base[...] = jnp.zeros_like(base)

    def glu(ref):
        val = ref[:, 0:D_MODEL].astype(F32)
        gate = ref[:, D_MODEL:2 * D_MODEL].astype(F32)
        return val * _sigmoid(gate)

    ubuf[0:HIST, :] = glu(hist_ref)
    ubuf[HIST:HIST + TM_MIX, :] = glu(glu_ref)

    lead = HIST - (CONV_WIDTH - 1)
    acc = jnp.zeros((TM_MIX, D_MODEL), F32) + cb_ref[...]
    for rho in range(SUBLANES):
        taps = [j for j in range(CONV_WIDTH) if (lead + j) % SUBLANES == rho]
        if not taps:
            continue
        m_max = max((lead + j) // SUBLANES for j in taps)
        sh = ubuf[rho:rho + TM_MIX + SUBLANES * m_max, :]
        for j in taps:
            m = (lead + j) // SUBLANES
            acc = acc + cw_ref[j:j + 1, :] * sh[SUBLANES * m:SUBLANES * m + TM_MIX, :]

    mu = jnp.mean(acc, axis=-1, keepdims=True)
    xc = acc - mu
    ln = xc * lax.rsqrt(jnp.mean(xc * xc, axis=-1, keepdims=True) + EPS) * lg_ref[...] + lb_ref[...]
    branch_b = _dot(_silu(ln).astype(BF16), wpw_ref[...])

    branch_a = o_ref[...].astype(F32) * _silu(og_ref[...].astype(F32))
    g_a = _sigmoid(gt_ref[:, 0:D_MODEL].astype(F32) + bg_ref[:, 0:D_MODEL])
    g_b = _sigmoid(gt_ref[:, D_MODEL:2 * D_MODEL].astype(F32) + bg_ref[:, D_MODEL:2 * D_MODEL])
    merged = (g_a * branch_a + g_b * branch_b).astype(BF16)
    h1 = x_ref[...] + _dot(merged, wo_ref[...])
    h_o[...] = h1

    hn2 = h1 * lax.rsqrt(jnp.mean(h1 * h1, axis=-1, keepdims=True) + EPS) * g2_ref[...]
    for s in range(ROW_SLABS):
        hn_o[:, s, :] = hn2[:, s * LANES:(s + 1) * LANES]

    logits = lax.dot_general(wr_ref[...], hn2, (((1,), (1,)), ((), ())), precision=HIGHEST,
                             preferred_element_type=F32) + br_ref[...]
    row = lax.broadcasted_iota(jnp.int32, logits.shape, 0)
    rowf = row.astype(F32)
    neg = -jnp.inf
    is_g = row < N_GROUPS
    lg = jnp.where(is_g, logits, neg)
    gmax = jnp.max(lg, axis=0, keepdims=True)
    gidx = jnp.min(jnp.where(lg == gmax, rowf, float(N_GROUPS)), axis=0, keepdims=True)
    g_w = 1.0 / jnp.sum(jnp.where(is_g, jnp.exp(lg - gmax), 0.0), axis=0, keepdims=True)
    erow = rowf - float(N_GROUPS)
    egrp = ((row - N_GROUPS) >> 3).astype(F32)
    in_grp = (row >= N_GROUPS) & (row < N_GROUPS + N_EXPERTS) & (egrp == gidx)
    le = jnp.where(in_grp, logits, neg)
    m1 = jnp.max(le, axis=0, keepdims=True)
    i1 = jnp.min(jnp.where(le == m1, erow, float(N_EXPERTS)), axis=0, keepdims=True)
    le2 = jnp.where(erow == i1, neg, le)
    m2 = jnp.max(le2, axis=0, keepdims=True)
    i2 = jnp.min(jnp.where(le2 == m2, erow, float(N_EXPERTS)), axis=0, keepdims=True)
    t = jnp.exp(m2 - m1)
    w1 = g_w / (1.0 + t)
    w2 = g_w * t / (1.0 + t)

    oh1 = erow == i1
    oh2 = erow == i2
    oh = jnp.where(oh1 | oh2, 1.0, 0.0).astype(BF16)
    tr = lax.broadcasted_iota(jnp.int32, (TM_MIX, TM_MIX), 0)
    tc = lax.broadcasted_iota(jnp.int32, (TM_MIX, TM_MIX), 1)
    upper = (tr < tc).astype(BF16)
    cum = _dot(oh, upper) + base[...]
    rank1 = jnp.sum(jnp.where(oh1, cum, 0.0), axis=0, keepdims=True)
    rank2 = jnp.sum(jnp.where(oh2, cum, 0.0), axis=0, keepdims=True)
    base[...] = base[...] + jnp.sum(oh.astype(F32), axis=1, keepdims=True)
    cnt_o[...] = base[...]

    ri_o[...] = jnp.zeros_like(ri_o)
    ri_o[0:1, :] = i1.astype(jnp.int32)
    ri_o[1:2, :] = i2.astype(jnp.int32)
    ri_o[2:3, :] = rank1.astype(jnp.int32)
    ri_o[3:4, :] = rank2.astype(jnp.int32)
    rw_o[...] = jnp.zeros_like(rw_o)
    rw_o[0:1, :] = w1
    rw_o[1:2, :] = w2


def _mix(o_n, og, glu, gates, x2d, cw, cb, lg, lb, wpw, bg, wo, g2, wr, br):
    n_front = FRONT // TM_MIX
    grid = (SEQ // TM_MIX,)
    rowp = lambda w: pl.BlockSpec((TM_MIX, w), lambda i: (i + n_front, 0))
    full = lambda a: pl.BlockSpec(a.shape, lambda i: (0,) * a.ndim)
    hist_blocks = TM_MIX // HIST
    return pl.pallas_call(
        _mix_kernel,
        grid=grid,
        in_specs=[
            rowp(V_W), rowp(D_MODEL), rowp(2 * D_MODEL),
            pl.BlockSpec((HIST, 2 * D_MODEL), lambda i: ((i + n_front) * hist_blocks - 1, 0)),
            rowp(2 * D_MODEL),
            pl.BlockSpec((TM_MIX, D_MODEL), lambda i: (i, 0)),
            full(cw), full(cb), full(lg), full(lb), full(wpw), full(bg), full(wo), full(g2),
            full(wr), full(br),
        ],
        out_specs=[
            pl.BlockSpec((TM_MIX, D_MODEL), lambda i: (i, 0)),
            pl.BlockSpec((TM_MIX, ROW_SLABS, LANES), lambda i: (i, 0, 0)),
            pl.BlockSpec((SUBLANES, TM_MIX), lambda i: (0, i)),
            pl.BlockSpec((SUBLANES, TM_MIX), lambda i: (0, i)),
            pl.BlockSpec((LANES, 1), lambda i: (0, 0)),
        ],
        out_shape=[
            jax.ShapeDtypeStruct((SEQ, D_MODEL), F32),
            jax.ShapeDtypeStruct((SEQ, ROW_SLABS, LANES), F32),
            jax.ShapeDtypeStruct((SUBLANES, SEQ), jnp.int32),
            jax.ShapeDtypeStruct((SUBLANES, SEQ), F32),
            jax.ShapeDtypeStruct((LANES, 1), F32),
        ],
        scratch_shapes=[pltpu.VMEM((HIST + TM_MIX, D_MODEL), F32), pltpu.VMEM((LANES, 1), F32)],
        compiler_params=pltpu.CompilerParams(dimension_semantics=("arbitrary",),
                                             vmem_limit_bytes=VMEM_LIMIT),
        name="mix",
    )(o_n, og, glu, glu, gates, x2d, cw, cb, lg, lb, wpw, bg, wo, g2, wr, br)


def _dispatch_kernel(pos_ref, hn_ref, xb_ref, zbuf, sem):
    i = pl.program_id(0)

    @pl.when(i == 0)
    def _():
        zbuf[...] = jnp.zeros_like(zbuf)

        def fill(b, c):
            pltpu.make_async_copy(zbuf, xb_ref.at[pl.ds(b * BM, BM)], sem).start()
            return c

        lax.fori_loop(0, N_BLOCKS, fill, 0)

        def drain(b, c):
            pltpu.make_async_copy(zbuf, xb_ref.at[pl.ds(b * BM, BM)], sem).wait()
            return c

        lax.fori_loop(0, N_BLOCKS, drain, 0)

    base = i * TM_TOK

    def issue(r, c):
        for kk in range(TOP_K):
            dst = pos_ref[kk * SEQ + base + r]
            pltpu.make_async_copy(hn_ref.at[r], xb_ref.at[dst], sem).start()
        return c

    lax.fori_loop(0, TM_TOK, issue, 0)
    for kk in range(TOP_K):
        pltpu.make_async_copy(hn_ref, xb_ref.at[pl.ds(0, TM_TOK)], sem).wait()


def _dispatch(pos, hn2p):
    grid_spec = pltpu.PrefetchScalarGridSpec(
        num_scalar_prefetch=1,
        grid=(SEQ // TM_TOK,),
        in_specs=[pl.BlockSpec((TM_TOK, ROW_SLABS, LANES), lambda i, pos: (i, 0, 0))],
        out_specs=pl.BlockSpec(memory_space=pl.ANY),
        scratch_shapes=[pltpu.VMEM((BM, ROW_SLABS, LANES), F32), pltpu.SemaphoreType.DMA(())],
    )
    return pl.pallas_call(
        _dispatch_kernel,
        grid_spec=grid_spec,
        out_shape=jax.ShapeDtypeStruct((P_ROWS, ROW_SLABS, LANES), F32),
        compiler_params=pltpu.CompilerParams(dimension_semantics=("arbitrary",),
                                             vmem_limit_bytes=VMEM_LIMIT),
        name="dispatch",
    )(pos, hn2p)


def _experts_kernel(be_ref, first_ref, nused_ref, x_ref, wg_ref, wu_ref, wd_ref, y_ref,
                    wg_b, wu_b, wd_b):
    b = pl.program_id(0)

    @pl.when(b < nused_ref[0])
    def _():
        @pl.when(first_ref[b] == 1)
        def _():
            wg_b[...] = wg_ref[0].astype(BF16)
            wu_b[...] = wu_ref[0].astype(BF16)
            wd_b[...] = wd_ref[0].astype(BF16)

        x = jnp.concatenate([x_ref[:, s, :] for s in range(ROW_SLABS)], axis=-1).astype(BF16)
        a = _dot(x, wg_b[...])
        u = _dot(x, wu_b[...])
        y = _dot((_silu(a) * u).astype(BF16), wd_b[...])
        for s in range(ROW_SLABS):
            y_ref[:, s, :] = y[:, s * LANES:(s + 1) * LANES]

    @pl.when(b >= nused_ref[0])
    def _():
        y_ref[...] = jnp.zeros_like(y_ref)


def _experts(block_e, first, nused, xb, wg, wu, wd):
    def wmap(b, be, first, nused):
        return (be[b], 0, 0)

    def xmap(b, be, first, nused):
        return (b, 0, 0)

    grid_spec = pltpu.PrefetchScalarGridSpec(
        num_scalar_prefetch=3,
        grid=(N_BLOCKS,),
        in_specs=[
            pl.BlockSpec((BM, ROW_SLABS, LANES), xmap),
            pl.BlockSpec((1, D_MODEL, D_EXPERT), wmap),
            pl.BlockSpec((1, D_MODEL, D_EXPERT), wmap),
            pl.BlockSpec((1, D_EXPERT, D_MODEL), wmap),
        ],
        out_specs=pl.BlockSpec((BM, ROW_SLABS, LANES), xmap),
        scratch_shapes=[pltpu.VMEM((D_MODEL, D_EXPERT), BF16), pltpu.VMEM((D_MODEL, D_EXPERT), BF16),
                        pltpu.VMEM((D_EXPERT, D_MODEL), BF16)],
    )
    return pl.pallas_call(
        _experts_kernel,
        grid_spec=grid_spec,
        out_shape=jax.ShapeDtypeStruct((P_ROWS, ROW_SLABS, LANES), F32),
        compiler_params=pltpu.CompilerParams(dimension_semantics=("arbitrary",),
                                             vmem_limit_bytes=VMEM_LIMIT),
        name="experts",
    )(block_e, first, nused, xb, wg, wu, wd)


def _combine_kernel(pos_ref, yb_ref, h_ref, w_ref, g_ref, out_ref, buf, sem):
    i = pl.program_id(0)
    base = i * TM_TOK

    def issue(r, c):
        for kk in range(TOP_K):
            src = pos_ref[kk * SEQ + base + r]
            pltpu.make_async_copy(yb_ref.at[src], buf.at[kk, r], sem).start()
        return c

    lax.fori_loop(0, TM_TOK, issue, 0)
    for kk in range(TOP_K):
        pltpu.make_async_copy(yb_ref.at[pl.ds(0, TM_TOK)], buf.at[kk], sem).wait()

    w0 = w_ref[:, 0:1]
    w1 = w_ref[:, 1:2]
    hs = []
    ssq = jnp.zeros((TM_TOK, 1), F32)
    for s in range(ROW_SLABS):
        y = w0 * buf[0, :, s, :] + w1 * buf[1, :, s, :]
        hh = h_ref[:, s * LANES:(s + 1) * LANES] + y
        ssq = ssq + jnp.sum(hh * hh, axis=-1, keepdims=True)
        hs.append(hh)
    scale = lax.rsqrt(ssq * (1.0 / D_MODEL) + EPS)
    for s in range(ROW_SLABS):
        out_ref[:, s * LANES:(s + 1) * LANES] = hs[s] * scale * g_ref[:, s * LANES:(s + 1) * LANES]


def _combine(pos, yb, h1, wts, gf):
    grid_spec = pltpu.PrefetchScalarGridSpec(
        num_scalar_prefetch=1,
        grid=(SEQ // TM_TOK,),
        in_specs=[
            pl.BlockSpec(memory_space=pl.ANY),
            pl.BlockSpec((TM_TOK, D_MODEL), lambda i, pos: (i, 0)),
            pl.BlockSpec((TM_TOK, TOP_K), lambda i, pos: (i, 0)),
            pl.BlockSpec((1, D_MODEL), lambda i, pos: (0, 0)),
        ],
        out_specs=pl.BlockSpec((TM_TOK, D_MODEL), lambda i, pos: (i, 0)),
        scratch_shapes=[pltpu.VMEM((TOP_K, TM_TOK, ROW_SLABS, LANES), F32), pltpu.SemaphoreType.DMA(())],
    )
    return pl.pallas_call(
        _combine_kernel,
        grid_spec=grid_spec,
        out_shape=jax.ShapeDtypeStruct((SEQ, D_MODEL), F32),
        compiler_params=pltpu.CompilerParams(dimension_semantics=("arbitrary",),
                                             vmem_limit_bytes=VMEM_LIMIT),
        name="combine",
    )(pos, yb, h1, wts, gf)


def kernel(x, meta, norm1_g, w_in, w_decay_up, b_decay, gla_norm_g, conv_w, conv_b, conv_ln_g,
           conv_ln_b, w_pw2, b_gate, w_out, norm2_g, w_router_group, b_router_group,
           w_router_expert, b_router_expert, w_exp_gate, w_exp_up, w_exp_down, final_norm_g):
    assert x.shape == (1, SEQ, D_MODEL) and w_in.shape[0] == 1
    x2d = x[0]
    front = jnp.concatenate([jnp.zeros((FRONT - N_META, D_MODEL), F32), meta.astype(F32)], axis=0)

    w = w_in[0]
    o_q, o_k, o_v, o_og, o_dl, o_glu, o_gt = (0, QK_W, 2 * QK_W, 2 * QK_W + V_W, 2 * QK_W + 2 * V_W,
                                              2 * QK_W + 2 * V_W + GLA_RANK,
                                              2 * QK_W + 2 * V_W + GLA_RANK + 2 * D_MODEL)
    cols = lambda a, b: w[:, a:b].astype(BF16)
    row2 = lambda a: a.reshape(1, -1).astype(F32)
    q, k, v, og, la, glu, gates = _inproj(
        front, x2d, row2(norm1_g[0]), cols(o_q, o_k), cols(o_k, o_v), cols(o_v, o_og),
        cols(o_og, o_dl), cols(o_dl, o_glu), cols(o_glu, o_gt), cols(o_gt, w.shape[1]),
        w_decay_up[0].astype(F32), row2(b_decay[0]))

    o_n = _gla(q, k, v, la, row2(gla_norm_g[0]))

    wr = jnp.concatenate([w_router_group[0].T, w_router_expert[0].T,
                          jnp.zeros((LANES - N_GROUPS - N_EXPERTS, D_MODEL), F32)], axis=0).astype(F32)
    br = jnp.concatenate([b_router_group[0], b_router_expert[0],
                          jnp.zeros((LANES - N_GROUPS - N_EXPERTS,), F32)]).reshape(LANES, 1).astype(F32)
    h1, hn2p, ri, rw, cnt = _mix(
        o_n, og, glu, gates, x2d, conv_w[0].astype(F32), row2(conv_b[0]), row2(conv_ln_g[0]),
        row2(conv_ln_b[0]), w_pw2[0].astype(BF16), row2(b_gate[0]), w_out[0].astype(BF16),
        row2(norm2_g[0]), wr, br)

    counts = cnt[N_GROUPS:N_GROUPS + N_EXPERTS, 0].astype(jnp.int32)
    padded = (counts + BM - 1) // BM * BM
    pad_end = jnp.cumsum(padded)
    pad_start = pad_end - padded
    pos = jnp.concatenate([pad_start[ri[0]] + ri[2], pad_start[ri[1]] + ri[3]]).astype(jnp.int32)
    blk = jnp.arange(N_BLOCKS, dtype=jnp.int32)
    block_e = jnp.minimum(jnp.searchsorted(pad_end, blk * BM, side='right'), N_EXPERTS - 1).astype(jnp.int32)
    first = jnp.concatenate([jnp.ones((1,), jnp.int32), (block_e[1:] != block_e[:-1]).astype(jnp.int32)])
    nused = (pad_end[-1:] // BM).astype(jnp.int32)

    xb = _dispatch(pos, hn2p)
    yb = _experts(block_e, first, nused, xb, w_exp_gate[0], w_exp_up[0], w_exp_down[0])
    out = _combine(pos, yb, h1, rw[0:TOP_K].T, row2(final_norm_g))
    return out[None]
```

```python
import functools

import jax
import jax.numpy as jnp
import numpy as np
from jax import lax
from jax.experimental import pallas as pl
from jax.experimental.pallas import tpu as pltpu

F32 = jnp.float32
BF16 = jnp.bfloat16
HIGHEST = lax.Precision.HIGHEST

D_MODEL = 1024
SEQ = 16384
N_META = 16
GLA_HEADS = 4
GLA_DK = 128
GLA_DV = 256
GLA_RANK = 16
GLA_TAU = 16.0
CHUNK = 64
QK_W = GLA_HEADS * GLA_DK
V_W = GLA_HEADS * GLA_DV
CONV_WIDTH = 31
N_GROUPS = 8
EXPERTS_PER_GROUP = 8
N_EXPERTS = N_GROUPS * EXPERTS_PER_GROUP
TOP_K = 2
D_EXPERT = 512
EPS = 1e-6
LOG2E = 1.4426950408889634

LANES = 128
SUBLANES = 8
ROW_SLABS = D_MODEL // LANES

FRONT = 512
LP = FRONT + SEQ
TM_PROJ = 256
TM_GLA = 512
TM_MIX = 256
HIST = 32
SH_ROWS = TM_MIX + HIST - SUBLANES
CONV_RB = 64
TM_TOK = 256
BM = 256
N_BLOCKS = (SEQ * TOP_K) // BM + N_EXPERTS
P_ROWS = N_BLOCKS * BM
VMEM_LIMIT = 56 * 1024 * 1024


def _sigmoid(x):
    return 0.5 * jnp.tanh(0.5 * x) + 0.5


def _silu(x):
    return x * _sigmoid(x)


def _dot(a, b, **kw):
    return jnp.dot(a, b, preferred_element_type=F32, **kw)


def _inproj_kernel(front_ref, x_ref, g_ref, wq, wk, wv, wog, wdl, wglu, wgt, wdu, bd,
                   q_o, k_o, v_o, og_o, la_o, glu_o, gt_o):
    i = pl.program_id(0)
    h = jnp.where(i < FRONT // TM_PROJ, front_ref[...], x_ref[...])
    ms = jnp.mean(h * h, axis=-1, keepdims=True)
    hn = (h * lax.rsqrt(ms + EPS) * g_ref[...]).astype(BF16)
    q_o[...] = _dot(hn, wq[...]).astype(BF16)
    k_o[...] = _dot(hn, wk[...]).astype(BF16)
    v_o[...] = _dot(hn, wv[...]).astype(BF16)
    og_o[...] = _dot(hn, wog[...]).astype(BF16)
    glu_o[...] = _dot(hn, wglu[...]).astype(BF16)
    gt_o[...] = _dot(hn, wgt[...]).astype(BF16)
    dlow = _dot(hn, wdl[...])
    z = _dot(dlow, wdu[...], precision=HIGHEST) + bd[...]
    la_o[...] = (jnp.minimum(z, 0.0) - jnp.log(1.0 + jnp.exp(-jnp.abs(z)))) * (LOG2E / GLA_TAU)


def _inproj(front, x2d, g1, wq, wk, wv, wog, wdl, wglu, wgt, wdu, bd):
    n_front = FRONT // TM_PROJ
    grid = (LP // TM_PROJ,)
    row = lambda w: pl.BlockSpec((TM_PROJ, w), lambda i: (i, 0))
    full = lambda a: pl.BlockSpec(a.shape, lambda i: (0, 0))
    return pl.pallas_call(
        _inproj_kernel,
        grid=grid,
        in_specs=[
            pl.BlockSpec((TM_PROJ, D_MODEL), lambda i: (jnp.minimum(i, n_front - 1), 0)),
            pl.BlockSpec((TM_PROJ, D_MODEL), lambda i: (jnp.maximum(i - n_front, 0), 0)),
            full(g1), full(wq), full(wk), full(wv), full(wog), full(wdl), full(wglu), full(wgt),
            full(wdu), full(bd),
        ],
        out_specs=[row(QK_W), row(QK_W), row(V_W), row(D_MODEL), row(QK_W), row(2 * D_MODEL),
                   row(2 * D_MODEL)],
        out_shape=[
            jax.ShapeDtypeStruct((LP, QK_W), BF16), jax.ShapeDtypeStruct((LP, QK_W), BF16),
            jax.ShapeDtypeStruct((LP, V_W), BF16), jax.ShapeDtypeStruct((LP, D_MODEL), BF16),
            jax.ShapeDtypeStruct((LP, QK_W), F32), jax.ShapeDtypeStruct((LP, 2 * D_MODEL), BF16),
            jax.ShapeDtypeStruct((LP, 2 * D_MODEL), BF16),
        ],
        compiler_params=pltpu.CompilerParams(dimension_semantics=("arbitrary",),
                                             vmem_limit_bytes=VMEM_LIMIT),
        name="inproj",
    )(front, x2d, g1, wq, wk, wv, wog, wdl, wglu, wgt, wdu, bd)


GLA_LEVELS = (32, 16, 8, 4, 2, 1)
LA_SPLIT = 3


def _gla_tables():
    r = np.arange(CHUNK)
    t, c = r[:, None], r[None, :]
    rows = [c <= t, c > t]
    pair = []
    for m in GLA_LEVELS:
        mid = (t // (2 * m)) * (2 * m) + m - 1
        right = (t % (2 * m)) >= m
        rows.append(np.where(right, (c > mid) & (c <= t), (c > t) & (c <= mid)))
        pair.append(((t // (2 * m)) == (c // (2 * m))) & right & ((c % (2 * m)) < m))
    pair.append(t == c)
    sums = np.concatenate(rows, axis=0).astype(np.float32)
    sums = np.concatenate([sums] * LA_SPLIT, axis=1)
    return jnp.asarray(sums, BF16), jnp.asarray(np.stack(pair).astype(np.float32))


def _split_bf16(x):
    pieces = []
    rest = x
    for _ in range(LA_SPLIT):
        p = rest.astype(BF16)
        pieces.append(p)
        rest = rest - p.astype(F32)
    return jnp.concatenate(pieces, axis=0)


def _dot_nt(a, b):
    return lax.dot_general(a, b, (((1,), (1,)), ((), ())), preferred_element_type=F32)


def _dot_tn(a, b):
    return lax.dot_general(a, b, (((0,), (0,)), ((), ())), preferred_element_type=F32)


def _gla_chunk(q, k, vb, la2, s_prev, sums, pair_ref, right_rows, ones_col):
    la3 = _split_bf16(la2)
    e = jnp.exp2(_dot(sums, la3))

    qe = (q * e[0:CHUNK]).astype(BF16)
    o = _dot(qe, s_prev.astype(BF16))
    kd = (k * e[CHUNK:2 * CHUNK]).astype(BF16)
    kv = _dot_tn(kd, vb)
    dec = jnp.exp2(_dot_tn(la3, ones_col))
    s_new = s_prev * jnp.concatenate([dec] * (GLA_DV // LANES), axis=1) + kv

    sc = _dot_nt(q.astype(BF16), k.astype(BF16)) * pair_ref[len(GLA_LEVELS)]
    for lvl in range(len(GLA_LEVELS)):
        rows = e[(2 + lvl) * CHUNK:(3 + lvl) * CHUNK]
        x = (jnp.where(right_rows[lvl], q, k) * rows).astype(BF16)
        sc = sc + _dot_nt(x, x) * pair_ref[lvl]
    o = o + _dot(sc.astype(BF16), vb)
    return o, s_new


def _gla_kernel(q_ref, k_ref, v_ref, la_ref, g_ref, sums_ref, pair_ref, o_ref, s_ref):
    @pl.when(pl.program_id(0) == 0)
    def _():
        s_ref[...] = jnp.zeros_like(s_ref)

    row = lax.broadcasted_iota(jnp.int32, (CHUNK, GLA_DK), 0)
    right_rows = [(row & m) != 0 for m in GLA_LEVELS]
    ones_col = jnp.ones((LA_SPLIT * CHUNK, LANES), BF16)
    sums = sums_ref[...]

    def body(c, carry):
        r0 = pl.multiple_of(c * CHUNK, CHUNK)
        rows = pl.ds(r0, CHUNK)
        states = [s_ref[h] for h in range(GLA_HEADS)]
        new_states = []
        for h in range(GLA_HEADS):
            kc = slice(h * GLA_DK, (h + 1) * GLA_DK)
            vc = slice(h * GLA_DV, (h + 1) * GLA_DV)
            q = q_ref[rows, kc].astype(F32) * (GLA_DK ** -0.5)
            k = k_ref[rows, kc].astype(F32)
            o, s_new = _gla_chunk(q, k, v_ref[rows, vc], la_ref[rows, kc], states[h], sums, pair_ref,
                                  right_rows, ones_col)
            new_states.append(s_new)
            o = o * lax.rsqrt(jnp.mean(o * o, axis=-1, keepdims=True) + EPS) * g_ref[:, vc]
            o_ref[rows, vc] = o.astype(BF16)
        for h in range(GLA_HEADS):
            s_ref[h] = new_states[h]
        return carry

    lax.fori_loop(0, TM_GLA // CHUNK, body, 0, unroll=2)


def _gla(q, k, v, la, g):
    sums, pair = _gla_tables()
    row = lambda w: pl.BlockSpec((TM_GLA, w), lambda t: (t, 0))
    full = lambda a: pl.BlockSpec(a.shape, lambda t: (0,) * a.ndim)
    return pl.pallas_call(
        _gla_kernel,
        grid=(LP // TM_GLA,),
        in_specs=[row(QK_W), row(QK_W), row(V_W), row(QK_W), full(g), full(sums), full(pair)],
        out_specs=row(V_W),
        out_shape=jax.ShapeDtypeStruct((LP, V_W), BF16),
        scratch_shapes=[pltpu.VMEM((GLA_HEADS, GLA_DK, GLA_DV), F32)],
        compiler_params=pltpu.CompilerParams(dimension_semantics=("arbitrary",),
                                             vmem_limit_bytes=VMEM_LIMIT),
        name="gla",
    )(q, k, v, la, g, sums, pair)


def _mix_kernel(o_ref, og_ref, glu_ref, hist_ref, gt_ref, x_ref, cw_ref, cb_ref, lg_ref, lb_ref,
                wpw_ref, bg_ref, wo_ref, g2_ref, wr_ref, br_ref,
                h_o, hn_o, ri_o, rw_o, cnt_o, ubuf, shbuf, cbuf, base):
    i = pl.program_id(0)

    @pl.when(i == 0)
    def _():
        base[...] = jnp.zeros_like(base)

    def glu(ref):
        val = ref[:, 0:D_MODEL].astype(F32)
        gate = ref[:, D_MODEL:2 * D_MODEL].astype(F32)
        return val * _sigmoid(gate)

    ubuf[0:HIST, :] = glu(hist_ref)
    ubuf[HIST:HIST + TM_MIX, :] = glu(glu_ref)

    lead = HIST - (CONV_WIDTH - 1)
    for rho in range(1, SUBLANES):
        shbuf[rho - 1] = ubuf[rho:rho + SH_ROWS, :]
    for cb in range(D_MODEL // LANES):
        lanes = slice(cb * LANES, (cb + 1) * LANES)
        for rb in range(TM_MIX // CONV_RB):
            part = jnp.broadcast_to(cb_ref[:, lanes], (CONV_RB, LANES))
            for j in range(CONV_WIDTH):
                rho = (lead + j) % SUBLANES
                r0 = rb * CONV_RB + (lead + j) - rho
                src = ubuf[r0:r0 + CONV_RB, lanes] if rho == 0 else shbuf[rho - 1, r0:r0 + CONV_RB, lanes]
                part = part + cw_ref[j:j + 1, lanes] * src
            cbuf[rb * CONV_RB:(rb + 1) * CONV_RB, lanes] = part
    acc = cbuf[...]

    mu = jnp.mean(acc, axis=-1, keepdims=True)
    xc = acc - mu
    ln = xc * lax.rsqrt(jnp.mean(xc * xc, axis=-1, keepdims=True) + EPS) * lg_ref[...] + lb_ref[...]
    branch_b = _dot(_silu(ln).astype(BF16), wpw_ref[...])

    branch_a = o_ref[...].astype(F32) * _silu(og_ref[...].astype(F32))
    g_a = _sigmoid(gt_ref[:, 0:D_MODEL].astype(F32) + bg_ref[:, 0:D_MODEL])
    g_b = _sigmoid(gt_ref[:, D_MODEL:2 * D_MODEL].astype(F32) + bg_ref[:, D_MODEL:2 * D_MODEL])
    merged = (g_a * branch_a + g_b * branch_b).astype(BF16)
    h1 = x_ref[...] + _dot(merged, wo_ref[...])
    h_o[...] = h1

    hn2 = h1 * lax.rsqrt(jnp.mean(h1 * h1, axis=-1, keepdims=True) + EPS) * g2_ref[...]
    for s in range(ROW_SLABS):
        hn_o[:, s, :] = hn2[:, s * LANES:(s + 1) * LANES]

    logits = lax.dot_general(wr_ref[...], hn2, (((1,), (1,)), ((), ())), precision=HIGHEST,
                             preferred_element_type=F32) + br_ref[...]
    row = lax.broadcasted_iota(jnp.int32, logits.shape, 0)
    rowf = row.astype(F32)
    neg = -jnp.inf
    is_g = row < N_GROUPS
    lg = jnp.where(is_g, logits, neg)
    gmax = jnp.max(lg, axis=0, keepdims=True)
    gidx = jnp.min(jnp.where(lg == gmax, rowf, float(N_GROUPS)), axis=0, keepdims=True)
    g_w = 1.0 / jnp.sum(jnp.where(is_g, jnp.exp(lg - gmax), 0.0), axis=0, keepdims=True)
    erow = rowf - float(N_GROUPS)
    egrp = ((row - N_GROUPS) >> 3).astype(F32)
    in_grp = (row >= N_GROUPS) & (row < N_GROUPS + N_EXPERTS) & (egrp == gidx)
    le = jnp.where(in_grp, logits, neg)
    m1 = jnp.max(le, axis=0, keepdims=True)
    i1 = jnp.min(jnp.where(le == m1, erow, float(N_EXPERTS)), axis=0, keepdims=True)
    le2 = jnp.where(erow == i1, neg, le)
    m2 = jnp.max(le2, axis=0, keepdims=True)
    i2 = jnp.min(jnp.where(le2 == m2, erow, float(N_EXPERTS)), axis=0, keepdims=True)
    t = jnp.exp(m2 - m1)
    w1 = g_w / (1.0 + t)
    w2 = g_w * t / (1.0 + t)

    oh1 = erow == i1
    oh2 = erow == i2
    oh = jnp.where(oh1 | oh2, 1.0, 0.0).astype(BF16)
    tr = lax.broadcasted_iota(jnp.int32, (TM_MIX, TM_MIX), 0)
    tc = lax.broadcasted_iota(jnp.int32, (TM_MIX, TM_MIX), 1)
    upper = (tr < tc).astype(BF16)
    cum = _dot(oh, upper) + base[...]
    rank1 = jnp.sum(jnp.where(oh1, cum, 0.0), axis=0, keepdims=True)
    rank2 = jnp.sum(jnp.where(oh2, cum, 0.0), axis=0, keepdims=True)
    base[...] = base[...] + jnp.sum(oh.astype(F32), axis=1, keepdims=True)
    cnt_o[...] = base[...]

    ri_o[...] = jnp.zeros_like(ri_o)
    ri_o[0:1, :] = i1.astype(jnp.int32)
    ri_o[1:2, :] = i2.astype(jnp.int32)
    ri_o[2:3, :] = rank1.astype(jnp.int32)
    ri_o[3:4, :] = rank2.astype(jnp.int32)
    rw_o[...] = jnp.zeros_like(rw_o)
    rw_o[0:1, :] = w1
    rw_o[1:2, :] = w2


def _mix(o_n, og, glu, gates, x2d, cw, cb, lg, lb, wpw, bg, wo, g2, wr, br):
    n_front = FRONT // TM_MIX
    grid = (SEQ // TM_MIX,)
    rowp = lambda w: pl.BlockSpec((TM_MIX, w), lambda i: (i + n_front, 0))
    full = lambda a: pl.BlockSpec(a.shape, lambda i: (0,) * a.ndim)
    hist_blocks = TM_MIX // HIST
    return pl.pallas_call(
        _mix_kernel,
        grid=grid,
        in_specs=[
            rowp(V_W), rowp(D_MODEL), rowp(2 * D_MODEL),
            pl.BlockSpec((HIST, 2 * D_MODEL), lambda i: ((i + n_front) * hist_blocks - 1, 0)),
            rowp(2 * D_MODEL),
            pl.BlockSpec((TM_MIX, D_MODEL), lambda i: (i, 0)),
            full(cw), full(cb), full(lg), full(lb), full(wpw), full(bg), full(wo), full(g2),
            full(wr), full(br),
        ],
        out_specs=[
            pl.BlockSpec((TM_MIX, D_MODEL), lambda i: (i, 0)),
            pl.BlockSpec((TM_MIX, ROW_SLABS, LANES), lambda i: (i, 0, 0)),
            pl.BlockSpec((SUBLANES, TM_MIX), lambda i: (0, i)),
            pl.BlockSpec((SUBLANES, TM_MIX), lambda i: (0, i)),
            pl.BlockSpec((LANES, 1), lambda i: (0, 0)),
        ],
        out_shape=[
            jax.ShapeDtypeStruct((SEQ, D_MODEL), F32),
            jax.ShapeDtypeStruct((SEQ, ROW_SLABS, LANES), F32),
            jax.ShapeDtypeStruct((SUBLANES, SEQ), jnp.int32),
            jax.ShapeDtypeStruct((SUBLANES, SEQ), F32),
            jax.ShapeDtypeStruct((LANES, 1), F32),
        ],
        scratch_shapes=[pltpu.VMEM((HIST + TM_MIX, D_MODEL), F32),
                        pltpu.VMEM((SUBLANES - 1, SH_ROWS, D_MODEL), F32),
                        pltpu.VMEM((TM_MIX, D_MODEL), F32), pltpu.VMEM((LANES, 1), F32)],
        compiler_params=pltpu.CompilerParams(dimension_semantics=("arbitrary",),
                                             vmem_limit_bytes=VMEM_LIMIT),
        name="mix",
    )(o_n, og, glu, glu, gates, x2d, cw, cb, lg, lb, wpw, bg, wo, g2, wr, br)


def _slot_row(ps_ref, ri_ref, kk, tok):
    return ps_ref[ri_ref[kk * SEQ + tok]] + ri_ref[(TOP_K + kk) * SEQ + tok]


def _dispatch_kernel(ps_ref, ri_ref, hn_ref, xb_ref, zbuf, sem):
    i = pl.program_id(0)

    @pl.when(i == 0)
    def _():
        zbuf[...] = jnp.zeros_like(zbuf)

        def fill(b, c):
            pltpu.make_async_copy(zbuf, xb_ref.at[pl.ds(b * BM, BM)], sem).start()
            return c

        lax.fori_loop(0, N_BLOCKS, fill, 0)

        def drain(b, c):
            pltpu.make_async_copy(zbuf, xb_ref.at[pl.ds(b * BM, BM)], sem).wait()
            return c

        lax.fori_loop(0, N_BLOCKS, drain, 0)

    base = i * TM_TOK

    def issue(r, c):
        for kk in range(TOP_K):
            dst = _slot_row(ps_ref, ri_ref, kk, base + r)
            pltpu.make_async_copy(hn_ref.at[r], xb_ref.at[dst], sem).start()
        return c

    lax.fori_loop(0, TM_TOK, issue, 0)
    for kk in range(TOP_K):
        pltpu.make_async_copy(hn_ref, xb_ref.at[pl.ds(0, TM_TOK)], sem).wait()


def _dispatch(pad_start, ri_flat, hn2p):
    grid_spec = pltpu.PrefetchScalarGridSpec(
        num_scalar_prefetch=2,
        grid=(SEQ // TM_TOK,),
        in_specs=[pl.BlockSpec((TM_TOK, ROW_SLABS, LANES), lambda i, ps, ri: (i, 0, 0))],
        out_specs=pl.BlockSpec(memory_space=pl.ANY),
        scratch_shapes=[pltpu.VMEM((BM, ROW_SLABS, LANES), F32), pltpu.SemaphoreType.DMA(())],
    )
    return pl.pallas_call(
        _dispatch_kernel,
        grid_spec=grid_spec,
        out_shape=jax.ShapeDtypeStruct((P_ROWS, ROW_SLABS, LANES), F32),
        compiler_params=pltpu.CompilerParams(dimension_semantics=("arbitrary",),
                                             vmem_limit_bytes=VMEM_LIMIT),
        name="dispatch",
    )(pad_start, ri_flat, hn2p)


def _experts_kernel(be_ref, first_ref, nused_ref, x_ref, wg_ref, wu_ref, wd_ref, y_ref,
                    wg_b, wu_b, wd_b):
    b = pl.program_id(0)

    @pl.when(b < nused_ref[0])
    def _():
        @pl.when(first_ref[b] == 1)
        def _():
            wg_b[...] = wg_ref[0].astype(BF16)
            wu_b[...] = wu_ref[0].astype(BF16)
            wd_b[...] = wd_ref[0].astype(BF16)

        x = jnp.concatenate([x_ref[:, s, :] for s in range(ROW_SLABS)], axis=-1).astype(BF16)
        a = _dot(x, wg_b[...])
        u = _dot(x, wu_b[...])
        y = _dot((_silu(a) * u).astype(BF16), wd_b[...])
        for s in range(ROW_SLABS):
            y_ref[:, s, :] = y[:, s * LANES:(s + 1) * LANES]

    @pl.when(b >= nused_ref[0])
    def _():
        y_ref[...] = jnp.zeros_like(y_ref)


def _experts(block_e, first, nused, xb, wg, wu, wd):
    def wmap(b, be, first, nused):
        return (be[b], 0, 0)

    def xmap(b, be, first, nused):
        return (b, 0, 0)

    grid_spec = pltpu.PrefetchScalarGridSpec(
        num_scalar_prefetch=3,
        grid=(N_BLOCKS,),
        in_specs=[
            pl.BlockSpec((BM, ROW_SLABS, LANES), xmap),
            pl.BlockSpec((1, D_MODEL, D_EXPERT), wmap),
            pl.BlockSpec((1, D_MODEL, D_EXPERT), wmap),
            pl.BlockSpec((1, D_EXPERT, D_MODEL), wmap),
        ],
        out_specs=pl.BlockSpec((BM, ROW_SLABS, LANES), xmap),
        scratch_shapes=[pltpu.VMEM((D_MODEL, D_EXPERT), BF16), pltpu.VMEM((D_MODEL, D_EXPERT), BF16),
                        pltpu.VMEM((D_EXPERT, D_MODEL), BF16)],
    )
    return pl.pallas_call(
        _experts_kernel,
        grid_spec=grid_spec,
        out_shape=jax.ShapeDtypeStruct((P_ROWS, ROW_SLABS, LANES), F32),
        compiler_params=pltpu.CompilerParams(dimension_semantics=("arbitrary",),
                                             vmem_limit_bytes=VMEM_LIMIT),
        name="experts",
    )(block_e, first, nused, xb, wg, wu, wd)


def _combine_kernel(ps_ref, ri_ref, yb_ref, h_ref, w_ref, g_ref, out_ref, buf, sem):
    i = pl.program_id(0)
    base = i * TM_TOK

    def issue(r, c):
        for kk in range(TOP_K):
            src = _slot_row(ps_ref, ri_ref, kk, base + r)
            pltpu.make_async_copy(yb_ref.at[src], buf.at[kk, r], sem).start()
        return c

    lax.fori_loop(0, TM_TOK, issue, 0)
    for kk in range(TOP_K):
        pltpu.make_async_copy(yb_ref.at[pl.ds(0, TM_TOK)], buf.at[kk], sem).wait()

    w0 = w_ref[:, 0:1]
    w1 = w_ref[:, 1:2]
    hs = []
    ssq = jnp.zeros((TM_TOK, 1), F32)
    for s in range(ROW_SLABS):
        y = w0 * buf[0, :, s, :] + w1 * buf[1, :, s, :]
        hh = h_ref[:, s * LANES:(s + 1) * LANES] + y
        ssq = ssq + jnp.sum(hh * hh, axis=-1, keepdims=True)
        hs.append(hh)
    scale = lax.rsqrt(ssq * (1.0 / D_MODEL) + EPS)
    for s in range(ROW_SLABS):
        out_ref[:, s * LANES:(s + 1) * LANES] = hs[s] * scale * g_ref[:, s * LANES:(s + 1) * LANES]


def _combine(pad_start, ri_flat, yb, h1, wts, gf):
    grid_spec = pltpu.PrefetchScalarGridSpec(
        num_scalar_prefetch=2,
        grid=(SEQ // TM_TOK,),
        in_specs=[
            pl.BlockSpec(memory_space=pl.ANY),
            pl.BlockSpec((TM_TOK, D_MODEL), lambda i, ps, ri: (i, 0)),
            pl.BlockSpec((TM_TOK, TOP_K), lambda i, ps, ri: (i, 0)),
            pl.BlockSpec((1, D_MODEL), lambda i, ps, ri: (0, 0)),
        ],
        out_specs=pl.BlockSpec((TM_TOK, D_MODEL), lambda i, ps, ri: (i, 0)),
        scratch_shapes=[pltpu.VMEM((TOP_K, TM_TOK, ROW_SLABS, LANES), F32), pltpu.SemaphoreType.DMA(())],
    )
    return pl.pallas_call(
        _combine_kernel,
        grid_spec=grid_spec,
        out_shape=jax.ShapeDtypeStruct((SEQ, D_MODEL), F32),
        compiler_params=pltpu.CompilerParams(dimension_semantics=("arbitrary",),
                                             vmem_limit_bytes=VMEM_LIMIT),
        name="combine",
    )(pad_start, ri_flat, yb, h1, wts, gf)


def kernel(x, meta, norm1_g, w_in, w_decay_up, b_decay, gla_norm_g, conv_w, conv_b, conv_ln_g,
           conv_ln_b, w_pw2, b_gate, w_out, norm2_g, w_router_group, b_router_group,
           w_router_expert, b_router_expert, w_exp_gate, w_exp_up, w_exp_down, final_norm_g):
    assert x.shape == (1, SEQ, D_MODEL) and w_in.shape[0] == 1
    x2d = x[0]
    front = jnp.concatenate([jnp.zeros((FRONT - N_META, D_MODEL), F32), meta.astype(F32)], axis=0)

    w = w_in[0]
    o_q, o_k, o_v, o_og, o_dl, o_glu, o_gt = (0, QK_W, 2 * QK_W, 2 * QK_W + V_W, 2 * QK_W + 2 * V_W,
                                              2 * QK_W + 2 * V_W + GLA_RANK,
                                              2 * QK_W + 2 * V_W + GLA_RANK + 2 * D_MODEL)
    cols = lambda a, b: w[:, a:b].astype(BF16)
    row2 = lambda a: a.reshape(1, -1).astype(F32)
    q, k, v, og, la, glu, gates = _inproj(
        front, x2d, row2(norm1_g[0]), cols(o_q, o_k), cols(o_k, o_v), cols(o_v, o_og),
        cols(o_og, o_dl), cols(o_dl, o_glu), cols(o_glu, o_gt), cols(o_gt, w.shape[1]),
        w_decay_up[0].astype(F32), row2(b_decay[0]))

    o_n = _gla(q, k, v, la, row2(gla_norm_g[0]))

    wr = jnp.concatenate([w_router_group[0].T, w_router_expert[0].T,
                          jnp.zeros((LANES - N_GROUPS - N_EXPERTS, D_MODEL), F32)], axis=0).astype(F32)
    br = jnp.concatenate([b_router_group[0], b_router_expert[0],
                          jnp.zeros((LANES - N_GROUPS - N_EXPERTS,), F32)]).reshape(LANES, 1).astype(F32)
    h1, hn2p, ri, rw, cnt = _mix(
        o_n, og, glu, gates, x2d, conv_w[0].astype(F32), row2(conv_b[0]), row2(conv_ln_g[0]),
        row2(conv_ln_b[0]), w_pw2[0].astype(BF16), row2(b_gate[0]), w_out[0].astype(BF16),
        row2(norm2_g[0]), wr, br)

    counts = cnt[N_GROUPS:N_GROUPS + N_EXPERTS, 0].astype(jnp.int32)
    padded = (counts + BM - 1) // BM * BM
    pad_end = jnp.cumsum(padded)
    pad_start = pad_end - padded
    pad_start = pad_start.astype(jnp.int32)
    ri_flat = ri[0:2 * TOP_K].reshape(-1)
    blk = jnp.arange(N_BLOCKS, dtype=jnp.int32)
    block_e = jnp.minimum(jnp.sum((pad_end[None, :] <= blk[:, None] * BM).astype(jnp.int32), axis=1),
                          N_EXPERTS - 1).astype(jnp.int32)
    first = jnp.concatenate([jnp.ones((1,), jnp.int32), (block_e[1:] != block_e[:-1]).astype(jnp.int32)])
    nused = (pad_end[-1:] // BM).astype(jnp.int32)

    xb = _dispatch(pad_start, ri_flat, hn2p)
    yb = _experts(block_e, first, nused, xb, w_exp_gate[0], w_exp_up[0], w_exp_down[0])
    out = _combine(pad_start, ri_flat, yb, h1, rw[0:TOP_K].T, row2(final_norm_g))
    return out[None]
```

```python
import functools

import jax
import jax.numpy as jnp
import numpy as np
from jax import lax
from jax.experimental import pallas as pl
from jax.experimental.pallas import tpu as pltpu

F32 = jnp.float32
BF16 = jnp.bfloat16
HIGHEST = lax.Precision.HIGHEST

D_MODEL = 1024
SEQ = 16384
N_META = 16
GLA_HEADS = 4
GLA_DK = 128
GLA_DV = 256
GLA_RANK = 16
GLA_TAU = 16.0
CHUNK = 64
QK_W = GLA_HEADS * GLA_DK
V_W = GLA_HEADS * GLA_DV
CONV_WIDTH = 31
N_GROUPS = 8
EXPERTS_PER_GROUP = 8
N_EXPERTS = N_GROUPS * EXPERTS_PER_GROUP
TOP_K = 2
D_EXPERT = 512
EPS = 1e-6
LOG2E = 1.4426950408889634

LANES = 128
SUBLANES = 8
ROW_SLABS = D_MODEL // LANES

FRONT = 512
LP = FRONT + SEQ
TM_PROJ = 512
TM_GLA = 512
TM_MIX = 256
HIST = 32
SH_ROWS = TM_MIX + HIST - SUBLANES
CONV_RB = 64
TM_DISP = 512
TM_TOK = 256
DMA_UNROLL = 8
BM = 256
N_BLOCKS = (SEQ * TOP_K) // BM + N_EXPERTS
P_ROWS = N_BLOCKS * BM
VMEM_LIMIT = 56 * 1024 * 1024


def _sigmoid(x):
    return 0.5 * jnp.tanh(0.5 * x) + 0.5


def _silu(x):
    return x * _sigmoid(x)


def _dot(a, b, **kw):
    return jnp.dot(a, b, preferred_element_type=F32, **kw)


def _load_token_tiles(ref, n_tok):
    return jnp.concatenate([ref[pl.ds(s, n_tok, stride=ROW_SLABS), :] for s in range(ROW_SLABS)], axis=-1)


def _store_token_tiles(ref, val, n_tok):
    for s in range(ROW_SLABS):
        ref[pl.ds(s, n_tok, stride=ROW_SLABS), :] = val[:, s * LANES:(s + 1) * LANES]


def _token_rows(tok):
    return pl.ds(pl.multiple_of(tok * ROW_SLABS, ROW_SLABS), ROW_SLABS)


def _inproj_kernel(front_ref, x_ref, g_ref, wq, wk, wv, wog, wdl, wglu, wgt, wdu, bd,
                   q_o, k_o, v_o, og_o, la_o, glu_o, gt_o):
    i = pl.program_id(0)
    h = jnp.where(i < FRONT // TM_PROJ, front_ref[...], x_ref[...])
    ms = jnp.mean(h * h, axis=-1, keepdims=True)
    hn = (h * lax.rsqrt(ms + EPS) * g_ref[...]).astype(BF16)
    q_o[...] = _dot(hn, wq[...]).astype(BF16)
    k_o[...] = _dot(hn, wk[...]).astype(BF16)
    v_o[...] = _dot(hn, wv[...]).astype(BF16)
    og_o[...] = _dot(hn, wog[...]).astype(BF16)
    glu_o[...] = _dot(hn, wglu[...]).astype(BF16)
    gt_o[...] = _dot(hn, wgt[...]).astype(BF16)
    dlow = _dot(hn, wdl[...])
    z = _dot(dlow, wdu[...], precision=HIGHEST) + bd[...]
    la_o[...] = (jnp.minimum(z, 0.0) - jnp.log(1.0 + jnp.exp(-jnp.abs(z)))) * (LOG2E / GLA_TAU)


def _inproj(front, x2d, g1, wq, wk, wv, wog, wdl, wglu, wgt, wdu, bd):
    n_front = FRONT // TM_PROJ
    grid = (LP // TM_PROJ,)
    row = lambda w: pl.BlockSpec((TM_PROJ, w), lambda i: (i, 0))
    full = lambda a: pl.BlockSpec(a.shape, lambda i: (0, 0))
    return pl.pallas_call(
        _inproj_kernel,
        grid=grid,
        in_specs=[
            pl.BlockSpec((TM_PROJ, D_MODEL), lambda i: (jnp.minimum(i, n_front - 1), 0)),
            pl.BlockSpec((TM_PROJ, D_MODEL), lambda i: (jnp.maximum(i - n_front, 0), 0)),
            full(g1), full(wq), full(wk), full(wv), full(wog), full(wdl), full(wglu), full(wgt),
            full(wdu), full(bd),
        ],
        out_specs=[row(QK_W), row(QK_W), row(V_W), row(D_MODEL), row(QK_W), row(2 * D_MODEL),
                   row(2 * D_MODEL)],
        out_shape=[
            jax.ShapeDtypeStruct((LP, QK_W), BF16), jax.ShapeDtypeStruct((LP, QK_W), BF16),
            jax.ShapeDtypeStruct((LP, V_W), BF16), jax.ShapeDtypeStruct((LP, D_MODEL), BF16),
            jax.ShapeDtypeStruct((LP, QK_W), F32), jax.ShapeDtypeStruct((LP, 2 * D_MODEL), BF16),
            jax.ShapeDtypeStruct((LP, 2 * D_MODEL), BF16),
        ],
        compiler_params=pltpu.CompilerParams(dimension_semantics=("arbitrary",),
                                             vmem_limit_bytes=VMEM_LIMIT),
        name="inproj",
    )(front, x2d, g1, wq, wk, wv, wog, wdl, wglu, wgt, wdu, bd)


GLA_LEVELS = (32, 16, 8, 4, 2, 1)
LA_SPLIT = 3


def _gla_tables():
    r = np.arange(CHUNK)
    t, c = r[:, None], r[None, :]
    rows = [c <= t, c > t]
    pair = []
    for m in GLA_LEVELS:
        mid = (t // (2 * m)) * (2 * m) + m - 1
        right = (t % (2 * m)) >= m
        rows.append(np.where(right, (c > mid) & (c <= t), (c > t) & (c <= mid)))
        pair.append(((t // (2 * m)) == (c // (2 * m))) & right & ((c % (2 * m)) < m))
    pair.append(t == c)
    sums = np.concatenate(rows, axis=0).astype(np.float32)
    sums = np.concatenate([sums] * LA_SPLIT, axis=1)
    return jnp.asarray(sums, BF16), jnp.asarray(np.stack(pair).astype(np.float32))


def _split_bf16(x):
    pieces = []
    rest = x
    for _ in range(LA_SPLIT):
        p = rest.astype(BF16)
        pieces.append(p)
        rest = rest - p.astype(F32)
    return jnp.concatenate(pieces, axis=0)


def _dot_nt(a, b):
    return lax.dot_general(a, b, (((1,), (1,)), ((), ())), preferred_element_type=F32)


def _dot_tn(a, b):
    return lax.dot_general(a, b, (((0,), (0,)), ((), ())), preferred_element_type=F32)


def _gla_chunk(q, k, vb, la2, s_prev, sums, pair_ref, right_rows, ones_col):
    la3 = _split_bf16(la2)
    e = jnp.exp2(_dot(sums, la3))

    qe = (q * e[0:CHUNK]).astype(BF16)
    o = _dot(qe, s_prev.astype(BF16))
    kd = (k * e[CHUNK:2 * CHUNK]).astype(BF16)
    kv = _dot_tn(kd, vb)
    dec = jnp.exp2(_dot_tn(la3, ones_col))
    s_new = s_prev * jnp.concatenate([dec] * (GLA_DV // LANES), axis=1) + kv

    sc = _dot_nt(q.astype(BF16), k.astype(BF16)) * pair_ref[len(GLA_LEVELS)]
    for lvl in range(len(GLA_LEVELS)):
        rows = e[(2 + lvl) * CHUNK:(3 + lvl) * CHUNK]
        x = (jnp.where(right_rows[lvl], q, k) * rows).astype(BF16)
        sc = sc + _dot_nt(x, x) * pair_ref[lvl]
    o = o + _dot(sc.astype(BF16), vb)
    return o, s_new


def _gla_kernel(q_ref, k_ref, v_ref, la_ref, g_ref, sums_ref, pair_ref, o_ref, s_ref):
    @pl.when(pl.program_id(0) == 0)
    def _():
        s_ref[...] = jnp.zeros_like(s_ref)

    row = lax.broadcasted_iota(jnp.int32, (CHUNK, GLA_DK), 0)
    right_rows = [(row & m) != 0 for m in GLA_LEVELS]
    ones_col = jnp.ones((LA_SPLIT * CHUNK, LANES), BF16)
    sums = sums_ref[...]

    def body(c, carry):
        r0 = pl.multiple_of(c * CHUNK, CHUNK)
        rows = pl.ds(r0, CHUNK)
        states = [s_ref[h] for h in range(GLA_HEADS)]
        new_states = []
        for h in range(GLA_HEADS):
            kc = slice(h * GLA_DK, (h + 1) * GLA_DK)
            vc = slice(h * GLA_DV, (h + 1) * GLA_DV)
            q = q_ref[rows, kc].astype(F32) * (GLA_DK ** -0.5)
            k = k_ref[rows, kc].astype(F32)
            o, s_new = _gla_chunk(q, k, v_ref[rows, vc], la_ref[rows, kc], states[h], sums, pair_ref,
                                  right_rows, ones_col)
            new_states.append(s_new)
            o = o * lax.rsqrt(jnp.mean(o * o, axis=-1, keepdims=True) + EPS) * g_ref[:, vc]
            o_ref[rows, vc] = o.astype(BF16)
        for h in range(GLA_HEADS):
            s_ref[h] = new_states[h]
        return carry

    lax.fori_loop(0, TM_GLA // CHUNK, body, 0, unroll=2)


def _gla(q, k, v, la, g):
    sums, pair = _gla_tables()
    row = lambda w: pl.BlockSpec((TM_GLA, w), lambda t: (t, 0))
    full = lambda a: pl.BlockSpec(a.shape, lambda t: (0,) * a.ndim)
    return pl.pallas_call(
        _gla_kernel,
        grid=(LP // TM_GLA,),
        in_specs=[row(QK_W), row(QK_W), row(V_W), row(QK_W), full(g), full(sums), full(pair)],
        out_specs=row(V_W),
        out_shape=jax.ShapeDtypeStruct((LP, V_W), BF16),
        scratch_shapes=[pltpu.VMEM((GLA_HEADS, GLA_DK, GLA_DV), F32)],
        compiler_params=pltpu.CompilerParams(dimension_semantics=("arbitrary",),
                                             vmem_limit_bytes=VMEM_LIMIT),
        name="gla",
    )(q, k, v, la, g, sums, pair)


def _mix_kernel(o_ref, og_ref, glu_ref, hist_ref, gt_ref, x_ref, cw_ref, cb_ref, lg_ref, lb_ref,
                wpw_ref, bg_ref, wo_ref, g2_ref, wr_ref, br_ref,
                h_o, hn_o, ri_o, rw_o, cnt_o, ubuf, shbuf, cbuf, base):
    i = pl.program_id(0)

    @pl.when(i == 0)
    def _():
        base[...] = jnp.zeros_like(base)

    def glu(ref):
        val = ref[:, 0:D_MODEL].astype(F32)
        gate = ref[:, D_MODEL:2 * D_MODEL].astype(F32)
        return val * _sigmoid(gate)

    ubuf[0:HIST, :] = glu(hist_ref)
    ubuf[HIST:HIST + TM_MIX, :] = glu(glu_ref)

    lead = HIST - (CONV_WIDTH - 1)
    for rho in range(1, SUBLANES):
        shbuf[rho - 1] = ubuf[rho:rho + SH_ROWS, :]
    for cb in range(D_MODEL // LANES):
        lanes = slice(cb * LANES, (cb + 1) * LANES)
        for rb in range(TM_MIX // CONV_RB):
            part = jnp.broadcast_to(cb_ref[:, lanes], (CONV_RB, LANES))
            for j in range(CONV_WIDTH):
                rho = (lead + j) % SUBLANES
                r0 = rb * CONV_RB + (lead + j) - rho
                src = ubuf[r0:r0 + CONV_RB, lanes] if rho == 0 else shbuf[rho - 1, r0:r0 + CONV_RB, lanes]
                part = part + cw_ref[j:j + 1, lanes] * src
            cbuf[rb * CONV_RB:(rb + 1) * CONV_RB, lanes] = part
    acc = cbuf[...]

    mu = jnp.mean(acc, axis=-1, keepdims=True)
    xc = acc - mu
    ln = xc * lax.rsqrt(jnp.mean(xc * xc, axis=-1, keepdims=True) + EPS) * lg_ref[...] + lb_ref[...]
    branch_b = _dot(_silu(ln).astype(BF16), wpw_ref[...])

    branch_a = o_ref[...].astype(F32) * _silu(og_ref[...].astype(F32))
    g_a = _sigmoid(gt_ref[:, 0:D_MODEL].astype(F32) + bg_ref[:, 0:D_MODEL])
    g_b = _sigmoid(gt_ref[:, D_MODEL:2 * D_MODEL].astype(F32) + bg_ref[:, D_MODEL:2 * D_MODEL])
    merged = (g_a * branch_a + g_b * branch_b).astype(BF16)
    h1 = x_ref[...] + _dot(merged, wo_ref[...])
    h_o[...] = h1

    hn2 = h1 * lax.rsqrt(jnp.mean(h1 * h1, axis=-1, keepdims=True) + EPS) * g2_ref[...]
    _store_token_tiles(hn_o, hn2, TM_MIX)

    logits = lax.dot_general(wr_ref[...], hn2, (((1,), (1,)), ((), ())), precision=HIGHEST,
                             preferred_element_type=F32) + br_ref[...]
    row = lax.broadcasted_iota(jnp.int32, logits.shape, 0)
    rowf = row.astype(F32)
    neg = -jnp.inf
    is_g = row < N_GROUPS
    lg = jnp.where(is_g, logits, neg)
    gmax = jnp.max(lg, axis=0, keepdims=True)
    gidx = jnp.min(jnp.where(lg == gmax, rowf, float(N_GROUPS)), axis=0, keepdims=True)
    g_w = 1.0 / jnp.sum(jnp.where(is_g, jnp.exp(lg - gmax), 0.0), axis=0, keepdims=True)
    erow = rowf - float(N_GROUPS)
    egrp = ((row - N_GROUPS) >> 3).astype(F32)
    in_grp = (row >= N_GROUPS) & (row < N_GROUPS + N_EXPERTS) & (egrp == gidx)
    le = jnp.where(in_grp, logits, neg)
    m1 = jnp.max(le, axis=0, keepdims=True)
    i1 = jnp.min(jnp.where(le == m1, erow, float(N_EXPERTS)), axis=0, keepdims=True)
    le2 = jnp.where(erow == i1, neg, le)
    m2 = jnp.max(le2, axis=0, keepdims=True)
    i2 = jnp.min(jnp.where(le2 == m2, erow, float(N_EXPERTS)), axis=0, keepdims=True)
    t = jnp.exp(m2 - m1)
    w1 = g_w / (1.0 + t)
    w2 = g_w * t / (1.0 + t)

    oh1 = erow == i1
    oh2 = erow == i2
    oh = jnp.where(oh1 | oh2, 1.0, 0.0).astype(BF16)
    tr = lax.broadcasted_iota(jnp.int32, (TM_MIX, TM_MIX), 0)
    tc = lax.broadcasted_iota(jnp.int32, (TM_MIX, TM_MIX), 1)
    upper = (tr < tc).astype(BF16)
    cum = _dot(oh, upper) + base[...]
    rank1 = jnp.sum(jnp.where(oh1, cum, 0.0), axis=0, keepdims=True)
    rank2 = jnp.sum(jnp.where(oh2, cum, 0.0), axis=0, keepdims=True)
    base[...] = base[...] + jnp.sum(oh.astype(F32), axis=1, keepdims=True)
    cnt_o[...] = base[...]

    ri_o[...] = jnp.zeros_like(ri_o)
    ri_o[0:1, :] = i1.astype(jnp.int32)
    ri_o[1:2, :] = i2.astype(jnp.int32)
    ri_o[2:3, :] = rank1.astype(jnp.int32)
    ri_o[3:4, :] = rank2.astype(jnp.int32)
    rw_o[...] = jnp.zeros_like(rw_o)
    rw_o[0:1, :] = w1
    rw_o[1:2, :] = w2


def _mix(o_n, og, glu, gates, x2d, cw, cb, lg, lb, wpw, bg, wo, g2, wr, br):
    n_front = FRONT // TM_MIX
    grid = (SEQ // TM_MIX,)
    rowp = lambda w: pl.BlockSpec((TM_MIX, w), lambda i: (i + n_front, 0))
    full = lambda a: pl.BlockSpec(a.shape, lambda i: (0,) * a.ndim)
    hist_blocks = TM_MIX // HIST
    return pl.pallas_call(
        _mix_kernel,
        grid=grid,
        in_specs=[
            rowp(V_W), rowp(D_MODEL), rowp(2 * D_MODEL),
            pl.BlockSpec((HIST, 2 * D_MODEL), lambda i: ((i + n_front) * hist_blocks - 1, 0)),
            rowp(2 * D_MODEL),
            pl.BlockSpec((TM_MIX, D_MODEL), lambda i: (i, 0)),
            full(cw), full(cb), full(lg), full(lb), full(wpw), full(bg), full(wo), full(g2),
            full(wr), full(br),
        ],
        out_specs=[
            pl.BlockSpec((TM_MIX, D_MODEL), lambda i: (i, 0)),
            pl.BlockSpec((TM_MIX * ROW_SLABS, LANES), lambda i: (i, 0)),
            pl.BlockSpec((SUBLANES, TM_MIX), lambda i: (0, i)),
            pl.BlockSpec((SUBLANES, TM_MIX), lambda i: (0, i)),
            pl.BlockSpec((LANES, 1), lambda i: (0, 0)),
        ],
        out_shape=[
            jax.ShapeDtypeStruct((SEQ, D_MODEL), F32),
            jax.ShapeDtypeStruct((SEQ * ROW_SLABS, LANES), F32),
            jax.ShapeDtypeStruct((SUBLANES, SEQ), jnp.int32),
            jax.ShapeDtypeStruct((SUBLANES, SEQ), F32),
            jax.ShapeDtypeStruct((LANES, 1), F32),
        ],
        scratch_shapes=[pltpu.VMEM((HIST + TM_MIX, D_MODEL), F32),
                        pltpu.VMEM((SUBLANES - 1, SH_ROWS, D_MODEL), F32),
                        pltpu.VMEM((TM_MIX, D_MODEL), F32), pltpu.VMEM((LANES, 1), F32)],
        compiler_params=pltpu.CompilerParams(dimension_semantics=("arbitrary",),
                                             vmem_limit_bytes=VMEM_LIMIT),
        name="mix",
    )(o_n, og, glu, glu, gates, x2d, cw, cb, lg, lb, wpw, bg, wo, g2, wr, br)


def _slot_row(ps_ref, ri_ref, kk, tok):
    return ps_ref[ri_ref[kk * SEQ + tok]] + ri_ref[(TOP_K + kk) * SEQ + tok]


def _dispatch_kernel(ps_ref, ri_ref, hn_ref, xb_ref, zbuf, sem):
    i = pl.program_id(0)

    @pl.when(i == 0)
    def _():
        zbuf[...] = jnp.zeros_like(zbuf)

        def block_fill(b):
            rows = pl.ds(pl.multiple_of(b * (BM * ROW_SLABS), BM * ROW_SLABS), BM * ROW_SLABS)
            return pltpu.make_async_copy(zbuf, xb_ref.at[rows], sem)

        def fill(b, c):
            block_fill(b).start()
            return c

        lax.fori_loop(0, N_BLOCKS, fill, 0)

        def drain(b, c):
            block_fill(b).wait()
            return c

        lax.fori_loop(0, N_BLOCKS, drain, 0)

    base = i * TM_DISP

    def issue(r, c):
        for kk in range(TOP_K):
            dst = _slot_row(ps_ref, ri_ref, kk, base + r)
            pltpu.make_async_copy(hn_ref.at[_token_rows(r)], xb_ref.at[_token_rows(dst)], sem).start()
        return c

    lax.fori_loop(0, TM_DISP, issue, 0, unroll=DMA_UNROLL)
    for kk in range(TOP_K):
        pltpu.make_async_copy(hn_ref, xb_ref.at[pl.ds(0, TM_DISP * ROW_SLABS)], sem).wait()


def _dispatch(pad_start, ri_flat, hn2p):
    grid_spec = pltpu.PrefetchScalarGridSpec(
        num_scalar_prefetch=2,
        grid=(SEQ // TM_DISP,),
        in_specs=[pl.BlockSpec((TM_DISP * ROW_SLABS, LANES), lambda i, ps, ri: (i, 0))],
        out_specs=pl.BlockSpec(memory_space=pl.ANY),
        scratch_shapes=[pltpu.VMEM((BM * ROW_SLABS, LANES), F32), pltpu.SemaphoreType.DMA(())],
    )
    return pl.pallas_call(
        _dispatch_kernel,
        grid_spec=grid_spec,
        out_shape=jax.ShapeDtypeStruct((P_ROWS * ROW_SLABS, LANES), F32),
        compiler_params=pltpu.CompilerParams(dimension_semantics=("arbitrary",),
                                             vmem_limit_bytes=VMEM_LIMIT),
        name="dispatch",
    )(pad_start, ri_flat, hn2p)


def _experts_kernel(be_ref, first_ref, next_ref, nused_ref, x_ref, wg_hbm, wu_hbm, wd_hbm, y_ref,
                    wg_f, wu_f, wd_f, wg_b, wu_b, wd_b, sem):
    b = pl.program_id(0)

    def weight_copies(e):
        return (pltpu.make_async_copy(wg_hbm.at[e], wg_f, sem.at[0]),
                pltpu.make_async_copy(wu_hbm.at[e], wu_f, sem.at[1]),
                pltpu.make_async_copy(wd_hbm.at[e], wd_f, sem.at[2]))

    @pl.when(b == 0)
    def _():
        for c in weight_copies(be_ref[0]):
            c.start()

    @pl.when(b < nused_ref[0])
    def _():
        @pl.when(first_ref[b] == 1)
        def _():
            for c in weight_copies(be_ref[b]):
                c.wait()
            wg_b[...] = wg_f[...].astype(BF16)
            wu_b[...] = wu_f[...].astype(BF16)
            wd_b[...] = wd_f[...].astype(BF16)

            @pl.when(next_ref[b] >= 0)
            def _():
                for c in weight_copies(next_ref[b]):
                    c.start()

        x = _load_token_tiles(x_ref, BM).astype(BF16)
        a = _dot(x, wg_b[...])
        u = _dot(x, wu_b[...])
        y = _dot((_silu(a) * u).astype(BF16), wd_b[...])
        _store_token_tiles(y_ref, y, BM)

    @pl.when(b >= nused_ref[0])
    def _():
        y_ref[...] = jnp.zeros_like(y_ref)


def _experts(block_e, first, next_e, nused, xb, wg, wu, wd):
    def xmap(b, be, first, nxt, nused):
        return (b, 0)

    grid_spec = pltpu.PrefetchScalarGridSpec(
        num_scalar_prefetch=4,
        grid=(N_BLOCKS,),
        in_specs=[
            pl.BlockSpec((BM * ROW_SLABS, LANES), xmap),
            pl.BlockSpec(memory_space=pl.ANY),
            pl.BlockSpec(memory_space=pl.ANY),
            pl.BlockSpec(memory_space=pl.ANY),
        ],
        out_specs=pl.BlockSpec((BM * ROW_SLABS, LANES), xmap),
        scratch_shapes=[pltpu.VMEM((D_MODEL, D_EXPERT), F32), pltpu.VMEM((D_MODEL, D_EXPERT), F32),
                        pltpu.VMEM((D_EXPERT, D_MODEL), F32),
                        pltpu.VMEM((D_MODEL, D_EXPERT), BF16), pltpu.VMEM((D_MODEL, D_EXPERT), BF16),
                        pltpu.VMEM((D_EXPERT, D_MODEL), BF16), pltpu.SemaphoreType.DMA((3,))],
    )
    return pl.pallas_call(
        _experts_kernel,
        grid_spec=grid_spec,
        out_shape=jax.ShapeDtypeStruct((P_ROWS * ROW_SLABS, LANES), F32),
        compiler_params=pltpu.CompilerParams(dimension_semantics=("arbitrary",),
                                             vmem_limit_bytes=VMEM_LIMIT),
        name="experts",
    )(block_e, first, next_e, nused, xb, wg, wu, wd)


def _combine_kernel(ps_ref, ri_ref, yb_ref, h_ref, w_ref, g_ref, out_ref, b00, b01, b10, b11, sem):
    i = pl.program_id(0)
    bufs = ((b00, b01), (b10, b11))

    def gather(tile, slot, start):
        if start:
            def issue(r, c):
                for kk in range(TOP_K):
                    src = _slot_row(ps_ref, ri_ref, kk, tile * TM_TOK + r)
                    pltpu.make_async_copy(yb_ref.at[_token_rows(src)], bufs[slot][kk].at[_token_rows(r)],
                                          sem.at[slot]).start()
                return c

            lax.fori_loop(0, TM_TOK, issue, 0, unroll=DMA_UNROLL)
        else:
            for kk in range(TOP_K):
                pltpu.make_async_copy(yb_ref.at[pl.ds(0, TM_TOK * ROW_SLABS)], bufs[slot][kk],
                                      sem.at[slot]).wait()

    @pl.when(i == 0)
    def _():
        gather(0, 0, start=True)

    def combine(slot):
        @pl.when(i + 1 < pl.num_programs(0))
        def _():
            gather(i + 1, 1 - slot, start=True)

        gather(i, slot, start=False)
        w0 = w_ref[:, 0:1]
        w1 = w_ref[:, 1:2]
        hs = []
        ssq = jnp.zeros((TM_TOK, 1), F32)
        for s in range(ROW_SLABS):
            rows = pl.ds(s, TM_TOK, stride=ROW_SLABS)
            y = w0 * bufs[slot][0][rows, :] + w1 * bufs[slot][1][rows, :]
            hh = h_ref[:, s * LANES:(s + 1) * LANES] + y
            ssq = ssq + jnp.sum(hh * hh, axis=-1, keepdims=True)
            hs.append(hh)
        scale = lax.rsqrt(ssq * (1.0 / D_MODEL) + EPS)
        for s in range(ROW_SLABS):
            out_ref[:, s * LANES:(s + 1) * LANES] = hs[s] * scale * g_ref[:, s * LANES:(s + 1) * LANES]

    for slot in range(2):
        pl.when(i % 2 == slot)(functools.partial(combine, slot))


def _combine(pad_start, ri_flat, yb, h1, wts, gf):
    grid_spec = pltpu.PrefetchScalarGridSpec(
        num_scalar_prefetch=2,
        grid=(SEQ // TM_TOK,),
        in_specs=[
            pl.BlockSpec(memory_space=pl.ANY),
            pl.BlockSpec((TM_TOK, D_MODEL), lambda i, ps, ri: (i, 0)),
            pl.BlockSpec((TM_TOK, TOP_K), lambda i, ps, ri: (i, 0)),
            pl.BlockSpec((1, D_MODEL), lambda i, ps, ri: (0, 0)),
        ],
        out_specs=pl.BlockSpec((TM_TOK, D_MODEL), lambda i, ps, ri: (i, 0)),
        scratch_shapes=[pltpu.VMEM((TM_TOK * ROW_SLABS, LANES), F32)] * (2 * TOP_K)
                       + [pltpu.SemaphoreType.DMA((2,))],
    )
    return pl.pallas_call(
        _combine_kernel,
        grid_spec=grid_spec,
        out_shape=jax.ShapeDtypeStruct((SEQ, D_MODEL), F32),
        compiler_params=pltpu.CompilerParams(dimension_semantics=("arbitrary",),
                                             vmem_limit_bytes=VMEM_LIMIT),
        name="combine",
    )(pad_start, ri_flat, yb, h1, wts, gf)


def kernel(x, meta, norm1_g, w_in, w_decay_up, b_decay, gla_norm_g, conv_w, conv_b, conv_ln_g,
           conv_ln_b, w_pw2, b_gate, w_out, norm2_g, w_router_group, b_router_group,
           w_router_expert, b_router_expert, w_exp_gate, w_exp_up, w_exp_down, final_norm_g):
    assert x.shape == (1, SEQ, D_MODEL) and w_in.shape[0] == 1
    x2d = x[0]
    front = jnp.concatenate([jnp.zeros((FRONT - N_META, D_MODEL), F32), meta.astype(F32)], axis=0)

    w = w_in[0]
    o_q, o_k, o_v, o_og, o_dl, o_glu, o_gt = (0, QK_W, 2 * QK_W, 2 * QK_W + V_W, 2 * QK_W + 2 * V_W,
                                              2 * QK_W + 2 * V_W + GLA_RANK,
                                              2 * QK_W + 2 * V_W + GLA_RANK + 2 * D_MODEL)
    cols = lambda a, b: w[:, a:b].astype(BF16)
    row2 = lambda a: a.reshape(1, -1).astype(F32)
    q, k, v, og, la, glu, gates = _inproj(
        front, x2d, row2(norm1_g[0]), cols(o_q, o_k), cols(o_k, o_v), cols(o_v, o_og),
        cols(o_og, o_dl), cols(o_dl, o_glu), cols(o_glu, o_gt), cols(o_gt, w.shape[1]),
        w_decay_up[0].astype(F32), row2(b_decay[0]))

    o_n = _gla(q, k, v, la, row2(gla_norm_g[0]))

    wr = jnp.concatenate([w_router_group[0].T, w_router_expert[0].T,
                          jnp.zeros((LANES - N_GROUPS - N_EXPERTS, D_MODEL), F32)], axis=0).astype(F32)
    br = jnp.concatenate([b_router_group[0], b_router_expert[0],
                          jnp.zeros((LANES - N_GROUPS - N_EXPERTS,), F32)]).reshape(LANES, 1).astype(F32)
    h1, hn2p, ri, rw, cnt = _mix(
        o_n, og, glu, gates, x2d, conv_w[0].astype(F32), row2(conv_b[0]), row2(conv_ln_g[0]),
        row2(conv_ln_b[0]), w_pw2[0].astype(BF16), row2(b_gate[0]), w_out[0].astype(BF16),
        row2(norm2_g[0]), wr, br)

    counts = cnt[N_GROUPS:N_GROUPS + N_EXPERTS, 0].astype(jnp.int32)
    padded = (counts + BM - 1) // BM * BM
    pad_end = jnp.cumsum(padded)
    pad_start = pad_end - padded
    pad_start = pad_start.astype(jnp.int32)
    ri_flat = ri[0:2 * TOP_K].reshape(-1)
    blk = jnp.arange(N_BLOCKS, dtype=jnp.int32)
    block_e = jnp.minimum(jnp.sum((pad_end[None, :] <= blk[:, None] * BM).astype(jnp.int32), axis=1),
                          N_EXPERTS - 1).astype(jnp.int32)
    first = jnp.concatenate([jnp.ones((1,), jnp.int32), (block_e[1:] != block_e[:-1]).astype(jnp.int32)])
    nused = (pad_end[-1:] // BM).astype(jnp.int32)
    eid = jnp.arange(N_EXPERTS, dtype=jnp.int32)
    later = jnp.flip(lax.cummin(jnp.flip(jnp.where(counts > 0, eid, N_EXPERTS))))
    nxt = jnp.concatenate([later[1:], jnp.full((1,), N_EXPERTS, jnp.int32)])
    next_e = jnp.where(nxt < N_EXPERTS, nxt, -1)[block_e].astype(jnp.int32)

    xb = _dispatch(pad_start, ri_flat, hn2p)
    yb = _experts(block_e, first, next_e, nused, xb, w_exp_gate[0], w_exp_up[0], w_exp_down[0])
    out = _combine(pad_start, ri_flat, yb, h1, rw[0:TOP_K].T, row2(final_norm_g))
    return out[None]
```

```python
import functools

import jax
import jax.numpy as jnp
import numpy as np
from jax import lax
from jax.experimental import pallas as pl
from jax.experimental.pallas import tpu as pltpu

F32 = jnp.float32
BF16 = jnp.bfloat16
HIGHEST = lax.Precision.HIGHEST

D_MODEL = 1024
SEQ = 16384
N_META = 16
GLA_HEADS = 4
GLA_DK = 128
GLA_DV = 256
GLA_RANK = 16
GLA_TAU = 16.0
CHUNK = 64
QK_W = GLA_HEADS * GLA_DK
V_W = GLA_HEADS * GLA_DV
CONV_WIDTH = 31
N_GROUPS = 8
EXPERTS_PER_GROUP = 8
N_EXPERTS = N_GROUPS * EXPERTS_PER_GROUP
TOP_K = 2
D_EXPERT = 512
EPS = 1e-6
LOG2E = 1.4426950408889634

LANES = 128
SUBLANES = 8
ROW_SLABS = D_MODEL // LANES

FRONT = 512
LP = FRONT + SEQ
TM_PROJ = 512
TM_GLA = 512
TM_MIX = 256
HIST = 32
SH_ROWS = TM_MIX + HIST - SUBLANES
CONV_RB = 64
N_TILES = SEQ // TM_MIX
SLOTS = TOP_K * TM_MIX
BM = 256
N_BLOCKS = (SEQ * TOP_K) // BM + N_EXPERTS
P_ROWS = N_BLOCKS * BM
VMEM_LIMIT = 56 * 1024 * 1024


def _sigmoid(x):
    return 0.5 * jnp.tanh(0.5 * x) + 0.5


def _silu(x):
    return x * _sigmoid(x)


def _dot(a, b, **kw):
    return jnp.dot(a, b, preferred_element_type=F32, **kw)


def _load_token_tiles(ref, n_tok):
    return jnp.concatenate([ref[pl.ds(s, n_tok, stride=ROW_SLABS), :] for s in range(ROW_SLABS)], axis=-1)


def _store_token_tiles(ref, val, n_tok):
    for s in range(ROW_SLABS):
        ref[pl.ds(s, n_tok, stride=ROW_SLABS), :] = val[:, s * LANES:(s + 1) * LANES]


def _inproj_kernel(front_ref, x_ref, g_ref, wq, wk, wv, wog, wdl, wglu, wgt, wdu, bd,
                   q_o, k_o, v_o, og_o, la_o, glu_o, gt_o):
    i = pl.program_id(0)
    h = jnp.where(i < FRONT // TM_PROJ, front_ref[...], x_ref[...])
    ms = jnp.mean(h * h, axis=-1, keepdims=True)
    hn = (h * lax.rsqrt(ms + EPS) * g_ref[...]).astype(BF16)
    q_o[...] = _dot(hn, wq[...]).astype(BF16)
    k_o[...] = _dot(hn, wk[...]).astype(BF16)
    v_o[...] = _dot(hn, wv[...]).astype(BF16)
    og_o[...] = _dot(hn, wog[...]).astype(BF16)
    glu_o[...] = _dot(hn, wglu[...]).astype(BF16)
    gt_o[...] = _dot(hn, wgt[...]).astype(BF16)
    dlow = _dot(hn, wdl[...])
    z = _dot(dlow, wdu[...], precision=HIGHEST) + bd[...]
    la_o[...] = (jnp.minimum(z, 0.0) - jnp.log(1.0 + jnp.exp(-jnp.abs(z)))) * (LOG2E / GLA_TAU)


def _inproj(front, x2d, g1, wq, wk, wv, wog, wdl, wglu, wgt, wdu, bd):
    n_front = FRONT // TM_PROJ
    grid = (LP // TM_PROJ,)
    row = lambda w: pl.BlockSpec((TM_PROJ, w), lambda i: (i, 0))
    full = lambda a: pl.BlockSpec(a.shape, lambda i: (0, 0))
    return pl.pallas_call(
        _inproj_kernel,
        grid=grid,
        in_specs=[
            pl.BlockSpec((TM_PROJ, D_MODEL), lambda i: (jnp.minimum(i, n_front - 1), 0)),
            pl.BlockSpec((TM_PROJ, D_MODEL), lambda i: (jnp.maximum(i - n_front, 0), 0)),
            full(g1), full(wq), full(wk), full(wv), full(wog), full(wdl), full(wglu), full(wgt),
            full(wdu), full(bd),
        ],
        out_specs=[row(QK_W), row(QK_W), row(V_W), row(D_MODEL), row(QK_W), row(2 * D_MODEL),
                   row(2 * D_MODEL)],
        out_shape=[
            jax.ShapeDtypeStruct((LP, QK_W), BF16), jax.ShapeDtypeStruct((LP, QK_W), BF16),
            jax.ShapeDtypeStruct((LP, V_W), BF16), jax.ShapeDtypeStruct((LP, D_MODEL), BF16),
            jax.ShapeDtypeStruct((LP, QK_W), F32), jax.ShapeDtypeStruct((LP, 2 * D_MODEL), BF16),
            jax.ShapeDtypeStruct((LP, 2 * D_MODEL), BF16),
        ],
        compiler_params=pltpu.CompilerParams(dimension_semantics=("arbitrary",),
                                             vmem_limit_bytes=VMEM_LIMIT),
        name="inproj",
    )(front, x2d, g1, wq, wk, wv, wog, wdl, wglu, wgt, wdu, bd)


GLA_LEVELS = (32, 16, 8, 4, 2, 1)
LA_SPLIT = 3


def _gla_tables():
    r = np.arange(CHUNK)
    t, c = r[:, None], r[None, :]
    rows = [c <= t, c > t]
    pair = []
    for m in GLA_LEVELS:
        mid = (t // (2 * m)) * (2 * m) + m - 1
        right = (t % (2 * m)) >= m
        rows.append(np.where(right, (c > mid) & (c <= t), (c > t) & (c <= mid)))
        pair.append(((t // (2 * m)) == (c // (2 * m))) & right & ((c % (2 * m)) < m))
    pair.append(t == c)
    sums = np.concatenate(rows, axis=0).astype(np.float32)
    sums = np.concatenate([sums] * LA_SPLIT, axis=1)
    return jnp.asarray(sums, BF16), jnp.asarray(np.stack(pair).astype(np.float32))


def _split_bf16(x):
    pieces = []
    rest = x
    for _ in range(LA_SPLIT):
        p = rest.astype(BF16)
        pieces.append(p)
        rest = rest - p.astype(F32)
    return jnp.concatenate(pieces, axis=0)


def _dot_nt(a, b):
    return lax.dot_general(a, b, (((1,), (1,)), ((), ())), preferred_element_type=F32)


def _dot_tn(a, b):
    return lax.dot_general(a, b, (((0,), (0,)), ((), ())), preferred_element_type=F32)


def _gla_chunk(q, k, vb, e, dec, s_prev, pair_ref, right_rows):
    qe = (q * e[0:CHUNK]).astype(BF16)
    kd = (k * e[CHUNK:2 * CHUNK]).astype(BF16)
    s_new = s_prev * jnp.concatenate([dec] * (GLA_DV // LANES), axis=1) + _dot_tn(kd, vb)

    sc = _dot_nt(q.astype(BF16), k.astype(BF16)) * pair_ref[len(GLA_LEVELS)]
    for lvl in range(len(GLA_LEVELS)):
        rows = e[(2 + lvl) * CHUNK:(3 + lvl) * CHUNK]
        x = (jnp.where(right_rows[lvl], q, k) * rows).astype(BF16)
        sc = sc + _dot_nt(x, x) * pair_ref[lvl]
    o = _dot(jnp.concatenate([qe, sc.astype(BF16)], axis=1),
             jnp.concatenate([s_prev.astype(BF16), vb], axis=0))
    return o, s_new


def _gla_kernel(q_ref, k_ref, v_ref, la_ref, g_ref, sums_ref, pair_ref, o_ref, s_ref):
    @pl.when(pl.program_id(0) == 0)
    def _():
        s_ref[...] = jnp.zeros_like(s_ref)

    row = lax.broadcasted_iota(jnp.int32, (CHUNK, GLA_DK), 0)
    right_rows = [(row & m) != 0 for m in GLA_LEVELS]
    ones_col = jnp.ones((LA_SPLIT * CHUNK, LANES), BF16)
    sums = sums_ref[...]

    def body(c, carry):
        r0 = pl.multiple_of(c * CHUNK, CHUNK)
        rows = pl.ds(r0, CHUNK)
        states = [s_ref[h] for h in range(GLA_HEADS)]
        new_states = []
        for hp in range(GLA_HEADS // 2):
            la3 = _split_bf16(la_ref[rows, 2 * hp * GLA_DK:2 * (hp + 1) * GLA_DK])
            e2 = jnp.exp2(_dot(sums, la3))
            dec2 = jnp.exp2(_dot_tn(la3, ones_col))
            for h in (2 * hp, 2 * hp + 1):
                half = slice((h % 2) * GLA_DK, (h % 2 + 1) * GLA_DK)
                kc = slice(h * GLA_DK, (h + 1) * GLA_DK)
                vc = slice(h * GLA_DV, (h + 1) * GLA_DV)
                q = q_ref[rows, kc].astype(F32) * (GLA_DK ** -0.5)
                k = k_ref[rows, kc].astype(F32)
                o, s_new = _gla_chunk(q, k, v_ref[rows, vc], e2[:, half], dec2[half], states[h],
                                      pair_ref, right_rows)
                new_states.append(s_new)
                o = o * lax.rsqrt(jnp.mean(o * o, axis=-1, keepdims=True) + EPS) * g_ref[:, vc]
                o_ref[rows, vc] = o.astype(BF16)
        for h in range(GLA_HEADS):
            s_ref[h] = new_states[h]
        return carry

    lax.fori_loop(0, TM_GLA // CHUNK, body, 0, unroll=4)


def _gla(q, k, v, la, g):
    sums, pair = _gla_tables()
    row = lambda w: pl.BlockSpec((TM_GLA, w), lambda t: (t, 0))
    full = lambda a: pl.BlockSpec(a.shape, lambda t: (0,) * a.ndim)
    return pl.pallas_call(
        _gla_kernel,
        grid=(LP // TM_GLA,),
        in_specs=[row(QK_W), row(QK_W), row(V_W), row(QK_W), full(g), full(sums), full(pair)],
        out_specs=row(V_W),
        out_shape=jax.ShapeDtypeStruct((LP, V_W), BF16),
        scratch_shapes=[pltpu.VMEM((GLA_HEADS, GLA_DK, GLA_DV), F32)],
        compiler_params=pltpu.CompilerParams(dimension_semantics=("arbitrary",),
                                             vmem_limit_bytes=VMEM_LIMIT),
        name="gla",
    )(q, k, v, la, g, sums, pair)


def _mix_kernel(o_ref, og_ref, glu_ref, hist_ref, gt_ref, x_ref, cw_ref, cb_ref, lg_ref, lb_ref,
                wpw_ref, bg_ref, wo_ref, g2_ref, wr_ref, br_ref,
                h_o, hn_o, ri_o, rw_o, tabn_o, tabb_o, ubuf, shbuf, cbuf, seen):
    i = pl.program_id(0)

    @pl.when(i == 0)
    def _():
        seen[...] = jnp.zeros_like(seen)

    def glu(ref):
        val = ref[:, 0:D_MODEL].astype(F32)
        gate = ref[:, D_MODEL:2 * D_MODEL].astype(F32)
        return val * _sigmoid(gate)

    ubuf[0:HIST, :] = glu(hist_ref)
    ubuf[HIST:HIST + TM_MIX, :] = glu(glu_ref)

    lead = HIST - (CONV_WIDTH - 1)
    for rho in range(1, SUBLANES):
        shbuf[rho - 1] = ubuf[rho:rho + SH_ROWS, :]
    for cb in range(D_MODEL // LANES):
        lanes = slice(cb * LANES, (cb + 1) * LANES)
        for rb in range(TM_MIX // CONV_RB):
            part = jnp.broadcast_to(cb_ref[:, lanes], (CONV_RB, LANES))
            for j in range(CONV_WIDTH):
                rho = (lead + j) % SUBLANES
                r0 = rb * CONV_RB + (lead + j) - rho
                src = ubuf[r0:r0 + CONV_RB, lanes] if rho == 0 else shbuf[rho - 1, r0:r0 + CONV_RB, lanes]
                part = part + cw_ref[j:j + 1, lanes] * src
            cbuf[rb * CONV_RB:(rb + 1) * CONV_RB, lanes] = part
    acc = cbuf[...]

    mu = jnp.mean(acc, axis=-1, keepdims=True)
    xc = acc - mu
    ln = xc * lax.rsqrt(jnp.mean(xc * xc, axis=-1, keepdims=True) + EPS) * lg_ref[...] + lb_ref[...]
    branch_b = _dot(_silu(ln).astype(BF16), wpw_ref[...])

    branch_a = o_ref[...].astype(F32) * _silu(og_ref[...].astype(F32))
    g_a = _sigmoid(gt_ref[:, 0:D_MODEL].astype(F32) + bg_ref[:, 0:D_MODEL])
    g_b = _sigmoid(gt_ref[:, D_MODEL:2 * D_MODEL].astype(F32) + bg_ref[:, D_MODEL:2 * D_MODEL])
    merged = (g_a * branch_a + g_b * branch_b).astype(BF16)
    h1 = x_ref[...] + _dot(merged, wo_ref[...])
    h_o[...] = h1

    hn2 = h1 * lax.rsqrt(jnp.mean(h1 * h1, axis=-1, keepdims=True) + EPS) * g2_ref[...]
    hn_o[...] = hn2

    logits = lax.dot_general(wr_ref[...], hn2, (((1,), (1,)), ((), ())), precision=HIGHEST,
                             preferred_element_type=F32) + br_ref[...]
    row = lax.broadcasted_iota(jnp.int32, logits.shape, 0)
    rowf = row.astype(F32)
    neg = -jnp.inf
    is_g = row < N_GROUPS
    lg = jnp.where(is_g, logits, neg)
    gmax = jnp.max(lg, axis=0, keepdims=True)
    gidx = jnp.min(jnp.where(lg == gmax, rowf, float(N_GROUPS)), axis=0, keepdims=True)
    g_w = 1.0 / jnp.sum(jnp.where(is_g, jnp.exp(lg - gmax), 0.0), axis=0, keepdims=True)
    erow = rowf - float(N_GROUPS)
    egrp = ((row - N_GROUPS) >> 3).astype(F32)
    in_grp = (row >= N_GROUPS) & (row < N_GROUPS + N_EXPERTS) & (egrp == gidx)
    le = jnp.where(in_grp, logits, neg)
    m1 = jnp.max(le, axis=0, keepdims=True)
    i1 = jnp.min(jnp.where(le == m1, erow, float(N_EXPERTS)), axis=0, keepdims=True)
    le2 = jnp.where(erow == i1, neg, le)
    m2 = jnp.max(le2, axis=0, keepdims=True)
    i2 = jnp.min(jnp.where(le2 == m2, erow, float(N_EXPERTS)), axis=0, keepdims=True)
    t = jnp.exp(m2 - m1)
    w1 = g_w / (1.0 + t)
    w2 = g_w * t / (1.0 + t)

    oh1 = erow == i1
    oh2 = erow == i2
    oh = jnp.where(oh1 | oh2, 1.0, 0.0).astype(BF16)
    tr = lax.broadcasted_iota(jnp.int32, (TM_MIX, TM_MIX), 0)
    tc = lax.broadcasted_iota(jnp.int32, (TM_MIX, TM_MIX), 1)
    before_tok = _dot(oh, (tr < tc).astype(BF16))
    n_col = jnp.sum(oh.astype(F32), axis=1, keepdims=True)
    er = lax.broadcasted_iota(jnp.int32, (LANES, LANES), 0)
    ec = lax.broadcasted_iota(jnp.int32, (LANES, LANES), 1)
    before_exp = _dot((ec < er).astype(BF16), jnp.broadcast_to(n_col, logits.shape).astype(BF16))
    where_to = before_exp + before_tok
    q1 = jnp.sum(jnp.where(oh1, where_to, 0.0), axis=0, keepdims=True)
    q2 = jnp.sum(jnp.where(oh2, where_to, 0.0), axis=0, keepdims=True)

    n_rows = _dot_nt(jnp.ones((SUBLANES, TM_MIX), BF16), oh)
    tabn_o[...] = n_rows
    tabb_o[...] = seen[...]
    seen[...] = seen[...] + n_rows

    ri_o[...] = jnp.zeros_like(ri_o)
    ri_o[0:1, :] = q1.astype(jnp.int32)
    ri_o[1:2, :] = q2.astype(jnp.int32)
    rw_o[...] = jnp.zeros_like(rw_o)
    rw_o[0:1, :] = w1
    rw_o[1:2, :] = w2


def _mix(o_n, og, glu, gates, x2d, cw, cb, lg, lb, wpw, bg, wo, g2, wr, br):
    n_front = FRONT // TM_MIX
    grid = (SEQ // TM_MIX,)
    rowp = lambda w: pl.BlockSpec((TM_MIX, w), lambda i: (i + n_front, 0))
    full = lambda a: pl.BlockSpec(a.shape, lambda i: (0,) * a.ndim)
    hist_blocks = TM_MIX // HIST
    return pl.pallas_call(
        _mix_kernel,
        grid=grid,
        in_specs=[
            rowp(V_W), rowp(D_MODEL), rowp(2 * D_MODEL),
            pl.BlockSpec((HIST, 2 * D_MODEL), lambda i: ((i + n_front) * hist_blocks - 1, 0)),
            rowp(2 * D_MODEL),
            pl.BlockSpec((TM_MIX, D_MODEL), lambda i: (i, 0)),
            full(cw), full(cb), full(lg), full(lb), full(wpw), full(bg), full(wo), full(g2),
            full(wr), full(br),
        ],
        out_specs=[
            pl.BlockSpec((TM_MIX, D_MODEL), lambda i: (i, 0)),
            pl.BlockSpec((TM_MIX, D_MODEL), lambda i: (i, 0)),
            pl.BlockSpec((SUBLANES, TM_MIX), lambda i: (0, i)),
            pl.BlockSpec((SUBLANES, TM_MIX), lambda i: (0, i)),
            pl.BlockSpec((SUBLANES, LANES), lambda i: (i, 0)),
            pl.BlockSpec((SUBLANES, LANES), lambda i: (i, 0)),
        ],
        out_shape=[
            jax.ShapeDtypeStruct((SEQ, D_MODEL), F32),
            jax.ShapeDtypeStruct((SEQ, D_MODEL), F32),
            jax.ShapeDtypeStruct((SUBLANES, SEQ), jnp.int32),
            jax.ShapeDtypeStruct((SUBLANES, SEQ), F32),
            jax.ShapeDtypeStruct((N_TILES * SUBLANES, LANES), F32),
            jax.ShapeDtypeStruct((N_TILES * SUBLANES, LANES), F32),
        ],
        scratch_shapes=[pltpu.VMEM((HIST + TM_MIX, D_MODEL), F32),
                        pltpu.VMEM((SUBLANES - 1, SH_ROWS, D_MODEL), F32),
                        pltpu.VMEM((TM_MIX, D_MODEL), F32), pltpu.VMEM((SUBLANES, LANES), F32)],
        compiler_params=pltpu.CompilerParams(dimension_semantics=("arbitrary",),
                                             vmem_limit_bytes=VMEM_LIMIT),
        name="mix",
    )(o_n, og, glu, glu, gates, x2d, cw, cb, lg, lb, wpw, bg, wo, g2, wr, br)


def _tok_rows(start_tok, n_tok):
    return pl.ds(pl.multiple_of(start_tok * ROW_SLABS, ROW_SLABS), n_tok * ROW_SLABS)


def _start_tile_runs(n_ref, row_ref, tile, make_copy):
    def run(e, off):
        n = n_ref[tile * N_EXPERTS + e]

        @pl.when(n > 0)
        def _():
            make_copy(_tok_rows(off, n), _tok_rows(row_ref[tile * N_EXPERTS + e], n)).start()

        return off + n

    lax.fori_loop(0, N_EXPERTS, run, 0)


def _dispatch_kernel(n_ref, row_ref, ts_ref, tl_ref, nused_ref, hn_ref, q_ref, xb_ref,
                     sb0, sb1, zbuf, sem, fsem):
    i = pl.program_id(0)
    last = pl.num_programs(0) - 1

    def fills(start):
        def tail(e, c):
            n = tl_ref[e]

            @pl.when(n > 0)
            def _():
                cp = pltpu.make_async_copy(zbuf.at[pl.ds(0, n * ROW_SLABS)],
                                           xb_ref.at[_tok_rows(ts_ref[e], n)], fsem)
                cp.start() if start else cp.wait()

            return c

        lax.fori_loop(0, N_EXPERTS, tail, 0)

        def block(b, c):
            cp = pltpu.make_async_copy(zbuf, xb_ref.at[_tok_rows(b * BM, BM)], fsem)
            cp.start() if start else cp.wait()
            return c

        lax.fori_loop(nused_ref[0], N_BLOCKS, block, 0)

    @pl.when(i == 0)
    def _():
        zbuf[...] = jnp.zeros_like(zbuf)
        fills(start=True)

    slot_i = lax.broadcasted_iota(jnp.int32, (SLOTS, TM_MIX), 0)
    onehot = jnp.where((slot_i == q_ref[0:1, :]) | (slot_i == q_ref[1:2, :]), 1.0, 0.0).astype(BF16)
    srt = _dot(onehot, hn_ref[...].astype(BF16))

    def whole(s, sbuf):
        return pltpu.make_async_copy(sbuf, xb_ref.at[pl.ds(0, SLOTS * ROW_SLABS)], sem.at[s])

    def step(s, sbuf):
        @pl.when(i >= 2)
        def _():
            whole(s, sbuf).wait()

        _store_token_tiles(sbuf, srt, SLOTS)
        _start_tile_runs(n_ref, row_ref, i,
                         lambda loc, glob: pltpu.make_async_copy(sbuf.at[loc], xb_ref.at[glob], sem.at[s]))

    for s, sbuf in enumerate((sb0, sb1)):
        pl.when(i % 2 == s)(functools.partial(step, s, sbuf))

    @pl.when(i == last)
    def _():
        for s, sbuf in enumerate((sb0, sb1)):
            whole(s, sbuf).wait()
        fills(start=False)


def _dispatch(n_flat, run_rows, tail_start, tail_len, nused, hn2, ri):
    assert N_TILES >= 2
    grid_spec = pltpu.PrefetchScalarGridSpec(
        num_scalar_prefetch=5,
        grid=(N_TILES,),
        in_specs=[pl.BlockSpec((TM_MIX, D_MODEL), lambda i, *_: (i, 0)),
                  pl.BlockSpec((SUBLANES, TM_MIX), lambda i, *_: (0, i))],
        out_specs=pl.BlockSpec(memory_space=pl.ANY),
        scratch_shapes=[pltpu.VMEM((SLOTS * ROW_SLABS, LANES), F32), pltpu.VMEM((SLOTS * ROW_SLABS, LANES), F32),
                        pltpu.VMEM((BM * ROW_SLABS, LANES), F32),
                        pltpu.SemaphoreType.DMA((2,)), pltpu.SemaphoreType.DMA(())],
    )
    return pl.pallas_call(
        _dispatch_kernel,
        grid_spec=grid_spec,
        out_shape=jax.ShapeDtypeStruct((P_ROWS * ROW_SLABS, LANES), F32),
        compiler_params=pltpu.CompilerParams(dimension_semantics=("arbitrary",),
                                             vmem_limit_bytes=VMEM_LIMIT),
        name="dispatch",
    )(n_flat, run_rows, tail_start, tail_len, nused, hn2, ri)


def _experts_kernel(be_ref, first_ref, next_ref, nused_ref, x_ref, wg_hbm, wu_hbm, wd_hbm, y_ref,
                    wg_f, wu_f, wd_f, wg_b, wu_b, wd_b, sem):
    b = pl.program_id(0)

    def weight_copies(e):
        return (pltpu.make_async_copy(wg_hbm.at[e], wg_f, sem.at[0]),
                pltpu.make_async_copy(wu_hbm.at[e], wu_f, sem.at[1]),
                pltpu.make_async_copy(wd_hbm.at[e], wd_f, sem.at[2]))

    @pl.when(b == 0)
    def _():
        for c in weight_copies(be_ref[0]):
            c.start()

    @pl.when(b < nused_ref[0])
    def _():
        @pl.when(first_ref[b] == 1)
        def _():
            for c in weight_copies(be_ref[b]):
                c.wait()
            wg_b[...] = wg_f[...].astype(BF16)
            wu_b[...] = wu_f[...].astype(BF16)
            wd_b[...] = wd_f[...].astype(BF16)

            @pl.when(next_ref[b] >= 0)
            def _():
                for c in weight_copies(next_ref[b]):
                    c.start()

        x = _load_token_tiles(x_ref, BM).astype(BF16)
        a = _dot(x, wg_b[...])
        u = _dot(x, wu_b[...])
        y = _dot((_silu(a) * u).astype(BF16), wd_b[...])
        _store_token_tiles(y_ref, y, BM)

    @pl.when(b >= nused_ref[0])
    def _():
        y_ref[...] = jnp.zeros_like(y_ref)


def _experts(block_e, first, next_e, nused, xb, wg, wu, wd):
    def xmap(b, be, first, nxt, nused):
        return (b, 0)

    grid_spec = pltpu.PrefetchScalarGridSpec(
        num_scalar_prefetch=4,
        grid=(N_BLOCKS,),
        in_specs=[
            pl.BlockSpec((BM * ROW_SLABS, LANES), xmap),
            pl.BlockSpec(memory_space=pl.ANY),
            pl.BlockSpec(memory_space=pl.ANY),
            pl.BlockSpec(memory_space=pl.ANY),
        ],
        out_specs=pl.BlockSpec((BM * ROW_SLABS, LANES), xmap),
        scratch_shapes=[pltpu.VMEM((D_MODEL, D_EXPERT), F32), pltpu.VMEM((D_MODEL, D_EXPERT), F32),
                        pltpu.VMEM((D_EXPERT, D_MODEL), F32),
                        pltpu.VMEM((D_MODEL, D_EXPERT), BF16), pltpu.VMEM((D_MODEL, D_EXPERT), BF16),
                        pltpu.VMEM((D_EXPERT, D_MODEL), BF16), pltpu.SemaphoreType.DMA((3,))],
    )
    return pl.pallas_call(
        _experts_kernel,
        grid_spec=grid_spec,
        out_shape=jax.ShapeDtypeStruct((P_ROWS * ROW_SLABS, LANES), F32),
        compiler_params=pltpu.CompilerParams(dimension_semantics=("arbitrary",),
                                             vmem_limit_bytes=VMEM_LIMIT),
        name="experts",
    )(block_e, first, next_e, nused, xb, wg, wu, wd)


def _combine_kernel(n_ref, row_ref, yb_ref, h_ref, q_ref, w_ref, g_ref, out_ref, yb0, yb1, sem):
    i = pl.program_id(0)
    bufs = (yb0, yb1)

    def start_gather(tile, s):
        _start_tile_runs(n_ref, row_ref, tile,
                         lambda loc, glob: pltpu.make_async_copy(yb_ref.at[glob], bufs[s].at[loc], sem.at[s]))

    @pl.when(i == 0)
    def _():
        start_gather(0, 0)

    def combine(s):
        @pl.when(i + 1 < pl.num_programs(0))
        def _():
            start_gather(i + 1, 1 - s)

        pltpu.make_async_copy(yb_ref.at[pl.ds(0, SLOTS * ROW_SLABS)], bufs[s], sem.at[s]).wait()
        ys = _load_token_tiles(bufs[s], SLOTS).astype(BF16)
        col = lax.broadcasted_iota(jnp.int32, (TM_MIX, SLOTS), 1)
        wmat = (jnp.where(col == q_ref[:, 0:1], w_ref[:, 0:1], 0.0)
                + jnp.where(col == q_ref[:, 1:2], w_ref[:, 1:2], 0.0)).astype(BF16)
        hh = h_ref[...] + _dot(wmat, ys)
        out_ref[...] = hh * lax.rsqrt(jnp.mean(hh * hh, axis=-1, keepdims=True) + EPS) * g_ref[...]

    for s in range(2):
        pl.when(i % 2 == s)(functools.partial(combine, s))


def _combine(n_flat, run_rows, yb, h1, qt, wts, gf):
    grid_spec = pltpu.PrefetchScalarGridSpec(
        num_scalar_prefetch=2,
        grid=(N_TILES,),
        in_specs=[
            pl.BlockSpec(memory_space=pl.ANY),
            pl.BlockSpec((TM_MIX, D_MODEL), lambda i, *_: (i, 0)),
            pl.BlockSpec((TM_MIX, TOP_K), lambda i, *_: (i, 0)),
            pl.BlockSpec((TM_MIX, TOP_K), lambda i, *_: (i, 0)),
            pl.BlockSpec((1, D_MODEL), lambda i, *_: (0, 0)),
        ],
        out_specs=pl.BlockSpec((TM_MIX, D_MODEL), lambda i, *_: (i, 0)),
        scratch_shapes=[pltpu.VMEM((SLOTS * ROW_SLABS, LANES), F32), pltpu.VMEM((SLOTS * ROW_SLABS, LANES), F32),
                        pltpu.SemaphoreType.DMA((2,))],
    )
    return pl.pallas_call(
        _combine_kernel,
        grid_spec=grid_spec,
        out_shape=jax.ShapeDtypeStruct((SEQ, D_MODEL), F32),
        compiler_params=pltpu.CompilerParams(dimension_semantics=("arbitrary",),
                                             vmem_limit_bytes=VMEM_LIMIT),
        name="combine",
    )(n_flat, run_rows, yb, h1, qt, wts, gf)


def kernel(x, meta, norm1_g, w_in, w_decay_up, b_decay, gla_norm_g, conv_w, conv_b, conv_ln_g,
           conv_ln_b, w_pw2, b_gate, w_out, norm2_g, w_router_group, b_router_group,
           w_router_expert, b_router_expert, w_exp_gate, w_exp_up, w_exp_down, final_norm_g):
    assert x.shape == (1, SEQ, D_MODEL) and w_in.shape[0] == 1
    x2d = x[0]
    front = jnp.concatenate([jnp.zeros((FRONT - N_META, D_MODEL), F32), meta.astype(F32)], axis=0)

    w = w_in[0]
    o_q, o_k, o_v, o_og, o_dl, o_glu, o_gt = (0, QK_W, 2 * QK_W, 2 * QK_W + V_W, 2 * QK_W + 2 * V_W,
                                              2 * QK_W + 2 * V_W + GLA_RANK,
                                              2 * QK_W + 2 * V_W + GLA_RANK + 2 * D_MODEL)
    cols = lambda a, b: w[:, a:b].astype(BF16)
    row2 = lambda a: a.reshape(1, -1).astype(F32)
    q, k, v, og, la, glu, gates = _inproj(
        front, x2d, row2(norm1_g[0]), cols(o_q, o_k), cols(o_k, o_v), cols(o_v, o_og),
        cols(o_og, o_dl), cols(o_dl, o_glu), cols(o_glu, o_gt), cols(o_gt, w.shape[1]),
        w_decay_up[0].astype(F32), row2(b_decay[0]))

    o_n = _gla(q, k, v, la, row2(gla_norm_g[0]))

    wr = jnp.concatenate([w_router_group[0].T, w_router_expert[0].T,
                          jnp.zeros((LANES - N_GROUPS - N_EXPERTS, D_MODEL), F32)], axis=0).astype(F32)
    br = jnp.concatenate([b_router_group[0], b_router_expert[0],
                          jnp.zeros((LANES - N_GROUPS - N_EXPERTS,), F32)]).reshape(LANES, 1).astype(F32)
    h1, hn2, ri, rw, tabn, tabb = _mix(
        o_n, og, glu, gates, x2d, conv_w[0].astype(F32), row2(conv_b[0]), row2(conv_ln_g[0]),
        row2(conv_ln_b[0]), w_pw2[0].astype(BF16), row2(b_gate[0]), w_out[0].astype(BF16),
        row2(norm2_g[0]), wr, br)

    experts = slice(N_GROUPS, N_GROUPS + N_EXPERTS)
    n_te = tabn[::SUBLANES, experts].astype(jnp.int32)
    seen_te = tabb[::SUBLANES, experts].astype(jnp.int32)
    counts = seen_te[-1] + n_te[-1]
    padded = (counts + BM - 1) // BM * BM
    pad_end = jnp.cumsum(padded)
    pad_start = pad_end - padded
    n_flat = n_te.reshape(-1)
    run_rows = (pad_start[None, :] + seen_te).reshape(-1).astype(jnp.int32)
    tail_start = (pad_start + counts).astype(jnp.int32)
    tail_len = (padded - counts).astype(jnp.int32)
    blk = jnp.arange(N_BLOCKS, dtype=jnp.int32)
    block_e = jnp.minimum(jnp.sum((pad_end[None, :] <= blk[:, None] * BM).astype(jnp.int32), axis=1),
                          N_EXPERTS - 1).astype(jnp.int32)
    first = jnp.concatenate([jnp.ones((1,), jnp.int32), (block_e[1:] != block_e[:-1]).astype(jnp.int32)])
    nused = (pad_end[-1:] // BM).astype(jnp.int32)
    eid = jnp.arange(N_EXPERTS, dtype=jnp.int32)
    later = jnp.flip(lax.cummin(jnp.flip(jnp.where(counts > 0, eid, N_EXPERTS))))
    nxt = jnp.concatenate([later[1:], jnp.full((1,), N_EXPERTS, jnp.int32)])
    next_e = jnp.where(nxt < N_EXPERTS, nxt, -1)[block_e].astype(jnp.int32)

    xb = _dispatch(n_flat, run_rows, tail_start, tail_len, nused, hn2, ri)
    yb = _experts(block_e, first, next_e, nused, xb, w_exp_gate[0], w_exp_up[0], w_exp_down[0])
    out = _combine(n_flat, run_rows, yb, h1, ri[0:TOP_K].T, rw[0:TOP_K].T, row2(final_norm_g))
    return out[None]
```

```python
import functools

import jax
import jax.numpy as jnp
import numpy as np
from jax import lax
from jax.experimental import pallas as pl
from jax.experimental.pallas import tpu as pltpu

F32 = jnp.float32
BF16 = jnp.bfloat16
HIGHEST = lax.Precision.HIGHEST

D_MODEL = 1024
SEQ = 16384
N_META = 16
GLA_HEADS = 4
GLA_DK = 128
GLA_DV = 256
GLA_RANK = 16
GLA_TAU = 16.0
CHUNK = 64
QK_W = GLA_HEADS * GLA_DK
V_W = GLA_HEADS * GLA_DV
CONV_WIDTH = 31
N_GROUPS = 8
EXPERTS_PER_GROUP = 8
N_EXPERTS = N_GROUPS * EXPERTS_PER_GROUP
TOP_K = 2
D_EXPERT = 512
EPS = 1e-6
LOG2E = 1.4426950408889634

LANES = 128
SUBLANES = 8
ROW_SLABS = D_MODEL // LANES

FRONT = 512
LP = FRONT + SEQ
TM_PROJ = 512
TM_GLA = 512
TM_MIX = 256
HIST = 32
SH_ROWS = TM_MIX + HIST - SUBLANES
CONV_RB = 64
N_TILES = SEQ // TM_MIX
SLOTS = TOP_K * TM_MIX
BM = 256
N_BLOCKS = (SEQ * TOP_K) // BM + N_EXPERTS
P_ROWS = N_BLOCKS * BM
VMEM_LIMIT = 56 * 1024 * 1024


def _sigmoid(x):
    return 0.5 * jnp.tanh(0.5 * x) + 0.5


def _silu(x):
    return x * _sigmoid(x)


def _dot(a, b, **kw):
    return jnp.dot(a, b, preferred_element_type=F32, **kw)


def _load_token_tiles(ref, n_tok):
    return jnp.concatenate([ref[pl.ds(s, n_tok, stride=ROW_SLABS), :] for s in range(ROW_SLABS)], axis=-1)


def _store_token_tiles(ref, val, n_tok):
    for s in range(ROW_SLABS):
        ref[pl.ds(s, n_tok, stride=ROW_SLABS), :] = val[:, s * LANES:(s + 1) * LANES]


def _inproj_kernel(front_ref, x_ref, g_ref, wq, wk, wv, wog, wdl, wglu, wgt, wdu, bd, bg,
                   q_o, k_o, v_o, og_o, la_o, u_o, gt_o):
    i = pl.program_id(0)
    h = jnp.where(i < FRONT // TM_PROJ, front_ref[...], x_ref[...])
    ms = jnp.mean(h * h, axis=-1, keepdims=True)
    hn = (h * lax.rsqrt(ms + EPS) * g_ref[...]).astype(BF16)
    q_o[...] = _dot(hn, wq[...]).astype(BF16)
    k_o[...] = _dot(hn, wk[...]).astype(BF16)
    v_o[...] = _dot(hn, wv[...]).astype(BF16)
    og_o[...] = _silu(_dot(hn, wog[...])).astype(BF16)
    glu = _dot(hn, wglu[...])
    u_o[...] = (glu[:, 0:D_MODEL] * _sigmoid(glu[:, D_MODEL:2 * D_MODEL])).astype(BF16)
    gt_o[...] = _sigmoid(_dot(hn, wgt[...]) + bg[...]).astype(BF16)
    dlow = _dot(hn, wdl[...])
    z = _dot(dlow, wdu[...], precision=HIGHEST) + bd[...]
    la_o[...] = (jnp.minimum(z, 0.0) - jnp.log(1.0 + jnp.exp(-jnp.abs(z)))) * (LOG2E / GLA_TAU)


def _inproj(front, x2d, g1, wq, wk, wv, wog, wdl, wglu, wgt, wdu, bd, bg):
    n_front = FRONT // TM_PROJ
    grid = (LP // TM_PROJ,)
    row = lambda w: pl.BlockSpec((TM_PROJ, w), lambda i: (i, 0))
    full = lambda a: pl.BlockSpec(a.shape, lambda i: (0, 0))
    return pl.pallas_call(
        _inproj_kernel,
        grid=grid,
        in_specs=[
            pl.BlockSpec((TM_PROJ, D_MODEL), lambda i: (jnp.minimum(i, n_front - 1), 0)),
            pl.BlockSpec((TM_PROJ, D_MODEL), lambda i: (jnp.maximum(i - n_front, 0), 0)),
            full(g1), full(wq), full(wk), full(wv), full(wog), full(wdl), full(wglu), full(wgt),
            full(wdu), full(bd), full(bg),
        ],
        out_specs=[row(QK_W), row(QK_W), row(V_W), row(D_MODEL), row(QK_W), row(D_MODEL),
                   row(2 * D_MODEL)],
        out_shape=[
            jax.ShapeDtypeStruct((LP, QK_W), BF16), jax.ShapeDtypeStruct((LP, QK_W), BF16),
            jax.ShapeDtypeStruct((LP, V_W), BF16), jax.ShapeDtypeStruct((LP, D_MODEL), BF16),
            jax.ShapeDtypeStruct((LP, QK_W), F32), jax.ShapeDtypeStruct((LP, D_MODEL), BF16),
            jax.ShapeDtypeStruct((LP, 2 * D_MODEL), BF16),
        ],
        compiler_params=pltpu.CompilerParams(dimension_semantics=("arbitrary",),
                                             vmem_limit_bytes=VMEM_LIMIT),
        name="inproj",
    )(front, x2d, g1, wq, wk, wv, wog, wdl, wglu, wgt, wdu, bd, bg)


GLA_LEVELS = (32, 16, 8, 4, 2, 1)
LA_SPLIT = 3


def _gla_tables():
    r = np.arange(CHUNK)
    t, c = r[:, None], r[None, :]
    rows = [c <= t, c > t]
    pair = []
    for m in GLA_LEVELS:
        mid = (t // (2 * m)) * (2 * m) + m - 1
        right = (t % (2 * m)) >= m
        rows.append(np.where(right, (c > mid) & (c <= t), (c > t) & (c <= mid)))
        pair.append(((t // (2 * m)) == (c // (2 * m))) & right & ((c % (2 * m)) < m))
    pair.append(t == c)
    sums = np.concatenate(rows, axis=0).astype(np.float32)
    sums = np.concatenate([sums] * LA_SPLIT, axis=1)
    return jnp.asarray(sums, BF16), jnp.asarray(np.stack(pair).astype(np.float32))


def _split_bf16(x):
    pieces = []
    rest = x
    for _ in range(LA_SPLIT):
        p = rest.astype(BF16)
        pieces.append(p)
        rest = rest - p.astype(F32)
    return jnp.concatenate(pieces, axis=0)


def _dot_nt(a, b):
    return lax.dot_general(a, b, (((1,), (1,)), ((), ())), preferred_element_type=F32)


def _dot_tn(a, b):
    return lax.dot_general(a, b, (((0,), (0,)), ((), ())), preferred_element_type=F32)


def _gla_chunk(q, k, vb, e, dec, s_prev, pair_ref, right_rows):
    qe = (q * e[0:CHUNK]).astype(BF16)
    kd = (k * e[CHUNK:2 * CHUNK]).astype(BF16)
    s_new = s_prev * jnp.concatenate([dec] * (GLA_DV // LANES), axis=1) + _dot_tn(kd, vb)

    sc = _dot_nt(q.astype(BF16), k.astype(BF16)) * pair_ref[len(GLA_LEVELS)]
    for lvl in range(len(GLA_LEVELS)):
        rows = e[(2 + lvl) * CHUNK:(3 + lvl) * CHUNK]
        x = (jnp.where(right_rows[lvl], q, k) * rows).astype(BF16)
        sc = sc + _dot_nt(x, x) * pair_ref[lvl]
    o = _dot(jnp.concatenate([qe, sc.astype(BF16)], axis=1),
             jnp.concatenate([s_prev.astype(BF16), vb], axis=0))
    return o, s_new


def _gla_kernel(q_ref, k_ref, v_ref, la_ref, g_ref, sums_ref, pair_ref, o_ref, s_ref):
    @pl.when(pl.program_id(0) == 0)
    def _():
        s_ref[...] = jnp.zeros_like(s_ref)

    row = lax.broadcasted_iota(jnp.int32, (CHUNK, GLA_DK), 0)
    right_rows = [(row & m) != 0 for m in GLA_LEVELS]
    ones_col = jnp.ones((LA_SPLIT * CHUNK, LANES), BF16)
    sums = sums_ref[...]

    def body(c, carry):
        r0 = pl.multiple_of(c * CHUNK, CHUNK)
        rows = pl.ds(r0, CHUNK)
        states = [s_ref[h] for h in range(GLA_HEADS)]
        new_states = []
        for hp in range(GLA_HEADS // 2):
            la3 = _split_bf16(la_ref[rows, 2 * hp * GLA_DK:2 * (hp + 1) * GLA_DK])
            e2 = jnp.exp2(_dot(sums, la3))
            dec2 = jnp.exp2(_dot_tn(la3, ones_col))
            for h in (2 * hp, 2 * hp + 1):
                half = slice((h % 2) * GLA_DK, (h % 2 + 1) * GLA_DK)
                kc = slice(h * GLA_DK, (h + 1) * GLA_DK)
                vc = slice(h * GLA_DV, (h + 1) * GLA_DV)
                q = q_ref[rows, kc].astype(F32) * (GLA_DK ** -0.5)
                k = k_ref[rows, kc].astype(F32)
                o, s_new = _gla_chunk(q, k, v_ref[rows, vc], e2[:, half], dec2[half], states[h],
                                      pair_ref, right_rows)
                new_states.append(s_new)
                o = o * lax.rsqrt(jnp.mean(o * o, axis=-1, keepdims=True) + EPS) * g_ref[:, vc]
                o_ref[rows, vc] = o.astype(BF16)
        for h in range(GLA_HEADS):
            s_ref[h] = new_states[h]
        return carry

    lax.fori_loop(0, TM_GLA // CHUNK, body, 0, unroll=4)


def _gla(q, k, v, la, g):
    sums, pair = _gla_tables()
    row = lambda w: pl.BlockSpec((TM_GLA, w), lambda t: (t, 0))
    full = lambda a: pl.BlockSpec(a.shape, lambda t: (0,) * a.ndim)
    return pl.pallas_call(
        _gla_kernel,
        grid=(LP // TM_GLA,),
        in_specs=[row(QK_W), row(QK_W), row(V_W), row(QK_W), full(g), full(sums), full(pair)],
        out_specs=row(V_W),
        out_shape=jax.ShapeDtypeStruct((LP, V_W), BF16),
        scratch_shapes=[pltpu.VMEM((GLA_HEADS, GLA_DK, GLA_DV), F32)],
        compiler_params=pltpu.CompilerParams(dimension_semantics=("arbitrary",),
                                             vmem_limit_bytes=VMEM_LIMIT),
        name="gla",
    )(q, k, v, la, g, sums, pair)


def _mix_kernel(o_ref, og_ref, u_ref, hist_ref, gt_ref, x_ref, cw_ref, cb_ref, lg_ref, lb_ref,
                wpw_ref, wo_ref, g2_ref, wrh_ref, wrl_ref, br_ref,
                h_o, hn_o, ri_o, rw_o, tabn_o, tabb_o, ubuf, shbuf, cbuf, seen):
    i = pl.program_id(0)

    @pl.when(i == 0)
    def _():
        seen[...] = jnp.zeros_like(seen)

    ubuf[0:HIST, :] = hist_ref[...].astype(F32)
    ubuf[HIST:HIST + TM_MIX, :] = u_ref[...].astype(F32)

    lead = HIST - (CONV_WIDTH - 1)
    for rho in range(1, SUBLANES):
        shbuf[rho - 1] = ubuf[rho:rho + SH_ROWS, :]
    for cb in range(D_MODEL // LANES):
        lanes = slice(cb * LANES, (cb + 1) * LANES)
        for rb in range(TM_MIX // CONV_RB):
            part = jnp.broadcast_to(cb_ref[:, lanes], (CONV_RB, LANES))
            for j in range(CONV_WIDTH):
                rho = (lead + j) % SUBLANES
                r0 = rb * CONV_RB + (lead + j) - rho
                src = ubuf[r0:r0 + CONV_RB, lanes] if rho == 0 else shbuf[rho - 1, r0:r0 + CONV_RB, lanes]
                part = part + cw_ref[j:j + 1, lanes] * src
            cbuf[rb * CONV_RB:(rb + 1) * CONV_RB, lanes] = part
    acc = cbuf[...]

    mu = jnp.mean(acc, axis=-1, keepdims=True)
    xc = acc - mu
    ln = xc * lax.rsqrt(jnp.mean(xc * xc, axis=-1, keepdims=True) + EPS) * lg_ref[...] + lb_ref[...]
    branch_b = _dot(_silu(ln).astype(BF16), wpw_ref[...])

    branch_a = o_ref[...].astype(F32) * og_ref[...].astype(F32)
    g_a = gt_ref[:, 0:D_MODEL].astype(F32)
    g_b = gt_ref[:, D_MODEL:2 * D_MODEL].astype(F32)
    merged = (g_a * branch_a + g_b * branch_b).astype(BF16)
    h1 = x_ref[...] + _dot(merged, wo_ref[...])
    h_o[...] = h1

    hn2 = h1 * lax.rsqrt(jnp.mean(h1 * h1, axis=-1, keepdims=True) + EPS) * g2_ref[...]
    hn_hi = hn2.astype(BF16)
    hn_o[...] = hn_hi

    hn_lo = (hn2 - hn_hi.astype(F32)).astype(BF16)
    logits = (_dot_nt(wrh_ref[...], hn_hi) + _dot_nt(wrh_ref[...], hn_lo) + _dot_nt(wrl_ref[...], hn_hi)
              + br_ref[...])
    row = lax.broadcasted_iota(jnp.int32, logits.shape, 0)
    rowf = row.astype(F32)
    neg = -jnp.inf
    is_g = row < N_GROUPS
    lg = jnp.where(is_g, logits, neg)
    gmax = jnp.max(lg, axis=0, keepdims=True)
    gidx = jnp.min(jnp.where(lg == gmax, rowf, float(N_GROUPS)), axis=0, keepdims=True)
    g_w = 1.0 / jnp.sum(jnp.where(is_g, jnp.exp(lg - gmax), 0.0), axis=0, keepdims=True)
    erow = rowf - float(N_GROUPS)
    egrp = ((row - N_GROUPS) >> 3).astype(F32)
    in_grp = (row >= N_GROUPS) & (row < N_GROUPS + N_EXPERTS) & (egrp == gidx)
    le = jnp.where(in_grp, logits, neg)
    m1 = jnp.max(le, axis=0, keepdims=True)
    i1 = jnp.min(jnp.where(le == m1, erow, float(N_EXPERTS)), axis=0, keepdims=True)
    le2 = jnp.where(erow == i1, neg, le)
    m2 = jnp.max(le2, axis=0, keepdims=True)
    i2 = jnp.min(jnp.where(le2 == m2, erow, float(N_EXPERTS)), axis=0, keepdims=True)
    t = jnp.exp(m2 - m1)
    w1 = g_w / (1.0 + t)
    w2 = g_w * t / (1.0 + t)

    oh1 = erow == i1
    oh2 = erow == i2
    oh = jnp.where(oh1 | oh2, 1.0, 0.0).astype(BF16)
    tr = lax.broadcasted_iota(jnp.int32, (TM_MIX, TM_MIX), 0)
    tc = lax.broadcasted_iota(jnp.int32, (TM_MIX, TM_MIX), 1)
    before_tok = _dot(oh, (tr < tc).astype(BF16))
    n_col = jnp.sum(oh.astype(F32), axis=1, keepdims=True)
    er = lax.broadcasted_iota(jnp.int32, (LANES, LANES), 0)
    ec = lax.broadcasted_iota(jnp.int32, (LANES, LANES), 1)
    before_exp = _dot((ec < er).astype(BF16), jnp.broadcast_to(n_col, logits.shape).astype(BF16))
    where_to = before_exp + before_tok
    q1 = jnp.sum(jnp.where(oh1, where_to, 0.0), axis=0, keepdims=True)
    q2 = jnp.sum(jnp.where(oh2, where_to, 0.0), axis=0, keepdims=True)

    n_rows = _dot_nt(jnp.ones((SUBLANES, TM_MIX), BF16), oh)
    tabn_o[...] = n_rows
    tabb_o[...] = seen[...]
    seen[...] = seen[...] + n_rows

    ri_o[...] = jnp.zeros_like(ri_o)
    ri_o[0:1, :] = q1.astype(jnp.int32)
    ri_o[1:2, :] = q2.astype(jnp.int32)
    rw_o[...] = jnp.zeros_like(rw_o)
    rw_o[0:1, :] = w1
    rw_o[1:2, :] = w2


def _mix(o_n, og, u, gates, x2d, cw, cb, lg, lb, wpw, wo, g2, wrh, wrl, br):
    n_front = FRONT // TM_MIX
    grid = (SEQ // TM_MIX,)
    rowp = lambda w: pl.BlockSpec((TM_MIX, w), lambda i: (i + n_front, 0))
    full = lambda a: pl.BlockSpec(a.shape, lambda i: (0,) * a.ndim)
    hist_blocks = TM_MIX // HIST
    return pl.pallas_call(
        _mix_kernel,
        grid=grid,
        in_specs=[
            rowp(V_W), rowp(D_MODEL), rowp(D_MODEL),
            pl.BlockSpec((HIST, D_MODEL), lambda i: ((i + n_front) * hist_blocks - 1, 0)),
            rowp(2 * D_MODEL),
            pl.BlockSpec((TM_MIX, D_MODEL), lambda i: (i, 0)),
            full(cw), full(cb), full(lg), full(lb), full(wpw), full(wo), full(g2),
            full(wrh), full(wrl), full(br),
        ],
        out_specs=[
            pl.BlockSpec((TM_MIX, D_MODEL), lambda i: (i, 0)),
            pl.BlockSpec((TM_MIX, D_MODEL), lambda i: (i, 0)),
            pl.BlockSpec((SUBLANES, TM_MIX), lambda i: (0, i)),
            pl.BlockSpec((SUBLANES, TM_MIX), lambda i: (0, i)),
            pl.BlockSpec((SUBLANES, LANES), lambda i: (i, 0)),
            pl.BlockSpec((SUBLANES, LANES), lambda i: (i, 0)),
        ],
        out_shape=[
            jax.ShapeDtypeStruct((SEQ, D_MODEL), F32),
            jax.ShapeDtypeStruct((SEQ, D_MODEL), BF16),
            jax.ShapeDtypeStruct((SUBLANES, SEQ), jnp.int32),
            jax.ShapeDtypeStruct((SUBLANES, SEQ), F32),
            jax.ShapeDtypeStruct((N_TILES * SUBLANES, LANES), F32),
            jax.ShapeDtypeStruct((N_TILES * SUBLANES, LANES), F32),
        ],
        scratch_shapes=[pltpu.VMEM((HIST + TM_MIX, D_MODEL), F32),
                        pltpu.VMEM((SUBLANES - 1, SH_ROWS, D_MODEL), F32),
                        pltpu.VMEM((TM_MIX, D_MODEL), F32), pltpu.VMEM((SUBLANES, LANES), F32)],
        compiler_params=pltpu.CompilerParams(dimension_semantics=("arbitrary",),
                                             vmem_limit_bytes=VMEM_LIMIT),
        name="mix",
    )(o_n, og, u, u, gates, x2d, cw, cb, lg, lb, wpw, wo, g2, wrh, wrl, br)


def _tok_rows(start_tok, n_tok):
    return pl.ds(pl.multiple_of(start_tok * ROW_SLABS, ROW_SLABS), n_tok * ROW_SLABS)


def _start_tile_runs(n_ref, row_ref, tile, make_copy):
    def run(e, off):
        n = n_ref[tile * N_EXPERTS + e]

        @pl.when(n > 0)
        def _():
            make_copy(_tok_rows(off, n), _tok_rows(row_ref[tile * N_EXPERTS + e], n)).start()

        return off + n

    lax.fori_loop(0, N_EXPERTS, run, 0)


def _dispatch_kernel(n_ref, row_ref, ts_ref, tl_ref, nused_ref, hn_ref, q_ref, xb_ref,
                     sb0, sb1, zbuf, sem, fsem):
    i = pl.program_id(0)
    last = pl.num_programs(0) - 1

    def fills(start):
        def tail(e, c):
            n = tl_ref[e]

            @pl.when(n > 0)
            def _():
                cp = pltpu.make_async_copy(zbuf.at[pl.ds(0, n * ROW_SLABS)],
                                           xb_ref.at[_tok_rows(ts_ref[e], n)], fsem)
                cp.start() if start else cp.wait()

            return c

        lax.fori_loop(0, N_EXPERTS, tail, 0)

        def block(b, c):
            cp = pltpu.make_async_copy(zbuf, xb_ref.at[_tok_rows(b * BM, BM)], fsem)
            cp.start() if start else cp.wait()
            return c

        lax.fori_loop(nused_ref[0], N_BLOCKS, block, 0)

    @pl.when(i == 0)
    def _():
        zbuf[...] = jnp.zeros_like(zbuf)
        fills(start=True)

    slot_i = lax.broadcasted_iota(jnp.int32, (SLOTS, TM_MIX), 0)
    onehot = jnp.where((slot_i == q_ref[0:1, :]) | (slot_i == q_ref[1:2, :]), 1.0, 0.0).astype(BF16)
    srt = _dot(onehot, hn_ref[...])

    def whole(s, sbuf):
        return pltpu.make_async_copy(sbuf, xb_ref.at[pl.ds(0, SLOTS * ROW_SLABS)], sem.at[s])

    def step(s, sbuf):
        @pl.when(i >= 2)
        def _():
            whole(s, sbuf).wait()

        _store_token_tiles(sbuf, srt, SLOTS)
        _start_tile_runs(n_ref, row_ref, i,
                         lambda loc, glob: pltpu.make_async_copy(sbuf.at[loc], xb_ref.at[glob], sem.at[s]))

    for s, sbuf in enumerate((sb0, sb1)):
        pl.when(i % 2 == s)(functools.partial(step, s, sbuf))

    @pl.when(i == last)
    def _():
        for s, sbuf in enumerate((sb0, sb1)):
            whole(s, sbuf).wait()
        fills(start=False)


def _dispatch(n_flat, run_rows, tail_start, tail_len, nused, hn2, ri):
    assert N_TILES >= 2
    grid_spec = pltpu.PrefetchScalarGridSpec(
        num_scalar_prefetch=5,
        grid=(N_TILES,),
        in_specs=[pl.BlockSpec((TM_MIX, D_MODEL), lambda i, *_: (i, 0)),
                  pl.BlockSpec((SUBLANES, TM_MIX), lambda i, *_: (0, i))],
        out_specs=pl.BlockSpec(memory_space=pl.ANY),
        scratch_shapes=[pltpu.VMEM((SLOTS * ROW_SLABS, LANES), F32), pltpu.VMEM((SLOTS * ROW_SLABS, LANES), F32),
                        pltpu.VMEM((BM * ROW_SLABS, LANES), F32),
                        pltpu.SemaphoreType.DMA((2,)), pltpu.SemaphoreType.DMA(())],
    )
    return pl.pallas_call(
        _dispatch_kernel,
        grid_spec=grid_spec,
        out_shape=jax.ShapeDtypeStruct((P_ROWS * ROW_SLABS, LANES), F32),
        compiler_params=pltpu.CompilerParams(dimension_semantics=("arbitrary",),
                                             vmem_limit_bytes=VMEM_LIMIT),
        name="dispatch",
    )(n_flat, run_rows, tail_start, tail_len, nused, hn2, ri)


def _experts_kernel(be_ref, first_ref, next_ref, nused_ref, x_ref, wg_hbm, wu_hbm, wd_hbm, y_ref,
                    wg_f, wu_f, wd_f, wg_b, wu_b, wd_b, sem):
    b = pl.program_id(0)

    def weight_copies(e):
        return (pltpu.make_async_copy(wg_hbm.at[e], wg_f, sem.at[0]),
                pltpu.make_async_copy(wu_hbm.at[e], wu_f, sem.at[1]),
                pltpu.make_async_copy(wd_hbm.at[e], wd_f, sem.at[2]))

    @pl.when((b == 0) & (nused_ref[0] > 0))
    def _():
        for c in weight_copies(be_ref[0]):
            c.start()

    @pl.when(b < nused_ref[0])
    def _():
        @pl.when(first_ref[b] == 1)
        def _():
            for c in weight_copies(be_ref[b]):
                c.wait()
            wg_b[...] = wg_f[...].astype(BF16)
            wu_b[...] = wu_f[...].astype(BF16)
            wd_b[...] = wd_f[...].astype(BF16)

            @pl.when(next_ref[b] >= 0)
            def _():
                for c in weight_copies(next_ref[b]):
                    c.start()

        x = _load_token_tiles(x_ref, BM).astype(BF16)
        a = _dot(x, wg_b[...])
        u = _dot(x, wu_b[...])
        y = _dot((_silu(a) * u).astype(BF16), wd_b[...])
        _store_token_tiles(y_ref, y, BM)

    @pl.when(b >= nused_ref[0])
    def _():
        y_ref[...] = jnp.zeros_like(y_ref)


def _experts(block_e, first, next_e, nused, xb, wg, wu, wd):
    def xmap(b, be, first, nxt, nused):
        return (b, 0)

    grid_spec = pltpu.PrefetchScalarGridSpec(
        num_scalar_prefetch=4,
        grid=(N_BLOCKS,),
        in_specs=[
            pl.BlockSpec((BM * ROW_SLABS, LANES), xmap),
            pl.BlockSpec(memory_space=pl.ANY),
            pl.BlockSpec(memory_space=pl.ANY),
            pl.BlockSpec(memory_space=pl.ANY),
        ],
        out_specs=pl.BlockSpec((BM * ROW_SLABS, LANES), xmap),
        scratch_shapes=[pltpu.VMEM((D_MODEL, D_EXPERT), F32), pltpu.VMEM((D_MODEL, D_EXPERT), F32),
                        pltpu.VMEM((D_EXPERT, D_MODEL), F32),
                        pltpu.VMEM((D_MODEL, D_EXPERT), BF16), pltpu.VMEM((D_MODEL, D_EXPERT), BF16),
                        pltpu.VMEM((D_EXPERT, D_MODEL), BF16), pltpu.SemaphoreType.DMA((3,))],
    )
    return pl.pallas_call(
        _experts_kernel,
        grid_spec=grid_spec,
        out_shape=jax.ShapeDtypeStruct((P_ROWS * ROW_SLABS, LANES), F32),
        compiler_params=pltpu.CompilerParams(dimension_semantics=("arbitrary",),
                                             vmem_limit_bytes=VMEM_LIMIT),
        name="experts",
    )(block_e, first, next_e, nused, xb, wg, wu, wd)


def _combine_kernel(n_ref, row_ref, yb_ref, h_ref, q_ref, w_ref, g_ref, out_ref, yb0, yb1, sem):
    i = pl.program_id(0)
    bufs = (yb0, yb1)

    def start_gather(tile, s):
        _start_tile_runs(n_ref, row_ref, tile,
                         lambda loc, glob: pltpu.make_async_copy(yb_ref.at[glob], bufs[s].at[loc], sem.at[s]))

    @pl.when(i == 0)
    def _():
        start_gather(0, 0)

    def combine(s):
        @pl.when(i + 1 < pl.num_programs(0))
        def _():
            start_gather(i + 1, 1 - s)

        pltpu.make_async_copy(yb_ref.at[pl.ds(0, SLOTS * ROW_SLABS)], bufs[s], sem.at[s]).wait()
        ys = _load_token_tiles(bufs[s], SLOTS).astype(BF16)
        col = lax.broadcasted_iota(jnp.int32, (TM_MIX, SLOTS), 1)
        wmat = (jnp.where(col == q_ref[:, 0:1], w_ref[:, 0:1], 0.0)
                + jnp.where(col == q_ref[:, 1:2], w_ref[:, 1:2], 0.0)).astype(BF16)
        hh = h_ref[...] + _dot(wmat, ys)
        out_ref[...] = hh * lax.rsqrt(jnp.mean(hh * hh, axis=-1, keepdims=True) + EPS) * g_ref[...]

    for s in range(2):
        pl.when(i % 2 == s)(functools.partial(combine, s))


def _combine(n_flat, run_rows, yb, h1, qt, wts, gf):
    grid_spec = pltpu.PrefetchScalarGridSpec(
        num_scalar_prefetch=2,
        grid=(N_TILES,),
        in_specs=[
            pl.BlockSpec(memory_space=pl.ANY),
            pl.BlockSpec((TM_MIX, D_MODEL), lambda i, *_: (i, 0)),
            pl.BlockSpec((TM_MIX, TOP_K), lambda i, *_: (i, 0)),
            pl.BlockSpec((TM_MIX, TOP_K), lambda i, *_: (i, 0)),
            pl.BlockSpec((1, D_MODEL), lambda i, *_: (0, 0)),
        ],
        out_specs=pl.BlockSpec((TM_MIX, D_MODEL), lambda i, *_: (i, 0)),
        scratch_shapes=[pltpu.VMEM((SLOTS * ROW_SLABS, LANES), F32), pltpu.VMEM((SLOTS * ROW_SLABS, LANES), F32),
                        pltpu.SemaphoreType.DMA((2,))],
    )
    return pl.pallas_call(
        _combine_kernel,
        grid_spec=grid_spec,
        out_shape=jax.ShapeDtypeStruct((SEQ, D_MODEL), F32),
        compiler_params=pltpu.CompilerParams(dimension_semantics=("arbitrary",),
                                             vmem_limit_bytes=VMEM_LIMIT),
        name="combine",
    )(n_flat, run_rows, yb, h1, qt, wts, gf)


def kernel(x, meta, norm1_g, w_in, w_decay_up, b_decay, gla_norm_g, conv_w, conv_b, conv_ln_g,
           conv_ln_b, w_pw2, b_gate, w_out, norm2_g, w_router_group, b_router_group,
           w_router_expert, b_router_expert, w_exp_gate, w_exp_up, w_exp_down, final_norm_g):
    assert x.shape == (1, SEQ, D_MODEL) and w_in.shape[0] == 1
    x2d = x[0]
    front = jnp.concatenate([jnp.zeros((FRONT - N_META, D_MODEL), F32), meta.astype(F32)], axis=0)

    w = w_in[0]
    o_q, o_k, o_v, o_og, o_dl, o_glu, o_gt = (0, QK_W, 2 * QK_W, 2 * QK_W + V_W, 2 * QK_W + 2 * V_W,
                                              2 * QK_W + 2 * V_W + GLA_RANK,
                                              2 * QK_W + 2 * V_W + GLA_RANK + 2 * D_MODEL)
    cols = lambda a, b: w[:, a:b].astype(BF16)
    row2 = lambda a: a.reshape(1, -1).astype(F32)
    q, k, v, og, la, u, gates = _inproj(
        front, x2d, row2(norm1_g[0]), cols(o_q, o_k), cols(o_k, o_v), cols(o_v, o_og),
        cols(o_og, o_dl), cols(o_dl, o_glu), cols(o_glu, o_gt), cols(o_gt, w.shape[1]),
        w_decay_up[0].astype(F32), row2(b_decay[0]), row2(b_gate[0]))

    o_n = _gla(q, k, v, la, row2(gla_norm_g[0]))

    wr = jnp.concatenate([w_router_group[0].T, w_router_expert[0].T,
                          jnp.zeros((LANES - N_GROUPS - N_EXPERTS, D_MODEL), F32)], axis=0).astype(F32)
    br = jnp.concatenate([b_router_group[0], b_router_expert[0],
                          jnp.zeros((LANES - N_GROUPS - N_EXPERTS,), F32)]).reshape(LANES, 1).astype(F32)
    wr_hi = wr.astype(BF16)
    wr_lo = (wr - wr_hi.astype(F32)).astype(BF16)
    h1, hn2, ri, rw, tabn, tabb = _mix(
        o_n, og, u, gates, x2d, conv_w[0].astype(F32), row2(conv_b[0]), row2(conv_ln_g[0]),
        row2(conv_ln_b[0]), w_pw2[0].astype(BF16), w_out[0].astype(BF16),
        row2(norm2_g[0]), wr_hi, wr_lo, br)

    experts = slice(N_GROUPS, N_GROUPS + N_EXPERTS)
    n_te = tabn[::SUBLANES, experts].astype(jnp.int32)
    seen_te = tabb[::SUBLANES, experts].astype(jnp.int32)
    counts = seen_te[-1] + n_te[-1]
    padded = (counts + BM - 1) // BM * BM
    pad_end = jnp.cumsum(padded)
    pad_start = pad_end - padded
    n_flat = n_te.reshape(-1)
    run_rows = (pad_start[None, :] + seen_te).reshape(-1).astype(jnp.int32)
    tail_start = (pad_start + counts).astype(jnp.int32)
    tail_len = (padded - counts).astype(jnp.int32)
    blk = jnp.arange(N_BLOCKS, dtype=jnp.int32)
    block_e = jnp.minimum(jnp.sum((pad_end[None, :] <= blk[:, None] * BM).astype(jnp.int32), axis=1),
                          N_EXPERTS - 1).astype(jnp.int32)
    first = jnp.concatenate([jnp.ones((1,), jnp.int32), (block_e[1:] != block_e[:-1]).astype(jnp.int32)])
    nused = (pad_end[-1:] // BM).astype(jnp.int32)
    eid = jnp.arange(N_EXPERTS, dtype=jnp.int32)
    later = jnp.flip(lax.cummin(jnp.flip(jnp.where(counts > 0, eid, N_EXPERTS))))
    nxt = jnp.concatenate([later[1:], jnp.full((1,), N_EXPERTS, jnp.int32)])
    next_e = jnp.where(nxt < N_EXPERTS, nxt, -1)[block_e].astype(jnp.int32)

    xb = _dispatch(n_flat, run_rows, tail_start, tail_len, nused, hn2, ri)
    yb = _experts(block_e, first, next_e, nused, xb, w_exp_gate[0], w_exp_up[0], w_exp_down[0])
    out = _combine(n_flat, run_rows, yb, h1, ri[0:TOP_K].T, rw[0:TOP_K].T, row2(final_norm_g))
    return out[None]
```

```python
import functools

import jax
import jax.numpy as jnp
import numpy as np
from jax import lax
from jax.experimental import pallas as pl
from jax.experimental.pallas import tpu as pltpu

F32 = jnp.float32
BF16 = jnp.bfloat16
HIGHEST = lax.Precision.HIGHEST

D_MODEL = 1024
SEQ = 16384
N_META = 16
GLA_HEADS = 4
GLA_DK = 128
GLA_DV = 256
GLA_RANK = 16
GLA_TAU = 16.0
CHUNK = 64
QK_W = GLA_HEADS * GLA_DK
V_W = GLA_HEADS * GLA_DV
CONV_WIDTH = 31
N_GROUPS = 8
EXPERTS_PER_GROUP = 8
N_EXPERTS = N_GROUPS * EXPERTS_PER_GROUP
TOP_K = 2
D_EXPERT = 512
EPS = 1e-6
LOG2E = 1.4426950408889634

LANES = 128
SUBLANES = 8
ROW_SLABS = D_MODEL // LANES

FRONT = 512
LP = FRONT + SEQ
TM_PROJ = 512
TM_GLA = 512
TM_MIX = 256
HIST = 32
SH_ROWS = TM_MIX + HIST - SUBLANES
CONV_RB = 64
N_TILES = SEQ // TM_MIX
SLOTS = TOP_K * TM_MIX
BM = 256
N_BLOCKS = (SEQ * TOP_K) // BM + N_EXPERTS
P_ROWS = N_BLOCKS * BM
VMEM_LIMIT = 56 * 1024 * 1024


def _sigmoid(x):
    return 0.5 * jnp.tanh(0.5 * x) + 0.5


def _silu(x):
    return x * _sigmoid(x)


def _dot(a, b, **kw):
    return jnp.dot(a, b, preferred_element_type=F32, **kw)


def _load_token_tiles(ref, n_tok):
    return jnp.concatenate([ref[pl.ds(s, n_tok, stride=ROW_SLABS), :] for s in range(ROW_SLABS)], axis=-1)


def _store_token_tiles(ref, val, n_tok):
    for s in range(ROW_SLABS):
        ref[pl.ds(s, n_tok, stride=ROW_SLABS), :] = val[:, s * LANES:(s + 1) * LANES]


def _inproj_kernel(front_ref, x_ref, g_ref, wq, wk, wv, wog, wdl, wglu, wgt, wdu, bd, bg,
                   q_o, k_o, v_o, og_o, la_o, u_o, gt_o):
    i = pl.program_id(0)
    h = jnp.where(i < FRONT // TM_PROJ, front_ref[...], x_ref[...])
    ms = jnp.mean(h * h, axis=-1, keepdims=True)
    hn = (h * lax.rsqrt(ms + EPS) * g_ref[...]).astype(BF16)
    q_o[...] = _dot(hn, wq[...]).astype(BF16)
    k_o[...] = _dot(hn, wk[...]).astype(BF16)
    v_o[...] = _dot(hn, wv[...]).astype(BF16)
    og_o[...] = _silu(_dot(hn, wog[...])).astype(BF16)
    glu = _dot(hn, wglu[...])
    u_o[...] = (glu[:, 0:D_MODEL] * _sigmoid(glu[:, D_MODEL:2 * D_MODEL])).astype(BF16)
    gt_o[...] = _sigmoid(_dot(hn, wgt[...]) + bg[...]).astype(BF16)
    dlow = _dot(hn, wdl[...])
    z = _dot(dlow, wdu[...], precision=HIGHEST) + bd[...]
    la_o[...] = (jnp.minimum(z, 0.0) - jnp.log(1.0 + jnp.exp(-jnp.abs(z)))) * (LOG2E / GLA_TAU)


def _inproj(front, x2d, g1, wq, wk, wv, wog, wdl, wglu, wgt, wdu, bd, bg):
    n_front = FRONT // TM_PROJ
    grid = (LP // TM_PROJ,)
    row = lambda w: pl.BlockSpec((TM_PROJ, w), lambda i: (i, 0))
    full = lambda a: pl.BlockSpec(a.shape, lambda i: (0, 0))
    return pl.pallas_call(
        _inproj_kernel,
        grid=grid,
        in_specs=[
            pl.BlockSpec((TM_PROJ, D_MODEL), lambda i: (jnp.minimum(i, n_front - 1), 0)),
            pl.BlockSpec((TM_PROJ, D_MODEL), lambda i: (jnp.maximum(i - n_front, 0), 0)),
            full(g1), full(wq), full(wk), full(wv), full(wog), full(wdl), full(wglu), full(wgt),
            full(wdu), full(bd), full(bg),
        ],
        out_specs=[row(QK_W), row(QK_W), row(V_W), row(D_MODEL), row(QK_W), row(D_MODEL),
                   row(2 * D_MODEL)],
        out_shape=[
            jax.ShapeDtypeStruct((LP, QK_W), BF16), jax.ShapeDtypeStruct((LP, QK_W), BF16),
            jax.ShapeDtypeStruct((LP, V_W), BF16), jax.ShapeDtypeStruct((LP, D_MODEL), BF16),
            jax.ShapeDtypeStruct((LP, QK_W), F32), jax.ShapeDtypeStruct((LP, D_MODEL), BF16),
            jax.ShapeDtypeStruct((LP, 2 * D_MODEL), BF16),
        ],
        compiler_params=pltpu.CompilerParams(dimension_semantics=("arbitrary",),
                                             vmem_limit_bytes=VMEM_LIMIT),
        name="inproj",
    )(front, x2d, g1, wq, wk, wv, wog, wdl, wglu, wgt, wdu, bd, bg)


GLA_LEVELS = (32, 16, 8, 4, 2, 1)
GLA_FINE = tuple(m for m in GLA_LEVELS if m < SUBLANES)
LA_SPLIT = 3


def _gla_tables():
    r = np.arange(CHUNK)
    t, c = r[:, None], r[None, :]
    rows = [c <= t]
    pair = []
    for m in GLA_LEVELS:
        mid = (t // (2 * m)) * (2 * m) + m - 1
        right = (t % (2 * m)) >= m
        if m in GLA_FINE:
            rows.append(np.where(right, (c > mid) & (c <= t), (c > t) & (c <= mid)))
        pair.append(((t // (2 * m)) == (c // (2 * m))) & right & ((c % (2 * m)) < m))
    pair.append(t == c)
    sums = np.concatenate(rows, axis=0).astype(np.float32)
    sums = np.concatenate([sums] * LA_SPLIT, axis=1)
    return jnp.asarray(sums, BF16), jnp.asarray(np.stack(pair).astype(np.float32))


def _split_bf16(x):
    pieces = []
    rest = x
    for _ in range(LA_SPLIT):
        p = rest.astype(BF16)
        pieces.append(p)
        rest = rest - p.astype(F32)
    return jnp.concatenate(pieces, axis=0)


def _dot_nt(a, b):
    return lax.dot_general(a, b, (((1,), (1,)), ((), ())), preferred_element_type=F32)


def _dot_tn(a, b):
    return lax.dot_general(a, b, (((0,), (0,)), ((), ())), preferred_element_type=F32)


def _gla_exponents(a):
    b = a[0:CHUNK]
    parts = [b, b[CHUNK - 1:CHUNK] - b]
    for m in GLA_LEVELS:
        if m in GLA_FINE:
            continue
        for lo in range(0, CHUNK, 2 * m):
            ref = b[lo + m - 1:lo + m]
            parts += [ref - b[lo:lo + m], b[lo + m:lo + 2 * m] - ref]
    return jnp.concatenate(parts + [a[CHUNK:]], axis=0)


def _gla_chunk(q, k, vb, e, dec, s_prev, pair_ref, right_rows):
    qe = (q * e[0:CHUNK]).astype(BF16)
    kd = (k * e[CHUNK:2 * CHUNK]).astype(BF16)
    s_new = s_prev * jnp.concatenate([dec] * (GLA_DV // LANES), axis=1) + _dot_tn(kd, vb)

    sc = _dot_nt(q.astype(BF16), k.astype(BF16)) * pair_ref[len(GLA_LEVELS)]
    for lvl in range(len(GLA_LEVELS)):
        rows = e[(2 + lvl) * CHUNK:(3 + lvl) * CHUNK]
        x = (jnp.where(right_rows[lvl], q, k) * rows).astype(BF16)
        sc = sc + _dot_nt(x, x) * pair_ref[lvl]
    o = _dot(jnp.concatenate([qe, sc.astype(BF16)], axis=1),
             jnp.concatenate([s_prev.astype(BF16), vb], axis=0))
    return o, s_new


def _gla_kernel(q_ref, k_ref, v_ref, la_ref, g_ref, sums_ref, pair_ref, o_ref, s_ref):
    @pl.when(pl.program_id(0) == 0)
    def _():
        s_ref[...] = jnp.zeros_like(s_ref)

    row = lax.broadcasted_iota(jnp.int32, (CHUNK, GLA_DK), 0)
    right_rows = [(row & m) != 0 for m in GLA_LEVELS]
    ones_col = jnp.ones((LA_SPLIT * CHUNK, LANES), BF16)
    sums = sums_ref[...]

    def body(c, carry):
        r0 = pl.multiple_of(c * CHUNK, CHUNK)
        rows = pl.ds(r0, CHUNK)
        states = [s_ref[h] for h in range(GLA_HEADS)]
        new_states = []
        for hp in range(GLA_HEADS // 2):
            la3 = _split_bf16(la_ref[rows, 2 * hp * GLA_DK:2 * (hp + 1) * GLA_DK])
            e2 = jnp.exp2(_gla_exponents(_dot(sums, la3)))
            dec2 = jnp.exp2(_dot_tn(la3, ones_col))
            for h in (2 * hp, 2 * hp + 1):
                half = slice((h % 2) * GLA_DK, (h % 2 + 1) * GLA_DK)
                kc = slice(h * GLA_DK, (h + 1) * GLA_DK)
                vc = slice(h * GLA_DV, (h + 1) * GLA_DV)
                q = q_ref[rows, kc].astype(F32) * (GLA_DK ** -0.5)
                k = k_ref[rows, kc].astype(F32)
                o, s_new = _gla_chunk(q, k, v_ref[rows, vc], e2[:, half], dec2[half], states[h],
                                      pair_ref, right_rows)
                new_states.append(s_new)
                o = o * lax.rsqrt(jnp.mean(o * o, axis=-1, keepdims=True) + EPS) * g_ref[:, vc]
                o_ref[rows, vc] = o.astype(BF16)
        for h in range(GLA_HEADS):
            s_ref[h] = new_states[h]
        return carry

    lax.fori_loop(0, TM_GLA // CHUNK, body, 0, unroll=4)


def _gla(q, k, v, la, g):
    sums, pair = _gla_tables()
    row = lambda w: pl.BlockSpec((TM_GLA, w), lambda t: (t, 0))
    full = lambda a: pl.BlockSpec(a.shape, lambda t: (0,) * a.ndim)
    return pl.pallas_call(
        _gla_kernel,
        grid=(LP // TM_GLA,),
        in_specs=[row(QK_W), row(QK_W), row(V_W), row(QK_W), full(g), full(sums), full(pair)],
        out_specs=row(V_W),
        out_shape=jax.ShapeDtypeStruct((LP, V_W), BF16),
        scratch_shapes=[pltpu.VMEM((GLA_HEADS, GLA_DK, GLA_DV), F32)],
        compiler_params=pltpu.CompilerParams(dimension_semantics=("arbitrary",),
                                             vmem_limit_bytes=VMEM_LIMIT),
        name="gla",
    )(q, k, v, la, g, sums, pair)


def _mix_kernel(o_ref, og_ref, u_ref, hist_ref, gt_ref, x_ref, cw_ref, cb_ref, lg_ref, lb_ref,
                wpw_ref, wo_ref, g2_ref, wrh_ref, wrl_ref, br_ref,
                h_o, hn_o, ri_o, rw_o, tabn_o, tabb_o, ubuf, shbuf, cbuf, seen):
    i = pl.program_id(0)

    @pl.when(i == 0)
    def _():
        seen[...] = jnp.zeros_like(seen)

    ubuf[0:HIST, :] = hist_ref[...].astype(F32)
    ubuf[HIST:HIST + TM_MIX, :] = u_ref[...].astype(F32)

    lead = HIST - (CONV_WIDTH - 1)
    for rho in range(1, SUBLANES):
        shbuf[rho - 1] = ubuf[rho:rho + SH_ROWS, :]
    for cb in range(D_MODEL // LANES):
        lanes = slice(cb * LANES, (cb + 1) * LANES)
        for rb in range(TM_MIX // CONV_RB):
            part = jnp.broadcast_to(cb_ref[:, lanes], (CONV_RB, LANES))
            for j in range(CONV_WIDTH):
                rho = (lead + j) % SUBLANES
                r0 = rb * CONV_RB + (lead + j) - rho
                src = ubuf[r0:r0 + CONV_RB, lanes] if rho == 0 else shbuf[rho - 1, r0:r0 + CONV_RB, lanes]
                part = part + cw_ref[j:j + 1, lanes] * src
            cbuf[rb * CONV_RB:(rb + 1) * CONV_RB, lanes] = part
    acc = cbuf[...]

    mu = jnp.mean(acc, axis=-1, keepdims=True)
    xc = acc - mu
    ln = xc * lax.rsqrt(jnp.mean(xc * xc, axis=-1, keepdims=True) + EPS) * lg_ref[...] + lb_ref[...]
    branch_b = _dot(_silu(ln).astype(BF16), wpw_ref[...])

    branch_a = o_ref[...].astype(F32) * og_ref[...].astype(F32)
    g_a = gt_ref[:, 0:D_MODEL].astype(F32)
    g_b = gt_ref[:, D_MODEL:2 * D_MODEL].astype(F32)
    merged = (g_a * branch_a + g_b * branch_b).astype(BF16)
    h1 = x_ref[...] + _dot(merged, wo_ref[...])
    h_o[...] = h1

    hn2 = h1 * lax.rsqrt(jnp.mean(h1 * h1, axis=-1, keepdims=True) + EPS) * g2_ref[...]
    hn_hi = hn2.astype(BF16)
    hn_o[...] = hn_hi

    hn_lo = (hn2 - hn_hi.astype(F32)).astype(BF16)
    logits = (_dot_nt(wrh_ref[...], hn_hi) + _dot_nt(wrh_ref[...], hn_lo) + _dot_nt(wrl_ref[...], hn_hi)
              + br_ref[...])
    row = lax.broadcasted_iota(jnp.int32, logits.shape, 0)
    rowf = row.astype(F32)
    neg = -jnp.inf
    is_g = row < N_GROUPS
    lg = jnp.where(is_g, logits, neg)
    gmax = jnp.max(lg, axis=0, keepdims=True)
    gidx = jnp.min(jnp.where(lg == gmax, rowf, float(N_GROUPS)), axis=0, keepdims=True)
    g_w = 1.0 / jnp.sum(jnp.where(is_g, jnp.exp(lg - gmax), 0.0), axis=0, keepdims=True)
    erow = rowf - float(N_GROUPS)
    egrp = ((row - N_GROUPS) >> 3).astype(F32)
    in_grp = (row >= N_GROUPS) & (row < N_GROUPS + N_EXPERTS) & (egrp == gidx)
    le = jnp.where(in_grp, logits, neg)
    m1 = jnp.max(le, axis=0, keepdims=True)
    i1 = jnp.min(jnp.where(le == m1, erow, float(N_EXPERTS)), axis=0, keepdims=True)
    le2 = jnp.where(erow == i1, neg, le)
    m2 = jnp.max(le2, axis=0, keepdims=True)
    i2 = jnp.min(jnp.where(le2 == m2, erow, float(N_EXPERTS)), axis=0, keepdims=True)
    t = jnp.exp(m2 - m1)
    w1 = g_w / (1.0 + t)
    w2 = g_w * t / (1.0 + t)

    oh1 = erow == i1
    oh2 = erow == i2
    oh = jnp.where(oh1 | oh2, 1.0, 0.0).astype(BF16)
    tr = lax.broadcasted_iota(jnp.int32, (TM_MIX, TM_MIX), 0)
    tc = lax.broadcasted_iota(jnp.int32, (TM_MIX, TM_MIX), 1)
    before_tok = _dot(oh, (tr < tc).astype(BF16))
    n_col = jnp.sum(oh.astype(F32), axis=1, keepdims=True)
    er = lax.broadcasted_iota(jnp.int32, (LANES, LANES), 0)
    ec = lax.broadcasted_iota(jnp.int32, (LANES, LANES), 1)
    before_exp = _dot((ec < er).astype(BF16), jnp.broadcast_to(n_col, logits.shape).astype(BF16))
    where_to = before_exp + before_tok
    q1 = jnp.sum(jnp.where(oh1, where_to, 0.0), axis=0, keepdims=True)
    q2 = jnp.sum(jnp.where(oh2, where_to, 0.0), axis=0, keepdims=True)

    n_rows = _dot_nt(jnp.ones((SUBLANES, TM_MIX), BF16), oh)
    tabn_o[...] = n_rows
    tabb_o[...] = seen[...]
    seen[...] = seen[...] + n_rows

    ri_o[...] = jnp.zeros_like(ri_o)
    ri_o[0:1, :] = q1.astype(jnp.int32)
    ri_o[1:2, :] = q2.astype(jnp.int32)
    rw_o[...] = jnp.zeros_like(rw_o)
    rw_o[0:1, :] = w1
    rw_o[1:2, :] = w2


def _mix(o_n, og, u, gates, x2d, cw, cb, lg, lb, wpw, wo, g2, wrh, wrl, br):
    n_front = FRONT // TM_MIX
    grid = (SEQ // TM_MIX,)
    rowp = lambda w: pl.BlockSpec((TM_MIX, w), lambda i: (i + n_front, 0))
    full = lambda a: pl.BlockSpec(a.shape, lambda i: (0,) * a.ndim)
    hist_blocks = TM_MIX // HIST
    return pl.pallas_call(
        _mix_kernel,
        grid=grid,
        in_specs=[
            rowp(V_W), rowp(D_MODEL), rowp(D_MODEL),
            pl.BlockSpec((HIST, D_MODEL), lambda i: ((i + n_front) * hist_blocks - 1, 0)),
            rowp(2 * D_MODEL),
            pl.BlockSpec((TM_MIX, D_MODEL), lambda i: (i, 0)),
            full(cw), full(cb), full(lg), full(lb), full(wpw), full(wo), full(g2),
            full(wrh), full(wrl), full(br),
        ],
        out_specs=[
            pl.BlockSpec((TM_MIX, D_MODEL), lambda i: (i, 0)),
            pl.BlockSpec((TM_MIX, D_MODEL), lambda i: (i, 0)),
            pl.BlockSpec((SUBLANES, TM_MIX), lambda i: (0, i)),
            pl.BlockSpec((SUBLANES, TM_MIX), lambda i: (0, i)),
            pl.BlockSpec((SUBLANES, LANES), lambda i: (i, 0)),
            pl.BlockSpec((SUBLANES, LANES), lambda i: (i, 0)),
        ],
        out_shape=[
            jax.ShapeDtypeStruct((SEQ, D_MODEL), F32),
            jax.ShapeDtypeStruct((SEQ, D_MODEL), BF16),
            jax.ShapeDtypeStruct((SUBLANES, SEQ), jnp.int32),
            jax.ShapeDtypeStruct((SUBLANES, SEQ), F32),
            jax.ShapeDtypeStruct((N_TILES * SUBLANES, LANES), F32),
            jax.ShapeDtypeStruct((N_TILES * SUBLANES, LANES), F32),
        ],
        scratch_shapes=[pltpu.VMEM((HIST + TM_MIX, D_MODEL), F32),
                        pltpu.VMEM((SUBLANES - 1, SH_ROWS, D_MODEL), F32),
                        pltpu.VMEM((TM_MIX, D_MODEL), F32), pltpu.VMEM((SUBLANES, LANES), F32)],
        compiler_params=pltpu.CompilerParams(dimension_semantics=("arbitrary",),
                                             vmem_limit_bytes=VMEM_LIMIT),
        name="mix",
    )(o_n, og, u, u, gates, x2d, cw, cb, lg, lb, wpw, wo, g2, wrh, wrl, br)


def _tok_rows(start_tok, n_tok):
    return pl.ds(pl.multiple_of(start_tok * ROW_SLABS, ROW_SLABS), n_tok * ROW_SLABS)


def _start_tile_runs(n_ref, row_ref, tile, make_copy):
    def run(e, off):
        n = n_ref[tile * N_EXPERTS + e]

        @pl.when(n > 0)
        def _():
            make_copy(_tok_rows(off, n), _tok_rows(row_ref[tile * N_EXPERTS + e], n)).start()

        return off + n

    lax.fori_loop(0, N_EXPERTS, run, 0, unroll=8)


def _dispatch_kernel(n_ref, row_ref, ts_ref, tl_ref, nused_ref, hn_ref, q_ref, xb_ref,
                     sb0, sb1, zbuf, sem, fsem):
    i = pl.program_id(0)
    last = pl.num_programs(0) - 1

    def fills(start):
        def tail(e, c):
            n = tl_ref[e]

            @pl.when(n > 0)
            def _():
                cp = pltpu.make_async_copy(zbuf.at[pl.ds(0, n * ROW_SLABS)],
                                           xb_ref.at[_tok_rows(ts_ref[e], n)], fsem)
                cp.start() if start else cp.wait()

            return c

        lax.fori_loop(0, N_EXPERTS, tail, 0)

        def block(b, c):
            cp = pltpu.make_async_copy(zbuf, xb_ref.at[_tok_rows(b * BM, BM)], fsem)
            cp.start() if start else cp.wait()
            return c

        lax.fori_loop(nused_ref[0], N_BLOCKS, block, 0)

    @pl.when(i == 0)
    def _():
        zbuf[...] = jnp.zeros_like(zbuf)
        fills(start=True)

    slot_i = lax.broadcasted_iota(jnp.int32, (SLOTS, TM_MIX), 0)
    onehot = jnp.where((slot_i == q_ref[0:1, :]) | (slot_i == q_ref[1:2, :]), 1.0, 0.0).astype(BF16)
    srt = _dot(onehot, hn_ref[...])

    def whole(s, sbuf):
        return pltpu.make_async_copy(sbuf, xb_ref.at[pl.ds(0, SLOTS * ROW_SLABS)], sem.at[s])

    def step(s, sbuf):
        @pl.when(i >= 2)
        def _():
            whole(s, sbuf).wait()

        _store_token_tiles(sbuf, srt, SLOTS)
        _start_tile_runs(n_ref, row_ref, i,
                         lambda loc, glob: pltpu.make_async_copy(sbuf.at[loc], xb_ref.at[glob], sem.at[s]))

    for s, sbuf in enumerate((sb0, sb1)):
        pl.when(i % 2 == s)(functools.partial(step, s, sbuf))

    @pl.when(i == last)
    def _():
        for s, sbuf in enumerate((sb0, sb1)):
            whole(s, sbuf).wait()
        fills(start=False)


def _dispatch(n_flat, run_rows, tail_start, tail_len, nused, hn2, ri):
    assert N_TILES >= 2
    grid_spec = pltpu.PrefetchScalarGridSpec(
        num_scalar_prefetch=5,
        grid=(N_TILES,),
        in_specs=[pl.BlockSpec((TM_MIX, D_MODEL), lambda i, *_: (i, 0)),
                  pl.BlockSpec((SUBLANES, TM_MIX), lambda i, *_: (0, i))],
        out_specs=pl.BlockSpec(memory_space=pl.ANY),
        scratch_shapes=[pltpu.VMEM((SLOTS * ROW_SLABS, LANES), F32), pltpu.VMEM((SLOTS * ROW_SLABS, LANES), F32),
                        pltpu.VMEM((BM * ROW_SLABS, LANES), F32),
                        pltpu.SemaphoreType.DMA((2,)), pltpu.SemaphoreType.DMA(())],
    )
    return pl.pallas_call(
        _dispatch_kernel,
        grid_spec=grid_spec,
        out_shape=jax.ShapeDtypeStruct((P_ROWS * ROW_SLABS, LANES), F32),
        compiler_params=pltpu.CompilerParams(dimension_semantics=("arbitrary",),
                                             vmem_limit_bytes=VMEM_LIMIT),
        name="dispatch",
    )(n_flat, run_rows, tail_start, tail_len, nused, hn2, ri)


def _experts_kernel(be_ref, first_ref, next_ref, next2_ref, stage_ref, nused_ref,
                    x_ref, wg_hbm, wu_hbm, wd_hbm, y_ref, wg_f, wu_f, wd_f, wg_b, wu_b, wd_b, sem):
    b = pl.program_id(0)

    def weight_copies(e, st):
        return (pltpu.make_async_copy(wg_hbm.at[e], wg_f.at[st], sem.at[st, 0]),
                pltpu.make_async_copy(wu_hbm.at[e], wu_f.at[st], sem.at[st, 1]),
                pltpu.make_async_copy(wd_hbm.at[e], wd_f.at[st], sem.at[st, 2]))

    @pl.when((b == 0) & (nused_ref[0] > 0))
    def _():
        for c in weight_copies(be_ref[0], 0):
            c.start()

        @pl.when(next_ref[0] >= 0)
        def _():
            for c in weight_copies(next_ref[0], 1):
                c.start()

    @pl.when(b < nused_ref[0])
    def _():
        def new_expert(st):
            for c in weight_copies(be_ref[b], st):
                c.wait()
            wg_b[...] = wg_f[st].astype(BF16)
            wu_b[...] = wu_f[st].astype(BF16)
            wd_b[...] = wd_f[st].astype(BF16)

            @pl.when(next2_ref[b] >= 0)
            def _():
                for c in weight_copies(next2_ref[b], st):
                    c.start()

        for st in range(2):
            pl.when((first_ref[b] == 1) & (stage_ref[b] == st))(functools.partial(new_expert, st))

        x = _load_token_tiles(x_ref, BM).astype(BF16)
        a = _dot(x, wg_b[...])
        u = _dot(x, wu_b[...])
        y = _dot((_silu(a) * u).astype(BF16), wd_b[...])
        _store_token_tiles(y_ref, y, BM)

    @pl.when(b >= nused_ref[0])
    def _():
        y_ref[...] = jnp.zeros_like(y_ref)


def _experts(block_e, first, next_e, next2_e, stage, nused, xb, wg, wu, wd):
    def xmap(b, *_):
        return (b, 0)

    grid_spec = pltpu.PrefetchScalarGridSpec(
        num_scalar_prefetch=6,
        grid=(N_BLOCKS,),
        in_specs=[
            pl.BlockSpec((BM * ROW_SLABS, LANES), xmap),
            pl.BlockSpec(memory_space=pl.ANY),
            pl.BlockSpec(memory_space=pl.ANY),
            pl.BlockSpec(memory_space=pl.ANY),
        ],
        out_specs=pl.BlockSpec((BM * ROW_SLABS, LANES), xmap),
        scratch_shapes=[pltpu.VMEM((2, D_MODEL, D_EXPERT), F32), pltpu.VMEM((2, D_MODEL, D_EXPERT), F32),
                        pltpu.VMEM((2, D_EXPERT, D_MODEL), F32),
                        pltpu.VMEM((D_MODEL, D_EXPERT), BF16), pltpu.VMEM((D_MODEL, D_EXPERT), BF16),
                        pltpu.VMEM((D_EXPERT, D_MODEL), BF16), pltpu.SemaphoreType.DMA((2, 3))],
    )
    return pl.pallas_call(
        _experts_kernel,
        grid_spec=grid_spec,
        out_shape=jax.ShapeDtypeStruct((P_ROWS * ROW_SLABS, LANES), F32),
        compiler_params=pltpu.CompilerParams(dimension_semantics=("arbitrary",),
                                             vmem_limit_bytes=VMEM_LIMIT),
        name="experts",
    )(block_e, first, next_e, next2_e, stage, nused, xb, wg, wu, wd)


def _combine_kernel(n_ref, row_ref, yb_ref, h_ref, q_ref, w_ref, g_ref, out_ref, yb0, yb1, sem):
    i = pl.program_id(0)
    bufs = (yb0, yb1)

    def start_gather(tile, s):
        _start_tile_runs(n_ref, row_ref, tile,
                         lambda loc, glob: pltpu.make_async_copy(yb_ref.at[glob], bufs[s].at[loc], sem.at[s]))

    @pl.when(i == 0)
    def _():
        start_gather(0, 0)

    def combine(s):
        @pl.when(i + 1 < pl.num_programs(0))
        def _():
            start_gather(i + 1, 1 - s)

        pltpu.make_async_copy(yb_ref.at[pl.ds(0, SLOTS * ROW_SLABS)], bufs[s], sem.at[s]).wait()
        ys = _load_token_tiles(bufs[s], SLOTS).astype(BF16)
        col = lax.broadcasted_iota(jnp.int32, (TM_MIX, SLOTS), 1)
        wmat = (jnp.where(col == q_ref[:, 0:1], w_ref[:, 0:1], 0.0)
                + jnp.where(col == q_ref[:, 1:2], w_ref[:, 1:2], 0.0)).astype(BF16)
        hh = h_ref[...] + _dot(wmat, ys)
        out_ref[...] = hh * lax.rsqrt(jnp.mean(hh * hh, axis=-1, keepdims=True) + EPS) * g_ref[...]

    for s in range(2):
        pl.when(i % 2 == s)(functools.partial(combine, s))


def _combine(n_flat, run_rows, yb, h1, qt, wts, gf):
    grid_spec = pltpu.PrefetchScalarGridSpec(
        num_scalar_prefetch=2,
        grid=(N_TILES,),
        in_specs=[
            pl.BlockSpec(memory_space=pl.ANY),
            pl.BlockSpec((TM_MIX, D_MODEL), lambda i, *_: (i, 0)),
            pl.BlockSpec((TM_MIX, TOP_K), lambda i, *_: (i, 0)),
            pl.BlockSpec((TM_MIX, TOP_K), lambda i, *_: (i, 0)),
            pl.BlockSpec((1, D_MODEL), lambda i, *_: (0, 0)),
        ],
        out_specs=pl.BlockSpec((TM_MIX, D_MODEL), lambda i, *_: (i, 0)),
        scratch_shapes=[pltpu.VMEM((SLOTS * ROW_SLABS, LANES), F32), pltpu.VMEM((SLOTS * ROW_SLABS, LANES), F32),
                        pltpu.SemaphoreType.DMA((2,))],
    )
    return pl.pallas_call(
        _combine_kernel,
        grid_spec=grid_spec,
        out_shape=jax.ShapeDtypeStruct((SEQ, D_MODEL), F32),
        compiler_params=pltpu.CompilerParams(dimension_semantics=("arbitrary",),
                                             vmem_limit_bytes=VMEM_LIMIT),
        name="combine",
    )(n_flat, run_rows, yb, h1, qt, wts, gf)


def kernel(x, meta, norm1_g, w_in, w_decay_up, b_decay, gla_norm_g, conv_w, conv_b, conv_ln_g,
           conv_ln_b, w_pw2, b_gate, w_out, norm2_g, w_router_group, b_router_group,
           w_router_expert, b_router_expert, w_exp_gate, w_exp_up, w_exp_down, final_norm_g):
    assert x.shape == (1, SEQ, D_MODEL) and w_in.shape[0] == 1
    x2d = x[0]
    front = jnp.concatenate([jnp.zeros((FRONT - N_META, D_MODEL), F32), meta.astype(F32)], axis=0)

    w = w_in[0]
    o_q, o_k, o_v, o_og, o_dl, o_glu, o_gt = (0, QK_W, 2 * QK_W, 2 * QK_W + V_W, 2 * QK_W + 2 * V_W,
                                              2 * QK_W + 2 * V_W + GLA_RANK,
                                              2 * QK_W + 2 * V_W + GLA_RANK + 2 * D_MODEL)
    cols = lambda a, b: w[:, a:b].astype(BF16)
    row2 = lambda a: a.reshape(1, -1).astype(F32)
    q, k, v, og, la, u, gates = _inproj(
        front, x2d, row2(norm1_g[0]), cols(o_q, o_k), cols(o_k, o_v), cols(o_v, o_og),
        cols(o_og, o_dl), cols(o_dl, o_glu), cols(o_glu, o_gt), cols(o_gt, w.shape[1]),
        w_decay_up[0].astype(F32), row2(b_decay[0]), row2(b_gate[0]))

    o_n = _gla(q, k, v, la, row2(gla_norm_g[0]))

    wr = jnp.concatenate([w_router_group[0].T, w_router_expert[0].T,
                          jnp.zeros((LANES - N_GROUPS - N_EXPERTS, D_MODEL), F32)], axis=0).astype(F32)
    br = jnp.concatenate([b_router_group[0], b_router_expert[0],
                          jnp.zeros((LANES - N_GROUPS - N_EXPERTS,), F32)]).reshape(LANES, 1).astype(F32)
    wr_hi = wr.astype(BF16)
    wr_lo = (wr - wr_hi.astype(F32)).astype(BF16)
    h1, hn2, ri, rw, tabn, tabb = _mix(
        o_n, og, u, gates, x2d, conv_w[0].astype(F32), row2(conv_b[0]), row2(conv_ln_g[0]),
        row2(conv_ln_b[0]), w_pw2[0].astype(BF16), w_out[0].astype(BF16),
        row2(norm2_g[0]), wr_hi, wr_lo, br)

    experts = slice(N_GROUPS, N_GROUPS + N_EXPERTS)
    n_te = tabn[::SUBLANES, experts].astype(jnp.int32)
    seen_te = tabb[::SUBLANES, experts].astype(jnp.int32)
    counts = seen_te[-1] + n_te[-1]
    padded = (counts + BM - 1) // BM * BM
    pad_end = jnp.cumsum(padded)
    pad_start = pad_end - padded
    n_flat = n_te.reshape(-1)
    run_rows = (pad_start[None, :] + seen_te).reshape(-1).astype(jnp.int32)
    tail_start = (pad_start + counts).astype(jnp.int32)
    tail_len = (padded - counts).astype(jnp.int32)
    blk = jnp.arange(N_BLOCKS, dtype=jnp.int32)
    block_e = jnp.minimum(jnp.sum((pad_end[None, :] <= blk[:, None] * BM).astype(jnp.int32), axis=1),
                          N_EXPERTS - 1).astype(jnp.int32)
    first = jnp.concatenate([jnp.ones((1,), jnp.int32), (block_e[1:] != block_e[:-1]).astype(jnp.int32)])
    nused = (pad_end[-1:] // BM).astype(jnp.int32)
    eid = jnp.arange(N_EXPERTS, dtype=jnp.int32)
    later = jnp.flip(lax.cummin(jnp.flip(jnp.where(counts > 0, eid, N_EXPERTS))))
    nxt = jnp.concatenate([later[1:], jnp.full((2,), N_EXPERTS, jnp.int32)])
    nxt2 = nxt[nxt[:N_EXPERTS]]
    or_none = lambda t: jnp.where(t < N_EXPERTS, t, -1)[block_e].astype(jnp.int32)
    next_e, next2_e = or_none(nxt[:N_EXPERTS]), or_none(nxt2)
    stage = ((jnp.cumsum(first) - 1) % 2).astype(jnp.int32)

    xb = _dispatch(n_flat, run_rows, tail_start, tail_len, nused, hn2, ri)
    yb = _experts(block_e, first, next_e, next2_e, stage, nused, xb, w_exp_gate[0], w_exp_up[0], w_exp_down[0])
    out = _combine(n_flat, run_rows, yb, h1, ri[0:TOP_K].T, rw[0:TOP_K].T, row2(final_norm_g))
    return out[None]
```

```python
import functools

import jax
import jax.numpy as jnp
import numpy as np
from jax import lax
from jax.experimental import pallas as pl
from jax.experimental.pallas import tpu as pltpu

F32 = jnp.float32
BF16 = jnp.bfloat16
HIGHEST = lax.Precision.HIGHEST

D_MODEL = 1024
SEQ = 16384
N_META = 16
GLA_HEADS = 4
GLA_DK = 128
GLA_DV = 256
GLA_RANK = 16
GLA_TAU = 16.0
CHUNK = 64
QK_W = GLA_HEADS * GLA_DK
V_W = GLA_HEADS * GLA_DV
CONV_WIDTH = 31
N_GROUPS = 8
EXPERTS_PER_GROUP = 8
N_EXPERTS = N_GROUPS * EXPERTS_PER_GROUP
TOP_K = 2
D_EXPERT = 512
EPS = 1e-6
LOG2E = 1.4426950408889634

LANES = 128
SUBLANES = 8
ROW_SLABS = D_MODEL // LANES

FRONT = 512
LP = FRONT + SEQ
TM_PROJ = 512
TM_GLA = 512
TM_MIX = 256
HIST = 32
SH_ROWS = TM_MIX + HIST - SUBLANES
CONV_RB = 64
N_TILES = SEQ // TM_MIX
SLOTS = TOP_K * TM_MIX
BM = 256
BLOCKS_PER_STEP = 4
N_BLOCKS = (SEQ * TOP_K) // BM + N_EXPERTS
P_ROWS = N_BLOCKS * BM
VMEM_LIMIT = 56 * 1024 * 1024


def _sigmoid(x):
    return 0.5 * jnp.tanh(0.5 * x) + 0.5


def _silu(x):
    return x * _sigmoid(x)


def _dot(a, b, **kw):
    return jnp.dot(a, b, preferred_element_type=F32, **kw)


def _load_token_tiles(ref, n_tok, first_tok=0):
    r0 = first_tok * ROW_SLABS
    return jnp.concatenate([ref[pl.ds(r0 + s, n_tok, stride=ROW_SLABS), :] for s in range(ROW_SLABS)], axis=-1)


def _store_token_tiles(ref, val, n_tok, first_tok=0):
    r0 = first_tok * ROW_SLABS
    for s in range(ROW_SLABS):
        ref[pl.ds(r0 + s, n_tok, stride=ROW_SLABS), :] = val[:, s * LANES:(s + 1) * LANES]


def _inproj_kernel(front_ref, x_ref, g_ref, wq, wk, wv, wog, wdl, wglu, wgt, wdu, bd, bg,
                   q_o, k_o, v_o, og_o, la_o, u_o, gt_o):
    i = pl.program_id(0)
    h = jnp.where(i < FRONT // TM_PROJ, front_ref[...], x_ref[...])
    ms = jnp.mean(h * h, axis=-1, keepdims=True)
    hn = (h * lax.rsqrt(ms + EPS) * g_ref[...]).astype(BF16)
    q_o[...] = _dot(hn, wq[...]).astype(BF16)
    k_o[...] = _dot(hn, wk[...]).astype(BF16)
    v_o[...] = _dot(hn, wv[...]).astype(BF16)
    og_o[...] = _silu(_dot(hn, wog[...])).astype(BF16)
    glu = _dot(hn, wglu[...])
    u_o[...] = (glu[:, 0:D_MODEL] * _sigmoid(glu[:, D_MODEL:2 * D_MODEL])).astype(BF16)
    gt_o[...] = _sigmoid(_dot(hn, wgt[...]) + bg[...]).astype(BF16)
    dlow = _dot(hn, wdl[...])
    z = _dot(dlow, wdu[...], precision=HIGHEST) + bd[...]
    la_o[...] = (jnp.minimum(z, 0.0) - jnp.log(1.0 + jnp.exp(-jnp.abs(z)))) * (LOG2E / GLA_TAU)


def _inproj(front, x2d, g1, wq, wk, wv, wog, wdl, wglu, wgt, wdu, bd, bg):
    n_front = FRONT // TM_PROJ
    grid = (LP // TM_PROJ,)
    row = lambda w: pl.BlockSpec((TM_PROJ, w), lambda i: (i, 0))
    full = lambda a: pl.BlockSpec(a.shape, lambda i: (0, 0))
    return pl.pallas_call(
        _inproj_kernel,
        grid=grid,
        in_specs=[
            pl.BlockSpec((TM_PROJ, D_MODEL), lambda i: (jnp.minimum(i, n_front - 1), 0)),
            pl.BlockSpec((TM_PROJ, D_MODEL), lambda i: (jnp.maximum(i - n_front, 0), 0)),
            full(g1), full(wq), full(wk), full(wv), full(wog), full(wdl), full(wglu), full(wgt),
            full(wdu), full(bd), full(bg),
        ],
        out_specs=[row(QK_W), row(QK_W), row(V_W), row(D_MODEL), row(QK_W), row(D_MODEL),
                   row(2 * D_MODEL)],
        out_shape=[
            jax.ShapeDtypeStruct((LP, QK_W), BF16), jax.ShapeDtypeStruct((LP, QK_W), BF16),
            jax.ShapeDtypeStruct((LP, V_W), BF16), jax.ShapeDtypeStruct((LP, D_MODEL), BF16),
            jax.ShapeDtypeStruct((LP, QK_W), F32), jax.ShapeDtypeStruct((LP, D_MODEL), BF16),
            jax.ShapeDtypeStruct((LP, 2 * D_MODEL), BF16),
        ],
        compiler_params=pltpu.CompilerParams(dimension_semantics=("arbitrary",),
                                             vmem_limit_bytes=VMEM_LIMIT),
        name="inproj",
    )(front, x2d, g1, wq, wk, wv, wog, wdl, wglu, wgt, wdu, bd, bg)


GLA_LEVELS = (32, 16, 8, 4, 2, 1)
GLA_FINE = tuple(m for m in GLA_LEVELS if m < SUBLANES)
LA_SPLIT = 3


def _gla_tables():
    r = np.arange(CHUNK)
    t, c = r[:, None], r[None, :]
    rows = [c <= t]
    pair = []
    for m in GLA_LEVELS:
        mid = (t // (2 * m)) * (2 * m) + m - 1
        right = (t % (2 * m)) >= m
        if m in GLA_FINE:
            rows.append(np.where(right, (c > mid) & (c <= t), (c > t) & (c <= mid)))
        pair.append(((t // (2 * m)) == (c // (2 * m))) & right & ((c % (2 * m)) < m))
    pair.append(t == c)
    sums = np.concatenate(rows, axis=0).astype(np.float32)
    sums = np.concatenate([sums] * LA_SPLIT, axis=1)
    return jnp.asarray(sums, BF16), jnp.asarray(np.stack(pair).astype(np.float32))


def _split_bf16(x):
    pieces = []
    rest = x
    for _ in range(LA_SPLIT):
        p = rest.astype(BF16)
        pieces.append(p)
        rest = rest - p.astype(F32)
    return jnp.concatenate(pieces, axis=0)


def _dot_nt(a, b):
    return lax.dot_general(a, b, (((1,), (1,)), ((), ())), preferred_element_type=F32)


def _dot_tn(a, b):
    return lax.dot_general(a, b, (((0,), (0,)), ((), ())), preferred_element_type=F32)


def _gla_exponents(a):
    b = a[0:CHUNK]
    parts = [b, b[CHUNK - 1:CHUNK] - b]
    for m in GLA_LEVELS:
        if m in GLA_FINE:
            continue
        for lo in range(0, CHUNK, 2 * m):
            ref = b[lo + m - 1:lo + m]
            parts += [ref - b[lo:lo + m], b[lo + m:lo + 2 * m] - ref]
    return jnp.concatenate(parts + [a[CHUNK:]], axis=0)


def _gla_chunk(q, k, vb, e, dec, s_prev, pair_ref, right_rows):
    qe = (q * e[0:CHUNK]).astype(BF16)
    kd = (k * e[CHUNK:2 * CHUNK]).astype(BF16)
    s_new = s_prev * jnp.concatenate([dec] * (GLA_DV // LANES), axis=1) + _dot_tn(kd, vb)

    sc = _dot_nt(q.astype(BF16), k.astype(BF16)) * pair_ref[len(GLA_LEVELS)]
    for lvl in range(len(GLA_LEVELS)):
        rows = e[(2 + lvl) * CHUNK:(3 + lvl) * CHUNK]
        x = (jnp.where(right_rows[lvl], q, k) * rows).astype(BF16)
        sc = sc + _dot_nt(x, x) * pair_ref[lvl]
    o = _dot(jnp.concatenate([qe, sc.astype(BF16)], axis=1),
             jnp.concatenate([s_prev.astype(BF16), vb], axis=0))
    return o, s_new


def _gla_kernel(q_ref, k_ref, v_ref, la_ref, g_ref, sums_ref, pair_ref, o_ref, s_ref):
    @pl.when(pl.program_id(0) == 0)
    def _():
        s_ref[...] = jnp.zeros_like(s_ref)

    row = lax.broadcasted_iota(jnp.int32, (CHUNK, GLA_DK), 0)
    right_rows = [(row & m) != 0 for m in GLA_LEVELS]
    ones_col = jnp.ones((LA_SPLIT * CHUNK, LANES), BF16)
    sums = sums_ref[...]

    def body(c, carry):
        r0 = pl.multiple_of(c * CHUNK, CHUNK)
        rows = pl.ds(r0, CHUNK)
        states = [s_ref[h] for h in range(GLA_HEADS)]
        new_states = []
        for hp in range(GLA_HEADS // 2):
            la3 = _split_bf16(la_ref[rows, 2 * hp * GLA_DK:2 * (hp + 1) * GLA_DK])
            e2 = jnp.exp2(_gla_exponents(_dot(sums, la3)))
            dec2 = jnp.exp2(_dot_tn(la3, ones_col))
            for h in (2 * hp, 2 * hp + 1):
                half = slice((h % 2) * GLA_DK, (h % 2 + 1) * GLA_DK)
                kc = slice(h * GLA_DK, (h + 1) * GLA_DK)
                vc = slice(h * GLA_DV, (h + 1) * GLA_DV)
                q = q_ref[rows, kc].astype(F32) * (GLA_DK ** -0.5)
                k = k_ref[rows, kc].astype(F32)
                o, s_new = _gla_chunk(q, k, v_ref[rows, vc], e2[:, half], dec2[half], states[h],
                                      pair_ref, right_rows)
                new_states.append(s_new)
                o = o * lax.rsqrt(jnp.mean(o * o, axis=-1, keepdims=True) + EPS) * g_ref[:, vc]
                o_ref[rows, vc] = o.astype(BF16)
        for h in range(GLA_HEADS):
            s_ref[h] = new_states[h]
        return carry

    lax.fori_loop(0, TM_GLA // CHUNK, body, 0, unroll=4)


def _gla(q, k, v, la, g):
    sums, pair = _gla_tables()
    row = lambda w: pl.BlockSpec((TM_GLA, w), lambda t: (t, 0))
    full = lambda a: pl.BlockSpec(a.shape, lambda t: (0,) * a.ndim)
    return pl.pallas_call(
        _gla_kernel,
        grid=(LP // TM_GLA,),
        in_specs=[row(QK_W), row(QK_W), row(V_W), row(QK_W), full(g), full(sums), full(pair)],
        out_specs=row(V_W),
        out_shape=jax.ShapeDtypeStruct((LP, V_W), BF16),
        scratch_shapes=[pltpu.VMEM((GLA_HEADS, GLA_DK, GLA_DV), F32)],
        compiler_params=pltpu.CompilerParams(dimension_semantics=("arbitrary",),
                                             vmem_limit_bytes=VMEM_LIMIT),
        name="gla",
    )(q, k, v, la, g, sums, pair)


def _mix_kernel(o_ref, og_ref, u_ref, hist_ref, gt_ref, x_ref, cw_ref, cb_ref, lg_ref, lb_ref,
                wpw_ref, wo_ref, g2_ref, wrh_ref, wrl_ref, br_ref,
                h_o, hn_o, ri_o, rw_o, tabn_o, tabb_o, ubuf, shbuf, cbuf, seen):
    i = pl.program_id(0)

    @pl.when(i == 0)
    def _():
        seen[...] = jnp.zeros_like(seen)

    ubuf[0:HIST, :] = hist_ref[...].astype(F32)
    ubuf[HIST:HIST + TM_MIX, :] = u_ref[...].astype(F32)

    lead = HIST - (CONV_WIDTH - 1)
    for rho in range(1, SUBLANES):
        shbuf[rho - 1] = ubuf[rho:rho + SH_ROWS, :]
    for cb in range(D_MODEL // LANES):
        lanes = slice(cb * LANES, (cb + 1) * LANES)
        for rb in range(TM_MIX // CONV_RB):
            part = jnp.broadcast_to(cb_ref[:, lanes], (CONV_RB, LANES))
            for j in range(CONV_WIDTH):
                rho = (lead + j) % SUBLANES
                r0 = rb * CONV_RB + (lead + j) - rho
                src = ubuf[r0:r0 + CONV_RB, lanes] if rho == 0 else shbuf[rho - 1, r0:r0 + CONV_RB, lanes]
                part = part + cw_ref[j:j + 1, lanes] * src
            cbuf[rb * CONV_RB:(rb + 1) * CONV_RB, lanes] = part
    acc = cbuf[...]

    mu = jnp.mean(acc, axis=-1, keepdims=True)
    xc = acc - mu
    ln = xc * lax.rsqrt(jnp.mean(xc * xc, axis=-1, keepdims=True) + EPS) * lg_ref[...] + lb_ref[...]
    branch_b = _dot(_silu(ln).astype(BF16), wpw_ref[...])

    branch_a = o_ref[...].astype(F32) * og_ref[...].astype(F32)
    g_a = gt_ref[:, 0:D_MODEL].astype(F32)
    g_b = gt_ref[:, D_MODEL:2 * D_MODEL].astype(F32)
    merged = (g_a * branch_a + g_b * branch_b).astype(BF16)
    h1 = x_ref[...] + _dot(merged, wo_ref[...])
    h_o[...] = h1

    hn2 = h1 * lax.rsqrt(jnp.mean(h1 * h1, axis=-1, keepdims=True) + EPS) * g2_ref[...]
    hn_hi = hn2.astype(BF16)
    hn_o[...] = hn_hi

    hn_lo = (hn2 - hn_hi.astype(F32)).astype(BF16)
    logits = (_dot_nt(wrh_ref[...], hn_hi) + _dot_nt(wrh_ref[...], hn_lo) + _dot_nt(wrl_ref[...], hn_hi)
              + br_ref[...])
    row = lax.broadcasted_iota(jnp.int32, logits.shape, 0)
    rowf = row.astype(F32)
    neg = -jnp.inf
    is_g = row < N_GROUPS
    lg = jnp.where(is_g, logits, neg)
    gmax = jnp.max(lg, axis=0, keepdims=True)
    gidx = jnp.min(jnp.where(lg == gmax, rowf, float(N_GROUPS)), axis=0, keepdims=True)
    g_w = 1.0 / jnp.sum(jnp.where(is_g, jnp.exp(lg - gmax), 0.0), axis=0, keepdims=True)
    erow = rowf - float(N_GROUPS)
    egrp = ((row - N_GROUPS) >> 3).astype(F32)
    in_grp = (row >= N_GROUPS) & (row < N_GROUPS + N_EXPERTS) & (egrp == gidx)
    le = jnp.where(in_grp, logits, neg)
    m1 = jnp.max(le, axis=0, keepdims=True)
    i1 = jnp.min(jnp.where(le == m1, erow, float(N_EXPERTS)), axis=0, keepdims=True)
    le2 = jnp.where(erow == i1, neg, le)
    m2 = jnp.max(le2, axis=0, keepdims=True)
    i2 = jnp.min(jnp.where(le2 == m2, erow, float(N_EXPERTS)), axis=0, keepdims=True)
    t = jnp.exp(m2 - m1)
    w1 = g_w / (1.0 + t)
    w2 = g_w * t / (1.0 + t)

    oh1 = erow == i1
    oh2 = erow == i2
    oh = jnp.where(oh1 | oh2, 1.0, 0.0).astype(BF16)
    tr = lax.broadcasted_iota(jnp.int32, (TM_MIX, TM_MIX), 0)
    tc = lax.broadcasted_iota(jnp.int32, (TM_MIX, TM_MIX), 1)
    before_tok = _dot(oh, (tr < tc).astype(BF16))
    n_col = jnp.sum(oh.astype(F32), axis=1, keepdims=True)
    er = lax.broadcasted_iota(jnp.int32, (LANES, LANES), 0)
    ec = lax.broadcasted_iota(jnp.int32, (LANES, LANES), 1)
    before_exp = _dot((ec < er).astype(BF16), jnp.broadcast_to(n_col, logits.shape).astype(BF16))
    where_to = before_exp + before_tok
    q1 = jnp.sum(jnp.where(oh1, where_to, 0.0), axis=0, keepdims=True)
    q2 = jnp.sum(jnp.where(oh2, where_to, 0.0), axis=0, keepdims=True)

    n_rows = _dot_nt(jnp.ones((SUBLANES, TM_MIX), BF16), oh)
    tabn_o[...] = n_rows
    tabb_o[...] = seen[...]
    seen[...] = seen[...] + n_rows

    ri_o[...] = jnp.zeros_like(ri_o)
    ri_o[0:1, :] = q1.astype(jnp.int32)
    ri_o[1:2, :] = q2.astype(jnp.int32)
    rw_o[...] = jnp.zeros_like(rw_o)
    rw_o[0:1, :] = w1
    rw_o[1:2, :] = w2


def _mix(o_n, og, u, gates, x2d, cw, cb, lg, lb, wpw, wo, g2, wrh, wrl, br):
    n_front = FRONT // TM_MIX
    grid = (SEQ // TM_MIX,)
    rowp = lambda w: pl.BlockSpec((TM_MIX, w), lambda i: (i + n_front, 0))
    full = lambda a: pl.BlockSpec(a.shape, lambda i: (0,) * a.ndim)
    hist_blocks = TM_MIX // HIST
    return pl.pallas_call(
        _mix_kernel,
        grid=grid,
        in_specs=[
            rowp(V_W), rowp(D_MODEL), rowp(D_MODEL),
            pl.BlockSpec((HIST, D_MODEL), lambda i: ((i + n_front) * hist_blocks - 1, 0)),
            rowp(2 * D_MODEL),
            pl.BlockSpec((TM_MIX, D_MODEL), lambda i: (i, 0)),
            full(cw), full(cb), full(lg), full(lb), full(wpw), full(wo), full(g2),
            full(wrh), full(wrl), full(br),
        ],
        out_specs=[
            pl.BlockSpec((TM_MIX, D_MODEL), lambda i: (i, 0)),
            pl.BlockSpec((TM_MIX, D_MODEL), lambda i: (i, 0)),
            pl.BlockSpec((SUBLANES, TM_MIX), lambda i: (0, i)),
            pl.BlockSpec((SUBLANES, TM_MIX), lambda i: (0, i)),
            pl.BlockSpec((SUBLANES, LANES), lambda i: (i, 0)),
            pl.BlockSpec((SUBLANES, LANES), lambda i: (i, 0)),
        ],
        out_shape=[
            jax.ShapeDtypeStruct((SEQ, D_MODEL), F32),
            jax.ShapeDtypeStruct((SEQ, D_MODEL), BF16),
            jax.ShapeDtypeStruct((SUBLANES, SEQ), jnp.int32),
            jax.ShapeDtypeStruct((SUBLANES, SEQ), F32),
            jax.ShapeDtypeStruct((N_TILES * SUBLANES, LANES), F32),
            jax.ShapeDtypeStruct((N_TILES * SUBLANES, LANES), F32),
        ],
        scratch_shapes=[pltpu.VMEM((HIST + TM_MIX, D_MODEL), F32),
                        pltpu.VMEM((SUBLANES - 1, SH_ROWS, D_MODEL), F32),
                        pltpu.VMEM((TM_MIX, D_MODEL), F32), pltpu.VMEM((SUBLANES, LANES), F32)],
        compiler_params=pltpu.CompilerParams(dimension_semantics=("arbitrary",),
                                             vmem_limit_bytes=VMEM_LIMIT),
        name="mix",
    )(o_n, og, u, u, gates, x2d, cw, cb, lg, lb, wpw, wo, g2, wrh, wrl, br)


def _tok_rows(start_tok, n_tok):
    return pl.ds(pl.multiple_of(start_tok * ROW_SLABS, ROW_SLABS), n_tok * ROW_SLABS)


def _start_tile_runs(n_ref, row_ref, tile, make_copy):
    def run(e, off):
        n = n_ref[tile * N_EXPERTS + e]

        @pl.when(n > 0)
        def _():
            make_copy(_tok_rows(off, n), _tok_rows(row_ref[tile * N_EXPERTS + e], n)).start()

        return off + n

    lax.fori_loop(0, N_EXPERTS, run, 0, unroll=8)


def _dispatch_kernel(n_ref, row_ref, ts_ref, tl_ref, nused_ref, hn_ref, q_ref, xb_ref,
                     sb0, sb1, zbuf, sem, fsem):
    i = pl.program_id(0)
    last = pl.num_programs(0) - 1

    def fills(start):
        def tail(e, c):
            n = tl_ref[e]

            @pl.when(n > 0)
            def _():
                cp = pltpu.make_async_copy(zbuf.at[pl.ds(0, n * ROW_SLABS)],
                                           xb_ref.at[_tok_rows(ts_ref[e], n)], fsem)
                cp.start() if start else cp.wait()

            return c

        lax.fori_loop(0, N_EXPERTS, tail, 0)

        def block(b, c):
            cp = pltpu.make_async_copy(zbuf, xb_ref.at[_tok_rows(b * BM, BM)], fsem)
            cp.start() if start else cp.wait()
            return c

        lax.fori_loop(nused_ref[0], N_BLOCKS, block, 0)

    @pl.when(i == 0)
    def _():
        zbuf[...] = jnp.zeros_like(zbuf)
        fills(start=True)

    slot_i = lax.broadcasted_iota(jnp.int32, (SLOTS, TM_MIX), 0)
    onehot = jnp.where((slot_i == q_ref[0:1, :]) | (slot_i == q_ref[1:2, :]), 1.0, 0.0).astype(BF16)
    srt = _dot(onehot, hn_ref[...])

    def whole(s, sbuf):
        return pltpu.make_async_copy(sbuf, xb_ref.at[pl.ds(0, SLOTS * ROW_SLABS)], sem.at[s])

    def step(s, sbuf):
        @pl.when(i >= 2)
        def _():
            whole(s, sbuf).wait()

        _store_token_tiles(sbuf, srt, SLOTS)
        _start_tile_runs(n_ref, row_ref, i,
                         lambda loc, glob: pltpu.make_async_copy(sbuf.at[loc], xb_ref.at[glob], sem.at[s]))

    for s, sbuf in enumerate((sb0, sb1)):
        pl.when(i % 2 == s)(functools.partial(step, s, sbuf))

    @pl.when(i == last)
    def _():
        for s, sbuf in enumerate((sb0, sb1)):
            whole(s, sbuf).wait()
        fills(start=False)


def _dispatch(n_flat, run_rows, tail_start, tail_len, nused, hn2, ri):
    assert N_TILES >= 2
    grid_spec = pltpu.PrefetchScalarGridSpec(
        num_scalar_prefetch=5,
        grid=(N_TILES,),
        in_specs=[pl.BlockSpec((TM_MIX, D_MODEL), lambda i, *_: (i, 0)),
                  pl.BlockSpec((SUBLANES, TM_MIX), lambda i, *_: (0, i))],
        out_specs=pl.BlockSpec(memory_space=pl.ANY),
        scratch_shapes=[pltpu.VMEM((SLOTS * ROW_SLABS, LANES), F32), pltpu.VMEM((SLOTS * ROW_SLABS, LANES), F32),
                        pltpu.VMEM((BM * ROW_SLABS, LANES), F32),
                        pltpu.SemaphoreType.DMA((2,)), pltpu.SemaphoreType.DMA(())],
    )
    return pl.pallas_call(
        _dispatch_kernel,
        grid_spec=grid_spec,
        out_shape=jax.ShapeDtypeStruct((P_ROWS * ROW_SLABS, LANES), F32),
        compiler_params=pltpu.CompilerParams(dimension_semantics=("arbitrary",),
                                             vmem_limit_bytes=VMEM_LIMIT),
        name="dispatch",
    )(n_flat, run_rows, tail_start, tail_len, nused, hn2, ri)


def _experts_kernel(be_ref, first_ref, next_ref, next2_ref, stage_ref, nused_ref,
                    x_ref, wg_hbm, wu_hbm, wd_hbm, y_ref, wg_f, wu_f, wd_f, wg_b, wu_b, wd_b, sem):
    def weight_copies(e, st):
        return (pltpu.make_async_copy(wg_hbm.at[e], wg_f.at[st], sem.at[st, 0]),
                pltpu.make_async_copy(wu_hbm.at[e], wu_f.at[st], sem.at[st, 1]),
                pltpu.make_async_copy(wd_hbm.at[e], wd_f.at[st], sem.at[st, 2]))

    @pl.when((pl.program_id(0) == 0) & (nused_ref[0] > 0))
    def _():
        for c in weight_copies(be_ref[0], 0):
            c.start()

        @pl.when(next_ref[0] >= 0)
        def _():
            for c in weight_copies(next_ref[0], 1):
                c.start()

    for j in range(BLOCKS_PER_STEP):
        _expert_block(pl.program_id(0) * BLOCKS_PER_STEP + j, j * BM, be_ref, first_ref, next2_ref,
                      stage_ref, nused_ref, x_ref, y_ref, wg_f, wu_f, wd_f, wg_b, wu_b, wd_b, weight_copies)


def _expert_block(b, tok0, be_ref, first_ref, next2_ref, stage_ref, nused_ref, x_ref, y_ref,
                  wg_f, wu_f, wd_f, wg_b, wu_b, wd_b, weight_copies):
    @pl.when(b < nused_ref[0])
    def _():
        def new_expert(st):
            for c in weight_copies(be_ref[b], st):
                c.wait()
            wg_b[...] = wg_f[st].astype(BF16)
            wu_b[...] = wu_f[st].astype(BF16)
            wd_b[...] = wd_f[st].astype(BF16)

            @pl.when(next2_ref[b] >= 0)
            def _():
                for c in weight_copies(next2_ref[b], st):
                    c.start()

        for st in range(2):
            pl.when((first_ref[b] == 1) & (stage_ref[b] == st))(functools.partial(new_expert, st))

        x = _load_token_tiles(x_ref, BM, tok0).astype(BF16)
        a = _dot(x, wg_b[...])
        u = _dot(x, wu_b[...])
        y = _dot((_silu(a) * u).astype(BF16), wd_b[...])
        _store_token_tiles(y_ref, y, BM, tok0)

    @pl.when(b >= nused_ref[0])
    def _():
        y_ref[tok0 * ROW_SLABS:(tok0 + BM) * ROW_SLABS, :] = jnp.zeros((BM * ROW_SLABS, LANES), F32)


def _experts(block_e, first, next_e, next2_e, stage, nused, xb, wg, wu, wd):
    def xmap(b, *_):
        return (b, 0)

    grid_spec = pltpu.PrefetchScalarGridSpec(
        num_scalar_prefetch=6,
        grid=(N_BLOCKS // BLOCKS_PER_STEP,),
        in_specs=[
            pl.BlockSpec((BLOCKS_PER_STEP * BM * ROW_SLABS, LANES), xmap),
            pl.BlockSpec(memory_space=pl.ANY),
            pl.BlockSpec(memory_space=pl.ANY),
            pl.BlockSpec(memory_space=pl.ANY),
        ],
        out_specs=pl.BlockSpec((BLOCKS_PER_STEP * BM * ROW_SLABS, LANES), xmap),
        scratch_shapes=[pltpu.VMEM((2, D_MODEL, D_EXPERT), F32), pltpu.VMEM((2, D_MODEL, D_EXPERT), F32),
                        pltpu.VMEM((2, D_EXPERT, D_MODEL), F32),
                        pltpu.VMEM((D_MODEL, D_EXPERT), BF16), pltpu.VMEM((D_MODEL, D_EXPERT), BF16),
                        pltpu.VMEM((D_EXPERT, D_MODEL), BF16), pltpu.SemaphoreType.DMA((2, 3))],
    )
    return pl.pallas_call(
        _experts_kernel,
        grid_spec=grid_spec,
        out_shape=jax.ShapeDtypeStruct((P_ROWS * ROW_SLABS, LANES), F32),
        compiler_params=pltpu.CompilerParams(dimension_semantics=("arbitrary",),
                                             vmem_limit_bytes=VMEM_LIMIT),
        name="experts",
    )(block_e, first, next_e, next2_e, stage, nused, xb, wg, wu, wd)


def _combine_kernel(n_ref, row_ref, yb_ref, h_ref, q_ref, w_ref, g_ref, out_ref, yb0, yb1, sem):
    i = pl.program_id(0)
    bufs = (yb0, yb1)

    def start_gather(tile, s):
        _start_tile_runs(n_ref, row_ref, tile,
                         lambda loc, glob: pltpu.make_async_copy(yb_ref.at[glob], bufs[s].at[loc], sem.at[s]))

    @pl.when(i == 0)
    def _():
        start_gather(0, 0)

    def combine(s):
        @pl.when(i + 1 < pl.num_programs(0))
        def _():
            start_gather(i + 1, 1 - s)

        pltpu.make_async_copy(yb_ref.at[pl.ds(0, SLOTS * ROW_SLABS)], bufs[s], sem.at[s]).wait()
        ys = _load_token_tiles(bufs[s], SLOTS).astype(BF16)
        col = lax.broadcasted_iota(jnp.int32, (TM_MIX, SLOTS), 1)
        wmat = (jnp.where(col == q_ref[:, 0:1], w_ref[:, 0:1], 0.0)
                + jnp.where(col == q_ref[:, 1:2], w_ref[:, 1:2], 0.0)).astype(BF16)
        hh = h_ref[...] + _dot(wmat, ys)
        out_ref[...] = hh * lax.rsqrt(jnp.mean(hh * hh, axis=-1, keepdims=True) + EPS) * g_ref[...]

    for s in range(2):
        pl.when(i % 2 == s)(functools.partial(combine, s))


def _combine(n_flat, run_rows, yb, h1, qt, wts, gf):
    grid_spec = pltpu.PrefetchScalarGridSpec(
        num_scalar_prefetch=2,
        grid=(N_TILES,),
        in_specs=[
            pl.BlockSpec(memory_space=pl.ANY),
            pl.BlockSpec((TM_MIX, D_MODEL), lambda i, *_: (i, 0)),
            pl.BlockSpec((TM_MIX, TOP_K), lambda i, *_: (i, 0)),
            pl.BlockSpec((TM_MIX, TOP_K), lambda i, *_: (i, 0)),
            pl.BlockSpec((1, D_MODEL), lambda i, *_: (0, 0)),
        ],
        out_specs=pl.BlockSpec((TM_MIX, D_MODEL), lambda i, *_: (i, 0)),
        scratch_shapes=[pltpu.VMEM((SLOTS * ROW_SLABS, LANES), F32), pltpu.VMEM((SLOTS * ROW_SLABS, LANES), F32),
                        pltpu.SemaphoreType.DMA((2,))],
    )
    return pl.pallas_call(
        _combine_kernel,
        grid_spec=grid_spec,
        out_shape=jax.ShapeDtypeStruct((SEQ, D_MODEL), F32),
        compiler_params=pltpu.CompilerParams(dimension_semantics=("arbitrary",),
                                             vmem_limit_bytes=VMEM_LIMIT),
        name="combine",
    )(n_flat, run_rows, yb, h1, qt, wts, gf)


def kernel(x, meta, norm1_g, w_in, w_decay_up, b_decay, gla_norm_g, conv_w, conv_b, conv_ln_g,
           conv_ln_b, w_pw2, b_gate, w_out, norm2_g, w_router_group, b_router_group,
           w_router_expert, b_router_expert, w_exp_gate, w_exp_up, w_exp_down, final_norm_g):
    assert x.shape == (1, SEQ, D_MODEL) and w_in.shape[0] == 1
    x2d = x[0]
    front = jnp.concatenate([jnp.zeros((FRONT - N_META, D_MODEL), F32), meta.astype(F32)], axis=0)

    w = w_in[0]
    o_q, o_k, o_v, o_og, o_dl, o_glu, o_gt = (0, QK_W, 2 * QK_W, 2 * QK_W + V_W, 2 * QK_W + 2 * V_W,
                                              2 * QK_W + 2 * V_W + GLA_RANK,
                                              2 * QK_W + 2 * V_W + GLA_RANK + 2 * D_MODEL)
    cols = lambda a, b: w[:, a:b].astype(BF16)
    row2 = lambda a: a.reshape(1, -1).astype(F32)
    q, k, v, og, la, u, gates = _inproj(
        front, x2d, row2(norm1_g[0]), cols(o_q, o_k), cols(o_k, o_v), cols(o_v, o_og),
        cols(o_og, o_dl), cols(o_dl, o_glu), cols(o_glu, o_gt), cols(o_gt, w.shape[1]),
        w_decay_up[0].astype(F32), row2(b_decay[0]), row2(b_gate[0]))

    o_n = _gla(q, k, v, la, row2(gla_norm_g[0]))

    wr = jnp.concatenate([w_router_group[0].T, w_router_expert[0].T,
                          jnp.zeros((LANES - N_GROUPS - N_EXPERTS, D_MODEL), F32)], axis=0).astype(F32)
    br = jnp.concatenate([b_router_group[0], b_router_expert[0],
                          jnp.zeros((LANES - N_GROUPS - N_EXPERTS,), F32)]).reshape(LANES, 1).astype(F32)
    wr_hi = wr.astype(BF16)
    wr_lo = (wr - wr_hi.astype(F32)).astype(BF16)
    h1, hn2, ri, rw, tabn, tabb = _mix(
        o_n, og, u, gates, x2d, conv_w[0].astype(F32), row2(conv_b[0]), row2(conv_ln_g[0]),
        row2(conv_ln_b[0]), w_pw2[0].astype(BF16), w_out[0].astype(BF16),
        row2(norm2_g[0]), wr_hi, wr_lo, br)

    experts = slice(N_GROUPS, N_GROUPS + N_EXPERTS)
    n_te = tabn[::SUBLANES, experts].astype(jnp.int32)
    seen_te = tabb[::SUBLANES, experts].astype(jnp.int32)
    counts = seen_te[-1] + n_te[-1]
    padded = (counts + BM - 1) // BM * BM
    pad_end = jnp.cumsum(padded)
    pad_start = pad_end - padded
    n_flat = n_te.reshape(-1)
    run_rows = (pad_start[None, :] + seen_te).reshape(-1).astype(jnp.int32)
    tail_start = (pad_start + counts).astype(jnp.int32)
    tail_len = (padded - counts).astype(jnp.int32)
    blk = jnp.arange(N_BLOCKS, dtype=jnp.int32)
    block_e = jnp.minimum(jnp.sum((pad_end[None, :] <= blk[:, None] * BM).astype(jnp.int32), axis=1),
                          N_EXPERTS - 1).astype(jnp.int32)
    first = jnp.concatenate([jnp.ones((1,), jnp.int32), (block_e[1:] != block_e[:-1]).astype(jnp.int32)])
    nused = (pad_end[-1:] // BM).astype(jnp.int32)
    eid = jnp.arange(N_EXPERTS, dtype=jnp.int32)
    later = jnp.flip(lax.cummin(jnp.flip(jnp.where(counts > 0, eid, N_EXPERTS))))
    nxt = jnp.concatenate([later[1:], jnp.full((2,), N_EXPERTS, jnp.int32)])
    nxt2 = nxt[nxt[:N_EXPERTS]]
    or_none = lambda t: jnp.where(t < N_EXPERTS, t, -1)[block_e].astype(jnp.int32)
    next_e, next2_e = or_none(nxt[:N_EXPERTS]), or_none(nxt2)
    stage = ((jnp.cumsum(first) - 1) % 2).astype(jnp.int32)

    xb = _dispatch(n_flat, run_rows, tail_start, tail_len, nused, hn2, ri)
    yb = _experts(block_e, first, next_e, next2_e, stage, nused, xb, w_exp_gate[0], w_exp_up[0], w_exp_down[0])
    out = _combine(n_flat, run_rows, yb, h1, ri[0:TOP_K].T, rw[0:TOP_K].T, row2(final_norm_g))
    return out[None]
```

```python
import functools

import jax
import jax.numpy as jnp
import numpy as np
from jax import lax
from jax.experimental import pallas as pl
from jax.experimental.pallas import tpu as pltpu

F32 = jnp.float32
BF16 = jnp.bfloat16
HIGHEST = lax.Precision.HIGHEST

D_MODEL = 1024
SEQ = 16384
N_META = 16
GLA_HEADS = 4
GLA_DK = 128
GLA_DV = 256
GLA_RANK = 16
GLA_TAU = 16.0
CHUNK = 64
QK_W = GLA_HEADS * GLA_DK
V_W = GLA_HEADS * GLA_DV
CONV_WIDTH = 31
N_GROUPS = 8
EXPERTS_PER_GROUP = 8
N_EXPERTS = N_GROUPS * EXPERTS_PER_GROUP
TOP_K = 2
D_EXPERT = 512
EPS = 1e-6
LOG2E = 1.4426950408889634

LANES = 128
SUBLANES = 8
ROW_SLABS = D_MODEL // LANES

FRONT = 512
LP = FRONT + SEQ
TM_PROJ = 512
TM_GLA = 512
TM_MIX = 256
HIST = 32
SH_ROWS = TM_MIX + HIST - SUBLANES
CONV_RB = 64
N_TILES = SEQ // TM_MIX
SLOTS = TOP_K * TM_MIX
TILES_PER_STEP = 2
BM = 256
BLOCKS_PER_STEP = 4
N_BLOCKS = (SEQ * TOP_K) // BM + N_EXPERTS
P_ROWS = N_BLOCKS * BM
VMEM_LIMIT = 56 * 1024 * 1024


def _sigmoid(x):
    return 0.5 * jnp.tanh(0.5 * x) + 0.5


def _silu(x):
    return x * _sigmoid(x)


def _dot(a, b, **kw):
    return jnp.dot(a, b, preferred_element_type=F32, **kw)


def _load_token_tiles(ref, n_tok, first_tok=0):
    r0 = first_tok * ROW_SLABS
    return jnp.concatenate([ref[pl.ds(r0 + s, n_tok, stride=ROW_SLABS), :] for s in range(ROW_SLABS)], axis=-1)


def _store_token_tiles(ref, val, n_tok, first_tok=0):
    r0 = first_tok * ROW_SLABS
    for s in range(ROW_SLABS):
        ref[pl.ds(r0 + s, n_tok, stride=ROW_SLABS), :] = val[:, s * LANES:(s + 1) * LANES]


def _inproj_kernel(front_ref, x_ref, g_ref, wq, wk, wv, wog, wdl, wglu, wgt, wdu, bd, bg,
                   q_o, k_o, v_o, og_o, la_o, u_o, gt_o):
    i = pl.program_id(0)
    h = jnp.where(i < FRONT // TM_PROJ, front_ref[...], x_ref[...])
    ms = jnp.mean(h * h, axis=-1, keepdims=True)
    hn = (h * lax.rsqrt(ms + EPS) * g_ref[...]).astype(BF16)
    q_o[...] = _dot(hn, wq[...]).astype(BF16)
    k_o[...] = _dot(hn, wk[...]).astype(BF16)
    v_o[...] = _dot(hn, wv[...]).astype(BF16)
    og_o[...] = _silu(_dot(hn, wog[...])).astype(BF16)
    glu = _dot(hn, wglu[...])
    u_o[...] = (glu[:, 0:D_MODEL] * _sigmoid(glu[:, D_MODEL:2 * D_MODEL])).astype(BF16)
    gt_o[...] = _sigmoid(_dot(hn, wgt[...]) + bg[...]).astype(BF16)
    dlow = _dot(hn, wdl[...])
    z = _dot(dlow, wdu[...], precision=HIGHEST) + bd[...]
    la_o[...] = (jnp.minimum(z, 0.0) - jnp.log(1.0 + jnp.exp(-jnp.abs(z)))) * (LOG2E / GLA_TAU)


def _inproj(front, x2d, g1, wq, wk, wv, wog, wdl, wglu, wgt, wdu, bd, bg):
    n_front = FRONT // TM_PROJ
    grid = (LP // TM_PROJ,)
    row = lambda w: pl.BlockSpec((TM_PROJ, w), lambda i: (i, 0))
    full = lambda a: pl.BlockSpec(a.shape, lambda i: (0, 0))
    return pl.pallas_call(
        _inproj_kernel,
        grid=grid,
        in_specs=[
            pl.BlockSpec((TM_PROJ, D_MODEL), lambda i: (jnp.minimum(i, n_front - 1), 0)),
            pl.BlockSpec((TM_PROJ, D_MODEL), lambda i: (jnp.maximum(i - n_front, 0), 0)),
            full(g1), full(wq), full(wk), full(wv), full(wog), full(wdl), full(wglu), full(wgt),
            full(wdu), full(bd), full(bg),
        ],
        out_specs=[row(QK_W), row(QK_W), row(V_W), row(D_MODEL), row(QK_W), row(D_MODEL),
                   row(2 * D_MODEL)],
        out_shape=[
            jax.ShapeDtypeStruct((LP, QK_W), BF16), jax.ShapeDtypeStruct((LP, QK_W), BF16),
            jax.ShapeDtypeStruct((LP, V_W), BF16), jax.ShapeDtypeStruct((LP, D_MODEL), BF16),
            jax.ShapeDtypeStruct((LP, QK_W), F32), jax.ShapeDtypeStruct((LP, D_MODEL), BF16),
            jax.ShapeDtypeStruct((LP, 2 * D_MODEL), BF16),
        ],
        compiler_params=pltpu.CompilerParams(dimension_semantics=("arbitrary",),
                                             vmem_limit_bytes=VMEM_LIMIT),
        name="inproj",
    )(front, x2d, g1, wq, wk, wv, wog, wdl, wglu, wgt, wdu, bd, bg)


GLA_LEVELS = (32, 16, 8, 4, 2, 1)
GLA_FINE = tuple(m for m in GLA_LEVELS if m < SUBLANES)
LA_SPLIT = 3


def _gla_tables():
    r = np.arange(CHUNK)
    t, c = r[:, None], r[None, :]
    rows = [c <= t]
    pair = []
    for m in GLA_LEVELS:
        mid = (t // (2 * m)) * (2 * m) + m - 1
        right = (t % (2 * m)) >= m
        if m in GLA_FINE:
            rows.append(np.where(right, (c > mid) & (c <= t), (c > t) & (c <= mid)))
        pair.append(((t // (2 * m)) == (c // (2 * m))) & right & ((c % (2 * m)) < m))
    pair.append(t == c)
    sums = np.concatenate(rows, axis=0).astype(np.float32)
    sums = np.concatenate([sums] * LA_SPLIT, axis=1)
    return jnp.asarray(sums, BF16), jnp.asarray(np.stack(pair).astype(np.float32))


def _split_bf16(x):
    pieces = []
    rest = x
    for _ in range(LA_SPLIT):
        p = rest.astype(BF16)
        pieces.append(p)
        rest = rest - p.astype(F32)
    return jnp.concatenate(pieces, axis=0)


def _dot_nt(a, b):
    return lax.dot_general(a, b, (((1,), (1,)), ((), ())), preferred_element_type=F32)


def _dot_tn(a, b):
    return lax.dot_general(a, b, (((0,), (0,)), ((), ())), preferred_element_type=F32)


def _gla_exponents(a):
    b = a[0:CHUNK]
    parts = [b, b[CHUNK - 1:CHUNK] - b]
    for m in GLA_LEVELS:
        if m in GLA_FINE:
            continue
        for lo in range(0, CHUNK, 2 * m):
            ref = b[lo + m - 1:lo + m]
            parts += [ref - b[lo:lo + m], b[lo + m:lo + 2 * m] - ref]
    return jnp.concatenate(parts + [a[CHUNK:]], axis=0)


def _gla_chunk(q, k, vb, e, dec, s_prev, pair_ref, right_rows):
    qe = (q * e[0:CHUNK]).astype(BF16)
    kd = (k * e[CHUNK:2 * CHUNK]).astype(BF16)
    s_new = s_prev * jnp.concatenate([dec] * (GLA_DV // LANES), axis=1) + _dot_tn(kd, vb)

    sc = _dot_nt(q.astype(BF16), k.astype(BF16)) * pair_ref[len(GLA_LEVELS)]
    for lvl in range(len(GLA_LEVELS)):
        rows = e[(2 + lvl) * CHUNK:(3 + lvl) * CHUNK]
        x = (jnp.where(right_rows[lvl], q, k) * rows).astype(BF16)
        sc = sc + _dot_nt(x, x) * pair_ref[lvl]
    o = _dot(jnp.concatenate([qe, sc.astype(BF16)], axis=1),
             jnp.concatenate([s_prev.astype(BF16), vb], axis=0))
    return o, s_new


def _gla_kernel(q_ref, k_ref, v_ref, la_ref, g_ref, sums_ref, pair_ref, o_ref, s_ref):
    @pl.when(pl.program_id(0) == 0)
    def _():
        s_ref[...] = jnp.zeros_like(s_ref)

    row = lax.broadcasted_iota(jnp.int32, (CHUNK, GLA_DK), 0)
    right_rows = [(row & m) != 0 for m in GLA_LEVELS]
    ones_col = jnp.ones((LA_SPLIT * CHUNK, LANES), BF16)
    sums = sums_ref[...]

    def body(c, carry):
        r0 = pl.multiple_of(c * CHUNK, CHUNK)
        rows = pl.ds(r0, CHUNK)
        states = [s_ref[h] for h in range(GLA_HEADS)]
        new_states = []
        for hp in range(GLA_HEADS // 2):
            la3 = _split_bf16(la_ref[rows, 2 * hp * GLA_DK:2 * (hp + 1) * GLA_DK])
            e2 = jnp.exp2(_gla_exponents(_dot(sums, la3)))
            dec2 = jnp.exp2(_dot_tn(la3, ones_col))
            for h in (2 * hp, 2 * hp + 1):
                half = slice((h % 2) * GLA_DK, (h % 2 + 1) * GLA_DK)
                kc = slice(h * GLA_DK, (h + 1) * GLA_DK)
                vc = slice(h * GLA_DV, (h + 1) * GLA_DV)
                q = q_ref[rows, kc].astype(F32) * (GLA_DK ** -0.5)
                k = k_ref[rows, kc].astype(F32)
                o, s_new = _gla_chunk(q, k, v_ref[rows, vc], e2[:, half], dec2[half], states[h],
                                      pair_ref, right_rows)
                new_states.append(s_new)
                o = o * lax.rsqrt(jnp.mean(o * o, axis=-1, keepdims=True) + EPS) * g_ref[:, vc]
                o_ref[rows, vc] = o.astype(BF16)
        for h in range(GLA_HEADS):
            s_ref[h] = new_states[h]
        return carry

    lax.fori_loop(0, TM_GLA // CHUNK, body, 0, unroll=4)


def _gla(q, k, v, la, g):
    sums, pair = _gla_tables()
    row = lambda w: pl.BlockSpec((TM_GLA, w), lambda t: (t, 0))
    full = lambda a: pl.BlockSpec(a.shape, lambda t: (0,) * a.ndim)
    return pl.pallas_call(
        _gla_kernel,
        grid=(LP // TM_GLA,),
        in_specs=[row(QK_W), row(QK_W), row(V_W), row(QK_W), full(g), full(sums), full(pair)],
        out_specs=row(V_W),
        out_shape=jax.ShapeDtypeStruct((LP, V_W), BF16),
        scratch_shapes=[pltpu.VMEM((GLA_HEADS, GLA_DK, GLA_DV), F32)],
        compiler_params=pltpu.CompilerParams(dimension_semantics=("arbitrary",),
                                             vmem_limit_bytes=VMEM_LIMIT),
        name="gla",
    )(q, k, v, la, g, sums, pair)


def _mix_kernel(o_ref, og_ref, u_ref, hist_ref, gt_ref, x_ref, cw_ref, cb_ref, lg_ref, lb_ref,
                wpw_ref, wo_ref, g2_ref, wrh_ref, wrl_ref, br_ref,
                h_o, hn_o, ri_o, rw_o, tabn_o, tabb_o, ubuf, shbuf, cbuf, seen):
    i = pl.program_id(0)

    @pl.when(i == 0)
    def _():
        seen[...] = jnp.zeros_like(seen)

    ubuf[0:HIST, :] = hist_ref[...].astype(F32)
    ubuf[HIST:HIST + TM_MIX, :] = u_ref[...].astype(F32)

    lead = HIST - (CONV_WIDTH - 1)
    for rho in range(1, SUBLANES):
        shbuf[rho - 1] = ubuf[rho:rho + SH_ROWS, :]
    for cb in range(D_MODEL // LANES):
        lanes = slice(cb * LANES, (cb + 1) * LANES)
        for rb in range(TM_MIX // CONV_RB):
            part = jnp.broadcast_to(cb_ref[:, lanes], (CONV_RB, LANES))
            for j in range(CONV_WIDTH):
                rho = (lead + j) % SUBLANES
                r0 = rb * CONV_RB + (lead + j) - rho
                src = ubuf[r0:r0 + CONV_RB, lanes] if rho == 0 else shbuf[rho - 1, r0:r0 + CONV_RB, lanes]
                part = part + cw_ref[j:j + 1, lanes] * src
            cbuf[rb * CONV_RB:(rb + 1) * CONV_RB, lanes] = part
    acc = cbuf[...]

    mu = jnp.mean(acc, axis=-1, keepdims=True)
    xc = acc - mu
    ln = xc * lax.rsqrt(jnp.mean(xc * xc, axis=-1, keepdims=True) + EPS) * lg_ref[...] + lb_ref[...]
    branch_b = _dot(_silu(ln).astype(BF16), wpw_ref[...])

    branch_a = o_ref[...].astype(F32) * og_ref[...].astype(F32)
    g_a = gt_ref[:, 0:D_MODEL].astype(F32)
    g_b = gt_ref[:, D_MODEL:2 * D_MODEL].astype(F32)
    merged = (g_a * branch_a + g_b * branch_b).astype(BF16)
    h1 = x_ref[...] + _dot(merged, wo_ref[...])
    h_o[...] = h1

    hn2 = h1 * lax.rsqrt(jnp.mean(h1 * h1, axis=-1, keepdims=True) + EPS) * g2_ref[...]
    hn_hi = hn2.astype(BF16)
    hn_o[...] = hn_hi

    hn_lo = (hn2 - hn_hi.astype(F32)).astype(BF16)
    logits = (_dot_nt(wrh_ref[...], hn_hi) + _dot_nt(wrh_ref[...], hn_lo) + _dot_nt(wrl_ref[...], hn_hi)
              + br_ref[...])
    row = lax.broadcasted_iota(jnp.int32, logits.shape, 0)
    rowf = row.astype(F32)
    neg = -jnp.inf
    is_g = row < N_GROUPS
    lg = jnp.where(is_g, logits, neg)
    gmax = jnp.max(lg, axis=0, keepdims=True)
    gidx = jnp.min(jnp.where(lg == gmax, rowf, float(N_GROUPS)), axis=0, keepdims=True)
    g_w = 1.0 / jnp.sum(jnp.where(is_g, jnp.exp(lg - gmax), 0.0), axis=0, keepdims=True)
    erow = rowf - float(N_GROUPS)
    egrp = ((row - N_GROUPS) >> 3).astype(F32)
    in_grp = (row >= N_GROUPS) & (row < N_GROUPS + N_EXPERTS) & (egrp == gidx)
    le = jnp.where(in_grp, logits, neg)
    m1 = jnp.max(le, axis=0, keepdims=True)
    i1 = jnp.min(jnp.where(le == m1, erow, float(N_EXPERTS)), axis=0, keepdims=True)
    le2 = jnp.where(erow == i1, neg, le)
    m2 = jnp.max(le2, axis=0, keepdims=True)
    i2 = jnp.min(jnp.where(le2 == m2, erow, float(N_EXPERTS)), axis=0, keepdims=True)
    t = jnp.exp(m2 - m1)
    w1 = g_w / (1.0 + t)
    w2 = g_w * t / (1.0 + t)

    oh1 = erow == i1
    oh2 = erow == i2
    oh = jnp.where(oh1 | oh2, 1.0, 0.0).astype(BF16)
    tr = lax.broadcasted_iota(jnp.int32, (TM_MIX, TM_MIX), 0)
    tc = lax.broadcasted_iota(jnp.int32, (TM_MIX, TM_MIX), 1)
    before_tok = _dot(oh, (tr < tc).astype(BF16))
    n_col = jnp.sum(oh.astype(F32), axis=1, keepdims=True)
    er = lax.broadcasted_iota(jnp.int32, (LANES, LANES), 0)
    ec = lax.broadcasted_iota(jnp.int32, (LANES, LANES), 1)
    before_exp = _dot((ec < er).astype(BF16), jnp.broadcast_to(n_col, logits.shape).astype(BF16))
    where_to = before_exp + before_tok
    q1 = jnp.sum(jnp.where(oh1, where_to, 0.0), axis=0, keepdims=True)
    q2 = jnp.sum(jnp.where(oh2, where_to, 0.0), axis=0, keepdims=True)

    n_rows = _dot_nt(jnp.ones((SUBLANES, TM_MIX), BF16), oh)
    tabn_o[...] = n_rows
    tabb_o[...] = seen[...]
    seen[...] = seen[...] + n_rows

    ri_o[...] = jnp.zeros_like(ri_o)
    ri_o[0:1, :] = q1.astype(jnp.int32)
    ri_o[1:2, :] = q2.astype(jnp.int32)
    rw_o[...] = jnp.zeros_like(rw_o)
    rw_o[0:1, :] = w1
    rw_o[1:2, :] = w2


def _mix(o_n, og, u, gates, x2d, cw, cb, lg, lb, wpw, wo, g2, wrh, wrl, br):
    n_front = FRONT // TM_MIX
    grid = (SEQ // TM_MIX,)
    rowp = lambda w: pl.BlockSpec((TM_MIX, w), lambda i: (i + n_front, 0))
    full = lambda a: pl.BlockSpec(a.shape, lambda i: (0,) * a.ndim)
    hist_blocks = TM_MIX // HIST
    return pl.pallas_call(
        _mix_kernel,
        grid=grid,
        in_specs=[
            rowp(V_W), rowp(D_MODEL), rowp(D_MODEL),
            pl.BlockSpec((HIST, D_MODEL), lambda i: ((i + n_front) * hist_blocks - 1, 0)),
            rowp(2 * D_MODEL),
            pl.BlockSpec((TM_MIX, D_MODEL), lambda i: (i, 0)),
            full(cw), full(cb), full(lg), full(lb), full(wpw), full(wo), full(g2),
            full(wrh), full(wrl), full(br),
        ],
        out_specs=[
            pl.BlockSpec((TM_MIX, D_MODEL), lambda i: (i, 0)),
            pl.BlockSpec((TM_MIX, D_MODEL), lambda i: (i, 0)),
            pl.BlockSpec((SUBLANES, TM_MIX), lambda i: (0, i)),
            pl.BlockSpec((SUBLANES, TM_MIX), lambda i: (0, i)),
            pl.BlockSpec((SUBLANES, LANES), lambda i: (i, 0)),
            pl.BlockSpec((SUBLANES, LANES), lambda i: (i, 0)),
        ],
        out_shape=[
            jax.ShapeDtypeStruct((SEQ, D_MODEL), F32),
            jax.ShapeDtypeStruct((SEQ, D_MODEL), BF16),
            jax.ShapeDtypeStruct((SUBLANES, SEQ), jnp.int32),
            jax.ShapeDtypeStruct((SUBLANES, SEQ), F32),
            jax.ShapeDtypeStruct((N_TILES * SUBLANES, LANES), F32),
            jax.ShapeDtypeStruct((N_TILES * SUBLANES, LANES), F32),
        ],
        scratch_shapes=[pltpu.VMEM((HIST + TM_MIX, D_MODEL), F32),
                        pltpu.VMEM((SUBLANES - 1, SH_ROWS, D_MODEL), F32),
                        pltpu.VMEM((TM_MIX, D_MODEL), F32), pltpu.VMEM((SUBLANES, LANES), F32)],
        compiler_params=pltpu.CompilerParams(dimension_semantics=("arbitrary",),
                                             vmem_limit_bytes=VMEM_LIMIT),
        name="mix",
    )(o_n, og, u, u, gates, x2d, cw, cb, lg, lb, wpw, wo, g2, wrh, wrl, br)


def _tok_rows(start_tok, n_tok):
    return pl.ds(pl.multiple_of(start_tok * ROW_SLABS, ROW_SLABS), n_tok * ROW_SLABS)


def _start_tile_runs(n_ref, row_ref, tile, make_copy):
    def run(e, off):
        n = n_ref[tile * N_EXPERTS + e]

        @pl.when(n > 0)
        def _():
            make_copy(_tok_rows(off, n), _tok_rows(row_ref[tile * N_EXPERTS + e], n)).start()

        return off + n

    lax.fori_loop(0, N_EXPERTS, run, 0, unroll=8)


def _dispatch_kernel(n_ref, row_ref, ts_ref, tl_ref, nused_ref, hn_ref, q_ref, xb_ref,
                     sb0, sb1, zbuf, sem, fsem):
    step = pl.program_id(0)
    last = pl.num_programs(0) - 1

    def fills(start):
        def tail(e, c):
            n = tl_ref[e]

            @pl.when(n > 0)
            def _():
                cp = pltpu.make_async_copy(zbuf.at[pl.ds(0, n * ROW_SLABS)],
                                           xb_ref.at[_tok_rows(ts_ref[e], n)], fsem)
                cp.start() if start else cp.wait()

            return c

        lax.fori_loop(0, N_EXPERTS, tail, 0)

        def block(b, c):
            cp = pltpu.make_async_copy(zbuf, xb_ref.at[_tok_rows(b * BM, BM)], fsem)
            cp.start() if start else cp.wait()
            return c

        lax.fori_loop(nused_ref[0], N_BLOCKS, block, 0)

    @pl.when(step == 0)
    def _():
        zbuf[...] = jnp.zeros_like(zbuf)
        fills(start=True)

    def whole(s, sbuf):
        return pltpu.make_async_copy(sbuf, xb_ref.at[pl.ds(0, SLOTS * ROW_SLABS)], sem.at[s])

    for s, sbuf in enumerate((sb0, sb1)):
        toks = slice(s * TM_MIX, (s + 1) * TM_MIX)
        slot_i = lax.broadcasted_iota(jnp.int32, (SLOTS, TM_MIX), 0)
        onehot = jnp.where((slot_i == q_ref[0:1, toks]) | (slot_i == q_ref[1:2, toks]), 1.0, 0.0).astype(BF16)
        srt = _dot(onehot, hn_ref[toks, :])

        @pl.when(step >= 1)
        def _():
            whole(s, sbuf).wait()

        _store_token_tiles(sbuf, srt, SLOTS)
        _start_tile_runs(n_ref, row_ref, step * TILES_PER_STEP + s,
                         lambda loc, glob: pltpu.make_async_copy(sbuf.at[loc], xb_ref.at[glob], sem.at[s]))

    @pl.when(step == last)
    def _():
        for s, sbuf in enumerate((sb0, sb1)):
            whole(s, sbuf).wait()
        fills(start=False)


def _dispatch(n_flat, run_rows, tail_start, tail_len, nused, hn2, ri):
    grid_spec = pltpu.PrefetchScalarGridSpec(
        num_scalar_prefetch=5,
        grid=(N_TILES // TILES_PER_STEP,),
        in_specs=[pl.BlockSpec((TILES_PER_STEP * TM_MIX, D_MODEL), lambda i, *_: (i, 0)),
                  pl.BlockSpec((SUBLANES, TILES_PER_STEP * TM_MIX), lambda i, *_: (0, i))],
        out_specs=pl.BlockSpec(memory_space=pl.ANY),
        scratch_shapes=[pltpu.VMEM((SLOTS * ROW_SLABS, LANES), F32), pltpu.VMEM((SLOTS * ROW_SLABS, LANES), F32),
                        pltpu.VMEM((BM * ROW_SLABS, LANES), F32),
                        pltpu.SemaphoreType.DMA((2,)), pltpu.SemaphoreType.DMA(())],
    )
    return pl.pallas_call(
        _dispatch_kernel,
        grid_spec=grid_spec,
        out_shape=jax.ShapeDtypeStruct((P_ROWS * ROW_SLABS, LANES), F32),
        compiler_params=pltpu.CompilerParams(dimension_semantics=("arbitrary",),
                                             vmem_limit_bytes=VMEM_LIMIT),
        name="dispatch",
    )(n_flat, run_rows, tail_start, tail_len, nused, hn2, ri)


def _experts_kernel(be_ref, first_ref, next_ref, next2_ref, stage_ref, nused_ref,
                    x_ref, wg_hbm, wu_hbm, wd_hbm, y_ref, wg_f, wu_f, wd_f, wg_b, wu_b, wd_b, sem):
    def weight_copies(e, st):
        return (pltpu.make_async_copy(wg_hbm.at[e], wg_f.at[st], sem.at[st, 0]),
                pltpu.make_async_copy(wu_hbm.at[e], wu_f.at[st], sem.at[st, 1]),
                pltpu.make_async_copy(wd_hbm.at[e], wd_f.at[st], sem.at[st, 2]))

    @pl.when((pl.program_id(0) == 0) & (nused_ref[0] > 0))
    def _():
        for c in weight_copies(be_ref[0], 0):
            c.start()

        @pl.when(next_ref[0] >= 0)
        def _():
            for c in weight_copies(next_ref[0], 1):
                c.start()

    for j in range(BLOCKS_PER_STEP):
        _expert_block(pl.program_id(0) * BLOCKS_PER_STEP + j, j * BM, be_ref, first_ref, next2_ref,
                      stage_ref, nused_ref, x_ref, y_ref, wg_f, wu_f, wd_f, wg_b, wu_b, wd_b, weight_copies)


def _expert_block(b, tok0, be_ref, first_ref, next2_ref, stage_ref, nused_ref, x_ref, y_ref,
                  wg_f, wu_f, wd_f, wg_b, wu_b, wd_b, weight_copies):
    @pl.when(b < nused_ref[0])
    def _():
        def new_expert(st):
            for c in weight_copies(be_ref[b], st):
                c.wait()
            wg_b[...] = wg_f[st].astype(BF16)
            wu_b[...] = wu_f[st].astype(BF16)
            wd_b[...] = wd_f[st].astype(BF16)

            @pl.when(next2_ref[b] >= 0)
            def _():
                for c in weight_copies(next2_ref[b], st):
                    c.start()

        for st in range(2):
            pl.when((first_ref[b] == 1) & (stage_ref[b] == st))(functools.partial(new_expert, st))

        x = _load_token_tiles(x_ref, BM, tok0).astype(BF16)
        a = _dot(x, wg_b[...])
        u = _dot(x, wu_b[...])
        y = _dot((_silu(a) * u).astype(BF16), wd_b[...])
        _store_token_tiles(y_ref, y, BM, tok0)

    @pl.when(b >= nused_ref[0])
    def _():
        y_ref[tok0 * ROW_SLABS:(tok0 + BM) * ROW_SLABS, :] = jnp.zeros((BM * ROW_SLABS, LANES), F32)


def _experts(block_e, first, next_e, next2_e, stage, nused, xb, wg, wu, wd):
    def xmap(b, *_):
        return (b, 0)

    grid_spec = pltpu.PrefetchScalarGridSpec(
        num_scalar_prefetch=6,
        grid=(N_BLOCKS // BLOCKS_PER_STEP,),
        in_specs=[
            pl.BlockSpec((BLOCKS_PER_STEP * BM * ROW_SLABS, LANES), xmap),
            pl.BlockSpec(memory_space=pl.ANY),
            pl.BlockSpec(memory_space=pl.ANY),
            pl.BlockSpec(memory_space=pl.ANY),
        ],
        out_specs=pl.BlockSpec((BLOCKS_PER_STEP * BM * ROW_SLABS, LANES), xmap),
        scratch_shapes=[pltpu.VMEM((2, D_MODEL, D_EXPERT), F32), pltpu.VMEM((2, D_MODEL, D_EXPERT), F32),
                        pltpu.VMEM((2, D_EXPERT, D_MODEL), F32),
                        pltpu.VMEM((D_MODEL, D_EXPERT), BF16), pltpu.VMEM((D_MODEL, D_EXPERT), BF16),
                        pltpu.VMEM((D_EXPERT, D_MODEL), BF16), pltpu.SemaphoreType.DMA((2, 3))],
    )
    return pl.pallas_call(
        _experts_kernel,
        grid_spec=grid_spec,
        out_shape=jax.ShapeDtypeStruct((P_ROWS * ROW_SLABS, LANES), F32),
        compiler_params=pltpu.CompilerParams(dimension_semantics=("arbitrary",),
                                             vmem_limit_bytes=VMEM_LIMIT),
        name="experts",
    )(block_e, first, next_e, next2_e, stage, nused, xb, wg, wu, wd)


def _combine_kernel(n_ref, row_ref, yb_ref, h_ref, q_ref, w_ref, g_ref, out_ref, yb0, yb1, sem):
    step = pl.program_id(0)
    bufs = (yb0, yb1)

    def start_gather(tile, s):
        _start_tile_runs(n_ref, row_ref, tile,
                         lambda loc, glob: pltpu.make_async_copy(yb_ref.at[glob], bufs[s].at[loc], sem.at[s]))

    @pl.when(step == 0)
    def _():
        start_gather(0, 0)

    for s in range(TILES_PER_STEP):
        tile = step * TILES_PER_STEP + s
        if s + 1 < TILES_PER_STEP:
            start_gather(tile + 1, s + 1)
        else:
            pl.when(step + 1 < pl.num_programs(0))(functools.partial(start_gather, tile + 1, 0))

        pltpu.make_async_copy(yb_ref.at[pl.ds(0, SLOTS * ROW_SLABS)], bufs[s], sem.at[s]).wait()
        ys = _load_token_tiles(bufs[s], SLOTS).astype(BF16)
        toks = slice(s * TM_MIX, (s + 1) * TM_MIX)
        col = lax.broadcasted_iota(jnp.int32, (TM_MIX, SLOTS), 1)
        wmat = (jnp.where(col == q_ref[toks, 0:1], w_ref[toks, 0:1], 0.0)
                + jnp.where(col == q_ref[toks, 1:2], w_ref[toks, 1:2], 0.0)).astype(BF16)
        hh = h_ref[toks, :] + _dot(wmat, ys)
        out_ref[toks, :] = hh * lax.rsqrt(jnp.mean(hh * hh, axis=-1, keepdims=True) + EPS) * g_ref[...]


def _combine(n_flat, run_rows, yb, h1, qt, wts, gf):
    grid_spec = pltpu.PrefetchScalarGridSpec(
        num_scalar_prefetch=2,
        grid=(N_TILES // TILES_PER_STEP,),
        in_specs=[
            pl.BlockSpec(memory_space=pl.ANY),
            pl.BlockSpec((TILES_PER_STEP * TM_MIX, D_MODEL), lambda i, *_: (i, 0)),
            pl.BlockSpec((TILES_PER_STEP * TM_MIX, TOP_K), lambda i, *_: (i, 0)),
            pl.BlockSpec((TILES_PER_STEP * TM_MIX, TOP_K), lambda i, *_: (i, 0)),
            pl.BlockSpec((1, D_MODEL), lambda i, *_: (0, 0)),
        ],
        out_specs=pl.BlockSpec((TILES_PER_STEP * TM_MIX, D_MODEL), lambda i, *_: (i, 0)),
        scratch_shapes=[pltpu.VMEM((SLOTS * ROW_SLABS, LANES), F32), pltpu.VMEM((SLOTS * ROW_SLABS, LANES), F32),
                        pltpu.SemaphoreType.DMA((2,))],
    )
    return pl.pallas_call(
        _combine_kernel,
        grid_spec=grid_spec,
        out_shape=jax.ShapeDtypeStruct((SEQ, D_MODEL), F32),
        compiler_params=pltpu.CompilerParams(dimension_semantics=("arbitrary",),
                                             vmem_limit_bytes=VMEM_LIMIT),
        name="combine",
    )(n_flat, run_rows, yb, h1, qt, wts, gf)


def kernel(x, meta, norm1_g, w_in, w_decay_up, b_decay, gla_norm_g, conv_w, conv_b, conv_ln_g,
           conv_ln_b, w_pw2, b_gate, w_out, norm2_g, w_router_group, b_router_group,
           w_router_expert, b_router_expert, w_exp_gate, w_exp_up, w_exp_down, final_norm_g):
    assert x.shape == (1, SEQ, D_MODEL) and w_in.shape[0] == 1
    x2d = x[0]
    front = jnp.concatenate([jnp.zeros((FRONT - N_META, D_MODEL), F32), meta.astype(F32)], axis=0)

    w = w_in[0]
    o_q, o_k, o_v, o_og, o_dl, o_glu, o_gt = (0, QK_W, 2 * QK_W, 2 * QK_W + V_W, 2 * QK_W + 2 * V_W,
                                              2 * QK_W + 2 * V_W + GLA_RANK,
                                              2 * QK_W + 2 * V_W + GLA_RANK + 2 * D_MODEL)
    cols = lambda a, b: w[:, a:b].astype(BF16)
    row2 = lambda a: a.reshape(1, -1).astype(F32)
    q, k, v, og, la, u, gates = _inproj(
        front, x2d, row2(norm1_g[0]), cols(o_q, o_k), cols(o_k, o_v), cols(o_v, o_og),
        cols(o_og, o_dl), cols(o_dl, o_glu), cols(o_glu, o_gt), cols(o_gt, w.shape[1]),
        w_decay_up[0].astype(F32), row2(b_decay[0]), row2(b_gate[0]))

    o_n = _gla(q, k, v, la, row2(gla_norm_g[0]))

    wr = jnp.concatenate([w_router_group[0].T, w_router_expert[0].T,
                          jnp.zeros((LANES - N_GROUPS - N_EXPERTS, D_MODEL), F32)], axis=0).astype(F32)
    br = jnp.concatenate([b_router_group[0], b_router_expert[0],
                          jnp.zeros((LANES - N_GROUPS - N_EXPERTS,), F32)]).reshape(LANES, 1).astype(F32)
    wr_hi = wr.astype(BF16)
    wr_lo = (wr - wr_hi.astype(F32)).astype(BF16)
    h1, hn2, ri, rw, tabn, tabb = _mix(
        o_n, og, u, gates, x2d, conv_w[0].astype(F32), row2(conv_b[0]), row2(conv_ln_g[0]),
        row2(conv_ln_b[0]), w_pw2[0].astype(BF16), w_out[0].astype(BF16),
        row2(norm2_g[0]), wr_hi, wr_lo, br)

    experts = slice(N_GROUPS, N_GROUPS + N_EXPERTS)
    n_te = tabn[::SUBLANES, experts].astype(jnp.int32)
    seen_te = tabb[::SUBLANES, experts].astype(jnp.int32)
    counts = seen_te[-1] + n_te[-1]
    padded = (counts + BM - 1) // BM * BM
    pad_end = jnp.cumsum(padded)
    pad_start = pad_end - padded
    n_flat = n_te.reshape(-1)
    run_rows = (pad_start[None, :] + seen_te).reshape(-1).astype(jnp.int32)
    tail_start = (pad_start + counts).astype(jnp.int32)
    tail_len = (padded - counts).astype(jnp.int32)
    blk = jnp.arange(N_BLOCKS, dtype=jnp.int32)
    block_e = jnp.minimum(jnp.sum((pad_end[None, :] <= blk[:, None] * BM).astype(jnp.int32), axis=1),
                          N_EXPERTS - 1).astype(jnp.int32)
    first = jnp.concatenate([jnp.ones((1,), jnp.int32), (block_e[1:] != block_e[:-1]).astype(jnp.int32)])
    nused = (pad_end[-1:] // BM).astype(jnp.int32)
    eid = jnp.arange(N_EXPERTS, dtype=jnp.int32)
    later = jnp.flip(lax.cummin(jnp.flip(jnp.where(counts > 0, eid, N_EXPERTS))))
    nxt = jnp.concatenate([later[1:], jnp.full((2,), N_EXPERTS, jnp.int32)])
    nxt2 = nxt[nxt[:N_EXPERTS]]
    or_none = lambda t: jnp.where(t < N_EXPERTS, t, -1)[block_e].astype(jnp.int32)
    next_e, next2_e = or_none(nxt[:N_EXPERTS]), or_none(nxt2)
    stage = ((jnp.cumsum(first) - 1) % 2).astype(jnp.int32)

    xb = _dispatch(n_flat, run_rows, tail_start, tail_len, nused, hn2, ri)
    yb = _experts(block_e, first, next_e, next2_e, stage, nused, xb, w_exp_gate[0], w_exp_up[0], w_exp_down[0])
    out = _combine(n_flat, run_rows, yb, h1, ri[0:TOP_K].T, rw[0:TOP_K].T, row2(final_norm_g))
    return out[None]
```

```python
import functools

import jax
import jax.numpy as jnp
import numpy as np
from jax import lax
from jax.experimental import pallas as pl
from jax.experimental.pallas import tpu as pltpu

F32 = jnp.float32
BF16 = jnp.bfloat16
HIGHEST = lax.Precision.HIGHEST

D_MODEL = 1024
SEQ = 16384
N_META = 16
GLA_HEADS = 4
GLA_DK = 128
GLA_DV = 256
GLA_RANK = 16
GLA_TAU = 16.0
CHUNK = 64
QK_W = GLA_HEADS * GLA_DK
V_W = GLA_HEADS * GLA_DV
CONV_WIDTH = 31
N_GROUPS = 8
EXPERTS_PER_GROUP = 8
N_EXPERTS = N_GROUPS * EXPERTS_PER_GROUP
TOP_K = 2
D_EXPERT = 512
EPS = 1e-6
LOG2E = 1.4426950408889634

LANES = 128
SUBLANES = 8
TOK_ROWS = D_MODEL // 2 // LANES
TOK_ALIGN = SUBLANES // TOK_ROWS

FRONT = 512
LP = FRONT + SEQ
TM_PROJ = 512
TM_GLA = 512
TM_MIX = 256
HIST = 32
SH_ROWS = TM_MIX + HIST - SUBLANES
CONV_RB = 64
N_TILES = SEQ // TM_MIX
SLOTS = TOP_K * TM_MIX + N_EXPERTS * (TOK_ALIGN - 1)
TILES_PER_STEP = 2
BM = 256
BLOCKS_PER_STEP = 4
N_BLOCKS = (N_TILES * SLOTS) // BM + N_EXPERTS
P_ROWS = N_BLOCKS * BM
VMEM_LIMIT = 56 * 1024 * 1024


def _sigmoid(x):
    return 0.5 * jnp.tanh(0.5 * x) + 0.5


def _silu(x):
    return x * _sigmoid(x)


def _dot(a, b, **kw):
    return jnp.dot(a, b, preferred_element_type=F32, **kw)


def _load_token_tiles(ref, n_tok, first_tok=0):
    r0 = first_tok * TOK_ROWS
    w = jnp.concatenate([ref[pl.ds(r0 + s, n_tok, stride=TOK_ROWS), :] for s in range(TOK_ROWS)], axis=-1)
    lo = pltpu.bitcast(w << 16, F32)
    hi = pltpu.bitcast(w & jnp.uint32(0xFFFF0000), F32)
    return jnp.concatenate([lo, hi], axis=-1).astype(BF16)


def _store_token_tiles(ref, val, n_tok, first_tok=0):
    bits = pltpu.bitcast(val, jnp.uint32)
    w = (bits[:, 0:D_MODEL // 2] >> 16) | (bits[:, D_MODEL // 2:D_MODEL] & jnp.uint32(0xFFFF0000))
    r0 = first_tok * TOK_ROWS
    for s in range(TOK_ROWS):
        ref[pl.ds(r0 + s, n_tok, stride=TOK_ROWS), :] = w[:, s * LANES:(s + 1) * LANES]


def _inproj_kernel(front_ref, x_ref, g_ref, wq, wk, wv, wog, wdl, wglu, wgt, wdu, bd, bg,
                   q_o, k_o, v_o, og_o, la_o, u_o, gt_o):
    i = pl.program_id(0)
    h = jnp.where(i < FRONT // TM_PROJ, front_ref[...], x_ref[...])
    ms = jnp.mean(h * h, axis=-1, keepdims=True)
    hn = (h * lax.rsqrt(ms + EPS) * g_ref[...]).astype(BF16)
    q_o[...] = _dot(hn, wq[...]).astype(BF16)
    k_o[...] = _dot(hn, wk[...]).astype(BF16)
    v_o[...] = _dot(hn, wv[...]).astype(BF16)
    og_o[...] = _silu(_dot(hn, wog[...])).astype(BF16)
    glu = _dot(hn, wglu[...])
    u_o[...] = (glu[:, 0:D_MODEL] * _sigmoid(glu[:, D_MODEL:2 * D_MODEL])).astype(BF16)
    gt_o[...] = _sigmoid(_dot(hn, wgt[...]) + bg[...]).astype(BF16)
    dlow = _dot(hn, wdl[...])
    z = _dot(dlow, wdu[...], precision=HIGHEST) + bd[...]
    la_o[...] = (jnp.minimum(z, 0.0) - jnp.log(1.0 + jnp.exp(-jnp.abs(z)))) * (LOG2E / GLA_TAU)


def _inproj(front, x2d, g1, wq, wk, wv, wog, wdl, wglu, wgt, wdu, bd, bg):
    n_front = FRONT // TM_PROJ
    grid = (LP // TM_PROJ,)
    row = lambda w: pl.BlockSpec((TM_PROJ, w), lambda i: (i, 0))
    full = lambda a: pl.BlockSpec(a.shape, lambda i: (0, 0))
    return pl.pallas_call(
        _inproj_kernel,
        grid=grid,
        in_specs=[
            pl.BlockSpec((TM_PROJ, D_MODEL), lambda i: (jnp.minimum(i, n_front - 1), 0)),
            pl.BlockSpec((TM_PROJ, D_MODEL), lambda i: (jnp.maximum(i - n_front, 0), 0)),
            full(g1), full(wq), full(wk), full(wv), full(wog), full(wdl), full(wglu), full(wgt),
            full(wdu), full(bd), full(bg),
        ],
        out_specs=[row(QK_W), row(QK_W), row(V_W), row(D_MODEL), row(QK_W), row(D_MODEL),
                   row(2 * D_MODEL)],
        out_shape=[
            jax.ShapeDtypeStruct((LP, QK_W), BF16), jax.ShapeDtypeStruct((LP, QK_W), BF16),
            jax.ShapeDtypeStruct((LP, V_W), BF16), jax.ShapeDtypeStruct((LP, D_MODEL), BF16),
            jax.ShapeDtypeStruct((LP, QK_W), F32), jax.ShapeDtypeStruct((LP, D_MODEL), BF16),
            jax.ShapeDtypeStruct((LP, 2 * D_MODEL), BF16),
        ],
        compiler_params=pltpu.CompilerParams(dimension_semantics=("arbitrary",),
                                             vmem_limit_bytes=VMEM_LIMIT),
        name="inproj",
    )(front, x2d, g1, wq, wk, wv, wog, wdl, wglu, wgt, wdu, bd, bg)


GLA_LEVELS = (32, 16, 8, 4, 2, 1)
GLA_FINE = tuple(m for m in GLA_LEVELS if m < SUBLANES)
LA_SPLIT = 3


def _gla_tables():
    r = np.arange(CHUNK)
    t, c = r[:, None], r[None, :]
    rows = [c <= t]
    pair = []
    for m in GLA_LEVELS:
        mid = (t // (2 * m)) * (2 * m) + m - 1
        right = (t % (2 * m)) >= m
        if m in GLA_FINE:
            rows.append(np.where(right, (c > mid) & (c <= t), (c > t) & (c <= mid)))
        pair.append(((t // (2 * m)) == (c // (2 * m))) & right & ((c % (2 * m)) < m))
    pair.append(t == c)
    sums = np.concatenate(rows, axis=0).astype(np.float32)
    sums = np.concatenate([sums] * LA_SPLIT, axis=1)
    return jnp.asarray(sums, BF16), jnp.asarray(np.stack(pair).astype(np.float32))


def _split_bf16(x):
    pieces = []
    rest = x
    for _ in range(LA_SPLIT):
        p = rest.astype(BF16)
        pieces.append(p)
        rest = rest - p.astype(F32)
    return jnp.concatenate(pieces, axis=0)


def _dot_nt(a, b):
    return lax.dot_general(a, b, (((1,), (1,)), ((), ())), preferred_element_type=F32)


def _dot_tn(a, b):
    return lax.dot_general(a, b, (((0,), (0,)), ((), ())), preferred_element_type=F32)


def _gla_exponents(a):
    b = a[0:CHUNK]
    parts = [b, b[CHUNK - 1:CHUNK] - b]
    for m in GLA_LEVELS:
        if m in GLA_FINE:
            continue
        for lo in range(0, CHUNK, 2 * m):
            ref = b[lo + m - 1:lo + m]
            parts += [ref - b[lo:lo + m], b[lo + m:lo + 2 * m] - ref]
    return jnp.concatenate(parts + [a[CHUNK:]], axis=0)


def _gla_chunk(q, k, vb, e, dec, s_prev, pair_ref, right_rows):
    qe = (q * e[0:CHUNK]).astype(BF16)
    kd = (k * e[CHUNK:2 * CHUNK]).astype(BF16)
    s_new = s_prev * jnp.concatenate([dec] * (GLA_DV // LANES), axis=1) + _dot_tn(kd, vb)

    sc = _dot_nt(q.astype(BF16), k.astype(BF16)) * pair_ref[len(GLA_LEVELS)]
    for lvl in range(len(GLA_LEVELS)):
        rows = e[(2 + lvl) * CHUNK:(3 + lvl) * CHUNK]
        x = (jnp.where(right_rows[lvl], q, k) * rows).astype(BF16)
        sc = sc + _dot_nt(x, x) * pair_ref[lvl]
    o = _dot(jnp.concatenate([qe, sc.astype(BF16)], axis=1),
             jnp.concatenate([s_prev.astype(BF16), vb], axis=0))
    return o, s_new


def _gla_kernel(q_ref, k_ref, v_ref, la_ref, g_ref, sums_ref, pair_ref, o_ref, s_ref):
    @pl.when(pl.program_id(0) == 0)
    def _():
        s_ref[...] = jnp.zeros_like(s_ref)

    row = lax.broadcasted_iota(jnp.int32, (CHUNK, GLA_DK), 0)
    right_rows = [(row & m) != 0 for m in GLA_LEVELS]
    ones_col = jnp.ones((LA_SPLIT * CHUNK, LANES), BF16)
    sums = sums_ref[...]

    def body(c, carry):
        r0 = pl.multiple_of(c * CHUNK, CHUNK)
        rows = pl.ds(r0, CHUNK)
        states = [s_ref[h] for h in range(GLA_HEADS)]
        new_states = []
        for hp in range(GLA_HEADS // 2):
            la3 = _split_bf16(la_ref[rows, 2 * hp * GLA_DK:2 * (hp + 1) * GLA_DK])
            e2 = jnp.exp2(_gla_exponents(_dot(sums, la3)))
            dec2 = jnp.exp2(_dot_tn(la3, ones_col))
            for h in (2 * hp, 2 * hp + 1):
                half = slice((h % 2) * GLA_DK, (h % 2 + 1) * GLA_DK)
                kc = slice(h * GLA_DK, (h + 1) * GLA_DK)
                vc = slice(h * GLA_DV, (h + 1) * GLA_DV)
                q = q_ref[rows, kc].astype(F32) * (GLA_DK ** -0.5)
                k = k_ref[rows, kc].astype(F32)
                o, s_new = _gla_chunk(q, k, v_ref[rows, vc], e2[:, half], dec2[half], states[h],
                                      pair_ref, right_rows)
                new_states.append(s_new)
                o = o * lax.rsqrt(jnp.mean(o * o, axis=-1, keepdims=True) + EPS) * g_ref[:, vc]
                o_ref[rows, vc] = o.astype(BF16)
        for h in range(GLA_HEADS):
            s_ref[h] = new_states[h]
        return carry

    lax.fori_loop(0, TM_GLA // CHUNK, body, 0, unroll=4)


def _gla(q, k, v, la, g):
    sums, pair = _gla_tables()
    row = lambda w: pl.BlockSpec((TM_GLA, w), lambda t: (t, 0))
    full = lambda a: pl.BlockSpec(a.shape, lambda t: (0,) * a.ndim)
    return pl.pallas_call(
        _gla_kernel,
        grid=(LP // TM_GLA,),
        in_specs=[row(QK_W), row(QK_W), row(V_W), row(QK_W), full(g), full(sums), full(pair)],
        out_specs=row(V_W),
        out_shape=jax.ShapeDtypeStruct((LP, V_W), BF16),
        scratch_shapes=[pltpu.VMEM((GLA_HEADS, GLA_DK, GLA_DV), F32)],
        compiler_params=pltpu.CompilerParams(dimension_semantics=("arbitrary",),
                                             vmem_limit_bytes=VMEM_LIMIT),
        name="gla",
    )(q, k, v, la, g, sums, pair)


def _mix_kernel(o_ref, og_ref, u_ref, hist_ref, gt_ref, x_ref, cw_ref, cb_ref, lg_ref, lb_ref,
                wpw_ref, wo_ref, g2_ref, wrh_ref, wrl_ref, br_ref,
                h_o, hn_o, ri_o, rw_o, tabn_o, tabb_o, ubuf, shbuf, cbuf, seen):
    i = pl.program_id(0)

    @pl.when(i == 0)
    def _():
        seen[...] = jnp.zeros_like(seen)

    ubuf[0:HIST, :] = hist_ref[...].astype(F32)
    ubuf[HIST:HIST + TM_MIX, :] = u_ref[...].astype(F32)

    lead = HIST - (CONV_WIDTH - 1)
    for rho in range(1, SUBLANES):
        shbuf[rho - 1] = ubuf[rho:rho + SH_ROWS, :]
    for cb in range(D_MODEL // LANES):
        lanes = slice(cb * LANES, (cb + 1) * LANES)
        for rb in range(TM_MIX // CONV_RB):
            part = jnp.broadcast_to(cb_ref[:, lanes], (CONV_RB, LANES))
            for j in range(CONV_WIDTH):
                rho = (lead + j) % SUBLANES
                r0 = rb * CONV_RB + (lead + j) - rho
                src = ubuf[r0:r0 + CONV_RB, lanes] if rho == 0 else shbuf[rho - 1, r0:r0 + CONV_RB, lanes]
                part = part + cw_ref[j:j + 1, lanes] * src
            cbuf[rb * CONV_RB:(rb + 1) * CONV_RB, lanes] = part
    acc = cbuf[...]

    mu = jnp.mean(acc, axis=-1, keepdims=True)
    xc = acc - mu
    ln = xc * lax.rsqrt(jnp.mean(xc * xc, axis=-1, keepdims=True) + EPS) * lg_ref[...] + lb_ref[...]
    branch_b = _dot(_silu(ln).astype(BF16), wpw_ref[...])

    branch_a = o_ref[...].astype(F32) * og_ref[...].astype(F32)
    g_a = gt_ref[:, 0:D_MODEL].astype(F32)
    g_b = gt_ref[:, D_MODEL:2 * D_MODEL].astype(F32)
    merged = (g_a * branch_a + g_b * branch_b).astype(BF16)
    h1 = x_ref[...] + _dot(merged, wo_ref[...])
    h_o[...] = h1

    hn2 = h1 * lax.rsqrt(jnp.mean(h1 * h1, axis=-1, keepdims=True) + EPS) * g2_ref[...]
    hn_hi = hn2.astype(BF16)
    hn_o[...] = hn_hi

    hn_lo = (hn2 - hn_hi.astype(F32)).astype(BF16)
    logits = (_dot_nt(wrh_ref[...], hn_hi) + _dot_nt(wrh_ref[...], hn_lo) + _dot_nt(wrl_ref[...], hn_hi)
              + br_ref[...])
    row = lax.broadcasted_iota(jnp.int32, logits.shape, 0)
    rowf = row.astype(F32)
    neg = -jnp.inf
    is_g = row < N_GROUPS
    lg = jnp.where(is_g, logits, neg)
    gmax = jnp.max(lg, axis=0, keepdims=True)
    gidx = jnp.min(jnp.where(lg == gmax, rowf, float(N_GROUPS)), axis=0, keepdims=True)
    g_w = 1.0 / jnp.sum(jnp.where(is_g, jnp.exp(lg - gmax), 0.0), axis=0, keepdims=True)
    erow = rowf - float(N_GROUPS)
    egrp = ((row - N_GROUPS) >> 3).astype(F32)
    in_grp = (row >= N_GROUPS) & (row < N_GROUPS + N_EXPERTS) & (egrp == gidx)
    le = jnp.where(in_grp, logits, neg)
    m1 = jnp.max(le, axis=0, keepdims=True)
    i1 = jnp.min(jnp.where(le == m1, erow, float(N_EXPERTS)), axis=0, keepdims=True)
    le2 = jnp.where(erow == i1, neg, le)
    m2 = jnp.max(le2, axis=0, keepdims=True)
    i2 = jnp.min(jnp.where(le2 == m2, erow, float(N_EXPERTS)), axis=0, keepdims=True)
    t = jnp.exp(m2 - m1)
    w1 = g_w / (1.0 + t)
    w2 = g_w * t / (1.0 + t)

    def run_len(n):
        return jnp.floor((n + (TOK_ALIGN - 1)) * (1.0 / TOK_ALIGN)) * TOK_ALIGN

    oh1 = erow == i1
    oh2 = erow == i2
    oh = jnp.where(oh1 | oh2, 1.0, 0.0).astype(BF16)
    tr = lax.broadcasted_iota(jnp.int32, (TM_MIX, TM_MIX), 0)
    tc = lax.broadcasted_iota(jnp.int32, (TM_MIX, TM_MIX), 1)
    before_tok = _dot(oh, (tr < tc).astype(BF16))
    n_col = jnp.sum(oh.astype(F32), axis=1, keepdims=True)
    er = lax.broadcasted_iota(jnp.int32, (LANES, LANES), 0)
    ec = lax.broadcasted_iota(jnp.int32, (LANES, LANES), 1)
    before_exp = _dot((ec < er).astype(BF16), jnp.broadcast_to(run_len(n_col), logits.shape).astype(BF16))
    where_to = before_exp + before_tok
    q1 = jnp.sum(jnp.where(oh1, where_to, 0.0), axis=0, keepdims=True)
    q2 = jnp.sum(jnp.where(oh2, where_to, 0.0), axis=0, keepdims=True)

    n_rows = run_len(_dot_nt(jnp.ones((SUBLANES, TM_MIX), BF16), oh))
    tabn_o[...] = n_rows
    tabb_o[...] = seen[...]
    seen[...] = seen[...] + n_rows

    ri_o[...] = jnp.zeros_like(ri_o)
    ri_o[0:1, :] = q1.astype(jnp.int32)
    ri_o[1:2, :] = q2.astype(jnp.int32)
    rw_o[...] = jnp.zeros_like(rw_o)
    rw_o[0:1, :] = w1
    rw_o[1:2, :] = w2


def _mix(o_n, og, u, gates, x2d, cw, cb, lg, lb, wpw, wo, g2, wrh, wrl, br):
    n_front = FRONT // TM_MIX
    grid = (SEQ // TM_MIX,)
    rowp = lambda w: pl.BlockSpec((TM_MIX, w), lambda i: (i + n_front, 0))
    full = lambda a: pl.BlockSpec(a.shape, lambda i: (0,) * a.ndim)
    hist_blocks = TM_MIX // HIST
    return pl.pallas_call(
        _mix_kernel,
        grid=grid,
        in_specs=[
            rowp(V_W), rowp(D_MODEL), rowp(D_MODEL),
            pl.BlockSpec((HIST, D_MODEL), lambda i: ((i + n_front) * hist_blocks - 1, 0)),
            rowp(2 * D_MODEL),
            pl.BlockSpec((TM_MIX, D_MODEL), lambda i: (i, 0)),
            full(cw), full(cb), full(lg), full(lb), full(wpw), full(wo), full(g2),
            full(wrh), full(wrl), full(br),
        ],
        out_specs=[
            pl.BlockSpec((TM_MIX, D_MODEL), lambda i: (i, 0)),
            pl.BlockSpec((TM_MIX, D_MODEL), lambda i: (i, 0)),
            pl.BlockSpec((SUBLANES, TM_MIX), lambda i: (0, i)),
            pl.BlockSpec((SUBLANES, TM_MIX), lambda i: (0, i)),
            pl.BlockSpec((SUBLANES, LANES), lambda i: (i, 0)),
            pl.BlockSpec((SUBLANES, LANES), lambda i: (i, 0)),
        ],
        out_shape=[
            jax.ShapeDtypeStruct((SEQ, D_MODEL), F32),
            jax.ShapeDtypeStruct((SEQ, D_MODEL), BF16),
            jax.ShapeDtypeStruct((SUBLANES, SEQ), jnp.int32),
            jax.ShapeDtypeStruct((SUBLANES, SEQ), F32),
            jax.ShapeDtypeStruct((N_TILES * SUBLANES, LANES), F32),
            jax.ShapeDtypeStruct((N_TILES * SUBLANES, LANES), F32),
        ],
        scratch_shapes=[pltpu.VMEM((HIST + TM_MIX, D_MODEL), F32),
                        pltpu.VMEM((SUBLANES - 1, SH_ROWS, D_MODEL), F32),
                        pltpu.VMEM((TM_MIX, D_MODEL), F32), pltpu.VMEM((SUBLANES, LANES), F32)],
        compiler_params=pltpu.CompilerParams(dimension_semantics=("arbitrary",),
                                             vmem_limit_bytes=VMEM_LIMIT),
        name="mix",
    )(o_n, og, u, u, gates, x2d, cw, cb, lg, lb, wpw, wo, g2, wrh, wrl, br)


def _tok_rows(start_tok, n_tok):
    return pl.ds(pl.multiple_of(start_tok * TOK_ROWS, SUBLANES), n_tok * TOK_ROWS)


def _start_tile_runs(n_ref, row_ref, tile, make_copy):
    def run(e, off):
        n = n_ref[tile * N_EXPERTS + e]

        @pl.when(n > 0)
        def _():
            make_copy(_tok_rows(off, n), _tok_rows(row_ref[tile * N_EXPERTS + e], n)).start()

        return off + n

    lax.fori_loop(0, N_EXPERTS, run, 0, unroll=8)


def _dispatch_kernel(n_ref, row_ref, tot_ref, ts_ref, tl_ref, nused_ref, hn_ref, q_ref, xb_ref,
                     sb0, sb1, zbuf, sem, fsem):
    step = pl.program_id(0)
    last = pl.num_programs(0) - 1

    def fills(start):
        def tail(e, c):
            n = tl_ref[e]

            @pl.when(n > 0)
            def _():
                cp = pltpu.make_async_copy(zbuf.at[pl.ds(0, n * TOK_ROWS)],
                                           xb_ref.at[_tok_rows(ts_ref[e], n)], fsem)
                cp.start() if start else cp.wait()

            return c

        lax.fori_loop(0, N_EXPERTS, tail, 0)

        def block(b, c):
            cp = pltpu.make_async_copy(zbuf, xb_ref.at[_tok_rows(b * BM, BM)], fsem)
            cp.start() if start else cp.wait()
            return c

        lax.fori_loop(nused_ref[0], N_BLOCKS, block, 0)

    @pl.when(step == 0)
    def _():
        zbuf[...] = jnp.zeros_like(zbuf)
        fills(start=True)

    def tile_runs(tile, s, sbuf):
        rows = pl.ds(0, tot_ref[tile] * TOK_ROWS)
        return pltpu.make_async_copy(sbuf.at[rows], xb_ref.at[rows], sem.at[s])

    for s, sbuf in enumerate((sb0, sb1)):
        tile = step * TILES_PER_STEP + s
        toks = slice(s * TM_MIX, (s + 1) * TM_MIX)
        slot_i = lax.broadcasted_iota(jnp.int32, (SLOTS, TM_MIX), 0)
        onehot = jnp.where((slot_i == q_ref[0:1, toks]) | (slot_i == q_ref[1:2, toks]), 1.0, 0.0).astype(BF16)
        srt = _dot(onehot, hn_ref[toks, :])

        @pl.when(step >= 1)
        def _():
            tile_runs(tile - TILES_PER_STEP, s, sbuf).wait()

        _store_token_tiles(sbuf, srt, SLOTS)
        _start_tile_runs(n_ref, row_ref, tile,
                         lambda loc, glob: pltpu.make_async_copy(sbuf.at[loc], xb_ref.at[glob], sem.at[s]))

    @pl.when(step == last)
    def _():
        for s, sbuf in enumerate((sb0, sb1)):
            tile_runs(step * TILES_PER_STEP + s, s, sbuf).wait()
        fills(start=False)


def _dispatch(n_flat, run_rows, tot, tail_start, tail_len, nused, hn2, ri):
    grid_spec = pltpu.PrefetchScalarGridSpec(
        num_scalar_prefetch=6,
        grid=(N_TILES // TILES_PER_STEP,),
        in_specs=[pl.BlockSpec((TILES_PER_STEP * TM_MIX, D_MODEL), lambda i, *_: (i, 0)),
                  pl.BlockSpec((SUBLANES, TILES_PER_STEP * TM_MIX), lambda i, *_: (0, i))],
        out_specs=pl.BlockSpec(memory_space=pl.ANY),
        scratch_shapes=[pltpu.VMEM((SLOTS * TOK_ROWS, LANES), jnp.uint32),
                        pltpu.VMEM((SLOTS * TOK_ROWS, LANES), jnp.uint32),
                        pltpu.VMEM((BM * TOK_ROWS, LANES), jnp.uint32),
                        pltpu.SemaphoreType.DMA((2,)), pltpu.SemaphoreType.DMA(())],
    )
    return pl.pallas_call(
        _dispatch_kernel,
        grid_spec=grid_spec,
        out_shape=jax.ShapeDtypeStruct((P_ROWS * TOK_ROWS, LANES), jnp.uint32),
        compiler_params=pltpu.CompilerParams(dimension_semantics=("arbitrary",),
                                             vmem_limit_bytes=VMEM_LIMIT),
        name="dispatch",
    )(n_flat, run_rows, tot, tail_start, tail_len, nused, hn2, ri)


def _experts_kernel(be_ref, first_ref, next_ref, next2_ref, stage_ref, nused_ref,
                    x_ref, wg_hbm, wu_hbm, wd_hbm, y_ref, wg_f, wu_f, wd_f, wg_b, wu_b, wd_b, sem):
    def weight_copies(e, st):
        return (pltpu.make_async_copy(wg_hbm.at[e], wg_f.at[st], sem.at[st, 0]),
                pltpu.make_async_copy(wu_hbm.at[e], wu_f.at[st], sem.at[st, 1]),
                pltpu.make_async_copy(wd_hbm.at[e], wd_f.at[st], sem.at[st, 2]))

    @pl.when((pl.program_id(0) == 0) & (nused_ref[0] > 0))
    def _():
        for c in weight_copies(be_ref[0], 0):
            c.start()

        @pl.when(next_ref[0] >= 0)
        def _():
            for c in weight_copies(next_ref[0], 1):
                c.start()

    for j in range(BLOCKS_PER_STEP):
        _expert_block(pl.program_id(0) * BLOCKS_PER_STEP + j, j * BM, be_ref, first_ref, next2_ref,
                      stage_ref, nused_ref, x_ref, y_ref, wg_f, wu_f, wd_f, wg_b, wu_b, wd_b, weight_copies)


def _expert_block(b, tok0, be_ref, first_ref, next2_ref, stage_ref, nused_ref, x_ref, y_ref,
                  wg_f, wu_f, wd_f, wg_b, wu_b, wd_b, weight_copies):
    @pl.when(b < nused_ref[0])
    def _():
        def new_expert(st):
            for c in weight_copies(be_ref[b], st):
                c.wait()
            wg_b[...] = wg_f[st].astype(BF16)
            wu_b[...] = wu_f[st].astype(BF16)
            wd_b[...] = wd_f[st].astype(BF16)

            @pl.when(next2_ref[b] >= 0)
            def _():
                for c in weight_copies(next2_ref[b], st):
                    c.start()

        for st in range(2):
            pl.when((first_ref[b] == 1) & (stage_ref[b] == st))(functools.partial(new_expert, st))

        x = _load_token_tiles(x_ref, BM, tok0)
        a = _dot(x, wg_b[...])
        u = _dot(x, wu_b[...])
        y = _dot((_silu(a) * u).astype(BF16), wd_b[...])
        _store_token_tiles(y_ref, y.astype(BF16).astype(F32), BM, tok0)

    @pl.when(b >= nused_ref[0])
    def _():
        y_ref[tok0 * TOK_ROWS:(tok0 + BM) * TOK_ROWS, :] = jnp.zeros((BM * TOK_ROWS, LANES), jnp.uint32)


def _experts(block_e, first, next_e, next2_e, stage, nused, xb, wg, wu, wd):
    def xmap(b, *_):
        return (b, 0)

    grid_spec = pltpu.PrefetchScalarGridSpec(
        num_scalar_prefetch=6,
        grid=(N_BLOCKS // BLOCKS_PER_STEP,),
        in_specs=[
            pl.BlockSpec((BLOCKS_PER_STEP * BM * TOK_ROWS, LANES), xmap),
            pl.BlockSpec(memory_space=pl.ANY),
            pl.BlockSpec(memory_space=pl.ANY),
            pl.BlockSpec(memory_space=pl.ANY),
        ],
        out_specs=pl.BlockSpec((BLOCKS_PER_STEP * BM * TOK_ROWS, LANES), xmap),
        scratch_shapes=[pltpu.VMEM((2, D_MODEL, D_EXPERT), F32), pltpu.VMEM((2, D_MODEL, D_EXPERT), F32),
                        pltpu.VMEM((2, D_EXPERT, D_MODEL), F32),
                        pltpu.VMEM((D_MODEL, D_EXPERT), BF16), pltpu.VMEM((D_MODEL, D_EXPERT), BF16),
                        pltpu.VMEM((D_EXPERT, D_MODEL), BF16), pltpu.SemaphoreType.DMA((2, 3))],
    )
    return pl.pallas_call(
        _experts_kernel,
        grid_spec=grid_spec,
        out_shape=jax.ShapeDtypeStruct((P_ROWS * TOK_ROWS, LANES), jnp.uint32),
        compiler_params=pltpu.CompilerParams(dimension_semantics=("arbitrary",),
                                             vmem_limit_bytes=VMEM_LIMIT),
        name="experts",
    )(block_e, first, next_e, next2_e, stage, nused, xb, wg, wu, wd)


def _combine_kernel(n_ref, row_ref, tot_ref, yb_ref, h_ref, q_ref, w_ref, g_ref, out_ref, yb0, yb1, sem):
    step = pl.program_id(0)
    bufs = (yb0, yb1)

    def start_gather(tile, s):
        _start_tile_runs(n_ref, row_ref, tile,
                         lambda loc, glob: pltpu.make_async_copy(yb_ref.at[glob], bufs[s].at[loc], sem.at[s]))

    @pl.when(step == 0)
    def _():
        for buf in bufs:
            buf[...] = jnp.zeros_like(buf)
        start_gather(0, 0)

    for s in range(TILES_PER_STEP):
        tile = step * TILES_PER_STEP + s
        if s + 1 < TILES_PER_STEP:
            start_gather(tile + 1, s + 1)
        else:
            pl.when(step + 1 < pl.num_programs(0))(functools.partial(start_gather, tile + 1, 0))

        rows = pl.ds(0, tot_ref[tile] * TOK_ROWS)
        pltpu.make_async_copy(yb_ref.at[rows], bufs[s].at[rows], sem.at[s]).wait()
        ys = _load_token_tiles(bufs[s], SLOTS)
        toks = slice(s * TM_MIX, (s + 1) * TM_MIX)
        col = lax.broadcasted_iota(jnp.int32, (TM_MIX, SLOTS), 1)
        wmat = (jnp.where(col == q_ref[toks, 0:1], w_ref[toks, 0:1], 0.0)
                + jnp.where(col == q_ref[toks, 1:2], w_ref[toks, 1:2], 0.0)).astype(BF16)
        hh = h_ref[toks, :] + _dot(wmat, ys)
        out_ref[toks, :] = hh * lax.rsqrt(jnp.mean(hh * hh, axis=-1, keepdims=True) + EPS) * g_ref[...]


def _combine(n_flat, run_rows, tot, yb, h1, qt, wts, gf):
    grid_spec = pltpu.PrefetchScalarGridSpec(
        num_scalar_prefetch=3,
        grid=(N_TILES // TILES_PER_STEP,),
        in_specs=[
            pl.BlockSpec(memory_space=pl.ANY),
            pl.BlockSpec((TILES_PER_STEP * TM_MIX, D_MODEL), lambda i, *_: (i, 0)),
            pl.BlockSpec((TILES_PER_STEP * TM_MIX, TOP_K), lambda i, *_: (i, 0)),
            pl.BlockSpec((TILES_PER_STEP * TM_MIX, TOP_K), lambda i, *_: (i, 0)),
            pl.BlockSpec((1, D_MODEL), lambda i, *_: (0, 0)),
        ],
        out_specs=pl.BlockSpec((TILES_PER_STEP * TM_MIX, D_MODEL), lambda i, *_: (i, 0)),
        scratch_shapes=[pltpu.VMEM((SLOTS * TOK_ROWS, LANES), jnp.uint32),
                        pltpu.VMEM((SLOTS * TOK_ROWS, LANES), jnp.uint32),
                        pltpu.SemaphoreType.DMA((2,))],
    )
    return pl.pallas_call(
        _combine_kernel,
        grid_spec=grid_spec,
        out_shape=jax.ShapeDtypeStruct((SEQ, D_MODEL), F32),
        compiler_params=pltpu.CompilerParams(dimension_semantics=("arbitrary",),
                                             vmem_limit_bytes=VMEM_LIMIT),
        name="combine",
    )(n_flat, run_rows, tot, yb, h1, qt, wts, gf)


def kernel(x, meta, norm1_g, w_in, w_decay_up, b_decay, gla_norm_g, conv_w, conv_b, conv_ln_g,
           conv_ln_b, w_pw2, b_gate, w_out, norm2_g, w_router_group, b_router_group,
           w_router_expert, b_router_expert, w_exp_gate, w_exp_up, w_exp_down, final_norm_g):
    assert x.shape == (1, SEQ, D_MODEL) and w_in.shape[0] == 1
    x2d = x[0]
    front = jnp.concatenate([jnp.zeros((FRONT - N_META, D_MODEL), F32), meta.astype(F32)], axis=0)

    w = w_in[0]
    o_q, o_k, o_v, o_og, o_dl, o_glu, o_gt = (0, QK_W, 2 * QK_W, 2 * QK_W + V_W, 2 * QK_W + 2 * V_W,
                                              2 * QK_W + 2 * V_W + GLA_RANK,
                                              2 * QK_W + 2 * V_W + GLA_RANK + 2 * D_MODEL)
    cols = lambda a, b: w[:, a:b].astype(BF16)
    row2 = lambda a: a.reshape(1, -1).astype(F32)
    q, k, v, og, la, u, gates = _inproj(
        front, x2d, row2(norm1_g[0]), cols(o_q, o_k), cols(o_k, o_v), cols(o_v, o_og),
        cols(o_og, o_dl), cols(o_dl, o_glu), cols(o_glu, o_gt), cols(o_gt, w.shape[1]),
        w_decay_up[0].astype(F32), row2(b_decay[0]), row2(b_gate[0]))

    o_n = _gla(q, k, v, la, row2(gla_norm_g[0]))

    wr = jnp.concatenate([w_router_group[0].T, w_router_expert[0].T,
                          jnp.zeros((LANES - N_GROUPS - N_EXPERTS, D_MODEL), F32)], axis=0).astype(F32)
    br = jnp.concatenate([b_router_group[0], b_router_expert[0],
                          jnp.zeros((LANES - N_GROUPS - N_EXPERTS,), F32)]).reshape(LANES, 1).astype(F32)
    wr_hi = wr.astype(BF16)
    wr_lo = (wr - wr_hi.astype(F32)).astype(BF16)
    h1, hn2, ri, rw, tabn, tabb = _mix(
        o_n, og, u, gates, x2d, conv_w[0].astype(F32), row2(conv_b[0]), row2(conv_ln_g[0]),
        row2(conv_ln_b[0]), w_pw2[0].astype(BF16), w_out[0].astype(BF16),
        row2(norm2_g[0]), wr_hi, wr_lo, br)

    experts = slice(N_GROUPS, N_GROUPS + N_EXPERTS)
    n_te = tabn[::SUBLANES, experts].astype(jnp.int32)
    seen_te = tabb[::SUBLANES, experts].astype(jnp.int32)
    counts = seen_te[-1] + n_te[-1]
    padded = (counts + BM - 1) // BM * BM
    pad_end = jnp.cumsum(padded)
    pad_start = pad_end - padded
    n_flat = n_te.reshape(-1)
    tot = jnp.sum(n_te, axis=1).astype(jnp.int32)
    run_rows = (pad_start[None, :] + seen_te).reshape(-1).astype(jnp.int32)
    tail_start = (pad_start + counts).astype(jnp.int32)
    tail_len = (padded - counts).astype(jnp.int32)
    blk = jnp.arange(N_BLOCKS, dtype=jnp.int32)
    block_e = jnp.minimum(jnp.sum((pad_end[None, :] <= blk[:, None] * BM).astype(jnp.int32), axis=1),
                          N_EXPERTS - 1).astype(jnp.int32)
    first = jnp.concatenate([jnp.ones((1,), jnp.int32), (block_e[1:] != block_e[:-1]).astype(jnp.int32)])
    nused = (pad_end[-1:] // BM).astype(jnp.int32)
    eid = jnp.arange(N_EXPERTS, dtype=jnp.int32)
    later = jnp.flip(lax.cummin(jnp.flip(jnp.where(counts > 0, eid, N_EXPERTS))))
    nxt = jnp.concatenate([later[1:], jnp.full((2,), N_EXPERTS, jnp.int32)])
    nxt2 = nxt[nxt[:N_EXPERTS]]
    or_none = lambda t: jnp.where(t < N_EXPERTS, t, -1)[block_e].astype(jnp.int32)
    next_e, next2_e = or_none(nxt[:N_EXPERTS]), or_none(nxt2)
    stage = ((jnp.cumsum(first) - 1) % 2).astype(jnp.int32)

    xb = _dispatch(n_flat, run_rows, tot, tail_start, tail_len, nused, hn2, ri)
    yb = _experts(block_e, first, next_e, next2_e, stage, nused, xb, w_exp_gate[0], w_exp_up[0], w_exp_down[0])
    out = _combine(n_flat, run_rows, tot, yb, h1, ri[0:TOP_K].T, rw[0:TOP_K].T, row2(final_norm_g))
    return out[None]
```

```python
import functools

import jax
import jax.numpy as jnp
import numpy as np
from jax import lax
from jax.experimental import pallas as pl
from jax.experimental.pallas import tpu as pltpu

F32 = jnp.float32
BF16 = jnp.bfloat16

D_MODEL = 1024
SEQ = 16384
N_META = 16
GLA_HEADS = 4
GLA_DK = 128
GLA_DV = 256
GLA_RANK = 16
GLA_TAU = 16.0
CHUNK = 64
QK_W = GLA_HEADS * GLA_DK
V_W = GLA_HEADS * GLA_DV
CONV_WIDTH = 31
N_GROUPS = 8
EXPERTS_PER_GROUP = 8
N_EXPERTS = N_GROUPS * EXPERTS_PER_GROUP
TOP_K = 2
D_EXPERT = 512
EPS = 1e-6
LOG2E = 1.4426950408889634

LANES = 128
SUBLANES = 8
TOK_ROWS = D_MODEL // 2 // LANES
TOK_ALIGN = SUBLANES // TOK_ROWS

FRONT = 512
LP = FRONT + SEQ
TM_PROJ = 512
DL_COPIES = 6
TM_GLA = 512
TM_MIX = 256
MIX_TILES = 2
HIST = 32
SH_ROWS = TM_MIX + HIST - SUBLANES
CONV_RB = 64
N_TILES = SEQ // TM_MIX
SLOTS = TOP_K * TM_MIX + N_EXPERTS * (TOK_ALIGN - 1)
TILES_PER_STEP = 2
BM = 256
BLOCKS_PER_STEP = 4
N_BLOCKS = (N_TILES * SLOTS) // BM + N_EXPERTS
P_ROWS = N_BLOCKS * BM
VMEM_LIMIT = 56 * 1024 * 1024


def _sigmoid(x):
    return 0.5 * jnp.tanh(0.5 * x) + 0.5


def _silu(x):
    return x * _sigmoid(x)


def _dot(a, b, **kw):
    return jnp.dot(a, b, preferred_element_type=F32, **kw)


def _load_token_tiles(ref, n_tok, first_tok=0):
    r0 = first_tok * TOK_ROWS
    w = jnp.concatenate([ref[pl.ds(r0 + s, n_tok, stride=TOK_ROWS), :] for s in range(TOK_ROWS)], axis=-1)
    lo = pltpu.bitcast(w << 16, F32)
    hi = pltpu.bitcast(w & jnp.uint32(0xFFFF0000), F32)
    return jnp.concatenate([lo, hi], axis=-1).astype(BF16)


def _store_token_tiles(ref, val, n_tok, first_tok=0):
    bits = pltpu.bitcast(val, jnp.uint32)
    w = (bits[:, 0:D_MODEL // 2] >> 16) | (bits[:, D_MODEL // 2:D_MODEL] & jnp.uint32(0xFFFF0000))
    r0 = first_tok * TOK_ROWS
    for s in range(TOK_ROWS):
        ref[pl.ds(r0 + s, n_tok, stride=TOK_ROWS), :] = w[:, s * LANES:(s + 1) * LANES]


IN_SPLITS = (QK_W, QK_W, V_W, D_MODEL, GLA_RANK, 2 * D_MODEL, 2 * D_MODEL)
WCAST_ROWS = 128


def _wcast_kernel(w_ref, *outs):
    w = w_ref[...]
    off = 0
    for out, n in zip(outs, IN_SPLITS):
        out[...] = w[:, off:off + n].astype(BF16)
        off += n


def _wcast(w):
    return pl.pallas_call(
        _wcast_kernel,
        grid=(D_MODEL // WCAST_ROWS,),
        in_specs=[pl.BlockSpec((WCAST_ROWS, sum(IN_SPLITS)), lambda i: (i, 0))],
        out_specs=[pl.BlockSpec((WCAST_ROWS, n), lambda i: (i, 0)) for n in IN_SPLITS],
        out_shape=[jax.ShapeDtypeStruct((D_MODEL, n), BF16) for n in IN_SPLITS],
        compiler_params=pltpu.CompilerParams(dimension_semantics=("arbitrary",),
                                             vmem_limit_bytes=VMEM_LIMIT),
        name="wcast",
    )(w)


def _inproj_kernel(front_ref, x_ref, g_ref, wq, wk, wv, wog, wdl, wglu, wgt, wdu, bd, bg,
                   q_o, k_o, v_o, og_o, la_o, u_o, gt_o):
    i = pl.program_id(0)
    h = jnp.where(i < FRONT // TM_PROJ, front_ref[...], x_ref[...])
    ms = jnp.mean(h * h, axis=-1, keepdims=True)
    hn = (h * lax.rsqrt(ms + EPS) * g_ref[...]).astype(BF16)
    q_o[...] = _dot(hn, wq[...]).astype(BF16)
    k_o[...] = _dot(hn, wk[...]).astype(BF16)
    v_o[...] = _dot(hn, wv[...]).astype(BF16)
    og_o[...] = _silu(_dot(hn, wog[...])).astype(BF16)
    glu = _dot(hn, wglu[...])
    u_o[...] = (glu[:, 0:D_MODEL] * _sigmoid(glu[:, D_MODEL:2 * D_MODEL])).astype(BF16)
    gt_o[...] = _sigmoid(_dot(hn, wgt[...]) + bg[...]).astype(BF16)
    d6 = _dot(hn, wdl[...])
    hi = d6.astype(BF16).astype(F32)
    mid = (d6 - hi).astype(BF16).astype(F32)
    lane = lax.broadcasted_iota(jnp.int32, d6.shape, 1)
    pieces = jnp.where(lane < 3 * GLA_RANK, hi, jnp.where(lane < 5 * GLA_RANK, mid, d6 - hi - mid))
    z = _dot(pieces.astype(BF16), wdu[...]) + bd[...]
    la_o[...] = (jnp.minimum(z, 0.0) - jnp.log(1.0 + jnp.exp(-jnp.abs(z)))) * (LOG2E / GLA_TAU)


def _inproj(front, x2d, g1, wq, wk, wv, wog, wdl, wglu, wgt, wdu, bd, bg):
    n_front = FRONT // TM_PROJ
    grid = (LP // TM_PROJ,)
    row = lambda w: pl.BlockSpec((TM_PROJ, w), lambda i: (i, 0))
    full = lambda a: pl.BlockSpec(a.shape, lambda i: (0, 0))
    return pl.pallas_call(
        _inproj_kernel,
        grid=grid,
        in_specs=[
            pl.BlockSpec((TM_PROJ, D_MODEL), lambda i: (jnp.minimum(i, n_front - 1), 0)),
            pl.BlockSpec((TM_PROJ, D_MODEL), lambda i: (jnp.maximum(i - n_front, 0), 0)),
            full(g1), full(wq), full(wk), full(wv), full(wog), full(wdl), full(wglu), full(wgt),
            full(wdu), full(bd), full(bg),
        ],
        out_specs=[row(QK_W), row(QK_W), row(V_W), row(D_MODEL), row(QK_W), row(D_MODEL),
                   row(2 * D_MODEL)],
        out_shape=[
            jax.ShapeDtypeStruct((LP, QK_W), BF16), jax.ShapeDtypeStruct((LP, QK_W), BF16),
            jax.ShapeDtypeStruct((LP, V_W), BF16), jax.ShapeDtypeStruct((LP, D_MODEL), BF16),
            jax.ShapeDtypeStruct((LP, QK_W), F32), jax.ShapeDtypeStruct((LP, D_MODEL), BF16),
            jax.ShapeDtypeStruct((LP, 2 * D_MODEL), BF16),
        ],
        compiler_params=pltpu.CompilerParams(dimension_semantics=("arbitrary",),
                                             vmem_limit_bytes=VMEM_LIMIT),
        name="inproj",
    )(front, x2d, g1, wq, wk, wv, wog, wdl, wglu, wgt, wdu, bd, bg)


GLA_LEVELS = (32, 16, 8, 4, 2, 1)
GLA_FINE = tuple(m for m in GLA_LEVELS if m < SUBLANES)
LA_SPLIT = 3


def _gla_tables():
    r = np.arange(CHUNK)
    t, c = r[:, None], r[None, :]
    rows = [c <= t]
    pair = []
    for m in GLA_LEVELS:
        mid = (t // (2 * m)) * (2 * m) + m - 1
        right = (t % (2 * m)) >= m
        if m in GLA_FINE:
            rows.append(np.where(right, (c > mid) & (c <= t), (c > t) & (c <= mid)))
        pair.append(((t // (2 * m)) == (c // (2 * m))) & right & ((c % (2 * m)) < m))
    pair.append(t == c)
    sums = np.concatenate(rows, axis=0).astype(np.float32)
    sums = np.concatenate([sums] * LA_SPLIT, axis=1)
    return jnp.asarray(sums, BF16), jnp.asarray(np.stack(pair).astype(np.float32))


def _split_bf16(x):
    pieces = []
    rest = x
    for _ in range(LA_SPLIT):
        p = rest.astype(BF16)
        pieces.append(p)
        rest = rest - p.astype(F32)
    return jnp.concatenate(pieces, axis=0)


def _dot_nt(a, b):
    return lax.dot_general(a, b, (((1,), (1,)), ((), ())), preferred_element_type=F32)


def _dot_tn(a, b):
    return lax.dot_general(a, b, (((0,), (0,)), ((), ())), preferred_element_type=F32)


def _gla_exponents(a):
    b = a[0:CHUNK]
    parts = [b, b[CHUNK - 1:CHUNK] - b]
    for m in GLA_LEVELS:
        if m in GLA_FINE:
            continue
        for lo in range(0, CHUNK, 2 * m):
            ref = b[lo + m - 1:lo + m]
            parts += [ref - b[lo:lo + m], b[lo + m:lo + 2 * m] - ref]
    return jnp.concatenate(parts + [a[CHUNK:]], axis=0)


def _gla_chunk(q, k, vb, e, dec, s_prev, pair_ref, right_rows):
    qe = (q * e[0:CHUNK]).astype(BF16)
    kd = (k * e[CHUNK:2 * CHUNK]).astype(BF16)
    s_new = s_prev * jnp.concatenate([dec] * (GLA_DV // LANES), axis=1) + _dot_tn(kd, vb)

    sc = _dot_nt(q.astype(BF16), k.astype(BF16)) * pair_ref[len(GLA_LEVELS)]
    for lvl in range(len(GLA_LEVELS)):
        rows = e[(2 + lvl) * CHUNK:(3 + lvl) * CHUNK]
        x = (jnp.where(right_rows[lvl], q, k) * rows).astype(BF16)
        sc = sc + _dot_nt(x, x) * pair_ref[lvl]
    o = _dot(jnp.concatenate([qe, sc.astype(BF16)], axis=1),
             jnp.concatenate([s_prev.astype(BF16), vb], axis=0))
    return o, s_new


def _gla_kernel(q_ref, k_ref, v_ref, la_ref, g_ref, sums_ref, pair_ref, o_ref, s_ref):
    @pl.when(pl.program_id(0) == 0)
    def _():
        s_ref[...] = jnp.zeros_like(s_ref)

    row = lax.broadcasted_iota(jnp.int32, (CHUNK, GLA_DK), 0)
    right_rows = [(row & m) != 0 for m in GLA_LEVELS]
    ones_col = jnp.ones((LA_SPLIT * CHUNK, LANES), BF16)
    sums = sums_ref[...]

    def body(c, carry):
        r0 = pl.multiple_of(c * CHUNK, CHUNK)
        rows = pl.ds(r0, CHUNK)
        states = [s_ref[h] for h in range(GLA_HEADS)]
        new_states = []
        for hp in range(GLA_HEADS // 2):
            la3 = _split_bf16(la_ref[rows, 2 * hp * GLA_DK:2 * (hp + 1) * GLA_DK])
            e2 = jnp.exp2(_gla_exponents(_dot(sums, la3)))
            dec2 = jnp.exp2(_dot_tn(la3, ones_col))
            for h in (2 * hp, 2 * hp + 1):
                half = slice((h % 2) * GLA_DK, (h % 2 + 1) * GLA_DK)
                kc = slice(h * GLA_DK, (h + 1) * GLA_DK)
                vc = slice(h * GLA_DV, (h + 1) * GLA_DV)
                q = q_ref[rows, kc].astype(F32) * (GLA_DK ** -0.5)
                k = k_ref[rows, kc].astype(F32)
                o, s_new = _gla_chunk(q, k, v_ref[rows, vc], e2[:, half], dec2[half], states[h],
                                      pair_ref, right_rows)
                new_states.append(s_new)
                o = o * lax.rsqrt(jnp.mean(o * o, axis=-1, keepdims=True) + EPS) * g_ref[:, vc]
                o_ref[rows, vc] = o.astype(BF16)
        for h in range(GLA_HEADS):
            s_ref[h] = new_states[h]
        return carry

    lax.fori_loop(0, TM_GLA // CHUNK, body, 0, unroll=4)


def _gla(q, k, v, la, g):
    sums, pair = _gla_tables()
    row = lambda w: pl.BlockSpec((TM_GLA, w), lambda t: (t, 0))
    full = lambda a: pl.BlockSpec(a.shape, lambda t: (0,) * a.ndim)
    return pl.pallas_call(
        _gla_kernel,
        grid=(LP // TM_GLA,),
        in_specs=[row(QK_W), row(QK_W), row(V_W), row(QK_W), full(g), full(sums), full(pair)],
        out_specs=row(V_W),
        out_shape=jax.ShapeDtypeStruct((LP, V_W), BF16),
        scratch_shapes=[pltpu.VMEM((GLA_HEADS, GLA_DK, GLA_DV), F32)],
        compiler_params=pltpu.CompilerParams(dimension_semantics=("arbitrary",),
                                             vmem_limit_bytes=VMEM_LIMIT),
        name="gla",
    )(q, k, v, la, g, sums, pair)


def _mix_kernel(o_ref, og_ref, u_ref, hist_ref, gt_ref, x_ref, cw_ref, cb_ref, lg_ref, lb_ref,
                wpw_ref, wo_ref, g2_ref, wrh_ref, wrl_ref, br_ref,
                h_o, hn_o, ri_o, rw_o, tabn_o, tabb_o, ubuf, shbuf, cbuf, seen):
    @pl.when(pl.program_id(0) == 0)
    def _():
        seen[...] = jnp.zeros_like(seen)

    for j in range(MIX_TILES):
        r = slice(j * TM_MIX, (j + 1) * TM_MIX)
        hist = hist_ref if j == 0 else u_ref.at[j * TM_MIX - HIST:j * TM_MIX]
        tab = slice(j * SUBLANES, (j + 1) * SUBLANES)
        _mix_tile(o_ref.at[r], og_ref.at[r], u_ref.at[r], hist, gt_ref.at[r], x_ref.at[r], cw_ref, cb_ref,
                  lg_ref, lb_ref, wpw_ref, wo_ref, g2_ref, wrh_ref, wrl_ref, br_ref,
                  h_o.at[r], hn_o.at[r], ri_o.at[:, r], rw_o.at[:, r], tabn_o.at[tab], tabb_o.at[tab],
                  ubuf, shbuf, cbuf, seen)


def _mix_tile(o_ref, og_ref, u_ref, hist_ref, gt_ref, x_ref, cw_ref, cb_ref, lg_ref, lb_ref,
              wpw_ref, wo_ref, g2_ref, wrh_ref, wrl_ref, br_ref,
              h_o, hn_o, ri_o, rw_o, tabn_o, tabb_o, ubuf, shbuf, cbuf, seen):
    ubuf[0:HIST, :] = hist_ref[...].astype(F32)
    ubuf[HIST:HIST + TM_MIX, :] = u_ref[...].astype(F32)

    lead = HIST - (CONV_WIDTH - 1)
    for rho in range(1, SUBLANES):
        shbuf[rho - 1] = ubuf[rho:rho + SH_ROWS, :]
    for cb in range(D_MODEL // LANES):
        lanes = slice(cb * LANES, (cb + 1) * LANES)
        for rb in range(TM_MIX // CONV_RB):
            part = jnp.broadcast_to(cb_ref[:, lanes], (CONV_RB, LANES))
            for j in range(CONV_WIDTH):
                rho = (lead + j) % SUBLANES
                r0 = rb * CONV_RB + (lead + j) - rho
                src = ubuf[r0:r0 + CONV_RB, lanes] if rho == 0 else shbuf[rho - 1, r0:r0 + CONV_RB, lanes]
                part = part + cw_ref[j:j + 1, lanes] * src
            cbuf[rb * CONV_RB:(rb + 1) * CONV_RB, lanes] = part
    acc = cbuf[...]

    mu = jnp.mean(acc, axis=-1, keepdims=True)
    xc = acc - mu
    ln = xc * lax.rsqrt(jnp.mean(xc * xc, axis=-1, keepdims=True) + EPS) * lg_ref[...] + lb_ref[...]
    branch_b = _dot(_silu(ln).astype(BF16), wpw_ref[...])

    branch_a = o_ref[...].astype(F32) * og_ref[...].astype(F32)
    g_a = gt_ref[:, 0:D_MODEL].astype(F32)
    g_b = gt_ref[:, D_MODEL:2 * D_MODEL].astype(F32)
    merged = (g_a * branch_a + g_b * branch_b).astype(BF16)
    h1 = x_ref[...] + _dot(merged, wo_ref[...])
    h_o[...] = h1

    hn2 = h1 * lax.rsqrt(jnp.mean(h1 * h1, axis=-1, keepdims=True) + EPS) * g2_ref[...]
    hn_hi = hn2.astype(BF16)
    hn_o[...] = hn_hi

    hn_lo = (hn2 - hn_hi.astype(F32)).astype(BF16)
    logits = (_dot_nt(wrh_ref[...], hn_hi) + _dot_nt(wrh_ref[...], hn_lo) + _dot_nt(wrl_ref[...], hn_hi)
              + br_ref[...])
    row = lax.broadcasted_iota(jnp.int32, logits.shape, 0)
    rowf = row.astype(F32)
    neg = -jnp.inf
    is_g = row < N_GROUPS
    lg = jnp.where(is_g, logits, neg)
    gmax = jnp.max(lg, axis=0, keepdims=True)
    gidx = jnp.min(jnp.where(lg == gmax, rowf, float(N_GROUPS)), axis=0, keepdims=True)
    g_w = 1.0 / jnp.sum(jnp.where(is_g, jnp.exp(lg - gmax), 0.0), axis=0, keepdims=True)
    erow = rowf - float(N_GROUPS)
    egrp = ((row - N_GROUPS) >> 3).astype(F32)
    in_grp = (row >= N_GROUPS) & (row < N_GROUPS + N_EXPERTS) & (egrp == gidx)
    le = jnp.where(in_grp, logits, neg)
    m1 = jnp.max(le, axis=0, keepdims=True)
    i1 = jnp.min(jnp.where(le == m1, erow, float(N_EXPERTS)), axis=0, keepdims=True)
    le2 = jnp.where(erow == i1, neg, le)
    m2 = jnp.max(le2, axis=0, keepdims=True)
    i2 = jnp.min(jnp.where(le2 == m2, erow, float(N_EXPERTS)), axis=0, keepdims=True)
    t = jnp.exp(m2 - m1)
    w1 = g_w / (1.0 + t)
    w2 = g_w * t / (1.0 + t)

    def run_len(n):
        return jnp.floor((n + (TOK_ALIGN - 1)) * (1.0 / TOK_ALIGN)) * TOK_ALIGN

    oh1 = erow == i1
    oh2 = erow == i2
    oh = jnp.where(oh1 | oh2, 1.0, 0.0).astype(BF16)
    tr = lax.broadcasted_iota(jnp.int32, (TM_MIX, TM_MIX), 0)
    tc = lax.broadcasted_iota(jnp.int32, (TM_MIX, TM_MIX), 1)
    before_tok = _dot(oh, (tr < tc).astype(BF16))
    n_col = jnp.sum(oh.astype(F32), axis=1, keepdims=True)
    er = lax.broadcasted_iota(jnp.int32, (LANES, LANES), 0)
    ec = lax.broadcasted_iota(jnp.int32, (LANES, LANES), 1)
    before_exp = _dot((ec < er).astype(BF16), jnp.broadcast_to(run_len(n_col), logits.shape).astype(BF16))
    where_to = before_exp + before_tok
    q1 = jnp.sum(jnp.where(oh1, where_to, 0.0), axis=0, keepdims=True)
    q2 = jnp.sum(jnp.where(oh2, where_to, 0.0), axis=0, keepdims=True)

    n_rows = run_len(_dot_nt(jnp.ones((SUBLANES, TM_MIX), BF16), oh))
    tabn_o[...] = n_rows
    tabb_o[...] = seen[...]
    seen[...] = seen[...] + n_rows

    ri_o[...] = jnp.zeros_like(ri_o)
    ri_o[0:1, :] = q1.astype(jnp.int32)
    ri_o[1:2, :] = q2.astype(jnp.int32)
    rw_o[...] = jnp.zeros_like(rw_o)
    rw_o[0:1, :] = w1
    rw_o[1:2, :] = w2


def _mix(o_n, og, u, gates, x2d, cw, cb, lg, lb, wpw, wo, g2, wrh, wrl, br):
    tm = MIX_TILES * TM_MIX
    n_front = FRONT // tm
    grid = (SEQ // tm,)
    rowp = lambda w: pl.BlockSpec((tm, w), lambda i: (i + n_front, 0))
    full = lambda a: pl.BlockSpec(a.shape, lambda i: (0,) * a.ndim)
    hist_blocks = tm // HIST
    return pl.pallas_call(
        _mix_kernel,
        grid=grid,
        in_specs=[
            rowp(V_W), rowp(D_MODEL), rowp(D_MODEL),
            pl.BlockSpec((HIST, D_MODEL), lambda i: ((i + n_front) * hist_blocks - 1, 0)),
            rowp(2 * D_MODEL),
            pl.BlockSpec((tm, D_MODEL), lambda i: (i, 0)),
            full(cw), full(cb), full(lg), full(lb), full(wpw), full(wo), full(g2),
            full(wrh), full(wrl), full(br),
        ],
        out_specs=[
            pl.BlockSpec((tm, D_MODEL), lambda i: (i, 0)),
            pl.BlockSpec((tm, D_MODEL), lambda i: (i, 0)),
            pl.BlockSpec((SUBLANES, tm), lambda i: (0, i)),
            pl.BlockSpec((SUBLANES, tm), lambda i: (0, i)),
            pl.BlockSpec((MIX_TILES * SUBLANES, LANES), lambda i: (i, 0)),
            pl.BlockSpec((MIX_TILES * SUBLANES, LANES), lambda i: (i, 0)),
        ],
        out_shape=[
            jax.ShapeDtypeStruct((SEQ, D_MODEL), F32),
            jax.ShapeDtypeStruct((SEQ, D_MODEL), BF16),
            jax.ShapeDtypeStruct((SUBLANES, SEQ), jnp.int32),
            jax.ShapeDtypeStruct((SUBLANES, SEQ), F32),
            jax.ShapeDtypeStruct((N_TILES * SUBLANES, LANES), F32),
            jax.ShapeDtypeStruct((N_TILES * SUBLANES, LANES), F32),
        ],
        scratch_shapes=[pltpu.VMEM((HIST + TM_MIX, D_MODEL), F32),
                        pltpu.VMEM((SUBLANES - 1, SH_ROWS, D_MODEL), F32),
                        pltpu.VMEM((TM_MIX, D_MODEL), F32), pltpu.VMEM((SUBLANES, LANES), F32)],
        compiler_params=pltpu.CompilerParams(dimension_semantics=("arbitrary",),
                                             vmem_limit_bytes=VMEM_LIMIT),
        name="mix",
    )(o_n, og, u, u, gates, x2d, cw, cb, lg, lb, wpw, wo, g2, wrh, wrl, br)


def _tok_rows(start_tok, n_tok):
    return pl.ds(pl.multiple_of(start_tok * TOK_ROWS, SUBLANES), n_tok * TOK_ROWS)


def _start_tile_runs(n_ref, row_ref, tile, make_copy):
    def run(e, off):
        n = n_ref[tile * N_EXPERTS + e]

        @pl.when(n > 0)
        def _():
            make_copy(_tok_rows(off, n), _tok_rows(row_ref[tile * N_EXPERTS + e], n)).start()

        return off + n

    lax.fori_loop(0, N_EXPERTS, run, 0, unroll=8)


def _dispatch_kernel(n_ref, row_ref, tot_ref, ts_ref, tl_ref, nused_ref, hn_ref, q_ref, xb_ref,
                     sb0, sb1, zbuf, sem, fsem):
    step = pl.program_id(0)
    last = pl.num_programs(0) - 1

    def fills(start):
        def tail(e, c):
            n = tl_ref[e]

            @pl.when(n > 0)
            def _():
                cp = pltpu.make_async_copy(zbuf.at[pl.ds(0, n * TOK_ROWS)],
                                           xb_ref.at[_tok_rows(ts_ref[e], n)], fsem)
                cp.start() if start else cp.wait()

            return c

        lax.fori_loop(0, N_EXPERTS, tail, 0)

        def block(b, c):
            cp = pltpu.make_async_copy(zbuf, xb_ref.at[_tok_rows(b * BM, BM)], fsem)
            cp.start() if start else cp.wait()
            return c

        lax.fori_loop(nused_ref[0], N_BLOCKS, block, 0)

    @pl.when(step == 0)
    def _():
        zbuf[...] = jnp.zeros_like(zbuf)
        fills(start=True)

    def tile_runs(tile, s, sbuf):
        rows = pl.ds(0, tot_ref[tile] * TOK_ROWS)
        return pltpu.make_async_copy(sbuf.at[rows], xb_ref.at[rows], sem.at[s])

    for s, sbuf in enumerate((sb0, sb1)):
        tile = step * TILES_PER_STEP + s
        toks = slice(s * TM_MIX, (s + 1) * TM_MIX)
        slot_i = lax.broadcasted_iota(jnp.int32, (SLOTS, TM_MIX), 0)
        onehot = jnp.where((slot_i == q_ref[0:1, toks]) | (slot_i == q_ref[1:2, toks]), 1.0, 0.0).astype(BF16)
        srt = _dot(onehot, hn_ref[toks, :])

        @pl.when(step >= 1)
        def _():
            tile_runs(tile - TILES_PER_STEP, s, sbuf).wait()

        _store_token_tiles(sbuf, srt, SLOTS)
        _start_tile_runs(n_ref, row_ref, tile,
                         lambda loc, glob: pltpu.make_async_copy(sbuf.at[loc], xb_ref.at[glob], sem.at[s]))

    @pl.when(step == last)
    def _():
        for s, sbuf in enumerate((sb0, sb1)):
            tile_runs(step * TILES_PER_STEP + s, s, sbuf).wait()
        fills(start=False)


def _dispatch(n_flat, run_rows, tot, tail_start, tail_len, nused, hn2, ri):
    grid_spec = pltpu.PrefetchScalarGridSpec(
        num_scalar_prefetch=6,
        grid=(N_TILES // TILES_PER_STEP,),
        in_specs=[pl.BlockSpec((TILES_PER_STEP * TM_MIX, D_MODEL), lambda i, *_: (i, 0)),
                  pl.BlockSpec((SUBLANES, TILES_PER_STEP * TM_MIX), lambda i, *_: (0, i))],
        out_specs=pl.BlockSpec(memory_space=pl.ANY),
        scratch_shapes=[pltpu.VMEM((SLOTS * TOK_ROWS, LANES), jnp.uint32),
                        pltpu.VMEM((SLOTS * TOK_ROWS, LANES), jnp.uint32),
                        pltpu.VMEM((BM * TOK_ROWS, LANES), jnp.uint32),
                        pltpu.SemaphoreType.DMA((2,)), pltpu.SemaphoreType.DMA(())],
    )
    return pl.pallas_call(
        _dispatch_kernel,
        grid_spec=grid_spec,
        out_shape=jax.ShapeDtypeStruct((P_ROWS * TOK_ROWS, LANES), jnp.uint32),
        compiler_params=pltpu.CompilerParams(dimension_semantics=("arbitrary",),
                                             vmem_limit_bytes=VMEM_LIMIT),
        name="dispatch",
    )(n_flat, run_rows, tot, tail_start, tail_len, nused, hn2, ri)


def _experts_kernel(be_ref, first_ref, next_ref, next2_ref, stage_ref, nused_ref,
                    x_ref, wg_hbm, wu_hbm, wd_hbm, y_ref, wg_f, wu_f, wd_f, wg_b, wu_b, wd_b, sem):
    def weight_copies(e, st):
        return (pltpu.make_async_copy(wg_hbm.at[e], wg_f.at[st], sem.at[st, 0]),
                pltpu.make_async_copy(wu_hbm.at[e], wu_f.at[st], sem.at[st, 1]),
                pltpu.make_async_copy(wd_hbm.at[e], wd_f.at[st], sem.at[st, 2]))

    @pl.when((pl.program_id(0) == 0) & (nused_ref[0] > 0))
    def _():
        for c in weight_copies(be_ref[0], 0):
            c.start()

        @pl.when(next_ref[0] >= 0)
        def _():
            for c in weight_copies(next_ref[0], 1):
                c.start()

    for j in range(BLOCKS_PER_STEP):
        _expert_block(pl.program_id(0) * BLOCKS_PER_STEP + j, j * BM, be_ref, first_ref, next2_ref,
                      stage_ref, nused_ref, x_ref, y_ref, wg_f, wu_f, wd_f, wg_b, wu_b, wd_b, weight_copies)


def _expert_block(b, tok0, be_ref, first_ref, next2_ref, stage_ref, nused_ref, x_ref, y_ref,
                  wg_f, wu_f, wd_f, wg_b, wu_b, wd_b, weight_copies):
    @pl.when(b < nused_ref[0])
    def _():
        def new_expert(st):
            for c in weight_copies(be_ref[b], st):
                c.wait()
            wg_b[...] = wg_f[st].astype(BF16)
            wu_b[...] = wu_f[st].astype(BF16)
            wd_b[...] = wd_f[st].astype(BF16)

            @pl.when(next2_ref[b] >= 0)
            def _():
                for c in weight_copies(next2_ref[b], st):
                    c.start()

        for st in range(2):
            pl.when((first_ref[b] == 1) & (stage_ref[b] == st))(functools.partial(new_expert, st))

        x = _load_token_tiles(x_ref, BM, tok0)
        a = _dot(x, wg_b[...])
        u = _dot(x, wu_b[...])
        y = _dot((_silu(a) * u).astype(BF16), wd_b[...])
        _store_token_tiles(y_ref, y.astype(BF16).astype(F32), BM, tok0)

    @pl.when(b >= nused_ref[0])
    def _():
        y_ref[tok0 * TOK_ROWS:(tok0 + BM) * TOK_ROWS, :] = jnp.zeros((BM * TOK_ROWS, LANES), jnp.uint32)


def _experts(block_e, first, next_e, next2_e, stage, nused, xb, wg, wu, wd):
    def xmap(b, *_):
        return (b, 0)

    grid_spec = pltpu.PrefetchScalarGridSpec(
        num_scalar_prefetch=6,
        grid=(N_BLOCKS // BLOCKS_PER_STEP,),
        in_specs=[
            pl.BlockSpec((BLOCKS_PER_STEP * BM * TOK_ROWS, LANES), xmap),
            pl.BlockSpec(memory_space=pl.ANY),
            pl.BlockSpec(memory_space=pl.ANY),
            pl.BlockSpec(memory_space=pl.ANY),
        ],
        out_specs=pl.BlockSpec((BLOCKS_PER_STEP * BM * TOK_ROWS, LANES), xmap),
        scratch_shapes=[pltpu.VMEM((2, D_MODEL, D_EXPERT), F32), pltpu.VMEM((2, D_MODEL, D_EXPERT), F32),
                        pltpu.VMEM((2, D_EXPERT, D_MODEL), F32),
                        pltpu.VMEM((D_MODEL, D_EXPERT), BF16), pltpu.VMEM((D_MODEL, D_EXPERT), BF16),
                        pltpu.VMEM((D_EXPERT, D_MODEL), BF16), pltpu.SemaphoreType.DMA((2, 3))],
    )
    return pl.pallas_call(
        _experts_kernel,
        grid_spec=grid_spec,
        out_shape=jax.ShapeDtypeStruct((P_ROWS * TOK_ROWS, LANES), jnp.uint32),
        compiler_params=pltpu.CompilerParams(dimension_semantics=("arbitrary",),
                                             vmem_limit_bytes=VMEM_LIMIT),
        name="experts",
    )(block_e, first, next_e, next2_e, stage, nused, xb, wg, wu, wd)


def _combine_kernel(n_ref, row_ref, tot_ref, yb_ref, h_ref, q_ref, w_ref, g_ref, out_ref, yb0, yb1, sem):
    step = pl.program_id(0)
    bufs = (yb0, yb1)

    def start_gather(tile, s):
        _start_tile_runs(n_ref, row_ref, tile,
                         lambda loc, glob: pltpu.make_async_copy(yb_ref.at[glob], bufs[s].at[loc], sem.at[s]))

    @pl.when(step == 0)
    def _():
        for buf in bufs:
            buf[...] = jnp.zeros_like(buf)
        start_gather(0, 0)

    for s in range(TILES_PER_STEP):
        tile = step * TILES_PER_STEP + s
        if s + 1 < TILES_PER_STEP:
            start_gather(tile + 1, s + 1)
        else:
            pl.when(step + 1 < pl.num_programs(0))(functools.partial(start_gather, tile + 1, 0))

        rows = pl.ds(0, tot_ref[tile] * TOK_ROWS)
        pltpu.make_async_copy(yb_ref.at[rows], bufs[s].at[rows], sem.at[s]).wait()
        ys = _load_token_tiles(bufs[s], SLOTS)
        toks = slice(s * TM_MIX, (s + 1) * TM_MIX)
        col = lax.broadcasted_iota(jnp.int32, (TM_MIX, SLOTS), 1)
        wmat = (jnp.where(col == q_ref[toks, 0:1], w_ref[toks, 0:1], 0.0)
                + jnp.where(col == q_ref[toks, 1:2], w_ref[toks, 1:2], 0.0)).astype(BF16)
        hh = h_ref[toks, :] + _dot(wmat, ys)
        out_ref[toks, :] = hh * lax.rsqrt(jnp.mean(hh * hh, axis=-1, keepdims=True) + EPS) * g_ref[...]


def _combine(n_flat, run_rows, tot, yb, h1, qt, wts, gf):
    grid_spec = pltpu.PrefetchScalarGridSpec(
        num_scalar_prefetch=3,
        grid=(N_TILES // TILES_PER_STEP,),
        in_specs=[
            pl.BlockSpec(memory_space=pl.ANY),
            pl.BlockSpec((TILES_PER_STEP * TM_MIX, D_MODEL), lambda i, *_: (i, 0)),
            pl.BlockSpec((TILES_PER_STEP * TM_MIX, TOP_K), lambda i, *_: (i, 0)),
            pl.BlockSpec((TILES_PER_STEP * TM_MIX, TOP_K), lambda i, *_: (i, 0)),
            pl.BlockSpec((1, D_MODEL), lambda i, *_: (0, 0)),
        ],
        out_specs=pl.BlockSpec((TILES_PER_STEP * TM_MIX, D_MODEL), lambda i, *_: (i, 0)),
        scratch_shapes=[pltpu.VMEM((SLOTS * TOK_ROWS, LANES), jnp.uint32),
                        pltpu.VMEM((SLOTS * TOK_ROWS, LANES), jnp.uint32),
                        pltpu.SemaphoreType.DMA((2,))],
    )
    return pl.pallas_call(
        _combine_kernel,
        grid_spec=grid_spec,
        out_shape=jax.ShapeDtypeStruct((SEQ, D_MODEL), F32),
        compiler_params=pltpu.CompilerParams(dimension_semantics=("arbitrary",),
                                             vmem_limit_bytes=VMEM_LIMIT),
        name="combine",
    )(n_flat, run_rows, tot, yb, h1, qt, wts, gf)


def kernel(x, meta, norm1_g, w_in, w_decay_up, b_decay, gla_norm_g, conv_w, conv_b, conv_ln_g,
           conv_ln_b, w_pw2, b_gate, w_out, norm2_g, w_router_group, b_router_group,
           w_router_expert, b_router_expert, w_exp_gate, w_exp_up, w_exp_down, final_norm_g):
    assert x.shape == (1, SEQ, D_MODEL) and w_in.shape[0] == 1
    x2d = x[0]
    front = jnp.concatenate([jnp.zeros((FRONT - N_META, D_MODEL), F32), meta.astype(F32)], axis=0)

    w = w_in[0]
    row2 = lambda a: a.reshape(1, -1).astype(F32)
    wq, wk, wv, wog, wdl, wglu, wgt = _wcast(w)
    up = w_decay_up[0].astype(F32)
    up_hi = up.astype(BF16)
    up_mid = (up - up_hi.astype(F32)).astype(BF16)
    up_lo = (up - up_hi.astype(F32) - up_mid.astype(F32)).astype(BF16)
    wdu = jnp.concatenate([up_hi, up_mid, up_lo, up_hi, up_mid, up_hi], axis=0)
    q, k, v, og, la, u, gates = _inproj(
        front, x2d, row2(norm1_g[0]), wq, wk, wv, wog, jnp.tile(wdl, (1, DL_COPIES)), wglu, wgt,
        wdu, row2(b_decay[0]), row2(b_gate[0]))

    o_n = _gla(q, k, v, la, row2(gla_norm_g[0]))

    wr = jnp.concatenate([w_router_group[0].T, w_router_expert[0].T,
                          jnp.zeros((LANES - N_GROUPS - N_EXPERTS, D_MODEL), F32)], axis=0).astype(F32)
    br = jnp.concatenate([b_router_group[0], b_router_expert[0],
                          jnp.zeros((LANES - N_GROUPS - N_EXPERTS,), F32)]).reshape(LANES, 1).astype(F32)
    wr_hi = wr.astype(BF16)
    wr_lo = (wr - wr_hi.astype(F32)).astype(BF16)
    h1, hn2, ri, rw, tabn, tabb = _mix(
        o_n, og, u, gates, x2d, conv_w[0].astype(F32), row2(conv_b[0]), row2(conv_ln_g[0]),
        row2(conv_ln_b[0]), w_pw2[0].astype(BF16), w_out[0].astype(BF16),
        row2(norm2_g[0]), wr_hi, wr_lo, br)

    experts = slice(N_GROUPS, N_GROUPS + N_EXPERTS)
    n_te = tabn[::SUBLANES, experts].astype(jnp.int32)
    seen_te = tabb[::SUBLANES, experts].astype(jnp.int32)
    counts = seen_te[-1] + n_te[-1]
    padded = (counts + BM - 1) // BM * BM
    pad_end = jnp.cumsum(padded)
    pad_start = pad_end - padded
    n_flat = n_te.reshape(-1)
    tot = jnp.sum(n_te, axis=1).astype(jnp.int32)
    run_rows = (pad_start[None, :] + seen_te).reshape(-1).astype(jnp.int32)
    tail_start = (pad_start + counts).astype(jnp.int32)
    tail_len = (padded - counts).astype(jnp.int32)
    blk = jnp.arange(N_BLOCKS, dtype=jnp.int32)
    block_e = jnp.minimum(jnp.sum((pad_end[None, :] <= blk[:, None] * BM).astype(jnp.int32), axis=1),
                          N_EXPERTS - 1).astype(jnp.int32)
    first = jnp.concatenate([jnp.ones((1,), jnp.int32), (block_e[1:] != block_e[:-1]).astype(jnp.int32)])
    nused = (pad_end[-1:] // BM).astype(jnp.int32)
    eid = jnp.arange(N_EXPERTS, dtype=jnp.int32)
    later = jnp.flip(lax.cummin(jnp.flip(jnp.where(counts > 0, eid, N_EXPERTS))))
    nxt = jnp.concatenate([later[1:], jnp.full((2,), N_EXPERTS, jnp.int32)])
    nxt2 = nxt[nxt[:N_EXPERTS]]
    or_none = lambda t: jnp.where(t < N_EXPERTS, t, -1)[block_e].astype(jnp.int32)
    next_e, next2_e = or_none(nxt[:N_EXPERTS]), or_none(nxt2)
    stage = ((jnp.cumsum(first) - 1) % 2).astype(jnp.int32)

    xb = _dispatch(n_flat, run_rows, tot, tail_start, tail_len, nused, hn2, ri)
    yb = _experts(block_e, first, next_e, next2_e, stage, nused, xb, w_exp_gate[0], w_exp_up[0], w_exp_down[0])
    out = _combine(n_flat, run_rows, tot, yb, h1, ri[0:TOP_K].T, rw[0:TOP_K].T, row2(final_norm_g))
    return out[None]
```

```python
import functools

import jax
import jax.numpy as jnp
import numpy as np
from jax import lax
from jax.experimental import pallas as pl
from jax.experimental.pallas import tpu as pltpu

F32 = jnp.float32
BF16 = jnp.bfloat16

D_MODEL = 1024
SEQ = 16384
N_META = 16
GLA_HEADS = 4
GLA_DK = 128
GLA_DV = 256
GLA_RANK = 16
GLA_TAU = 16.0
CHUNK = 64
QK_W = GLA_HEADS * GLA_DK
V_W = GLA_HEADS * GLA_DV
CONV_WIDTH = 31
N_GROUPS = 8
EXPERTS_PER_GROUP = 8
N_EXPERTS = N_GROUPS * EXPERTS_PER_GROUP
TOP_K = 2
D_EXPERT = 512
EPS = 1e-6
LOG2E = 1.4426950408889634

LANES = 128
SUBLANES = 8
TOK_ROWS = D_MODEL // 2 // LANES
TOK_ALIGN = SUBLANES // TOK_ROWS

FRONT = 512
LP = FRONT + SEQ
TM_PROJ = 512
DL_COPIES = 6
TM_GLA = 512
TM_MIX = 256
MIX_TILES = 2
HIST = 32
SH_ROWS = TM_MIX + HIST - SUBLANES
CONV_RB = 64
N_TILES = SEQ // TM_MIX
SLOTS = TOP_K * TM_MIX + N_EXPERTS * (TOK_ALIGN - 1)
TILES_PER_STEP = 2
BM = 256
BLOCKS_PER_STEP = 4
N_BLOCKS = (N_TILES * SLOTS) // BM + N_EXPERTS
P_ROWS = N_BLOCKS * BM
VMEM_LIMIT = 56 * 1024 * 1024


def _sigmoid(x):
    return 0.5 * jnp.tanh(0.5 * x) + 0.5


def _silu(x):
    return x * _sigmoid(x)


def _dot(a, b, **kw):
    return jnp.dot(a, b, preferred_element_type=F32, **kw)


def _load_token_tiles(ref, n_tok, first_tok=0):
    r0 = first_tok * TOK_ROWS
    w = jnp.concatenate([ref[pl.ds(r0 + s, n_tok, stride=TOK_ROWS), :] for s in range(TOK_ROWS)], axis=-1)
    lo = pltpu.bitcast(w << 16, F32)
    hi = pltpu.bitcast(w & jnp.uint32(0xFFFF0000), F32)
    return jnp.concatenate([lo, hi], axis=-1).astype(BF16)


def _store_token_tiles(ref, val, n_tok, first_tok=0):
    bits = pltpu.bitcast(val, jnp.uint32)
    w = (bits[:, 0:D_MODEL // 2] >> 16) | (bits[:, D_MODEL // 2:D_MODEL] & jnp.uint32(0xFFFF0000))
    r0 = first_tok * TOK_ROWS
    for s in range(TOK_ROWS):
        ref[pl.ds(r0 + s, n_tok, stride=TOK_ROWS), :] = w[:, s * LANES:(s + 1) * LANES]


IN_SPLITS = (QK_W, QK_W, V_W, D_MODEL, GLA_RANK, 2 * D_MODEL, 2 * D_MODEL)
WCAST_ROWS = 128


def _wcast_kernel(w_ref, *outs):
    w = w_ref[...]
    off = 0
    for out, n in zip(outs, IN_SPLITS):
        out[...] = w[:, off:off + n].astype(BF16)
        off += n


def _wcast(w):
    return pl.pallas_call(
        _wcast_kernel,
        grid=(D_MODEL // WCAST_ROWS,),
        in_specs=[pl.BlockSpec((WCAST_ROWS, sum(IN_SPLITS)), lambda i: (i, 0))],
        out_specs=[pl.BlockSpec((WCAST_ROWS, n), lambda i: (i, 0)) for n in IN_SPLITS],
        out_shape=[jax.ShapeDtypeStruct((D_MODEL, n), BF16) for n in IN_SPLITS],
        compiler_params=pltpu.CompilerParams(dimension_semantics=("arbitrary",),
                                             vmem_limit_bytes=VMEM_LIMIT),
        name="wcast",
    )(w)


def _inproj_kernel(front_ref, x_ref, g_ref, wq, wk, wv, wog, wdl, wglu, wgt, wdu, bd, bg,
                   q_o, k_o, v_o, og_o, la_o, u_o, gt_o):
    i = pl.program_id(0)
    h = jnp.where(i < FRONT // TM_PROJ, front_ref[...], x_ref[...])
    ms = jnp.mean(h * h, axis=-1, keepdims=True)
    hn = (h * lax.rsqrt(ms + EPS) * g_ref[...]).astype(BF16)
    q_o[...] = _dot(hn, wq[...]).astype(BF16)
    k_o[...] = _dot(hn, wk[...]).astype(BF16)
    v_o[...] = _dot(hn, wv[...]).astype(BF16)
    og_o[...] = _silu(_dot(hn, wog[...])).astype(BF16)
    glu = _dot(hn, wglu[...])
    u_o[...] = (glu[:, 0:D_MODEL] * _sigmoid(glu[:, D_MODEL:2 * D_MODEL])).astype(BF16)
    gt_o[...] = _sigmoid(_dot(hn, wgt[...]) + bg[...]).astype(BF16)
    d6 = _dot(hn, wdl[...])
    hi = d6.astype(BF16).astype(F32)
    mid = (d6 - hi).astype(BF16).astype(F32)
    lane = lax.broadcasted_iota(jnp.int32, d6.shape, 1)
    pieces = jnp.where(lane < 3 * GLA_RANK, hi, jnp.where(lane < 5 * GLA_RANK, mid, d6 - hi - mid))
    z = _dot(pieces.astype(BF16), wdu[...]) + bd[...]
    la_o[...] = (jnp.minimum(z, 0.0) - jnp.log(1.0 + jnp.exp(-jnp.abs(z)))) * (LOG2E / GLA_TAU)


def _inproj(front, x2d, g1, wq, wk, wv, wog, wdl, wglu, wgt, wdu, bd, bg):
    n_front = FRONT // TM_PROJ
    grid = (LP // TM_PROJ,)
    row = lambda w: pl.BlockSpec((TM_PROJ, w), lambda i: (i, 0))
    full = lambda a: pl.BlockSpec(a.shape, lambda i: (0, 0))
    return pl.pallas_call(
        _inproj_kernel,
        grid=grid,
        in_specs=[
            pl.BlockSpec((TM_PROJ, D_MODEL), lambda i: (jnp.minimum(i, n_front - 1), 0)),
            pl.BlockSpec((TM_PROJ, D_MODEL), lambda i: (jnp.maximum(i - n_front, 0), 0)),
            full(g1), full(wq), full(wk), full(wv), full(wog), full(wdl), full(wglu), full(wgt),
            full(wdu), full(bd), full(bg),
        ],
        out_specs=[row(QK_W), row(QK_W), row(V_W), row(D_MODEL), row(QK_W), row(D_MODEL),
                   row(2 * D_MODEL)],
        out_shape=[
            jax.ShapeDtypeStruct((LP, QK_W), BF16), jax.ShapeDtypeStruct((LP, QK_W), BF16),
            jax.ShapeDtypeStruct((LP, V_W), BF16), jax.ShapeDtypeStruct((LP, D_MODEL), BF16),
            jax.ShapeDtypeStruct((LP, QK_W), F32), jax.ShapeDtypeStruct((LP, D_MODEL), BF16),
            jax.ShapeDtypeStruct((LP, 2 * D_MODEL), BF16),
        ],
        compiler_params=pltpu.CompilerParams(dimension_semantics=("arbitrary",),
                                             vmem_limit_bytes=VMEM_LIMIT),
        name="inproj",
    )(front, x2d, g1, wq, wk, wv, wog, wdl, wglu, wgt, wdu, bd, bg)


GLA_LEVELS = (32, 16, 8, 4, 2, 1)
GLA_FINE = tuple(m for m in GLA_LEVELS if m < SUBLANES)
LA_SPLIT = 3


def _gla_tables():
    r = np.arange(CHUNK)
    t, c = r[:, None], r[None, :]
    rows = [c <= t]
    pair = []
    for m in GLA_LEVELS:
        mid = (t // (2 * m)) * (2 * m) + m - 1
        right = (t % (2 * m)) >= m
        if m in GLA_FINE:
            rows.append(np.where(right, (c > mid) & (c <= t), (c > t) & (c <= mid)))
        pair.append(((t // (2 * m)) == (c // (2 * m))) & right & ((c % (2 * m)) < m))
    pair.append(t == c)
    sums = np.concatenate(rows, axis=0).astype(np.float32)
    sums = np.concatenate([sums] * LA_SPLIT, axis=1)
    pair = np.stack([np.kron(np.eye(GLA_HEADS), p) for p in pair]).astype(np.float32)
    return jnp.asarray(sums, BF16), jnp.asarray(pair)


def _split_bf16(x):
    pieces = []
    rest = x
    for _ in range(LA_SPLIT):
        p = rest.astype(BF16)
        pieces.append(p)
        rest = rest - p.astype(F32)
    return jnp.concatenate(pieces, axis=0)


def _dot_nt(a, b):
    return lax.dot_general(a, b, (((1,), (1,)), ((), ())), preferred_element_type=F32)


def _dot_tn(a, b):
    return lax.dot_general(a, b, (((0,), (0,)), ((), ())), preferred_element_type=F32)


def _gla_exponents(a):
    b = a[0:CHUNK]
    parts = [b, b[CHUNK - 1:CHUNK] - b]
    for m in GLA_LEVELS:
        if m in GLA_FINE:
            continue
        for lo in range(0, CHUNK, 2 * m):
            ref = b[lo + m - 1:lo + m]
            parts += [ref - b[lo:lo + m], b[lo + m:lo + 2 * m] - ref]
    return jnp.concatenate(parts + [a[CHUNK:]], axis=0)


def _gla_kernel(q_ref, k_ref, v_ref, la_ref, g_ref, sums_ref, pair_ref, o_ref, s_ref):
    @pl.when(pl.program_id(0) == 0)
    def _():
        s_ref[...] = jnp.zeros_like(s_ref)

    row = lax.broadcasted_iota(jnp.int32, (CHUNK, GLA_DK), 0)
    right_rows = [(row & m) != 0 for m in GLA_LEVELS]
    ones_col = jnp.ones((LA_SPLIT * CHUNK, LANES), BF16)
    sums = sums_ref[...]
    n_lvl = len(GLA_LEVELS)

    def body(c, carry):
        r0 = pl.multiple_of(c * CHUNK, CHUNK)
        rows = pl.ds(r0, CHUNK)
        states = [s_ref[h] for h in range(GLA_HEADS)]
        qs, ks, es, decs = [], [], [], []
        for hp in range(GLA_HEADS // 2):
            la3 = _split_bf16(la_ref[rows, 2 * hp * GLA_DK:2 * (hp + 1) * GLA_DK])
            e2 = jnp.exp2(_gla_exponents(_dot(sums, la3)))
            dec2 = jnp.exp2(_dot_tn(la3, ones_col))
            for h in (2 * hp, 2 * hp + 1):
                half = slice((h % 2) * GLA_DK, (h % 2 + 1) * GLA_DK)
                kc = slice(h * GLA_DK, (h + 1) * GLA_DK)
                qs.append(q_ref[rows, kc].astype(F32) * (GLA_DK ** -0.5))
                ks.append(k_ref[rows, kc].astype(F32))
                es.append(e2[:, half])
                decs.append(dec2[half])
        vbs = [v_ref[rows, h * GLA_DV:(h + 1) * GLA_DV] for h in range(GLA_HEADS)]

        new_states, qes = [], []
        for h in range(GLA_HEADS):
            qes.append((qs[h] * es[h][0:CHUNK]).astype(BF16))
            kd = (ks[h] * es[h][CHUNK:2 * CHUNK]).astype(BF16)
            new_states.append(states[h] * jnp.concatenate([decs[h]] * (GLA_DV // LANES), axis=1)
                              + _dot_tn(kd, vbs[h]))

        stack = lambda parts: jnp.concatenate(parts, axis=0)
        sc = _dot_nt(stack([q.astype(BF16) for q in qs]), stack([k.astype(BF16) for k in ks])) * pair_ref[n_lvl]
        for lvl in range(n_lvl):
            x = stack([(jnp.where(right_rows[lvl], qs[h], ks[h])
                        * es[h][(2 + lvl) * CHUNK:(3 + lvl) * CHUNK]).astype(BF16) for h in range(GLA_HEADS)])
            sc = sc + _dot_nt(x, x) * pair_ref[lvl]
        o_all = _dot(sc.astype(BF16), stack(vbs))
        for h in range(GLA_HEADS):
            vc = slice(h * GLA_DV, (h + 1) * GLA_DV)
            o = o_all[h * CHUNK:(h + 1) * CHUNK] + _dot(qes[h], states[h].astype(BF16))
            o = o * lax.rsqrt(jnp.mean(o * o, axis=-1, keepdims=True) + EPS) * g_ref[:, vc]
            o_ref[rows, vc] = o.astype(BF16)
        for h in range(GLA_HEADS):
            s_ref[h] = new_states[h]
        return carry

    lax.fori_loop(0, TM_GLA // CHUNK, body, 0, unroll=4)


def _gla(q, k, v, la, g):
    sums, pair = _gla_tables()
    row = lambda w: pl.BlockSpec((TM_GLA, w), lambda t: (t, 0))
    full = lambda a: pl.BlockSpec(a.shape, lambda t: (0,) * a.ndim)
    return pl.pallas_call(
        _gla_kernel,
        grid=(LP // TM_GLA,),
        in_specs=[row(QK_W), row(QK_W), row(V_W), row(QK_W), full(g), full(sums), full(pair)],
        out_specs=row(V_W),
        out_shape=jax.ShapeDtypeStruct((LP, V_W), BF16),
        scratch_shapes=[pltpu.VMEM((GLA_HEADS, GLA_DK, GLA_DV), F32)],
        compiler_params=pltpu.CompilerParams(dimension_semantics=("arbitrary",),
                                             vmem_limit_bytes=VMEM_LIMIT),
        name="gla",
    )(q, k, v, la, g, sums, pair)


def _mix_kernel(o_ref, og_ref, u_ref, hist_ref, gt_ref, x_ref, cw_ref, cb_ref, lg_ref, lb_ref,
                wpw_ref, wo_ref, g2_ref, wrh_ref, wrl_ref, br_ref,
                h_o, hn_o, ri_o, rw_o, tabn_o, tabb_o, ubuf, shbuf, cbuf, seen):
    @pl.when(pl.program_id(0) == 0)
    def _():
        seen[...] = jnp.zeros_like(seen)

    for j in range(MIX_TILES):
        r = slice(j * TM_MIX, (j + 1) * TM_MIX)
        hist = hist_ref if j == 0 else u_ref.at[j * TM_MIX - HIST:j * TM_MIX]
        tab = slice(j * SUBLANES, (j + 1) * SUBLANES)
        _mix_tile(o_ref.at[r], og_ref.at[r], u_ref.at[r], hist, gt_ref.at[r], x_ref.at[r], cw_ref, cb_ref,
                  lg_ref, lb_ref, wpw_ref, wo_ref, g2_ref, wrh_ref, wrl_ref, br_ref,
                  h_o.at[r], hn_o.at[r], ri_o.at[:, r], rw_o.at[:, r], tabn_o.at[tab], tabb_o.at[tab],
                  ubuf, shbuf, cbuf, seen)


def _mix_tile(o_ref, og_ref, u_ref, hist_ref, gt_ref, x_ref, cw_ref, cb_ref, lg_ref, lb_ref,
              wpw_ref, wo_ref, g2_ref, wrh_ref, wrl_ref, br_ref,
              h_o, hn_o, ri_o, rw_o, tabn_o, tabb_o, ubuf, shbuf, cbuf, seen):
    ubuf[0:HIST, :] = hist_ref[...].astype(F32)
    ubuf[HIST:HIST + TM_MIX, :] = u_ref[...].astype(F32)

    lead = HIST - (CONV_WIDTH - 1)
    for rho in range(1, SUBLANES):
        shbuf[rho - 1] = ubuf[rho:rho + SH_ROWS, :]
    for cb in range(D_MODEL // LANES):
        lanes = slice(cb * LANES, (cb + 1) * LANES)
        for rb in range(TM_MIX // CONV_RB):
            part = jnp.broadcast_to(cb_ref[:, lanes], (CONV_RB, LANES))
            for j in range(CONV_WIDTH):
                rho = (lead + j) % SUBLANES
                r0 = rb * CONV_RB + (lead + j) - rho
                src = ubuf[r0:r0 + CONV_RB, lanes] if rho == 0 else shbuf[rho - 1, r0:r0 + CONV_RB, lanes]
                part = part + cw_ref[j:j + 1, lanes] * src
            cbuf[rb * CONV_RB:(rb + 1) * CONV_RB, lanes] = part
    acc = cbuf[...]

    mu = jnp.mean(acc, axis=-1, keepdims=True)
    xc = acc - mu
    ln = xc * lax.rsqrt(jnp.mean(xc * xc, axis=-1, keepdims=True) + EPS) * lg_ref[...] + lb_ref[...]
    branch_b = _dot(_silu(ln).astype(BF16), wpw_ref[...])

    branch_a = o_ref[...].astype(F32) * og_ref[...].astype(F32)
    g_a = gt_ref[:, 0:D_MODEL].astype(F32)
    g_b = gt_ref[:, D_MODEL:2 * D_MODEL].astype(F32)
    merged = (g_a * branch_a + g_b * branch_b).astype(BF16)
    h1 = x_ref[...] + _dot(merged, wo_ref[...])
    h_o[...] = h1

    hn2 = h1 * lax.rsqrt(jnp.mean(h1 * h1, axis=-1, keepdims=True) + EPS) * g2_ref[...]
    hn_hi = hn2.astype(BF16)
    hn_o[...] = hn_hi

    hn_lo = (hn2 - hn_hi.astype(F32)).astype(BF16)
    logits = (_dot_nt(wrh_ref[...], hn_hi) + _dot_nt(wrh_ref[...], hn_lo) + _dot_nt(wrl_ref[...], hn_hi)
              + br_ref[...])
    row = lax.broadcasted_iota(jnp.int32, logits.shape, 0)
    rowf = row.astype(F32)
    neg = -jnp.inf
    is_g = row < N_GROUPS
    lg = jnp.where(is_g, logits, neg)
    gmax = jnp.max(lg, axis=0, keepdims=True)
    gidx = jnp.min(jnp.where(lg == gmax, rowf, float(N_GROUPS)), axis=0, keepdims=True)
    g_w = 1.0 / jnp.sum(jnp.where(is_g, jnp.exp(lg - gmax), 0.0), axis=0, keepdims=True)
    erow = rowf - float(N_GROUPS)
    egrp = ((row - N_GROUPS) >> 3).astype(F32)
    in_grp = (row >= N_GROUPS) & (row < N_GROUPS + N_EXPERTS) & (egrp == gidx)
    le = jnp.where(in_grp, logits, neg)
    m1 = jnp.max(le, axis=0, keepdims=True)
    i1 = jnp.min(jnp.where(le == m1, erow, float(N_EXPERTS)), axis=0, keepdims=True)
    le2 = jnp.where(erow == i1, neg, le)
    m2 = jnp.max(le2, axis=0, keepdims=True)
    i2 = jnp.min(jnp.where(le2 == m2, erow, float(N_EXPERTS)), axis=0, keepdims=True)
    t = jnp.exp(m2 - m1)
    w1 = g_w / (1.0 + t)
    w2 = g_w * t / (1.0 + t)

    def run_len(n):
        return jnp.floor((n + (TOK_ALIGN - 1)) * (1.0 / TOK_ALIGN)) * TOK_ALIGN

    oh1 = erow == i1
    oh2 = erow == i2
    oh = jnp.where(oh1 | oh2, 1.0, 0.0).astype(BF16)
    tr = lax.broadcasted_iota(jnp.int32, (TM_MIX, TM_MIX), 0)
    tc = lax.broadcasted_iota(jnp.int32, (TM_MIX, TM_MIX), 1)
    before_tok = _dot(oh, (tr < tc).astype(BF16))
    n_col = jnp.sum(oh.astype(F32), axis=1, keepdims=True)
    er = lax.broadcasted_iota(jnp.int32, (LANES, LANES), 0)
    ec = lax.broadcasted_iota(jnp.int32, (LANES, LANES), 1)
    before_exp = _dot((ec < er).astype(BF16), jnp.broadcast_to(run_len(n_col), logits.shape).astype(BF16))
    where_to = before_exp + before_tok
    q1 = jnp.sum(jnp.where(oh1, where_to, 0.0), axis=0, keepdims=True)
    q2 = jnp.sum(jnp.where(oh2, where_to, 0.0), axis=0, keepdims=True)

    n_rows = run_len(_dot_nt(jnp.ones((SUBLANES, TM_MIX), BF16), oh))
    tabn_o[...] = n_rows
    tabb_o[...] = seen[...]
    seen[...] = seen[...] + n_rows

    ri_o[...] = jnp.zeros_like(ri_o)
    ri_o[0:1, :] = q1.astype(jnp.int32)
    ri_o[1:2, :] = q2.astype(jnp.int32)
    rw_o[...] = jnp.zeros_like(rw_o)
    rw_o[0:1, :] = w1
    rw_o[1:2, :] = w2


def _mix(o_n, og, u, gates, x2d, cw, cb, lg, lb, wpw, wo, g2, wrh, wrl, br):
    tm = MIX_TILES * TM_MIX
    n_front = FRONT // tm
    grid = (SEQ // tm,)
    rowp = lambda w: pl.BlockSpec((tm, w), lambda i: (i + n_front, 0))
    full = lambda a: pl.BlockSpec(a.shape, lambda i: (0,) * a.ndim)
    hist_blocks = tm // HIST
    return pl.pallas_call(
        _mix_kernel,
        grid=grid,
        in_specs=[
            rowp(V_W), rowp(D_MODEL), rowp(D_MODEL),
            pl.BlockSpec((HIST, D_MODEL), lambda i: ((i + n_front) * hist_blocks - 1, 0)),
            rowp(2 * D_MODEL),
            pl.BlockSpec((tm, D_MODEL), lambda i: (i, 0)),
            full(cw), full(cb), full(lg), full(lb), full(wpw), full(wo), full(g2),
            full(wrh), full(wrl), full(br),
        ],
        out_specs=[
            pl.BlockSpec((tm, D_MODEL), lambda i: (i, 0)),
            pl.BlockSpec((tm, D_MODEL), lambda i: (i, 0)),
            pl.BlockSpec((SUBLANES, tm), lambda i: (0, i)),
            pl.BlockSpec((SUBLANES, tm), lambda i: (0, i)),
            pl.BlockSpec((MIX_TILES * SUBLANES, LANES), lambda i: (i, 0)),
            pl.BlockSpec((MIX_TILES * SUBLANES, LANES), lambda i: (i, 0)),
        ],
        out_shape=[
            jax.ShapeDtypeStruct((SEQ, D_MODEL), F32),
            jax.ShapeDtypeStruct((SEQ, D_MODEL), BF16),
            jax.ShapeDtypeStruct((SUBLANES, SEQ), jnp.int32),
            jax.ShapeDtypeStruct((SUBLANES, SEQ), F32),
            jax.ShapeDtypeStruct((N_TILES * SUBLANES, LANES), F32),
            jax.ShapeDtypeStruct((N_TILES * SUBLANES, LANES), F32),
        ],
        scratch_shapes=[pltpu.VMEM((HIST + TM_MIX, D_MODEL), F32),
                        pltpu.VMEM((SUBLANES - 1, SH_ROWS, D_MODEL), F32),
                        pltpu.VMEM((TM_MIX, D_MODEL), F32), pltpu.VMEM((SUBLANES, LANES), F32)],
        compiler_params=pltpu.CompilerParams(dimension_semantics=("arbitrary",),
                                             vmem_limit_bytes=VMEM_LIMIT),
        name="mix",
    )(o_n, og, u, u, gates, x2d, cw, cb, lg, lb, wpw, wo, g2, wrh, wrl, br)


def _tok_rows(start_tok, n_tok):
    return pl.ds(pl.multiple_of(start_tok * TOK_ROWS, SUBLANES), n_tok * TOK_ROWS)


def _start_tile_runs(n_ref, row_ref, tile, make_copy):
    def run(e, off):
        n = n_ref[tile * N_EXPERTS + e]

        @pl.when(n > 0)
        def _():
            make_copy(_tok_rows(off, n), _tok_rows(row_ref[tile * N_EXPERTS + e], n)).start()

        return off + n

    lax.fori_loop(0, N_EXPERTS, run, 0, unroll=8)


def _dispatch_kernel(n_ref, row_ref, tot_ref, ts_ref, tl_ref, nused_ref, hn_ref, q_ref, xb_ref,
                     sb0, sb1, zbuf, sem, fsem):
    step = pl.program_id(0)
    last = pl.num_programs(0) - 1

    def fills(start):
        def tail(e, c):
            n = tl_ref[e]

            @pl.when(n > 0)
            def _():
                cp = pltpu.make_async_copy(zbuf.at[pl.ds(0, n * TOK_ROWS)],
                                           xb_ref.at[_tok_rows(ts_ref[e], n)], fsem)
                cp.start() if start else cp.wait()

            return c

        lax.fori_loop(0, N_EXPERTS, tail, 0)

        def block(b, c):
            cp = pltpu.make_async_copy(zbuf, xb_ref.at[_tok_rows(b * BM, BM)], fsem)
            cp.start() if start else cp.wait()
            return c

        lax.fori_loop(nused_ref[0], N_BLOCKS, block, 0)

    @pl.when(step == 0)
    def _():
        zbuf[...] = jnp.zeros_like(zbuf)
        fills(start=True)

    def tile_runs(tile, s, sbuf):
        rows = pl.ds(0, tot_ref[tile] * TOK_ROWS)
        return pltpu.make_async_copy(sbuf.at[rows], xb_ref.at[rows], sem.at[s])

    for s, sbuf in enumerate((sb0, sb1)):
        tile = step * TILES_PER_STEP + s
        toks = slice(s * TM_MIX, (s + 1) * TM_MIX)
        slot_i = lax.broadcasted_iota(jnp.int32, (SLOTS, TM_MIX), 0)
        onehot = jnp.where((slot_i == q_ref[0:1, toks]) | (slot_i == q_ref[1:2, toks]), 1.0, 0.0).astype(BF16)
        srt = _dot(onehot, hn_ref[toks, :])

        @pl.when(step >= 1)
        def _():
            tile_runs(tile - TILES_PER_STEP, s, sbuf).wait()

        _store_token_tiles(sbuf, srt, SLOTS)
        _start_tile_runs(n_ref, row_ref, tile,
                         lambda loc, glob: pltpu.make_async_copy(sbuf.at[loc], xb_ref.at[glob], sem.at[s]))

    @pl.when(step == last)
    def _():
        for s, sbuf in enumerate((sb0, sb1)):
            tile_runs(step * TILES_PER_STEP + s, s, sbuf).wait()
        fills(start=False)


def _dispatch(n_flat, run_rows, tot, tail_start, tail_len, nused, hn2, ri):
    grid_spec = pltpu.PrefetchScalarGridSpec(
        num_scalar_prefetch=6,
        grid=(N_TILES // TILES_PER_STEP,),
        in_specs=[pl.BlockSpec((TILES_PER_STEP * TM_MIX, D_MODEL), lambda i, *_: (i, 0)),
                  pl.BlockSpec((SUBLANES, TILES_PER_STEP * TM_MIX), lambda i, *_: (0, i))],
        out_specs=pl.BlockSpec(memory_space=pl.ANY),
        scratch_shapes=[pltpu.VMEM((SLOTS * TOK_ROWS, LANES), jnp.uint32),
                        pltpu.VMEM((SLOTS * TOK_ROWS, LANES), jnp.uint32),
                        pltpu.VMEM((BM * TOK_ROWS, LANES), jnp.uint32),
                        pltpu.SemaphoreType.DMA((2,)), pltpu.SemaphoreType.DMA(())],
    )
    return pl.pallas_call(
        _dispatch_kernel,
        grid_spec=grid_spec,
        out_shape=jax.ShapeDtypeStruct((P_ROWS * TOK_ROWS, LANES), jnp.uint32),
        compiler_params=pltpu.CompilerParams(dimension_semantics=("arbitrary",),
                                             vmem_limit_bytes=VMEM_LIMIT),
        name="dispatch",
    )(n_flat, run_rows, tot, tail_start, tail_len, nused, hn2, ri)


def _experts_kernel(be_ref, first_ref, next_ref, next2_ref, stage_ref, nused_ref,
                    x_ref, wg_hbm, wu_hbm, wd_hbm, y_ref, wg_f, wu_f, wd_f, wg_b, wu_b, wd_b, sem):
    def weight_copies(e, st):
        return (pltpu.make_async_copy(wg_hbm.at[e], wg_f.at[st], sem.at[st, 0]),
                pltpu.make_async_copy(wu_hbm.at[e], wu_f.at[st], sem.at[st, 1]),
                pltpu.make_async_copy(wd_hbm.at[e], wd_f.at[st], sem.at[st, 2]))

    @pl.when((pl.program_id(0) == 0) & (nused_ref[0] > 0))
    def _():
        for c in weight_copies(be_ref[0], 0):
            c.start()

        @pl.when(next_ref[0] >= 0)
        def _():
            for c in weight_copies(next_ref[0], 1):
                c.start()

    for j in range(BLOCKS_PER_STEP):
        _expert_block(pl.program_id(0) * BLOCKS_PER_STEP + j, j * BM, be_ref, first_ref, next2_ref,
                      stage_ref, nused_ref, x_ref, y_ref, wg_f, wu_f, wd_f, wg_b, wu_b, wd_b, weight_copies)


def _expert_block(b, tok0, be_ref, first_ref, next2_ref, stage_ref, nused_ref, x_ref, y_ref,
                  wg_f, wu_f, wd_f, wg_b, wu_b, wd_b, weight_copies):
    @pl.when(b < nused_ref[0])
    def _():
        def new_expert(st):
            for c in weight_copies(be_ref[b], st):
                c.wait()
            wg_b[...] = wg_f[st].astype(BF16)
            wu_b[...] = wu_f[st].astype(BF16)
            wd_b[...] = wd_f[st].astype(BF16)

            @pl.when(next2_ref[b] >= 0)
            def _():
                for c in weight_copies(next2_ref[b], st):
                    c.start()

        for st in range(2):
            pl.when((first_ref[b] == 1) & (stage_ref[b] == st))(functools.partial(new_expert, st))

        x = _load_token_tiles(x_ref, BM, tok0)
        a = _dot(x, wg_b[...])
        u = _dot(x, wu_b[...])
        y = _dot((_silu(a) * u).astype(BF16), wd_b[...])
        _store_token_tiles(y_ref, y.astype(BF16).astype(F32), BM, tok0)

    @pl.when(b >= nused_ref[0])
    def _():
        y_ref[tok0 * TOK_ROWS:(tok0 + BM) * TOK_ROWS, :] = jnp.zeros((BM * TOK_ROWS, LANES), jnp.uint32)


def _experts(block_e, first, next_e, next2_e, stage, nused, xb, wg, wu, wd):
    def xmap(b, *_):
        return (b, 0)

    grid_spec = pltpu.PrefetchScalarGridSpec(
        num_scalar_prefetch=6,
        grid=(N_BLOCKS // BLOCKS_PER_STEP,),
        in_specs=[
            pl.BlockSpec((BLOCKS_PER_STEP * BM * TOK_ROWS, LANES), xmap),
            pl.BlockSpec(memory_space=pl.ANY),
            pl.BlockSpec(memory_space=pl.ANY),
            pl.BlockSpec(memory_space=pl.ANY),
        ],
        out_specs=pl.BlockSpec((BLOCKS_PER_STEP * BM * TOK_ROWS, LANES), xmap),
        scratch_shapes=[pltpu.VMEM((2, D_MODEL, D_EXPERT), F32), pltpu.VMEM((2, D_MODEL, D_EXPERT), F32),
                        pltpu.VMEM((2, D_EXPERT, D_MODEL), F32),
                        pltpu.VMEM((D_MODEL, D_EXPERT), BF16), pltpu.VMEM((D_MODEL, D_EXPERT), BF16),
                        pltpu.VMEM((D_EXPERT, D_MODEL), BF16), pltpu.SemaphoreType.DMA((2, 3))],
    )
    return pl.pallas_call(
        _experts_kernel,
        grid_spec=grid_spec,
        out_shape=jax.ShapeDtypeStruct((P_ROWS * TOK_ROWS, LANES), jnp.uint32),
        compiler_params=pltpu.CompilerParams(dimension_semantics=("arbitrary",),
                                             vmem_limit_bytes=VMEM_LIMIT),
        name="experts",
    )(block_e, first, next_e, next2_e, stage, nused, xb, wg, wu, wd)


def _combine_kernel(n_ref, row_ref, tot_ref, yb_ref, h_ref, q_ref, w_ref, g_ref, out_ref, yb0, yb1, sem):
    step = pl.program_id(0)
    bufs = (yb0, yb1)

    def start_gather(tile, s):
        _start_tile_runs(n_ref, row_ref, tile,
                         lambda loc, glob: pltpu.make_async_copy(yb_ref.at[glob], bufs[s].at[loc], sem.at[s]))

    @pl.when(step == 0)
    def _():
        for buf in bufs:
            buf[...] = jnp.zeros_like(buf)
        start_gather(0, 0)

    for s in range(TILES_PER_STEP):
        tile = step * TILES_PER_STEP + s
        if s + 1 < TILES_PER_STEP:
            start_gather(tile + 1, s + 1)
        else:
            pl.when(step + 1 < pl.num_programs(0))(functools.partial(start_gather, tile + 1, 0))

        rows = pl.ds(0, tot_ref[tile] * TOK_ROWS)
        pltpu.make_async_copy(yb_ref.at[rows], bufs[s].at[rows], sem.at[s]).wait()
        ys = _load_token_tiles(bufs[s], SLOTS)
        toks = slice(s * TM_MIX, (s + 1) * TM_MIX)
        col = lax.broadcasted_iota(jnp.int32, (TM_MIX, SLOTS), 1)
        wmat = (jnp.where(col == q_ref[toks, 0:1], w_ref[toks, 0:1], 0.0)
                + jnp.where(col == q_ref[toks, 1:2], w_ref[toks, 1:2], 0.0)).astype(BF16)
        hh = h_ref[toks, :] + _dot(wmat, ys)
        out_ref[toks, :] = hh * lax.rsqrt(jnp.mean(hh * hh, axis=-1, keepdims=True) + EPS) * g_ref[...]


def _combine(n_flat, run_rows, tot, yb, h1, qt, wts, gf):
    grid_spec = pltpu.PrefetchScalarGridSpec(
        num_scalar_prefetch=3,
        grid=(N_TILES // TILES_PER_STEP,),
        in_specs=[
            pl.BlockSpec(memory_space=pl.ANY),
            pl.BlockSpec((TILES_PER_STEP * TM_MIX, D_MODEL), lambda i, *_: (i, 0)),
            pl.BlockSpec((TILES_PER_STEP * TM_MIX, TOP_K), lambda i, *_: (i, 0)),
            pl.BlockSpec((TILES_PER_STEP * TM_MIX, TOP_K), lambda i, *_: (i, 0)),
            pl.BlockSpec((1, D_MODEL), lambda i, *_: (0, 0)),
        ],
        out_specs=pl.BlockSpec((TILES_PER_STEP * TM_MIX, D_MODEL), lambda i, *_: (i, 0)),
        scratch_shapes=[pltpu.VMEM((SLOTS * TOK_ROWS, LANES), jnp.uint32),
                        pltpu.VMEM((SLOTS * TOK_ROWS, LANES), jnp.uint32),
                        pltpu.SemaphoreType.DMA((2,))],
    )
    return pl.pallas_call(
        _combine_kernel,
        grid_spec=grid_spec,
        out_shape=jax.ShapeDtypeStruct((SEQ, D_MODEL), F32),
        compiler_params=pltpu.CompilerParams(dimension_semantics=("arbitrary",),
                                             vmem_limit_bytes=VMEM_LIMIT),
        name="combine",
    )(n_flat, run_rows, tot, yb, h1, qt, wts, gf)


def kernel(x, meta, norm1_g, w_in, w_decay_up, b_decay, gla_norm_g, conv_w, conv_b, conv_ln_g,
           conv_ln_b, w_pw2, b_gate, w_out, norm2_g, w_router_group, b_router_group,
           w_router_expert, b_router_expert, w_exp_gate, w_exp_up, w_exp_down, final_norm_g):
    assert x.shape == (1, SEQ, D_MODEL) and w_in.shape[0] == 1
    x2d = x[0]
    front = jnp.concatenate([jnp.zeros((FRONT - N_META, D_MODEL), F32), meta.astype(F32)], axis=0)

    w = w_in[0]
    row2 = lambda a: a.reshape(1, -1).astype(F32)
    wq, wk, wv, wog, wdl, wglu, wgt = _wcast(w)
    up = w_decay_up[0].astype(F32)
    up_hi = up.astype(BF16)
    up_mid = (up - up_hi.astype(F32)).astype(BF16)
    up_lo = (up - up_hi.astype(F32) - up_mid.astype(F32)).astype(BF16)
    wdu = jnp.concatenate([up_hi, up_mid, up_lo, up_hi, up_mid, up_hi], axis=0)
    q, k, v, og, la, u, gates = _inproj(
        front, x2d, row2(norm1_g[0]), wq, wk, wv, wog, jnp.tile(wdl, (1, DL_COPIES)), wglu, wgt,
        wdu, row2(b_decay[0]), row2(b_gate[0]))

    o_n = _gla(q, k, v, la, row2(gla_norm_g[0]))

    wr = jnp.concatenate([w_router_group[0].T, w_router_expert[0].T,
                          jnp.zeros((LANES - N_GROUPS - N_EXPERTS, D_MODEL), F32)], axis=0).astype(F32)
    br = jnp.concatenate([b_router_group[0], b_router_expert[0],
                          jnp.zeros((LANES - N_GROUPS - N_EXPERTS,), F32)]).reshape(LANES, 1).astype(F32)
    wr_hi = wr.astype(BF16)
    wr_lo = (wr - wr_hi.astype(F32)).astype(BF16)
    h1, hn2, ri, rw, tabn, tabb = _mix(
        o_n, og, u, gates, x2d, conv_w[0].astype(F32), row2(conv_b[0]), row2(conv_ln_g[0]),
        row2(conv_ln_b[0]), w_pw2[0].astype(BF16), w_out[0].astype(BF16),
        row2(norm2_g[0]), wr_hi, wr_lo, br)

    experts = slice(N_GROUPS, N_GROUPS + N_EXPERTS)
    n_te = tabn[::SUBLANES, experts].astype(jnp.int32)
    seen_te = tabb[::SUBLANES, experts].astype(jnp.int32)
    counts = seen_te[-1] + n_te[-1]
    padded = (counts + BM - 1) // BM * BM
    pad_end = jnp.cumsum(padded)
    pad_start = pad_end - padded
    n_flat = n_te.reshape(-1)
    tot = jnp.sum(n_te, axis=1).astype(jnp.int32)
    run_rows = (pad_start[None, :] + seen_te).reshape(-1).astype(jnp.int32)
    tail_start = (pad_start + counts).astype(jnp.int32)
    tail_len = (padded - counts).astype(jnp.int32)
    blk = jnp.arange(N_BLOCKS, dtype=jnp.int32)
    block_e = jnp.minimum(jnp.sum((pad_end[None, :] <= blk[:, None] * BM).astype(jnp.int32), axis=1),
                          N_EXPERTS - 1).astype(jnp.int32)
    first = jnp.concatenate([jnp.ones((1,), jnp.int32), (block_e[1:] != block_e[:-1]).astype(jnp.int32)])
    nused = (pad_end[-1:] // BM).astype(jnp.int32)
    eid = jnp.arange(N_EXPERTS, dtype=jnp.int32)
    later = jnp.flip(lax.cummin(jnp.flip(jnp.where(counts > 0, eid, N_EXPERTS))))
    nxt = jnp.concatenate([later[1:], jnp.full((2,), N_EXPERTS, jnp.int32)])
    nxt2 = nxt[nxt[:N_EXPERTS]]
    or_none = lambda t: jnp.where(t < N_EXPERTS, t, -1)[block_e].astype(jnp.int32)
    next_e, next2_e = or_none(nxt[:N_EXPERTS]), or_none(nxt2)
    stage = ((jnp.cumsum(first) - 1) % 2).astype(jnp.int32)

    xb = _dispatch(n_flat, run_rows, tot, tail_start, tail_len, nused, hn2, ri)
    yb = _experts(block_e, first, next_e, next2_e, stage, nused, xb, w_exp_gate[0], w_exp_up[0], w_exp_down[0])
    out = _combine(n_flat, run_rows, tot, yb, h1, ri[0:TOP_K].T, rw[0:TOP_K].T, row2(final_norm_g))
    return out[None]
```

```python
import functools

import jax
import jax.numpy as jnp
import numpy as np
from jax import lax
from jax.experimental import pallas as pl
from jax.experimental.pallas import tpu as pltpu

F32 = jnp.float32
BF16 = jnp.bfloat16

D_MODEL = 1024
SEQ = 16384
N_META = 16
GLA_HEADS = 4
GLA_DK = 128
GLA_DV = 256
GLA_RANK = 16
GLA_TAU = 16.0
CHUNK = 64
QK_W = GLA_HEADS * GLA_DK
V_W = GLA_HEADS * GLA_DV
CONV_WIDTH = 31
N_GROUPS = 8
EXPERTS_PER_GROUP = 8
N_EXPERTS = N_GROUPS * EXPERTS_PER_GROUP
TOP_K = 2
D_EXPERT = 512
EPS = 1e-6
LOG2E = 1.4426950408889634

LANES = 128
SUBLANES = 8
ROW_TILE = D_MODEL // LANES
TOK_ROWS = D_MODEL // 2 // LANES
TOK_ALIGN = SUBLANES // TOK_ROWS

FRONT = 512
LP = FRONT + SEQ
TM_PROJ = 512
DL_COPIES = 6
TM_GLA = 512
TM_MIX = 256
MIX_TILES = 2
HIST = 32
CONV_RB = 16
N_TILES = SEQ // TM_MIX
SLOTS = TOP_K * TM_MIX + N_EXPERTS * (TOK_ALIGN - 1)
TILES_PER_STEP = 2
BM = 256
BLOCKS_PER_STEP = 4
N_BLOCKS = (N_TILES * SLOTS) // BM + N_EXPERTS
P_ROWS = N_BLOCKS * BM
VMEM_LIMIT = 56 * 1024 * 1024


def _sigmoid(x):
    return 0.5 * jnp.tanh(0.5 * x) + 0.5


def _silu(x):
    return x * _sigmoid(x)


def _dot(a, b, **kw):
    return jnp.dot(a, b, preferred_element_type=F32, **kw)


def _load_row_tiles(ref, n_rows):
    return jnp.concatenate([ref[pl.ds(s, n_rows, stride=ROW_TILE), :] for s in range(ROW_TILE)], axis=-1)


def _store_row_tiles(ref, val, n_rows):
    for s in range(ROW_TILE):
        ref[pl.ds(s, n_rows, stride=ROW_TILE), :] = val[:, s * LANES:(s + 1) * LANES]


def _load_token_tiles(ref, n_tok, first_tok=0):
    r0 = first_tok * TOK_ROWS
    w = jnp.concatenate([ref[pl.ds(r0 + s, n_tok, stride=TOK_ROWS), :] for s in range(TOK_ROWS)], axis=-1)
    lo = pltpu.bitcast(w << 16, F32)
    hi = pltpu.bitcast(w & jnp.uint32(0xFFFF0000), F32)
    return jnp.concatenate([lo, hi], axis=-1).astype(BF16)


def _store_token_tiles(ref, val, n_tok, first_tok=0):
    bits = pltpu.bitcast(val, jnp.uint32)
    w = (bits[:, 0:D_MODEL // 2] >> 16) | (bits[:, D_MODEL // 2:D_MODEL] & jnp.uint32(0xFFFF0000))
    r0 = first_tok * TOK_ROWS
    for s in range(TOK_ROWS):
        ref[pl.ds(r0 + s, n_tok, stride=TOK_ROWS), :] = w[:, s * LANES:(s + 1) * LANES]


IN_SPLITS = (QK_W, QK_W, V_W, D_MODEL, GLA_RANK, 2 * D_MODEL, 2 * D_MODEL)
WCAST_ROWS = 128


def _wcast_kernel(w_ref, *outs):
    w = w_ref[...]
    off = 0
    for out, n in zip(outs, IN_SPLITS):
        out[...] = w[:, off:off + n].astype(BF16)
        off += n


def _wcast(w):
    return pl.pallas_call(
        _wcast_kernel,
        grid=(D_MODEL // WCAST_ROWS,),
        in_specs=[pl.BlockSpec((WCAST_ROWS, sum(IN_SPLITS)), lambda i: (i, 0))],
        out_specs=[pl.BlockSpec((WCAST_ROWS, n), lambda i: (i, 0)) for n in IN_SPLITS],
        out_shape=[jax.ShapeDtypeStruct((D_MODEL, n), BF16) for n in IN_SPLITS],
        compiler_params=pltpu.CompilerParams(dimension_semantics=("arbitrary",),
                                             vmem_limit_bytes=VMEM_LIMIT),
        name="wcast",
    )(w)


def _inproj_kernel(front_ref, x_ref, g_ref, wq, wk, wv, wog, wdl, wglu, wgt, wdu, bd, bg,
                   q_o, k_o, v_o, og_o, la_o, u_o, gt_o):
    i = pl.program_id(0)
    h = jnp.where(i < FRONT // TM_PROJ, front_ref[...], x_ref[...])
    ms = jnp.mean(h * h, axis=-1, keepdims=True)
    hn = (h * lax.rsqrt(ms + EPS) * g_ref[...]).astype(BF16)
    q_o[...] = _dot(hn, wq[...]).astype(BF16)
    k_o[...] = _dot(hn, wk[...]).astype(BF16)
    v_o[...] = _dot(hn, wv[...]).astype(BF16)
    og_o[...] = _silu(_dot(hn, wog[...])).astype(BF16)
    glu = _dot(hn, wglu[...])
    _store_row_tiles(u_o, glu[:, 0:D_MODEL] * _sigmoid(glu[:, D_MODEL:2 * D_MODEL]), TM_PROJ)
    gt_o[...] = _sigmoid(_dot(hn, wgt[...]) + bg[...]).astype(BF16)
    d6 = _dot(hn, wdl[...])
    hi = d6.astype(BF16).astype(F32)
    mid = (d6 - hi).astype(BF16).astype(F32)
    lane = lax.broadcasted_iota(jnp.int32, d6.shape, 1)
    pieces = jnp.where(lane < 3 * GLA_RANK, hi, jnp.where(lane < 5 * GLA_RANK, mid, d6 - hi - mid))
    z = _dot(pieces.astype(BF16), wdu[...]) + bd[...]
    la_o[...] = (jnp.minimum(z, 0.0) - jnp.log(1.0 + jnp.exp(-jnp.abs(z)))) * (LOG2E / GLA_TAU)


def _inproj(front, x2d, g1, wq, wk, wv, wog, wdl, wglu, wgt, wdu, bd, bg):
    n_front = FRONT // TM_PROJ
    grid = (LP // TM_PROJ,)
    row = lambda w: pl.BlockSpec((TM_PROJ, w), lambda i: (i, 0))
    full = lambda a: pl.BlockSpec(a.shape, lambda i: (0, 0))
    return pl.pallas_call(
        _inproj_kernel,
        grid=grid,
        in_specs=[
            pl.BlockSpec((TM_PROJ, D_MODEL), lambda i: (jnp.minimum(i, n_front - 1), 0)),
            pl.BlockSpec((TM_PROJ, D_MODEL), lambda i: (jnp.maximum(i - n_front, 0), 0)),
            full(g1), full(wq), full(wk), full(wv), full(wog), full(wdl), full(wglu), full(wgt),
            full(wdu), full(bd), full(bg),
        ],
        out_specs=[row(QK_W), row(QK_W), row(V_W), row(D_MODEL), row(QK_W),
                   pl.BlockSpec((TM_PROJ * ROW_TILE, LANES), lambda i: (i, 0)), row(2 * D_MODEL)],
        out_shape=[
            jax.ShapeDtypeStruct((LP, QK_W), BF16), jax.ShapeDtypeStruct((LP, QK_W), BF16),
            jax.ShapeDtypeStruct((LP, V_W), BF16), jax.ShapeDtypeStruct((LP, D_MODEL), BF16),
            jax.ShapeDtypeStruct((LP, QK_W), F32), jax.ShapeDtypeStruct((LP * ROW_TILE, LANES), F32),
            jax.ShapeDtypeStruct((LP, 2 * D_MODEL), BF16),
        ],
        compiler_params=pltpu.CompilerParams(dimension_semantics=("arbitrary",),
                                             vmem_limit_bytes=VMEM_LIMIT),
        name="inproj",
    )(front, x2d, g1, wq, wk, wv, wog, wdl, wglu, wgt, wdu, bd, bg)


GLA_LEVELS = (32, 16, 8, 4, 2, 1)
GLA_FINE = tuple(m for m in GLA_LEVELS if m < SUBLANES)
LA_SPLIT = 3


def _gla_tables():
    r = np.arange(CHUNK)
    t, c = r[:, None], r[None, :]
    rows = [c <= t]
    pair = []
    for m in GLA_LEVELS:
        mid = (t // (2 * m)) * (2 * m) + m - 1
        right = (t % (2 * m)) >= m
        if m in GLA_FINE:
            rows.append(np.where(right, (c > mid) & (c <= t), (c > t) & (c <= mid)))
        pair.append(((t // (2 * m)) == (c // (2 * m))) & right & ((c % (2 * m)) < m))
    pair.append(t == c)
    sums = np.concatenate(rows, axis=0).astype(np.float32)
    sums = np.concatenate([sums] * LA_SPLIT, axis=1)
    pair = np.stack([np.kron(np.eye(GLA_HEADS), p) for p in pair]).astype(np.float32)
    return jnp.asarray(sums, BF16), jnp.asarray(pair)


def _split_bf16(x):
    pieces = []
    rest = x
    for _ in range(LA_SPLIT):
        p = rest.astype(BF16)
        pieces.append(p)
        rest = rest - p.astype(F32)
    return jnp.concatenate(pieces, axis=0)


def _dot_nt(a, b):
    return lax.dot_general(a, b, (((1,), (1,)), ((), ())), preferred_element_type=F32)


def _dot_tn(a, b):
    return lax.dot_general(a, b, (((0,), (0,)), ((), ())), preferred_element_type=F32)


def _gla_exponents(a):
    b = a[0:CHUNK]
    parts = [b, b[CHUNK - 1:CHUNK] - b]
    for m in GLA_LEVELS:
        if m in GLA_FINE:
            continue
        for lo in range(0, CHUNK, 2 * m):
            ref = b[lo + m - 1:lo + m]
            parts += [ref - b[lo:lo + m], b[lo + m:lo + 2 * m] - ref]
    return jnp.concatenate(parts + [a[CHUNK:]], axis=0)


def _gla_kernel(q_ref, k_ref, v_ref, la_ref, g_ref, sums_ref, pair_ref, o_ref, s_ref):
    @pl.when(pl.program_id(0) == 0)
    def _():
        s_ref[...] = jnp.zeros_like(s_ref)

    row = lax.broadcasted_iota(jnp.int32, (CHUNK, GLA_DK), 0)
    right_rows = [(row & m) != 0 for m in GLA_LEVELS]
    ones_col = jnp.ones((LA_SPLIT * CHUNK, LANES), BF16)
    sums = sums_ref[...]
    n_lvl = len(GLA_LEVELS)

    def body(c, carry):
        r0 = pl.multiple_of(c * CHUNK, CHUNK)
        rows = pl.ds(r0, CHUNK)
        states = [s_ref[h] for h in range(GLA_HEADS)]
        qs, ks, es, decs = [], [], [], []
        for hp in range(GLA_HEADS // 2):
            la3 = _split_bf16(la_ref[rows, 2 * hp * GLA_DK:2 * (hp + 1) * GLA_DK])
            e2 = jnp.exp2(_gla_exponents(_dot(sums, la3)))
            dec2 = jnp.exp2(_dot_tn(la3, ones_col))
            for h in (2 * hp, 2 * hp + 1):
                half = slice((h % 2) * GLA_DK, (h % 2 + 1) * GLA_DK)
                kc = slice(h * GLA_DK, (h + 1) * GLA_DK)
                qs.append(q_ref[rows, kc].astype(F32) * (GLA_DK ** -0.5))
                ks.append(k_ref[rows, kc].astype(F32))
                es.append(e2[:, half])
                decs.append(dec2[half])
        vbs = [v_ref[rows, h * GLA_DV:(h + 1) * GLA_DV] for h in range(GLA_HEADS)]

        new_states, qes = [], []
        for h in range(GLA_HEADS):
            qes.append((qs[h] * es[h][0:CHUNK]).astype(BF16))
            kd = (ks[h] * es[h][CHUNK:2 * CHUNK]).astype(BF16)
            new_states.append(states[h] * jnp.concatenate([decs[h]] * (GLA_DV // LANES), axis=1)
                              + _dot_tn(kd, vbs[h]))

        stack = lambda parts: jnp.concatenate(parts, axis=0)
        sc = _dot_nt(stack([q.astype(BF16) for q in qs]), stack([k.astype(BF16) for k in ks])) * pair_ref[n_lvl]
        for lvl in range(n_lvl):
            x = stack([(jnp.where(right_rows[lvl], qs[h], ks[h])
                        * es[h][(2 + lvl) * CHUNK:(3 + lvl) * CHUNK]).astype(BF16) for h in range(GLA_HEADS)])
            sc = sc + _dot_nt(x, x) * pair_ref[lvl]
        o_all = _dot(sc.astype(BF16), stack(vbs))
        for h in range(GLA_HEADS):
            vc = slice(h * GLA_DV, (h + 1) * GLA_DV)
            o = o_all[h * CHUNK:(h + 1) * CHUNK] + _dot(qes[h], states[h].astype(BF16))
            o = o * lax.rsqrt(jnp.mean(o * o, axis=-1, keepdims=True) + EPS) * g_ref[:, vc]
            o_ref[rows, vc] = o.astype(BF16)
        for h in range(GLA_HEADS):
            s_ref[h] = new_states[h]
        return carry

    lax.fori_loop(0, TM_GLA // CHUNK, body, 0, unroll=4)


def _gla(q, k, v, la, g):
    sums, pair = _gla_tables()
    row = lambda w: pl.BlockSpec((TM_GLA, w), lambda t: (t, 0))
    full = lambda a: pl.BlockSpec(a.shape, lambda t: (0,) * a.ndim)
    return pl.pallas_call(
        _gla_kernel,
        grid=(LP // TM_GLA,),
        in_specs=[row(QK_W), row(QK_W), row(V_W), row(QK_W), full(g), full(sums), full(pair)],
        out_specs=row(V_W),
        out_shape=jax.ShapeDtypeStruct((LP, V_W), BF16),
        scratch_shapes=[pltpu.VMEM((GLA_HEADS, GLA_DK, GLA_DV), F32)],
        compiler_params=pltpu.CompilerParams(dimension_semantics=("arbitrary",),
                                             vmem_limit_bytes=VMEM_LIMIT),
        name="gla",
    )(q, k, v, la, g, sums, pair)


def _mix_kernel(o_ref, og_ref, u_ref, hist_ref, gt_ref, x_ref, cw_ref, cb_ref, lg_ref, lb_ref,
                wpw_ref, wo_ref, g2_ref, wrh_ref, wrl_ref, br_ref,
                h_o, hn_o, ri_o, rw_o, tabn_o, tabb_o, cbuf, seen):
    @pl.when(pl.program_id(0) == 0)
    def _():
        seen[...] = jnp.zeros_like(seen)

    for j in range(MIX_TILES):
        r = slice(j * TM_MIX, (j + 1) * TM_MIX)
        tab = slice(j * SUBLANES, (j + 1) * SUBLANES)

        def u_row(t, j=j):
            t += j * TM_MIX
            ref, t = (hist_ref, t + HIST) if t < 0 else (u_ref, t)
            return ref[t * ROW_TILE:(t + 1) * ROW_TILE, :]

        _mix_tile(o_ref.at[r], og_ref.at[r], u_row, gt_ref.at[r], x_ref.at[r], cw_ref, cb_ref,
                  lg_ref, lb_ref, wpw_ref, wo_ref, g2_ref, wrh_ref, wrl_ref, br_ref,
                  h_o.at[r], hn_o.at[r], ri_o.at[:, r], rw_o.at[:, r], tabn_o.at[tab], tabb_o.at[tab],
                  cbuf, seen)


def _mix_tile(o_ref, og_ref, u_row, gt_ref, x_ref, cw_ref, cb_ref, lg_ref, lb_ref,
              wpw_ref, wo_ref, g2_ref, wrh_ref, wrl_ref, br_ref,
              h_o, hn_o, ri_o, rw_o, tabn_o, tabb_o, cbuf, seen):
    taps = [cw_ref[jj] for jj in range(CONV_WIDTH)]
    for t0 in range(0, TM_MIX, CONV_RB):
        part = [cb_ref[...]] * CONV_RB
        for rr in range(t0 - (CONV_WIDTH - 1), t0 + CONV_RB):
            ur = u_row(rr)
            for t in range(max(t0, rr), min(t0 + CONV_RB, rr + CONV_WIDTH)):
                part[t - t0] = part[t - t0] + taps[rr - t + CONV_WIDTH - 1] * ur
        for g in range(CONV_RB):
            cbuf[(t0 + g) * ROW_TILE:(t0 + g + 1) * ROW_TILE, :] = part[g]
    acc = _load_row_tiles(cbuf, TM_MIX)

    mu = jnp.mean(acc, axis=-1, keepdims=True)
    xc = acc - mu
    ln = xc * lax.rsqrt(jnp.mean(xc * xc, axis=-1, keepdims=True) + EPS) * lg_ref[...] + lb_ref[...]
    branch_b = _dot(_silu(ln).astype(BF16), wpw_ref[...])

    branch_a = o_ref[...].astype(F32) * og_ref[...].astype(F32)
    g_a = gt_ref[:, 0:D_MODEL].astype(F32)
    g_b = gt_ref[:, D_MODEL:2 * D_MODEL].astype(F32)
    merged = (g_a * branch_a + g_b * branch_b).astype(BF16)
    h1 = x_ref[...] + _dot(merged, wo_ref[...])
    h_o[...] = h1

    hn2 = h1 * lax.rsqrt(jnp.mean(h1 * h1, axis=-1, keepdims=True) + EPS) * g2_ref[...]
    hn_hi = hn2.astype(BF16)
    hn_o[...] = hn_hi

    hn_lo = (hn2 - hn_hi.astype(F32)).astype(BF16)
    logits = (_dot_nt(wrh_ref[...], hn_hi) + _dot_nt(wrh_ref[...], hn_lo) + _dot_nt(wrl_ref[...], hn_hi)
              + br_ref[...])
    row = lax.broadcasted_iota(jnp.int32, logits.shape, 0)
    rowf = row.astype(F32)
    neg = -jnp.inf
    is_g = row < N_GROUPS
    lg = jnp.where(is_g, logits, neg)
    gmax = jnp.max(lg, axis=0, keepdims=True)
    gidx = jnp.min(jnp.where(lg == gmax, rowf, float(N_GROUPS)), axis=0, keepdims=True)
    g_w = 1.0 / jnp.sum(jnp.where(is_g, jnp.exp(lg - gmax), 0.0), axis=0, keepdims=True)
    erow = rowf - float(N_GROUPS)
    egrp = ((row - N_GROUPS) >> 3).astype(F32)
    in_grp = (row >= N_GROUPS) & (row < N_GROUPS + N_EXPERTS) & (egrp == gidx)
    le = jnp.where(in_grp, logits, neg)
    m1 = jnp.max(le, axis=0, keepdims=True)
    i1 = jnp.min(jnp.where(le == m1, erow, float(N_EXPERTS)), axis=0, keepdims=True)
    le2 = jnp.where(erow == i1, neg, le)
    m2 = jnp.max(le2, axis=0, keepdims=True)
    i2 = jnp.min(jnp.where(le2 == m2, erow, float(N_EXPERTS)), axis=0, keepdims=True)
    t = jnp.exp(m2 - m1)
    w1 = g_w / (1.0 + t)
    w2 = g_w * t / (1.0 + t)

    def run_len(n):
        return jnp.floor((n + (TOK_ALIGN - 1)) * (1.0 / TOK_ALIGN)) * TOK_ALIGN

    oh1 = erow == i1
    oh2 = erow == i2
    oh = jnp.where(oh1 | oh2, 1.0, 0.0).astype(BF16)
    tr = lax.broadcasted_iota(jnp.int32, (TM_MIX, TM_MIX), 0)
    tc = lax.broadcasted_iota(jnp.int32, (TM_MIX, TM_MIX), 1)
    before_tok = _dot(oh, (tr < tc).astype(BF16))
    n_col = jnp.sum(oh.astype(F32), axis=1, keepdims=True)
    er = lax.broadcasted_iota(jnp.int32, (LANES, LANES), 0)
    ec = lax.broadcasted_iota(jnp.int32, (LANES, LANES), 1)
    before_exp = _dot((ec < er).astype(BF16), jnp.broadcast_to(run_len(n_col), logits.shape).astype(BF16))
    where_to = before_exp + before_tok
    q1 = jnp.sum(jnp.where(oh1, where_to, 0.0), axis=0, keepdims=True)
    q2 = jnp.sum(jnp.where(oh2, where_to, 0.0), axis=0, keepdims=True)

    n_rows = run_len(_dot_nt(jnp.ones((SUBLANES, TM_MIX), BF16), oh))
    tabn_o[...] = n_rows
    tabb_o[...] = seen[...]
    seen[...] = seen[...] + n_rows

    ri_o[...] = jnp.zeros_like(ri_o)
    ri_o[0:1, :] = q1.astype(jnp.int32)
    ri_o[1:2, :] = q2.astype(jnp.int32)
    rw_o[...] = jnp.zeros_like(rw_o)
    rw_o[0:1, :] = w1
    rw_o[1:2, :] = w2


def _mix(o_n, og, u, gates, x2d, cw, cb, lg, lb, wpw, wo, g2, wrh, wrl, br):
    tm = MIX_TILES * TM_MIX
    n_front = FRONT // tm
    grid = (SEQ // tm,)
    rowp = lambda w: pl.BlockSpec((tm, w), lambda i: (i + n_front, 0))
    full = lambda a: pl.BlockSpec(a.shape, lambda i: (0,) * a.ndim)
    hist_blocks = tm // HIST
    return pl.pallas_call(
        _mix_kernel,
        grid=grid,
        in_specs=[
            rowp(V_W), rowp(D_MODEL),
            pl.BlockSpec((tm * ROW_TILE, LANES), lambda i: (i + n_front, 0)),
            pl.BlockSpec((HIST * ROW_TILE, LANES), lambda i: ((i + n_front) * hist_blocks - 1, 0)),
            rowp(2 * D_MODEL),
            pl.BlockSpec((tm, D_MODEL), lambda i: (i, 0)),
            full(cw), full(cb), full(lg), full(lb), full(wpw), full(wo), full(g2),
            full(wrh), full(wrl), full(br),
        ],
        out_specs=[
            pl.BlockSpec((tm, D_MODEL), lambda i: (i, 0)),
            pl.BlockSpec((tm, D_MODEL), lambda i: (i, 0)),
            pl.BlockSpec((SUBLANES, tm), lambda i: (0, i)),
            pl.BlockSpec((SUBLANES, tm), lambda i: (0, i)),
            pl.BlockSpec((MIX_TILES * SUBLANES, LANES), lambda i: (i, 0)),
            pl.BlockSpec((MIX_TILES * SUBLANES, LANES), lambda i: (i, 0)),
        ],
        out_shape=[
            jax.ShapeDtypeStruct((SEQ, D_MODEL), F32),
            jax.ShapeDtypeStruct((SEQ, D_MODEL), BF16),
            jax.ShapeDtypeStruct((SUBLANES, SEQ), jnp.int32),
            jax.ShapeDtypeStruct((SUBLANES, SEQ), F32),
            jax.ShapeDtypeStruct((N_TILES * SUBLANES, LANES), F32),
            jax.ShapeDtypeStruct((N_TILES * SUBLANES, LANES), F32),
        ],
        scratch_shapes=[pltpu.VMEM((TM_MIX * ROW_TILE, LANES), F32), pltpu.VMEM((SUBLANES, LANES), F32)],
        compiler_params=pltpu.CompilerParams(dimension_semantics=("arbitrary",),
                                             vmem_limit_bytes=VMEM_LIMIT),
        name="mix",
    )(o_n, og, u, u, gates, x2d, cw, cb, lg, lb, wpw, wo, g2, wrh, wrl, br)


def _tok_rows(start_tok, n_tok):
    return pl.ds(pl.multiple_of(start_tok * TOK_ROWS, SUBLANES), n_tok * TOK_ROWS)


def _start_tile_runs(n_ref, row_ref, tile, make_copy):
    def run(e, off):
        n = n_ref[tile * N_EXPERTS + e]

        @pl.when(n > 0)
        def _():
            make_copy(_tok_rows(off, n), _tok_rows(row_ref[tile * N_EXPERTS + e], n)).start()

        return off + n

    lax.fori_loop(0, N_EXPERTS, run, 0, unroll=8)


def _dispatch_kernel(n_ref, row_ref, tot_ref, ts_ref, tl_ref, nused_ref, hn_ref, q_ref, xb_ref,
                     sb0, sb1, zbuf, sem, fsem):
    step = pl.program_id(0)
    last = pl.num_programs(0) - 1

    def fills(start):
        def tail(e, c):
            n = tl_ref[e]

            @pl.when(n > 0)
            def _():
                cp = pltpu.make_async_copy(zbuf.at[pl.ds(0, n * TOK_ROWS)],
                                           xb_ref.at[_tok_rows(ts_ref[e], n)], fsem)
                cp.start() if start else cp.wait()

            return c

        lax.fori_loop(0, N_EXPERTS, tail, 0)

        def block(b, c):
            cp = pltpu.make_async_copy(zbuf, xb_ref.at[_tok_rows(b * BM, BM)], fsem)
            cp.start() if start else cp.wait()
            return c

        lax.fori_loop(nused_ref[0], N_BLOCKS, block, 0)

    @pl.when(step == 0)
    def _():
        zbuf[...] = jnp.zeros_like(zbuf)
        fills(start=True)

    def tile_runs(tile, s, sbuf):
        rows = pl.ds(0, tot_ref[tile] * TOK_ROWS)
        return pltpu.make_async_copy(sbuf.at[rows], xb_ref.at[rows], sem.at[s])

    for s, sbuf in enumerate((sb0, sb1)):
        tile = step * TILES_PER_STEP + s
        toks = slice(s * TM_MIX, (s + 1) * TM_MIX)
        slot_i = lax.broadcasted_iota(jnp.int32, (SLOTS, TM_MIX), 0)
        onehot = jnp.where((slot_i == q_ref[0:1, toks]) | (slot_i == q_ref[1:2, toks]), 1.0, 0.0).astype(BF16)
        srt = _dot(onehot, hn_ref[toks, :])

        @pl.when(step >= 1)
        def _():
            tile_runs(tile - TILES_PER_STEP, s, sbuf).wait()

        _store_token_tiles(sbuf, srt, SLOTS)
        _start_tile_runs(n_ref, row_ref, tile,
                         lambda loc, glob: pltpu.make_async_copy(sbuf.at[loc], xb_ref.at[glob], sem.at[s]))

    @pl.when(step == last)
    def _():
        for s, sbuf in enumerate((sb0, sb1)):
            tile_runs(step * TILES_PER_STEP + s, s, sbuf).wait()
        fills(start=False)


def _dispatch(n_flat, run_rows, tot, tail_start, tail_len, nused, hn2, ri):
    grid_spec = pltpu.PrefetchScalarGridSpec(
        num_scalar_prefetch=6,
        grid=(N_TILES // TILES_PER_STEP,),
        in_specs=[pl.BlockSpec((TILES_PER_STEP * TM_MIX, D_MODEL), lambda i, *_: (i, 0)),
                  pl.BlockSpec((SUBLANES, TILES_PER_STEP * TM_MIX), lambda i, *_: (0, i))],
        out_specs=pl.BlockSpec(memory_space=pl.ANY),
        scratch_shapes=[pltpu.VMEM((SLOTS * TOK_ROWS, LANES), jnp.uint32),
                        pltpu.VMEM((SLOTS * TOK_ROWS, LANES), jnp.uint32),
                        pltpu.VMEM((BM * TOK_ROWS, LANES), jnp.uint32),
                        pltpu.SemaphoreType.DMA((2,)), pltpu.SemaphoreType.DMA(())],
    )
    return pl.pallas_call(
        _dispatch_kernel,
        grid_spec=grid_spec,
        out_shape=jax.ShapeDtypeStruct((P_ROWS * TOK_ROWS, LANES), jnp.uint32),
        compiler_params=pltpu.CompilerParams(dimension_semantics=("arbitrary",),
                                             vmem_limit_bytes=VMEM_LIMIT),
        name="dispatch",
    )(n_flat, run_rows, tot, tail_start, tail_len, nused, hn2, ri)


def _experts_kernel(be_ref, first_ref, next_ref, next2_ref, stage_ref, nused_ref,
                    x_ref, wg_hbm, wu_hbm, wd_hbm, y_ref, wg_f, wu_f, wd_f, wg_b, wu_b, wd_b, sem):
    def weight_copies(e, st):
        return (pltpu.make_async_copy(wg_hbm.at[e], wg_f.at[st], sem.at[st, 0]),
                pltpu.make_async_copy(wu_hbm.at[e], wu_f.at[st], sem.at[st, 1]),
                pltpu.make_async_copy(wd_hbm.at[e], wd_f.at[st], sem.at[st, 2]))

    @pl.when((pl.program_id(0) == 0) & (nused_ref[0] > 0))
    def _():
        for c in weight_copies(be_ref[0], 0):
            c.start()

        @pl.when(next_ref[0] >= 0)
        def _():
            for c in weight_copies(next_ref[0], 1):
                c.start()

    for j in range(BLOCKS_PER_STEP):
        _expert_block(pl.program_id(0) * BLOCKS_PER_STEP + j, j * BM, be_ref, first_ref, next2_ref,
                      stage_ref, nused_ref, x_ref, y_ref, wg_f, wu_f, wd_f, wg_b, wu_b, wd_b, weight_copies)


def _expert_block(b, tok0, be_ref, first_ref, next2_ref, stage_ref, nused_ref, x_ref, y_ref,
                  wg_f, wu_f, wd_f, wg_b, wu_b, wd_b, weight_copies):
    @pl.when(b < nused_ref[0])
    def _():
        def new_expert(st):
            for c in weight_copies(be_ref[b], st):
                c.wait()
            wg_b[...] = wg_f[st].astype(BF16)
            wu_b[...] = wu_f[st].astype(BF16)
            wd_b[...] = wd_f[st].astype(BF16)

            @pl.when(next2_ref[b] >= 0)
            def _():
                for c in weight_copies(next2_ref[b], st):
                    c.start()

        for st in range(2):
            pl.when((first_ref[b] == 1) & (stage_ref[b] == st))(functools.partial(new_expert, st))

        x = _load_token_tiles(x_ref, BM, tok0)
        a = _dot(x, wg_b[...])
        u = _dot(x, wu_b[...])
        y = _dot((_silu(a) * u).astype(BF16), wd_b[...])
        _store_token_tiles(y_ref, y.astype(BF16).astype(F32), BM, tok0)

    @pl.when(b >= nused_ref[0])
    def _():
        y_ref[tok0 * TOK_ROWS:(tok0 + BM) * TOK_ROWS, :] = jnp.zeros((BM * TOK_ROWS, LANES), jnp.uint32)


def _experts(block_e, first, next_e, next2_e, stage, nused, xb, wg, wu, wd):
    def xmap(b, *_):
        return (b, 0)

    grid_spec = pltpu.PrefetchScalarGridSpec(
        num_scalar_prefetch=6,
        grid=(N_BLOCKS // BLOCKS_PER_STEP,),
        in_specs=[
            pl.BlockSpec((BLOCKS_PER_STEP * BM * TOK_ROWS, LANES), xmap),
            pl.BlockSpec(memory_space=pl.ANY),
            pl.BlockSpec(memory_space=pl.ANY),
            pl.BlockSpec(memory_space=pl.ANY),
        ],
        out_specs=pl.BlockSpec((BLOCKS_PER_STEP * BM * TOK_ROWS, LANES), xmap),
        scratch_shapes=[pltpu.VMEM((2, D_MODEL, D_EXPERT), F32), pltpu.VMEM((2, D_MODEL, D_EXPERT), F32),
                        pltpu.VMEM((2, D_EXPERT, D_MODEL), F32),
                        pltpu.VMEM((D_MODEL, D_EXPERT), BF16), pltpu.VMEM((D_MODEL, D_EXPERT), BF16),
                        pltpu.VMEM((D_EXPERT, D_MODEL), BF16), pltpu.SemaphoreType.DMA((2, 3))],
    )
    return pl.pallas_call(
        _experts_kernel,
        grid_spec=grid_spec,
        out_shape=jax.ShapeDtypeStruct((P_ROWS * TOK_ROWS, LANES), jnp.uint32),
        compiler_params=pltpu.CompilerParams(dimension_semantics=("arbitrary",),
                                             vmem_limit_bytes=VMEM_LIMIT),
        name="experts",
    )(block_e, first, next_e, next2_e, stage, nused, xb, wg, wu, wd)


def _combine_kernel(n_ref, row_ref, tot_ref, yb_ref, h_ref, q_ref, w_ref, g_ref, out_ref, yb0, yb1, sem):
    step = pl.program_id(0)
    bufs = (yb0, yb1)

    def start_gather(tile, s):
        _start_tile_runs(n_ref, row_ref, tile,
                         lambda loc, glob: pltpu.make_async_copy(yb_ref.at[glob], bufs[s].at[loc], sem.at[s]))

    @pl.when(step == 0)
    def _():
        for buf in bufs:
            buf[...] = jnp.zeros_like(buf)
        start_gather(0, 0)

    for s in range(TILES_PER_STEP):
        tile = step * TILES_PER_STEP + s
        if s + 1 < TILES_PER_STEP:
            start_gather(tile + 1, s + 1)
        else:
            pl.when(step + 1 < pl.num_programs(0))(functools.partial(start_gather, tile + 1, 0))

        rows = pl.ds(0, tot_ref[tile] * TOK_ROWS)
        pltpu.make_async_copy(yb_ref.at[rows], bufs[s].at[rows], sem.at[s]).wait()
        ys = _load_token_tiles(bufs[s], SLOTS)
        toks = slice(s * TM_MIX, (s + 1) * TM_MIX)
        col = lax.broadcasted_iota(jnp.int32, (TM_MIX, SLOTS), 1)
        wmat = (jnp.where(col == q_ref[toks, 0:1], w_ref[toks, 0:1], 0.0)
                + jnp.where(col == q_ref[toks, 1:2], w_ref[toks, 1:2], 0.0)).astype(BF16)
        hh = h_ref[toks, :] + _dot(wmat, ys)
        out_ref[toks, :] = hh * lax.rsqrt(jnp.mean(hh * hh, axis=-1, keepdims=True) + EPS) * g_ref[...]


def _combine(n_flat, run_rows, tot, yb, h1, qt, wts, gf):
    grid_spec = pltpu.PrefetchScalarGridSpec(
        num_scalar_prefetch=3,
        grid=(N_TILES // TILES_PER_STEP,),
        in_specs=[
            pl.BlockSpec(memory_space=pl.ANY),
            pl.BlockSpec((TILES_PER_STEP * TM_MIX, D_MODEL), lambda i, *_: (i, 0)),
            pl.BlockSpec((TILES_PER_STEP * TM_MIX, TOP_K), lambda i, *_: (i, 0)),
            pl.BlockSpec((TILES_PER_STEP * TM_MIX, TOP_K), lambda i, *_: (i, 0)),
            pl.BlockSpec((1, D_MODEL), lambda i, *_: (0, 0)),
        ],
        out_specs=pl.BlockSpec((TILES_PER_STEP * TM_MIX, D_MODEL), lambda i, *_: (i, 0)),
        scratch_shapes=[pltpu.VMEM((SLOTS * TOK_ROWS, LANES), jnp.uint32),
                        pltpu.VMEM((SLOTS * TOK_ROWS, LANES), jnp.uint32),
                        pltpu.SemaphoreType.DMA((2,))],
    )
    return pl.pallas_call(
        _combine_kernel,
        grid_spec=grid_spec,
        out_shape=jax.ShapeDtypeStruct((SEQ, D_MODEL), F32),
        compiler_params=pltpu.CompilerParams(dimension_semantics=("arbitrary",),
                                             vmem_limit_bytes=VMEM_LIMIT),
        name="combine",
    )(n_flat, run_rows, tot, yb, h1, qt, wts, gf)


def kernel(x, meta, norm1_g, w_in, w_decay_up, b_decay, gla_norm_g, conv_w, conv_b, conv_ln_g,
           conv_ln_b, w_pw2, b_gate, w_out, norm2_g, w_router_group, b_router_group,
           w_router_expert, b_router_expert, w_exp_gate, w_exp_up, w_exp_down, final_norm_g):
    assert x.shape == (1, SEQ, D_MODEL) and w_in.shape[0] == 1
    x2d = x[0]
    front = jnp.concatenate([jnp.zeros((FRONT - N_META, D_MODEL), F32), meta.astype(F32)], axis=0)

    w = w_in[0]
    row2 = lambda a: a.reshape(1, -1).astype(F32)
    wq, wk, wv, wog, wdl, wglu, wgt = _wcast(w)
    up = w_decay_up[0].astype(F32)
    up_hi = up.astype(BF16)
    up_mid = (up - up_hi.astype(F32)).astype(BF16)
    up_lo = (up - up_hi.astype(F32) - up_mid.astype(F32)).astype(BF16)
    wdu = jnp.concatenate([up_hi, up_mid, up_lo, up_hi, up_mid, up_hi], axis=0)
    q, k, v, og, la, u, gates = _inproj(
        front, x2d, row2(norm1_g[0]), wq, wk, wv, wog, jnp.tile(wdl, (1, DL_COPIES)), wglu, wgt,
        wdu, row2(b_decay[0]), row2(b_gate[0]))

    o_n = _gla(q, k, v, la, row2(gla_norm_g[0]))

    wr = jnp.concatenate([w_router_group[0].T, w_router_expert[0].T,
                          jnp.zeros((LANES - N_GROUPS - N_EXPERTS, D_MODEL), F32)], axis=0).astype(F32)
    br = jnp.concatenate([b_router_group[0], b_router_expert[0],
                          jnp.zeros((LANES - N_GROUPS - N_EXPERTS,), F32)]).reshape(LANES, 1).astype(F32)
    wr_hi = wr.astype(BF16)
    wr_lo = (wr - wr_hi.astype(F32)).astype(BF16)
    h1, hn2, ri, rw, tabn, tabb = _mix(
        o_n, og, u, gates, x2d, conv_w[0].astype(F32).reshape(CONV_WIDTH, ROW_TILE, LANES),
        conv_b[0].astype(F32).reshape(ROW_TILE, LANES), row2(conv_ln_g[0]),
        row2(conv_ln_b[0]), w_pw2[0].astype(BF16), w_out[0].astype(BF16),
        row2(norm2_g[0]), wr_hi, wr_lo, br)

    experts = slice(N_GROUPS, N_GROUPS + N_EXPERTS)
    n_te = tabn[::SUBLANES, experts].astype(jnp.int32)
    seen_te = tabb[::SUBLANES, experts].astype(jnp.int32)
    counts = seen_te[-1] + n_te[-1]
    padded = (counts + BM - 1) // BM * BM
    pad_end = jnp.cumsum(padded)
    pad_start = pad_end - padded
    n_flat = n_te.reshape(-1)
    tot = jnp.sum(n_te, axis=1).astype(jnp.int32)
    run_rows = (pad_start[None, :] + seen_te).reshape(-1).astype(jnp.int32)
    tail_start = (pad_start + counts).astype(jnp.int32)
    tail_len = (padded - counts).astype(jnp.int32)
    blk = jnp.arange(N_BLOCKS, dtype=jnp.int32)
    block_e = jnp.minimum(jnp.sum((pad_end[None, :] <= blk[:, None] * BM).astype(jnp.int32), axis=1),
                          N_EXPERTS - 1).astype(jnp.int32)
    first = jnp.concatenate([jnp.ones((1,), jnp.int32), (block_e[1:] != block_e[:-1]).astype(jnp.int32)])
    nused = (pad_end[-1:] // BM).astype(jnp.int32)
    eid = jnp.arange(N_EXPERTS, dtype=jnp.int32)
    later = jnp.flip(lax.cummin(jnp.flip(jnp.where(counts > 0, eid, N_EXPERTS))))
    nxt = jnp.concatenate([later[1:], jnp.full((2,), N_EXPERTS, jnp.int32)])
    nxt2 = nxt[nxt[:N_EXPERTS]]
    or_none = lambda t: jnp.where(t < N_EXPERTS, t, -1)[block_e].astype(jnp.int32)
    next_e, next2_e = or_none(nxt[:N_EXPERTS]), or_none(nxt2)
    stage = ((jnp.cumsum(first) - 1) % 2).astype(jnp.int32)

    xb = _dispatch(n_flat, run_rows, tot, tail_start, tail_len, nused, hn2, ri)
    yb = _experts(block_e, first, next_e, next2_e, stage, nused, xb, w_exp_gate[0], w_exp_up[0], w_exp_down[0])
    out = _combine(n_flat, run_rows, tot, yb, h1, ri[0:TOP_K].T, rw[0:TOP_K].T, row2(final_norm_g))
    return out[None]
```

```python
import functools

import jax
import jax.numpy as jnp
import numpy as np
from jax import lax
from jax.experimental import pallas as pl
from jax.experimental.pallas import tpu as pltpu

F32 = jnp.float32
BF16 = jnp.bfloat16

D_MODEL = 1024
SEQ = 16384
N_META = 16
GLA_HEADS = 4
GLA_DK = 128
GLA_DV = 256
GLA_RANK = 16
GLA_TAU = 16.0
CHUNK = 64
QK_W = GLA_HEADS * GLA_DK
V_W = GLA_HEADS * GLA_DV
CONV_WIDTH = 31
N_GROUPS = 8
EXPERTS_PER_GROUP = 8
N_EXPERTS = N_GROUPS * EXPERTS_PER_GROUP
TOP_K = 2
D_EXPERT = 512
EPS = 1e-6
LOG2E = 1.4426950408889634

LANES = 128
SUBLANES = 8
ROW_TILE = D_MODEL // LANES
TOK_ROWS = D_MODEL // 2 // LANES
TOK_ALIGN = SUBLANES // TOK_ROWS

FRONT = 512
LP = FRONT + SEQ
TM_PROJ = 512
DL_COPIES = 6
TM_GLA = 512
TM_MIX = 256
MIX_TILES = 2
HIST = 32
CONV_RB = 16
N_TILES = SEQ // TM_MIX
SLOTS = TOP_K * TM_MIX + N_EXPERTS * (TOK_ALIGN - 1)
TILES_PER_STEP = 2
BM = 256
BLOCKS_PER_STEP = 4
N_BLOCKS = (N_TILES * SLOTS) // BM + N_EXPERTS
P_ROWS = N_BLOCKS * BM
VMEM_LIMIT = 56 * 1024 * 1024


def _sigmoid(x):
    return 0.5 * jnp.tanh(0.5 * x) + 0.5


def _silu(x):
    return x * _sigmoid(x)


def _dot(a, b, **kw):
    return jnp.dot(a, b, preferred_element_type=F32, **kw)


def _load_row_tiles(ref, n_rows):
    return jnp.concatenate([ref[pl.ds(s, n_rows, stride=ROW_TILE), :] for s in range(ROW_TILE)], axis=-1)


def _store_row_tiles(ref, val, n_rows):
    for s in range(ROW_TILE):
        ref[pl.ds(s, n_rows, stride=ROW_TILE), :] = val[:, s * LANES:(s + 1) * LANES]


def _load_token_tiles(ref, n_tok, first_tok=0):
    r0 = first_tok * TOK_ROWS
    w = jnp.concatenate([ref[pl.ds(r0 + s, n_tok, stride=TOK_ROWS), :] for s in range(TOK_ROWS)], axis=-1)
    lo = pltpu.bitcast(w << 16, F32)
    hi = pltpu.bitcast(w & jnp.uint32(0xFFFF0000), F32)
    return jnp.concatenate([lo, hi], axis=-1).astype(BF16)


def _store_token_tiles(ref, val, n_tok, first_tok=0):
    bits = pltpu.bitcast(val, jnp.uint32)
    w = (bits[:, 0:D_MODEL // 2] >> 16) | (bits[:, D_MODEL // 2:D_MODEL] & jnp.uint32(0xFFFF0000))
    r0 = first_tok * TOK_ROWS
    for s in range(TOK_ROWS):
        ref[pl.ds(r0 + s, n_tok, stride=TOK_ROWS), :] = w[:, s * LANES:(s + 1) * LANES]


IN_SPLITS = (QK_W, QK_W, V_W, D_MODEL, GLA_RANK, 2 * D_MODEL, 2 * D_MODEL)
WCAST_ROWS = 128


def _wcast_kernel(w_ref, *outs):
    w = w_ref[...]
    off = 0
    for out, n in zip(outs, IN_SPLITS):
        out[...] = w[:, off:off + n].astype(BF16)
        off += n


def _wcast(w):
    return pl.pallas_call(
        _wcast_kernel,
        grid=(D_MODEL // WCAST_ROWS,),
        in_specs=[pl.BlockSpec((WCAST_ROWS, sum(IN_SPLITS)), lambda i: (i, 0))],
        out_specs=[pl.BlockSpec((WCAST_ROWS, n), lambda i: (i, 0)) for n in IN_SPLITS],
        out_shape=[jax.ShapeDtypeStruct((D_MODEL, n), BF16) for n in IN_SPLITS],
        compiler_params=pltpu.CompilerParams(dimension_semantics=("arbitrary",),
                                             vmem_limit_bytes=VMEM_LIMIT),
        name="wcast",
    )(w)


def _inproj_kernel(front_ref, x_ref, g_ref, wq, wk, wv, wog, wdl, wglu, wgt, wdu, bd, bg,
                   q_o, k_o, v_o, og_o, la_o, u_o, gt_o):
    i = pl.program_id(0)
    h = jnp.where(i < FRONT // TM_PROJ, front_ref[...], x_ref[...])
    ms = jnp.mean(h * h, axis=-1, keepdims=True)
    hn = (h * lax.rsqrt(ms + EPS) * g_ref[...]).astype(BF16)
    q_o[...] = _dot(hn, wq[...]).astype(BF16)
    k_o[...] = _dot(hn, wk[...]).astype(BF16)
    v_o[...] = _dot(hn, wv[...]).astype(BF16)
    og_o[...] = _silu(_dot(hn, wog[...])).astype(BF16)
    glu = _dot(hn, wglu[...])
    _store_row_tiles(u_o, glu[:, 0:D_MODEL] * _sigmoid(glu[:, D_MODEL:2 * D_MODEL]), TM_PROJ)
    gt_o[...] = _sigmoid(_dot(hn, wgt[...]) + bg[...]).astype(BF16)
    d6 = _dot(hn, wdl[...])
    hi = d6.astype(BF16).astype(F32)
    mid = (d6 - hi).astype(BF16).astype(F32)
    lane = lax.broadcasted_iota(jnp.int32, d6.shape, 1)
    pieces = jnp.where(lane < 3 * GLA_RANK, hi, jnp.where(lane < 5 * GLA_RANK, mid, d6 - hi - mid))
    z = _dot(pieces.astype(BF16), wdu[...]) + bd[...]
    la_o[...] = (jnp.minimum(z, 0.0) - jnp.log(1.0 + jnp.exp(-jnp.abs(z)))) * (LOG2E / GLA_TAU)


def _inproj(front, x2d, g1, wq, wk, wv, wog, wdl, wglu, wgt, wdu, bd, bg):
    n_front = FRONT // TM_PROJ
    grid = (LP // TM_PROJ,)
    row = lambda w: pl.BlockSpec((TM_PROJ, w), lambda i: (i, 0))
    full = lambda a: pl.BlockSpec(a.shape, lambda i: (0, 0))
    return pl.pallas_call(
        _inproj_kernel,
        grid=grid,
        in_specs=[
            pl.BlockSpec((TM_PROJ, D_MODEL), lambda i: (jnp.minimum(i, n_front - 1), 0)),
            pl.BlockSpec((TM_PROJ, D_MODEL), lambda i: (jnp.maximum(i - n_front, 0), 0)),
            full(g1), full(wq), full(wk), full(wv), full(wog), full(wdl), full(wglu), full(wgt),
            full(wdu), full(bd), full(bg),
        ],
        out_specs=[row(QK_W), row(QK_W), row(V_W), row(D_MODEL), row(QK_W),
                   pl.BlockSpec((TM_PROJ * ROW_TILE, LANES), lambda i: (i, 0)), row(2 * D_MODEL)],
        out_shape=[
            jax.ShapeDtypeStruct((LP, QK_W), BF16), jax.ShapeDtypeStruct((LP, QK_W), BF16),
            jax.ShapeDtypeStruct((LP, V_W), BF16), jax.ShapeDtypeStruct((LP, D_MODEL), BF16),
            jax.ShapeDtypeStruct((LP, QK_W), F32), jax.ShapeDtypeStruct((LP * ROW_TILE, LANES), F32),
            jax.ShapeDtypeStruct((LP, 2 * D_MODEL), BF16),
        ],
        compiler_params=pltpu.CompilerParams(dimension_semantics=("arbitrary",),
                                             vmem_limit_bytes=VMEM_LIMIT),
        name="inproj",
    )(front, x2d, g1, wq, wk, wv, wog, wdl, wglu, wgt, wdu, bd, bg)


GLA_LEVELS = (32, 16, 8, 4, 2, 1)
GLA_FINE = tuple(m for m in GLA_LEVELS if m < SUBLANES)
LA_SPLIT = 3


def _gla_tables():
    r = np.arange(CHUNK)
    t, c = r[:, None], r[None, :]
    rows = [c <= t]
    pair = []
    for m in GLA_LEVELS:
        mid = (t // (2 * m)) * (2 * m) + m - 1
        right = (t % (2 * m)) >= m
        if m in GLA_FINE:
            rows.append(np.where(right, (c > mid) & (c <= t), (c > t) & (c <= mid)))
        pair.append(((t // (2 * m)) == (c // (2 * m))) & right & ((c % (2 * m)) < m))
    pair.append(t == c)
    sums = np.concatenate(rows, axis=0).astype(np.float32)
    sums = np.concatenate([sums] * LA_SPLIT, axis=1)
    pair = np.stack([np.kron(np.eye(GLA_HEADS), p) for p in pair]).astype(np.float32)
    return jnp.asarray(sums, BF16), jnp.asarray(pair)


def _split_bf16(x):
    pieces = []
    rest = x
    for _ in range(LA_SPLIT):
        p = rest.astype(BF16)
        pieces.append(p)
        rest = rest - p.astype(F32)
    return jnp.concatenate(pieces, axis=0)


def _dot_nt(a, b):
    return lax.dot_general(a, b, (((1,), (1,)), ((), ())), preferred_element_type=F32)


def _dot_tn(a, b):
    return lax.dot_general(a, b, (((0,), (0,)), ((), ())), preferred_element_type=F32)


def _gla_exponents(a):
    b = a[0:CHUNK]
    parts = [b, b[CHUNK - 1:CHUNK] - b]
    for m in GLA_LEVELS:
        if m in GLA_FINE:
            continue
        for lo in range(0, CHUNK, 2 * m):
            ref = b[lo + m - 1:lo + m]
            parts += [ref - b[lo:lo + m], b[lo + m:lo + 2 * m] - ref]
    return jnp.concatenate(parts + [a[CHUNK:]], axis=0)


def _gla_kernel(q_ref, k_ref, v_ref, la_ref, g_ref, sums_ref, pair_ref, o_ref, s_ref):
    @pl.when(pl.program_id(0) == 0)
    def _():
        s_ref[...] = jnp.zeros_like(s_ref)

    row = lax.broadcasted_iota(jnp.int32, (CHUNK, GLA_DK), 0)
    right_rows = [(row & m) != 0 for m in GLA_LEVELS]
    sums = sums_ref[...]
    n_lvl = len(GLA_LEVELS)

    def body(c, carry):
        r0 = pl.multiple_of(c * CHUNK, CHUNK)
        rows = pl.ds(r0, CHUNK)
        states = [s_ref[h] for h in range(GLA_HEADS)]
        qs, ks, es, decs = [], [], [], []
        for hp in range(GLA_HEADS // 2):
            la3 = _split_bf16(la_ref[rows, 2 * hp * GLA_DK:2 * (hp + 1) * GLA_DK])
            a2 = _dot(sums, la3)
            e2 = jnp.exp2(_gla_exponents(a2))
            dec2 = jnp.exp2(jnp.transpose(jnp.broadcast_to(a2[CHUNK - 1:CHUNK], (LANES, 2 * GLA_DK))))
            for h in (2 * hp, 2 * hp + 1):
                half = slice((h % 2) * GLA_DK, (h % 2 + 1) * GLA_DK)
                kc = slice(h * GLA_DK, (h + 1) * GLA_DK)
                qs.append(q_ref[rows, kc].astype(F32) * (GLA_DK ** -0.5))
                ks.append(k_ref[rows, kc].astype(F32))
                es.append(e2[:, half])
                decs.append(dec2[half])
        vbs = [v_ref[rows, h * GLA_DV:(h + 1) * GLA_DV] for h in range(GLA_HEADS)]

        new_states, qes = [], []
        for h in range(GLA_HEADS):
            qes.append((qs[h] * es[h][0:CHUNK]).astype(BF16))
            kd = (ks[h] * es[h][CHUNK:2 * CHUNK]).astype(BF16)
            new_states.append(states[h] * jnp.concatenate([decs[h]] * (GLA_DV // LANES), axis=1)
                              + _dot_tn(kd, vbs[h]))

        stack = lambda parts: jnp.concatenate(parts, axis=0)
        sc = _dot_nt(stack([q.astype(BF16) for q in qs]), stack([k.astype(BF16) for k in ks])) * pair_ref[n_lvl]
        for lvl in range(n_lvl):
            x = stack([(jnp.where(right_rows[lvl], qs[h], ks[h])
                        * es[h][(2 + lvl) * CHUNK:(3 + lvl) * CHUNK]).astype(BF16) for h in range(GLA_HEADS)])
            sc = sc + _dot_nt(x, x) * pair_ref[lvl]
        o_all = _dot(sc.astype(BF16), stack(vbs))
        for h in range(GLA_HEADS):
            vc = slice(h * GLA_DV, (h + 1) * GLA_DV)
            o = o_all[h * CHUNK:(h + 1) * CHUNK] + _dot(qes[h], states[h].astype(BF16))
            o = o * lax.rsqrt(jnp.mean(o * o, axis=-1, keepdims=True) + EPS) * g_ref[:, vc]
            o_ref[rows, vc] = o.astype(BF16)
        for h in range(GLA_HEADS):
            s_ref[h] = new_states[h]
        return carry

    lax.fori_loop(0, TM_GLA // CHUNK, body, 0, unroll=4)


def _gla(q, k, v, la, g):
    sums, pair = _gla_tables()
    row = lambda w: pl.BlockSpec((TM_GLA, w), lambda t: (t, 0))
    full = lambda a: pl.BlockSpec(a.shape, lambda t: (0,) * a.ndim)
    return pl.pallas_call(
        _gla_kernel,
        grid=(LP // TM_GLA,),
        in_specs=[row(QK_W), row(QK_W), row(V_W), row(QK_W), full(g), full(sums), full(pair)],
        out_specs=row(V_W),
        out_shape=jax.ShapeDtypeStruct((LP, V_W), BF16),
        scratch_shapes=[pltpu.VMEM((GLA_HEADS, GLA_DK, GLA_DV), F32)],
        compiler_params=pltpu.CompilerParams(dimension_semantics=("arbitrary",),
                                             vmem_limit_bytes=VMEM_LIMIT),
        name="gla",
    )(q, k, v, la, g, sums, pair)


def _mix_kernel(o_ref, og_ref, u_ref, hist_ref, gt_ref, x_ref, cw_ref, cb_ref, lg_ref, lb_ref,
                wpw_ref, wo_ref, g2_ref, wrh_ref, wrl_ref, br_ref,
                h_o, hn_o, ri_o, rw_o, tabn_o, tabb_o, cbuf, seen):
    @pl.when(pl.program_id(0) == 0)
    def _():
        seen[...] = jnp.zeros_like(seen)

    for j in range(MIX_TILES):
        r = slice(j * TM_MIX, (j + 1) * TM_MIX)
        tab = slice(j * SUBLANES, (j + 1) * SUBLANES)

        def u_row(t, j=j):
            t += j * TM_MIX
            ref, t = (hist_ref, t + HIST) if t < 0 else (u_ref, t)
            return ref[t * ROW_TILE:(t + 1) * ROW_TILE, :]

        _mix_tile(o_ref.at[r], og_ref.at[r], u_row, gt_ref.at[r], x_ref.at[r], cw_ref, cb_ref,
                  lg_ref, lb_ref, wpw_ref, wo_ref, g2_ref, wrh_ref, wrl_ref, br_ref,
                  h_o.at[r], hn_o.at[r], ri_o.at[:, r], rw_o.at[:, r], tabn_o.at[tab], tabb_o.at[tab],
                  cbuf, seen)


def _mix_tile(o_ref, og_ref, u_row, gt_ref, x_ref, cw_ref, cb_ref, lg_ref, lb_ref,
              wpw_ref, wo_ref, g2_ref, wrh_ref, wrl_ref, br_ref,
              h_o, hn_o, ri_o, rw_o, tabn_o, tabb_o, cbuf, seen):
    taps = [cw_ref[jj] for jj in range(CONV_WIDTH)]
    for t0 in range(0, TM_MIX, CONV_RB):
        part = [cb_ref[...]] * CONV_RB
        for rr in range(t0 - (CONV_WIDTH - 1), t0 + CONV_RB):
            ur = u_row(rr)
            for t in range(max(t0, rr), min(t0 + CONV_RB, rr + CONV_WIDTH)):
                part[t - t0] = part[t - t0] + taps[rr - t + CONV_WIDTH - 1] * ur
        for g in range(CONV_RB):
            cbuf[(t0 + g) * ROW_TILE:(t0 + g + 1) * ROW_TILE, :] = part[g]
    acc = _load_row_tiles(cbuf, TM_MIX)

    mu = jnp.mean(acc, axis=-1, keepdims=True)
    xc = acc - mu
    ln = xc * lax.rsqrt(jnp.mean(xc * xc, axis=-1, keepdims=True) + EPS) * lg_ref[...] + lb_ref[...]
    branch_b = _dot(_silu(ln).astype(BF16), wpw_ref[...])

    branch_a = o_ref[...].astype(F32) * og_ref[...].astype(F32)
    g_a = gt_ref[:, 0:D_MODEL].astype(F32)
    g_b = gt_ref[:, D_MODEL:2 * D_MODEL].astype(F32)
    merged = (g_a * branch_a + g_b * branch_b).astype(BF16)
    h1 = x_ref[...] + _dot(merged, wo_ref[...])
    h_o[...] = h1

    hn2 = h1 * lax.rsqrt(jnp.mean(h1 * h1, axis=-1, keepdims=True) + EPS) * g2_ref[...]
    hn_hi = hn2.astype(BF16)
    hn_o[...] = hn_hi

    hn_lo = (hn2 - hn_hi.astype(F32)).astype(BF16)
    logits = (_dot_nt(wrh_ref[...], hn_hi) + _dot_nt(wrh_ref[...], hn_lo) + _dot_nt(wrl_ref[...], hn_hi)
              + br_ref[...])
    row = lax.broadcasted_iota(jnp.int32, logits.shape, 0)
    rowf = row.astype(F32)
    neg = -jnp.inf
    is_g = row < N_GROUPS
    lg = jnp.where(is_g, logits, neg)
    gmax = jnp.max(lg, axis=0, keepdims=True)
    gidx = jnp.min(jnp.where(lg == gmax, rowf, float(N_GROUPS)), axis=0, keepdims=True)
    g_w = 1.0 / jnp.sum(jnp.where(is_g, jnp.exp(lg - gmax), 0.0), axis=0, keepdims=True)
    erow = rowf - float(N_GROUPS)
    egrp = ((row - N_GROUPS) >> 3).astype(F32)
    in_grp = (row >= N_GROUPS) & (row < N_GROUPS + N_EXPERTS) & (egrp == gidx)
    le = jnp.where(in_grp, logits, neg)
    m1 = jnp.max(le, axis=0, keepdims=True)
    i1 = jnp.min(jnp.where(le == m1, erow, float(N_EXPERTS)), axis=0, keepdims=True)
    le2 = jnp.where(erow == i1, neg, le)
    m2 = jnp.max(le2, axis=0, keepdims=True)
    i2 = jnp.min(jnp.where(le2 == m2, erow, float(N_EXPERTS)), axis=0, keepdims=True)
    t = jnp.exp(m2 - m1)
    w1 = g_w / (1.0 + t)
    w2 = g_w * t / (1.0 + t)

    def run_len(n):
        return jnp.floor((n + (TOK_ALIGN - 1)) * (1.0 / TOK_ALIGN)) * TOK_ALIGN

    oh1 = erow == i1
    oh2 = erow == i2
    oh = jnp.where(oh1 | oh2, 1.0, 0.0).astype(BF16)
    tr = lax.broadcasted_iota(jnp.int32, (TM_MIX, TM_MIX), 0)
    tc = lax.broadcasted_iota(jnp.int32, (TM_MIX, TM_MIX), 1)
    before_tok = _dot(oh, (tr < tc).astype(BF16))
    n_col = jnp.sum(oh.astype(F32), axis=1, keepdims=True)
    er = lax.broadcasted_iota(jnp.int32, (LANES, LANES), 0)
    ec = lax.broadcasted_iota(jnp.int32, (LANES, LANES), 1)
    before_exp = _dot((ec < er).astype(BF16), jnp.broadcast_to(run_len(n_col), logits.shape).astype(BF16))
    where_to = before_exp + before_tok
    q1 = jnp.sum(jnp.where(oh1, where_to, 0.0), axis=0, keepdims=True)
    q2 = jnp.sum(jnp.where(oh2, where_to, 0.0), axis=0, keepdims=True)

    n_rows = run_len(_dot_nt(jnp.ones((SUBLANES, TM_MIX), BF16), oh))
    tabn_o[...] = n_rows
    tabb_o[...] = seen[...]
    seen[...] = seen[...] + n_rows

    ri_o[...] = jnp.zeros_like(ri_o)
    ri_o[0:1, :] = q1.astype(jnp.int32)
    ri_o[1:2, :] = q2.astype(jnp.int32)
    rw_o[...] = jnp.zeros_like(rw_o)
    rw_o[0:1, :] = w1
    rw_o[1:2, :] = w2


def _mix(o_n, og, u, gates, x2d, cw, cb, lg, lb, wpw, wo, g2, wrh, wrl, br):
    tm = MIX_TILES * TM_MIX
    n_front = FRONT // tm
    grid = (SEQ // tm,)
    rowp = lambda w: pl.BlockSpec((tm, w), lambda i: (i + n_front, 0))
    full = lambda a: pl.BlockSpec(a.shape, lambda i: (0,) * a.ndim)
    hist_blocks = tm // HIST
    return pl.pallas_call(
        _mix_kernel,
        grid=grid,
        in_specs=[
            rowp(V_W), rowp(D_MODEL),
            pl.BlockSpec((tm * ROW_TILE, LANES), lambda i: (i + n_front, 0)),
            pl.BlockSpec((HIST * ROW_TILE, LANES), lambda i: ((i + n_front) * hist_blocks - 1, 0)),
            rowp(2 * D_MODEL),
            pl.BlockSpec((tm, D_MODEL), lambda i: (i, 0)),
            full(cw), full(cb), full(lg), full(lb), full(wpw), full(wo), full(g2),
            full(wrh), full(wrl), full(br),
        ],
        out_specs=[
            pl.BlockSpec((tm, D_MODEL), lambda i: (i, 0)),
            pl.BlockSpec((tm, D_MODEL), lambda i: (i, 0)),
            pl.BlockSpec((SUBLANES, tm), lambda i: (0, i)),
            pl.BlockSpec((SUBLANES, tm), lambda i: (0, i)),
            pl.BlockSpec((MIX_TILES * SUBLANES, LANES), lambda i: (i, 0)),
            pl.BlockSpec((MIX_TILES * SUBLANES, LANES), lambda i: (i, 0)),
        ],
        out_shape=[
            jax.ShapeDtypeStruct((SEQ, D_MODEL), F32),
            jax.ShapeDtypeStruct((SEQ, D_MODEL), BF16),
            jax.ShapeDtypeStruct((SUBLANES, SEQ), jnp.int32),
            jax.ShapeDtypeStruct((SUBLANES, SEQ), F32),
            jax.ShapeDtypeStruct((N_TILES * SUBLANES, LANES), F32),
            jax.ShapeDtypeStruct((N_TILES * SUBLANES, LANES), F32),
        ],
        scratch_shapes=[pltpu.VMEM((TM_MIX * ROW_TILE, LANES), F32), pltpu.VMEM((SUBLANES, LANES), F32)],
        compiler_params=pltpu.CompilerParams(dimension_semantics=("arbitrary",),
                                             vmem_limit_bytes=VMEM_LIMIT),
        name="mix",
    )(o_n, og, u, u, gates, x2d, cw, cb, lg, lb, wpw, wo, g2, wrh, wrl, br)


def _tok_rows(start_tok, n_tok):
    return pl.ds(pl.multiple_of(start_tok * TOK_ROWS, SUBLANES), n_tok * TOK_ROWS)


def _start_tile_runs(n_ref, row_ref, tile, make_copy):
    def run(e, off):
        n = n_ref[tile * N_EXPERTS + e]

        @pl.when(n > 0)
        def _():
            make_copy(_tok_rows(off, n), _tok_rows(row_ref[tile * N_EXPERTS + e], n)).start()

        return off + n

    lax.fori_loop(0, N_EXPERTS, run, 0, unroll=8)


def _dispatch_kernel(n_ref, row_ref, tot_ref, ts_ref, tl_ref, nused_ref, hn_ref, q_ref, xb_ref,
                     sb0, sb1, zbuf, sem, fsem):
    step = pl.program_id(0)
    last = pl.num_programs(0) - 1

    def fills(start):
        def tail(e, c):
            n = tl_ref[e]

            @pl.when(n > 0)
            def _():
                cp = pltpu.make_async_copy(zbuf.at[pl.ds(0, n * TOK_ROWS)],
                                           xb_ref.at[_tok_rows(ts_ref[e], n)], fsem)
                cp.start() if start else cp.wait()

            return c

        lax.fori_loop(0, N_EXPERTS, tail, 0)

        def block(b, c):
            cp = pltpu.make_async_copy(zbuf, xb_ref.at[_tok_rows(b * BM, BM)], fsem)
            cp.start() if start else cp.wait()
            return c

        lax.fori_loop(nused_ref[0], N_BLOCKS, block, 0)

    @pl.when(step == 0)
    def _():
        zbuf[...] = jnp.zeros_like(zbuf)
        fills(start=True)

    def tile_runs(tile, s, sbuf):
        rows = pl.ds(0, tot_ref[tile] * TOK_ROWS)
        return pltpu.make_async_copy(sbuf.at[rows], xb_ref.at[rows], sem.at[s])

    for s, sbuf in enumerate((sb0, sb1)):
        tile = step * TILES_PER_STEP + s
        toks = slice(s * TM_MIX, (s + 1) * TM_MIX)
        slot_i = lax.broadcasted_iota(jnp.int32, (SLOTS, TM_MIX), 0)
        onehot = jnp.where((slot_i == q_ref[0:1, toks]) | (slot_i == q_ref[1:2, toks]), 1.0, 0.0).astype(BF16)
        srt = _dot(onehot, hn_ref[toks, :])

        @pl.when(step >= 1)
        def _():
            tile_runs(tile - TILES_PER_STEP, s, sbuf).wait()

        _store_token_tiles(sbuf, srt, SLOTS)
        _start_tile_runs(n_ref, row_ref, tile,
                         lambda loc, glob: pltpu.make_async_copy(sbuf.at[loc], xb_ref.at[glob], sem.at[s]))

    @pl.when(step == last)
    def _():
        for s, sbuf in enumerate((sb0, sb1)):
            tile_runs(step * TILES_PER_STEP + s, s, sbuf).wait()
        fills(start=False)


def _dispatch(n_flat, run_rows, tot, tail_start, tail_len, nused, hn2, ri):
    grid_spec = pltpu.PrefetchScalarGridSpec(
        num_scalar_prefetch=6,
        grid=(N_TILES // TILES_PER_STEP,),
        in_specs=[pl.BlockSpec((TILES_PER_STEP * TM_MIX, D_MODEL), lambda i, *_: (i, 0)),
                  pl.BlockSpec((SUBLANES, TILES_PER_STEP * TM_MIX), lambda i, *_: (0, i))],
        out_specs=pl.BlockSpec(memory_space=pl.ANY),
        scratch_shapes=[pltpu.VMEM((SLOTS * TOK_ROWS, LANES), jnp.uint32),
                        pltpu.VMEM((SLOTS * TOK_ROWS, LANES), jnp.uint32),
                        pltpu.VMEM((BM * TOK_ROWS, LANES), jnp.uint32),
                        pltpu.SemaphoreType.DMA((2,)), pltpu.SemaphoreType.DMA(())],
    )
    return pl.pallas_call(
        _dispatch_kernel,
        grid_spec=grid_spec,
        out_shape=jax.ShapeDtypeStruct((P_ROWS * TOK_ROWS, LANES), jnp.uint32),
        compiler_params=pltpu.CompilerParams(dimension_semantics=("arbitrary",),
                                             vmem_limit_bytes=VMEM_LIMIT),
        name="dispatch",
    )(n_flat, run_rows, tot, tail_start, tail_len, nused, hn2, ri)


def _experts_kernel(be_ref, first_ref, next_ref, next2_ref, stage_ref, nused_ref,
                    x_ref, wg_hbm, wu_hbm, wd_hbm, y_ref, wg_f, wu_f, wd_f, wg_b, wu_b, wd_b, sem):
    def weight_copies(e, st):
        return (pltpu.make_async_copy(wg_hbm.at[e], wg_f.at[st], sem.at[st, 0]),
                pltpu.make_async_copy(wu_hbm.at[e], wu_f.at[st], sem.at[st, 1]),
                pltpu.make_async_copy(wd_hbm.at[e], wd_f.at[st], sem.at[st, 2]))

    @pl.when((pl.program_id(0) == 0) & (nused_ref[0] > 0))
    def _():
        for c in weight_copies(be_ref[0], 0):
            c.start()

        @pl.when(next_ref[0] >= 0)
        def _():
            for c in weight_copies(next_ref[0], 1):
                c.start()

    for j in range(BLOCKS_PER_STEP):
        _expert_block(pl.program_id(0) * BLOCKS_PER_STEP + j, j * BM, be_ref, first_ref, next2_ref,
                      stage_ref, nused_ref, x_ref, y_ref, wg_f, wu_f, wd_f, wg_b, wu_b, wd_b, weight_copies)


def _expert_block(b, tok0, be_ref, first_ref, next2_ref, stage_ref, nused_ref, x_ref, y_ref,
                  wg_f, wu_f, wd_f, wg_b, wu_b, wd_b, weight_copies):
    @pl.when(b < nused_ref[0])
    def _():
        def new_expert(st):
            for c in weight_copies(be_ref[b], st):
                c.wait()
            wg_b[...] = wg_f[st].astype(BF16)
            wu_b[...] = wu_f[st].astype(BF16)
            wd_b[...] = wd_f[st].astype(BF16)

            @pl.when(next2_ref[b] >= 0)
            def _():
                for c in weight_copies(next2_ref[b], st):
                    c.start()

        for st in range(2):
            pl.when((first_ref[b] == 1) & (stage_ref[b] == st))(functools.partial(new_expert, st))

        x = _load_token_tiles(x_ref, BM, tok0)
        a = _dot(x, wg_b[...])
        u = _dot(x, wu_b[...])
        y = _dot((_silu(a) * u).astype(BF16), wd_b[...])
        _store_token_tiles(y_ref, y.astype(BF16).astype(F32), BM, tok0)

    @pl.when(b >= nused_ref[0])
    def _():
        y_ref[tok0 * TOK_ROWS:(tok0 + BM) * TOK_ROWS, :] = jnp.zeros((BM * TOK_ROWS, LANES), jnp.uint32)


def _experts(block_e, first, next_e, next2_e, stage, nused, xb, wg, wu, wd):
    def xmap(b, *_):
        return (b, 0)

    grid_spec = pltpu.PrefetchScalarGridSpec(
        num_scalar_prefetch=6,
        grid=(N_BLOCKS // BLOCKS_PER_STEP,),
        in_specs=[
            pl.BlockSpec((BLOCKS_PER_STEP * BM * TOK_ROWS, LANES), xmap),
            pl.BlockSpec(memory_space=pl.ANY),
            pl.BlockSpec(memory_space=pl.ANY),
            pl.BlockSpec(memory_space=pl.ANY),
        ],
        out_specs=pl.BlockSpec((BLOCKS_PER_STEP * BM * TOK_ROWS, LANES), xmap),
        scratch_shapes=[pltpu.VMEM((2, D_MODEL, D_EXPERT), F32), pltpu.VMEM((2, D_MODEL, D_EXPERT), F32),
                        pltpu.VMEM((2, D_EXPERT, D_MODEL), F32),
                        pltpu.VMEM((D_MODEL, D_EXPERT), BF16), pltpu.VMEM((D_MODEL, D_EXPERT), BF16),
                        pltpu.VMEM((D_EXPERT, D_MODEL), BF16), pltpu.SemaphoreType.DMA((2, 3))],
    )
    return pl.pallas_call(
        _experts_kernel,
        grid_spec=grid_spec,
        out_shape=jax.ShapeDtypeStruct((P_ROWS * TOK_ROWS, LANES), jnp.uint32),
        compiler_params=pltpu.CompilerParams(dimension_semantics=("arbitrary",),
                                             vmem_limit_bytes=VMEM_LIMIT),
        name="experts",
    )(block_e, first, next_e, next2_e, stage, nused, xb, wg, wu, wd)


def _combine_kernel(n_ref, row_ref, tot_ref, yb_ref, h_ref, q_ref, w_ref, g_ref, out_ref, yb0, yb1, sem):
    step = pl.program_id(0)
    bufs = (yb0, yb1)

    def start_gather(tile, s):
        _start_tile_runs(n_ref, row_ref, tile,
                         lambda loc, glob: pltpu.make_async_copy(yb_ref.at[glob], bufs[s].at[loc], sem.at[s]))

    @pl.when(step == 0)
    def _():
        for buf in bufs:
            buf[...] = jnp.zeros_like(buf)
        start_gather(0, 0)

    for s in range(TILES_PER_STEP):
        tile = step * TILES_PER_STEP + s
        if s + 1 < TILES_PER_STEP:
            start_gather(tile + 1, s + 1)
        else:
            pl.when(step + 1 < pl.num_programs(0))(functools.partial(start_gather, tile + 1, 0))

        rows = pl.ds(0, tot_ref[tile] * TOK_ROWS)
        pltpu.make_async_copy(yb_ref.at[rows], bufs[s].at[rows], sem.at[s]).wait()
        ys = _load_token_tiles(bufs[s], SLOTS)
        toks = slice(s * TM_MIX, (s + 1) * TM_MIX)
        col = lax.broadcasted_iota(jnp.int32, (TM_MIX, SLOTS), 1)
        wmat = (jnp.where(col == q_ref[toks, 0:1], w_ref[toks, 0:1], 0.0)
                + jnp.where(col == q_ref[toks, 1:2], w_ref[toks, 1:2], 0.0)).astype(BF16)
        hh = h_ref[toks, :] + _dot(wmat, ys)
        out_ref[toks, :] = hh * lax.rsqrt(jnp.mean(hh * hh, axis=-1, keepdims=True) + EPS) * g_ref[...]


def _combine(n_flat, run_rows, tot, yb, h1, qt, wts, gf):
    grid_spec = pltpu.PrefetchScalarGridSpec(
        num_scalar_prefetch=3,
        grid=(N_TILES // TILES_PER_STEP,),
        in_specs=[
            pl.BlockSpec(memory_space=pl.ANY),
            pl.BlockSpec((TILES_PER_STEP * TM_MIX, D_MODEL), lambda i, *_: (i, 0)),
            pl.BlockSpec((TILES_PER_STEP * TM_MIX, TOP_K), lambda i, *_: (i, 0)),
            pl.BlockSpec((TILES_PER_STEP * TM_MIX, TOP_K), lambda i, *_: (i, 0)),
            pl.BlockSpec((1, D_MODEL), lambda i, *_: (0, 0)),
        ],
        out_specs=pl.BlockSpec((TILES_PER_STEP * TM_MIX, D_MODEL), lambda i, *_: (i, 0)),
        scratch_shapes=[pltpu.VMEM((SLOTS * TOK_ROWS, LANES), jnp.uint32),
                        pltpu.VMEM((SLOTS * TOK_ROWS, LANES), jnp.uint32),
                        pltpu.SemaphoreType.DMA((2,))],
    )
    return pl.pallas_call(
        _combine_kernel,
        grid_spec=grid_spec,
        out_shape=jax.ShapeDtypeStruct((SEQ, D_MODEL), F32),
        compiler_params=pltpu.CompilerParams(dimension_semantics=("arbitrary",),
                                             vmem_limit_bytes=VMEM_LIMIT),
        name="combine",
    )(n_flat, run_rows, tot, yb, h1, qt, wts, gf)


def kernel(x, meta, norm1_g, w_in, w_decay_up, b_decay, gla_norm_g, conv_w, conv_b, conv_ln_g,
           conv_ln_b, w_pw2, b_gate, w_out, norm2_g, w_router_group, b_router_group,
           w_router_expert, b_router_expert, w_exp_gate, w_exp_up, w_exp_down, final_norm_g):
    assert x.shape == (1, SEQ, D_MODEL) and w_in.shape[0] == 1
    x2d = x[0]
    front = jnp.concatenate([jnp.zeros((FRONT - N_META, D_MODEL), F32), meta.astype(F32)], axis=0)

    w = w_in[0]
    row2 = lambda a: a.reshape(1, -1).astype(F32)
    wq, wk, wv, wog, wdl, wglu, wgt = _wcast(w)
    up = w_decay_up[0].astype(F32)
    up_hi = up.astype(BF16)
    up_mid = (up - up_hi.astype(F32)).astype(BF16)
    up_lo = (up - up_hi.astype(F32) - up_mid.astype(F32)).astype(BF16)
    wdu = jnp.concatenate([up_hi, up_mid, up_lo, up_hi, up_mid, up_hi], axis=0)
    q, k, v, og, la, u, gates = _inproj(
        front, x2d, row2(norm1_g[0]), wq, wk, wv, wog, jnp.tile(wdl, (1, DL_COPIES)), wglu, wgt,
        wdu, row2(b_decay[0]), row2(b_gate[0]))

    o_n = _gla(q, k, v, la, row2(gla_norm_g[0]))

    wr = jnp.concatenate([w_router_group[0].T, w_router_expert[0].T,
                          jnp.zeros((LANES - N_GROUPS - N_EXPERTS, D_MODEL), F32)], axis=0).astype(F32)
    br = jnp.concatenate([b_router_group[0], b_router_expert[0],
                          jnp.zeros((LANES - N_GROUPS - N_EXPERTS,), F32)]).reshape(LANES, 1).astype(F32)
    wr_hi = wr.astype(BF16)
    wr_lo = (wr - wr_hi.astype(F32)).astype(BF16)
    h1, hn2, ri, rw, tabn, tabb = _mix(
        o_n, og, u, gates, x2d, conv_w[0].astype(F32).reshape(CONV_WIDTH, ROW_TILE, LANES),
        conv_b[0].astype(F32).reshape(ROW_TILE, LANES), row2(conv_ln_g[0]),
        row2(conv_ln_b[0]), w_pw2[0].astype(BF16), w_out[0].astype(BF16),
        row2(norm2_g[0]), wr_hi, wr_lo, br)

    experts = slice(N_GROUPS, N_GROUPS + N_EXPERTS)
    n_te = tabn[::SUBLANES, experts].astype(jnp.int32)
    seen_te = tabb[::SUBLANES, experts].astype(jnp.int32)
    counts = seen_te[-1] + n_te[-1]
    padded = (counts + BM - 1) // BM * BM
    pad_end = jnp.cumsum(padded)
    pad_start = pad_end - padded
    n_flat = n_te.reshape(-1)
    tot = jnp.sum(n_te, axis=1).astype(jnp.int32)
    run_rows = (pad_start[None, :] + seen_te).reshape(-1).astype(jnp.int32)
    tail_start = (pad_start + counts).astype(jnp.int32)
    tail_len = (padded - counts).astype(jnp.int32)
    blk = jnp.arange(N_BLOCKS, dtype=jnp.int32)
    block_e = jnp.minimum(jnp.sum((pad_end[None, :] <= blk[:, None] * BM).astype(jnp.int32), axis=1),
                          N_EXPERTS - 1).astype(jnp.int32)
    first = jnp.concatenate([jnp.ones((1,), jnp.int32), (block_e[1:] != block_e[:-1]).astype(jnp.int32)])
    nused = (pad_end[-1:] // BM).astype(jnp.int32)
    eid = jnp.arange(N_EXPERTS, dtype=jnp.int32)
    later = jnp.flip(lax.cummin(jnp.flip(jnp.where(counts > 0, eid, N_EXPERTS))))
    nxt = jnp.concatenate([later[1:], jnp.full((2,), N_EXPERTS, jnp.int32)])
    nxt2 = nxt[nxt[:N_EXPERTS]]
    or_none = lambda t: jnp.where(t < N_EXPERTS, t, -1)[block_e].astype(jnp.int32)
    next_e, next2_e = or_none(nxt[:N_EXPERTS]), or_none(nxt2)
    stage = ((jnp.cumsum(first) - 1) % 2).astype(jnp.int32)

    xb = _dispatch(n_flat, run_rows, tot, tail_start, tail_len, nused, hn2, ri)
    yb = _experts(block_e, first, next_e, next2_e, stage, nused, xb, w_exp_gate[0], w_exp_up[0], w_exp_down[0])
    out = _combine(n_flat, run_rows, tot, yb, h1, ri[0:TOP_K].T, rw[0:TOP_K].T, row2(final_norm_g))
    return out[None]
```

```python
import functools

import jax
import jax.numpy as jnp
import numpy as np
from jax import lax
from jax.experimental import pallas as pl
from jax.experimental.pallas import tpu as pltpu

F32 = jnp.float32
BF16 = jnp.bfloat16

D_MODEL = 1024
SEQ = 16384
N_META = 16
GLA_HEADS = 4
GLA_DK = 128
GLA_DV = 256
GLA_RANK = 16
GLA_TAU = 16.0
CHUNK = 64
QK_W = GLA_HEADS * GLA_DK
V_W = GLA_HEADS * GLA_DV
CONV_WIDTH = 31
N_GROUPS = 8
EXPERTS_PER_GROUP = 8
N_EXPERTS = N_GROUPS * EXPERTS_PER_GROUP
TOP_K = 2
D_EXPERT = 512
EPS = 1e-6
LOG2E = 1.4426950408889634

LANES = 128
SUBLANES = 8
ROW_TILE = D_MODEL // LANES
TOK_ROWS = D_MODEL // 2 // LANES
TOK_ALIGN = SUBLANES // TOK_ROWS

FRONT = 512
LP = FRONT + SEQ
TM_PROJ = 512
DL_COPIES = 6
TM_GLA = 512
TM_MIX = 256
MIX_TILES = 2
HIST = 32
CONV_RB = 16
N_TILES = SEQ // TM_MIX
SLOTS = TOP_K * TM_MIX + N_EXPERTS * (TOK_ALIGN - 1)
TILES_PER_STEP = 2
COMBINE_TILES = 4
GATHER_AHEAD = 2
BM = 256
BLOCKS_PER_STEP = 8
N_BLOCKS = (N_TILES * SLOTS) // BM + N_EXPERTS
P_ROWS = N_BLOCKS * BM
VMEM_LIMIT = 56 * 1024 * 1024


def _sigmoid(x):
    return 0.5 * jnp.tanh(0.5 * x) + 0.5


def _silu(x):
    return x * _sigmoid(x)


def _dot(a, b, **kw):
    return jnp.dot(a, b, preferred_element_type=F32, **kw)


def _load_row_tiles(ref, n_rows):
    return jnp.concatenate([ref[pl.ds(s, n_rows, stride=ROW_TILE), :] for s in range(ROW_TILE)], axis=-1)


def _store_row_tiles(ref, val, n_rows):
    for s in range(ROW_TILE):
        ref[pl.ds(s, n_rows, stride=ROW_TILE), :] = val[:, s * LANES:(s + 1) * LANES]


def _load_token_tiles(ref, n_tok, first_tok=0):
    r0 = first_tok * TOK_ROWS
    w = jnp.concatenate([ref[pl.ds(r0 + s, n_tok, stride=TOK_ROWS), :] for s in range(TOK_ROWS)], axis=-1)
    lo = pltpu.bitcast(w << 16, F32)
    hi = pltpu.bitcast(w & jnp.uint32(0xFFFF0000), F32)
    return jnp.concatenate([lo, hi], axis=-1).astype(BF16)


def _store_token_tiles(ref, val, n_tok, first_tok=0):
    bits = pltpu.bitcast(val, jnp.uint32)
    w = (bits[:, 0:D_MODEL // 2] >> 16) | (bits[:, D_MODEL // 2:D_MODEL] & jnp.uint32(0xFFFF0000))
    r0 = first_tok * TOK_ROWS
    for s in range(TOK_ROWS):
        ref[pl.ds(r0 + s, n_tok, stride=TOK_ROWS), :] = w[:, s * LANES:(s + 1) * LANES]


IN_SPLITS = (QK_W, QK_W, V_W, D_MODEL, GLA_RANK, 2 * D_MODEL, 2 * D_MODEL)
WCAST_ROWS = 128


def _wcast_kernel(w_ref, *outs):
    w = w_ref[...]
    off = 0
    for out, n in zip(outs, IN_SPLITS):
        out[...] = w[:, off:off + n].astype(BF16)
        off += n


def _wcast(w):
    return pl.pallas_call(
        _wcast_kernel,
        grid=(D_MODEL // WCAST_ROWS,),
        in_specs=[pl.BlockSpec((WCAST_ROWS, sum(IN_SPLITS)), lambda i: (i, 0))],
        out_specs=[pl.BlockSpec((WCAST_ROWS, n), lambda i: (i, 0)) for n in IN_SPLITS],
        out_shape=[jax.ShapeDtypeStruct((D_MODEL, n), BF16) for n in IN_SPLITS],
        compiler_params=pltpu.CompilerParams(dimension_semantics=("arbitrary",),
                                             vmem_limit_bytes=VMEM_LIMIT),
        name="wcast",
    )(w)


def _inproj_kernel(front_ref, x_ref, g_ref, wq, wk, wv, wog, wdl, wglu, wgt, wdu, bd, bg,
                   q_o, k_o, v_o, og_o, la_o, u_o, gt_o):
    i = pl.program_id(0)
    h = jnp.where(i < FRONT // TM_PROJ, front_ref[...], x_ref[...])
    ms = jnp.mean(h * h, axis=-1, keepdims=True)
    hn = (h * lax.rsqrt(ms + EPS) * g_ref[...]).astype(BF16)
    q_o[...] = _dot(hn, wq[...]).astype(BF16)
    k_o[...] = _dot(hn, wk[...]).astype(BF16)
    v_o[...] = _dot(hn, wv[...]).astype(BF16)
    og_o[...] = _silu(_dot(hn, wog[...])).astype(BF16)
    glu = _dot(hn, wglu[...])
    _store_row_tiles(u_o, glu[:, 0:D_MODEL] * _sigmoid(glu[:, D_MODEL:2 * D_MODEL]), TM_PROJ)
    gt_o[...] = _sigmoid(_dot(hn, wgt[...]) + bg[...]).astype(BF16)
    d6 = _dot(hn, wdl[...])
    hi = d6.astype(BF16).astype(F32)
    mid = (d6 - hi).astype(BF16).astype(F32)
    lane = lax.broadcasted_iota(jnp.int32, d6.shape, 1)
    pieces = jnp.where(lane < 3 * GLA_RANK, hi, jnp.where(lane < 5 * GLA_RANK, mid, d6 - hi - mid))
    z = _dot(pieces.astype(BF16), wdu[...]) + bd[...]
    la_o[...] = (jnp.minimum(z, 0.0) - jnp.log(1.0 + jnp.exp(-jnp.abs(z)))) * (LOG2E / GLA_TAU)


def _inproj(front, x2d, g1, wq, wk, wv, wog, wdl, wglu, wgt, wdu, bd, bg):
    n_front = FRONT // TM_PROJ
    grid = (LP // TM_PROJ,)
    row = lambda w: pl.BlockSpec((TM_PROJ, w), lambda i: (i, 0))
    full = lambda a: pl.BlockSpec(a.shape, lambda i: (0, 0))
    return pl.pallas_call(
        _inproj_kernel,
        grid=grid,
        in_specs=[
            pl.BlockSpec((TM_PROJ, D_MODEL), lambda i: (jnp.minimum(i, n_front - 1), 0)),
            pl.BlockSpec((TM_PROJ, D_MODEL), lambda i: (jnp.maximum(i - n_front, 0), 0)),
            full(g1), full(wq), full(wk), full(wv), full(wog), full(wdl), full(wglu), full(wgt),
            full(wdu), full(bd), full(bg),
        ],
        out_specs=[row(QK_W), row(QK_W), row(V_W), row(D_MODEL), row(QK_W),
                   pl.BlockSpec((TM_PROJ * ROW_TILE, LANES), lambda i: (i, 0)), row(2 * D_MODEL)],
        out_shape=[
            jax.ShapeDtypeStruct((LP, QK_W), BF16), jax.ShapeDtypeStruct((LP, QK_W), BF16),
            jax.ShapeDtypeStruct((LP, V_W), BF16), jax.ShapeDtypeStruct((LP, D_MODEL), BF16),
            jax.ShapeDtypeStruct((LP, QK_W), F32), jax.ShapeDtypeStruct((LP * ROW_TILE, LANES), F32),
            jax.ShapeDtypeStruct((LP, 2 * D_MODEL), BF16),
        ],
        compiler_params=pltpu.CompilerParams(dimension_semantics=("arbitrary",),
                                             vmem_limit_bytes=VMEM_LIMIT),
        name="inproj",
    )(front, x2d, g1, wq, wk, wv, wog, wdl, wglu, wgt, wdu, bd, bg)


GLA_LEVELS = (32, 16, 8, 4, 2, 1)
GLA_FINE = tuple(m for m in GLA_LEVELS if m < SUBLANES)
LA_SPLIT = 3


def _gla_tables():
    r = np.arange(CHUNK)
    t, c = r[:, None], r[None, :]
    rows = [c <= t]
    pair = []
    for m in GLA_LEVELS:
        mid = (t // (2 * m)) * (2 * m) + m - 1
        right = (t % (2 * m)) >= m
        if m in GLA_FINE:
            rows.append(np.where(right, (c > mid) & (c <= t), (c > t) & (c <= mid)))
        pair.append(((t // (2 * m)) == (c // (2 * m))) & right & ((c % (2 * m)) < m))
    pair.append(t == c)
    sums = np.concatenate(rows, axis=0).astype(np.float32)
    sums = np.concatenate([sums] * LA_SPLIT, axis=1)
    pair = np.stack([np.kron(np.eye(GLA_HEADS), p) for p in pair]).astype(np.float32)
    return jnp.asarray(sums, BF16), jnp.asarray(pair)


def _split_bf16(x):
    pieces = []
    rest = x
    for _ in range(LA_SPLIT):
        p = rest.astype(BF16)
        pieces.append(p)
        rest = rest - p.astype(F32)
    return jnp.concatenate(pieces, axis=0)


def _dot_nt(a, b):
    return lax.dot_general(a, b, (((1,), (1,)), ((), ())), preferred_element_type=F32)


def _dot_tn(a, b):
    return lax.dot_general(a, b, (((0,), (0,)), ((), ())), preferred_element_type=F32)


def _gla_exponents(a):
    b = a[0:CHUNK]
    parts = [b, b[CHUNK - 1:CHUNK] - b]
    for m in GLA_LEVELS:
        if m in GLA_FINE:
            continue
        for lo in range(0, CHUNK, 2 * m):
            ref = b[lo + m - 1:lo + m]
            parts += [ref - b[lo:lo + m], b[lo + m:lo + 2 * m] - ref]
    return jnp.concatenate(parts + [a[CHUNK:]], axis=0)


def _gla_kernel(q_ref, k_ref, v_ref, la_ref, g_ref, sums_ref, pair_ref, o_ref, s_ref):
    @pl.when(pl.program_id(0) == 0)
    def _():
        s_ref[...] = jnp.zeros_like(s_ref)

    row = lax.broadcasted_iota(jnp.int32, (CHUNK, GLA_DK), 0)
    right_rows = [(row & m) != 0 for m in GLA_LEVELS]
    sums = sums_ref[...]
    n_lvl = len(GLA_LEVELS)

    def body(c, carry):
        r0 = pl.multiple_of(c * CHUNK, CHUNK)
        rows = pl.ds(r0, CHUNK)
        states = [s_ref[h] for h in range(GLA_HEADS)]
        qs, ks, es, decs = [], [], [], []
        for hp in range(GLA_HEADS // 2):
            la3 = _split_bf16(la_ref[rows, 2 * hp * GLA_DK:2 * (hp + 1) * GLA_DK])
            a2 = _dot(sums, la3)
            e2 = jnp.exp2(_gla_exponents(a2))
            dec2 = jnp.exp2(jnp.transpose(jnp.broadcast_to(a2[CHUNK - 1:CHUNK], (LANES, 2 * GLA_DK))))
            for h in (2 * hp, 2 * hp + 1):
                half = slice((h % 2) * GLA_DK, (h % 2 + 1) * GLA_DK)
                kc = slice(h * GLA_DK, (h + 1) * GLA_DK)
                qs.append(q_ref[rows, kc].astype(F32) * (GLA_DK ** -0.5))
                ks.append(k_ref[rows, kc].astype(F32))
                es.append(e2[:, half])
                decs.append(dec2[half])
        vbs = [v_ref[rows, h * GLA_DV:(h + 1) * GLA_DV] for h in range(GLA_HEADS)]

        new_states, qes = [], []
        for h in range(GLA_HEADS):
            qes.append((qs[h] * es[h][0:CHUNK]).astype(BF16))
            kd = (ks[h] * es[h][CHUNK:2 * CHUNK]).astype(BF16)
            new_states.append(states[h] * jnp.concatenate([decs[h]] * (GLA_DV // LANES), axis=1)
                              + _dot_tn(kd, vbs[h]))

        stack = lambda parts: jnp.concatenate(parts, axis=0)
        sc = _dot_nt(stack([q.astype(BF16) for q in qs]), stack([k.astype(BF16) for k in ks])) * pair_ref[n_lvl]
        for lvl in range(n_lvl):
            x = stack([(jnp.where(right_rows[lvl], qs[h], ks[h])
                        * es[h][(2 + lvl) * CHUNK:(3 + lvl) * CHUNK]).astype(BF16) for h in range(GLA_HEADS)])
            sc = sc + _dot_nt(x, x) * pair_ref[lvl]
        o_all = _dot(sc.astype(BF16), stack(vbs))
        for h in range(GLA_HEADS):
            vc = slice(h * GLA_DV, (h + 1) * GLA_DV)
            o = o_all[h * CHUNK:(h + 1) * CHUNK] + _dot(qes[h], states[h].astype(BF16))
            o = o * lax.rsqrt(jnp.mean(o * o, axis=-1, keepdims=True) + EPS) * g_ref[:, vc]
            o_ref[rows, vc] = o.astype(BF16)
        for h in range(GLA_HEADS):
            s_ref[h] = new_states[h]
        return carry

    lax.fori_loop(0, TM_GLA // CHUNK, body, 0, unroll=4)


def _gla(q, k, v, la, g):
    sums, pair = _gla_tables()
    row = lambda w: pl.BlockSpec((TM_GLA, w), lambda t: (t, 0))
    full = lambda a: pl.BlockSpec(a.shape, lambda t: (0,) * a.ndim)
    return pl.pallas_call(
        _gla_kernel,
        grid=(LP // TM_GLA,),
        in_specs=[row(QK_W), row(QK_W), row(V_W), row(QK_W), full(g), full(sums), full(pair)],
        out_specs=row(V_W),
        out_shape=jax.ShapeDtypeStruct((LP, V_W), BF16),
        scratch_shapes=[pltpu.VMEM((GLA_HEADS, GLA_DK, GLA_DV), F32)],
        compiler_params=pltpu.CompilerParams(dimension_semantics=("arbitrary",),
                                             vmem_limit_bytes=VMEM_LIMIT),
        name="gla",
    )(q, k, v, la, g, sums, pair)


def _mix_kernel(o_ref, og_ref, u_ref, hist_ref, gt_ref, x_ref, cw_ref, cb_ref, lg_ref, lb_ref,
                wpw_ref, wo_ref, g2_ref, wrh_ref, wrl_ref, br_ref,
                h_o, hn_o, ri_o, rw_o, tabn_o, tabb_o, cbuf, seen):
    @pl.when(pl.program_id(0) == 0)
    def _():
        seen[...] = jnp.zeros_like(seen)

    for j in range(MIX_TILES):
        r = slice(j * TM_MIX, (j + 1) * TM_MIX)
        tab = slice(j * SUBLANES, (j + 1) * SUBLANES)

        def u_row(t, j=j):
            t += j * TM_MIX
            ref, t = (hist_ref, t + HIST) if t < 0 else (u_ref, t)
            return ref[t * ROW_TILE:(t + 1) * ROW_TILE, :]

        _mix_tile(o_ref.at[r], og_ref.at[r], u_row, gt_ref.at[r], x_ref.at[r], cw_ref, cb_ref,
                  lg_ref, lb_ref, wpw_ref, wo_ref, g2_ref, wrh_ref, wrl_ref, br_ref,
                  h_o.at[r], hn_o.at[r], ri_o.at[:, r], rw_o.at[:, r], tabn_o.at[tab], tabb_o.at[tab],
                  cbuf, seen)


def _mix_tile(o_ref, og_ref, u_row, gt_ref, x_ref, cw_ref, cb_ref, lg_ref, lb_ref,
              wpw_ref, wo_ref, g2_ref, wrh_ref, wrl_ref, br_ref,
              h_o, hn_o, ri_o, rw_o, tabn_o, tabb_o, cbuf, seen):
    taps = [cw_ref[jj] for jj in range(CONV_WIDTH)]
    for t0 in range(0, TM_MIX, CONV_RB):
        part = [cb_ref[...]] * CONV_RB
        for rr in range(t0 - (CONV_WIDTH - 1), t0 + CONV_RB):
            ur = u_row(rr)
            for t in range(max(t0, rr), min(t0 + CONV_RB, rr + CONV_WIDTH)):
                part[t - t0] = part[t - t0] + taps[rr - t + CONV_WIDTH - 1] * ur
        for g in range(CONV_RB):
            cbuf[(t0 + g) * ROW_TILE:(t0 + g + 1) * ROW_TILE, :] = part[g]
    acc = _load_row_tiles(cbuf, TM_MIX)

    mu = jnp.mean(acc, axis=-1, keepdims=True)
    xc = acc - mu
    ln = xc * lax.rsqrt(jnp.mean(xc * xc, axis=-1, keepdims=True) + EPS) * lg_ref[...] + lb_ref[...]
    branch_b = _dot(_silu(ln).astype(BF16), wpw_ref[...])

    branch_a = o_ref[...].astype(F32) * og_ref[...].astype(F32)
    g_a = gt_ref[:, 0:D_MODEL].astype(F32)
    g_b = gt_ref[:, D_MODEL:2 * D_MODEL].astype(F32)
    merged = (g_a * branch_a + g_b * branch_b).astype(BF16)
    h1 = x_ref[...] + _dot(merged, wo_ref[...])
    h_o[...] = h1

    hn2 = h1 * lax.rsqrt(jnp.mean(h1 * h1, axis=-1, keepdims=True) + EPS) * g2_ref[...]
    hn_hi = hn2.astype(BF16)
    hn_o[...] = hn_hi

    hn_lo = (hn2 - hn_hi.astype(F32)).astype(BF16)
    logits = (_dot_nt(wrh_ref[...], hn_hi) + _dot_nt(wrh_ref[...], hn_lo) + _dot_nt(wrl_ref[...], hn_hi)
              + br_ref[...])
    row = lax.broadcasted_iota(jnp.int32, logits.shape, 0)
    rowf = row.astype(F32)
    neg = -jnp.inf
    is_g = row < N_GROUPS
    lg = jnp.where(is_g, logits, neg)
    gmax = jnp.max(lg, axis=0, keepdims=True)
    gidx = jnp.min(jnp.where(lg == gmax, rowf, float(N_GROUPS)), axis=0, keepdims=True)
    g_w = 1.0 / jnp.sum(jnp.where(is_g, jnp.exp(lg - gmax), 0.0), axis=0, keepdims=True)
    erow = rowf - float(N_GROUPS)
    egrp = ((row - N_GROUPS) >> 3).astype(F32)
    in_grp = (row >= N_GROUPS) & (row < N_GROUPS + N_EXPERTS) & (egrp == gidx)
    le = jnp.where(in_grp, logits, neg)
    m1 = jnp.max(le, axis=0, keepdims=True)
    i1 = jnp.min(jnp.where(le == m1, erow, float(N_EXPERTS)), axis=0, keepdims=True)
    le2 = jnp.where(erow == i1, neg, le)
    m2 = jnp.max(le2, axis=0, keepdims=True)
    i2 = jnp.min(jnp.where(le2 == m2, erow, float(N_EXPERTS)), axis=0, keepdims=True)
    t = jnp.exp(m2 - m1)
    w1 = g_w / (1.0 + t)
    w2 = g_w * t / (1.0 + t)

    def run_len(n):
        return jnp.floor((n + (TOK_ALIGN - 1)) * (1.0 / TOK_ALIGN)) * TOK_ALIGN

    oh1 = erow == i1
    oh2 = erow == i2
    oh = jnp.where(oh1 | oh2, 1.0, 0.0).astype(BF16)
    tr = lax.broadcasted_iota(jnp.int32, (TM_MIX, TM_MIX), 0)
    tc = lax.broadcasted_iota(jnp.int32, (TM_MIX, TM_MIX), 1)
    before_tok = _dot(oh, (tr < tc).astype(BF16))
    n_col = jnp.sum(oh.astype(F32), axis=1, keepdims=True)
    er = lax.broadcasted_iota(jnp.int32, (LANES, LANES), 0)
    ec = lax.broadcasted_iota(jnp.int32, (LANES, LANES), 1)
    before_exp = _dot((ec < er).astype(BF16), jnp.broadcast_to(run_len(n_col), logits.shape).astype(BF16))
    where_to = before_exp + before_tok
    q1 = jnp.sum(jnp.where(oh1, where_to, 0.0), axis=0, keepdims=True)
    q2 = jnp.sum(jnp.where(oh2, where_to, 0.0), axis=0, keepdims=True)

    n_rows = run_len(_dot_nt(jnp.ones((SUBLANES, TM_MIX), BF16), oh))
    tabn_o[...] = n_rows
    tabb_o[...] = seen[...]
    seen[...] = seen[...] + n_rows

    ri_o[...] = jnp.zeros_like(ri_o)
    ri_o[0:1, :] = q1.astype(jnp.int32)
    ri_o[1:2, :] = q2.astype(jnp.int32)
    rw_o[...] = jnp.zeros_like(rw_o)
    rw_o[0:1, :] = w1
    rw_o[1:2, :] = w2


def _mix(o_n, og, u, gates, x2d, cw, cb, lg, lb, wpw, wo, g2, wrh, wrl, br):
    tm = MIX_TILES * TM_MIX
    n_front = FRONT // tm
    grid = (SEQ // tm,)
    rowp = lambda w: pl.BlockSpec((tm, w), lambda i: (i + n_front, 0))
    full = lambda a: pl.BlockSpec(a.shape, lambda i: (0,) * a.ndim)
    hist_blocks = tm // HIST
    return pl.pallas_call(
        _mix_kernel,
        grid=grid,
        in_specs=[
            rowp(V_W), rowp(D_MODEL),
            pl.BlockSpec((tm * ROW_TILE, LANES), lambda i: (i + n_front, 0)),
            pl.BlockSpec((HIST * ROW_TILE, LANES), lambda i: ((i + n_front) * hist_blocks - 1, 0)),
            rowp(2 * D_MODEL),
            pl.BlockSpec((tm, D_MODEL), lambda i: (i, 0)),
            full(cw), full(cb), full(lg), full(lb), full(wpw), full(wo), full(g2),
            full(wrh), full(wrl), full(br),
        ],
        out_specs=[
            pl.BlockSpec((tm, D_MODEL), lambda i: (i, 0)),
            pl.BlockSpec((tm, D_MODEL), lambda i: (i, 0)),
            pl.BlockSpec((SUBLANES, tm), lambda i: (0, i)),
            pl.BlockSpec((SUBLANES, tm), lambda i: (0, i)),
            pl.BlockSpec((MIX_TILES * SUBLANES, LANES), lambda i: (i, 0)),
            pl.BlockSpec((MIX_TILES * SUBLANES, LANES), lambda i: (i, 0)),
        ],
        out_shape=[
            jax.ShapeDtypeStruct((SEQ, D_MODEL), F32),
            jax.ShapeDtypeStruct((SEQ, D_MODEL), BF16),
            jax.ShapeDtypeStruct((SUBLANES, SEQ), jnp.int32),
            jax.ShapeDtypeStruct((SUBLANES, SEQ), F32),
            jax.ShapeDtypeStruct((N_TILES * SUBLANES, LANES), F32),
            jax.ShapeDtypeStruct((N_TILES * SUBLANES, LANES), F32),
        ],
        scratch_shapes=[pltpu.VMEM((TM_MIX * ROW_TILE, LANES), F32), pltpu.VMEM((SUBLANES, LANES), F32)],
        compiler_params=pltpu.CompilerParams(dimension_semantics=("arbitrary",),
                                             vmem_limit_bytes=VMEM_LIMIT),
        name="mix",
    )(o_n, og, u, u, gates, x2d, cw, cb, lg, lb, wpw, wo, g2, wrh, wrl, br)


def _tok_rows(start_tok, n_tok):
    return pl.ds(pl.multiple_of(start_tok * TOK_ROWS, SUBLANES), n_tok * TOK_ROWS)


def _start_tile_runs(n_ref, row_ref, tile, make_copy):
    def run(e, off):
        n = n_ref[tile * N_EXPERTS + e]

        @pl.when(n > 0)
        def _():
            make_copy(_tok_rows(off, n), _tok_rows(row_ref[tile * N_EXPERTS + e], n)).start()

        return off + n

    lax.fori_loop(0, N_EXPERTS, run, 0, unroll=8)


def _dispatch_kernel(n_ref, row_ref, tot_ref, ts_ref, tl_ref, nused_ref, hn_ref, q_ref, xb_ref,
                     sb0, sb1, zbuf, sem, fsem):
    step = pl.program_id(0)
    last = pl.num_programs(0) - 1

    def fills(start):
        def tail(e, c):
            n = tl_ref[e]

            @pl.when(n > 0)
            def _():
                cp = pltpu.make_async_copy(zbuf.at[pl.ds(0, n * TOK_ROWS)],
                                           xb_ref.at[_tok_rows(ts_ref[e], n)], fsem)
                cp.start() if start else cp.wait()

            return c

        lax.fori_loop(0, N_EXPERTS, tail, 0)

        def block(b, c):
            cp = pltpu.make_async_copy(zbuf, xb_ref.at[_tok_rows(b * BM, BM)], fsem)
            cp.start() if start else cp.wait()
            return c

        lax.fori_loop(nused_ref[0], N_BLOCKS, block, 0)

    @pl.when(step == 0)
    def _():
        zbuf[...] = jnp.zeros_like(zbuf)
        fills(start=True)

    def tile_runs(tile, s, sbuf):
        rows = pl.ds(0, tot_ref[tile] * TOK_ROWS)
        return pltpu.make_async_copy(sbuf.at[rows], xb_ref.at[rows], sem.at[s])

    for s, sbuf in enumerate((sb0, sb1)):
        tile = step * TILES_PER_STEP + s
        toks = slice(s * TM_MIX, (s + 1) * TM_MIX)
        slot_i = lax.broadcasted_iota(jnp.int32, (SLOTS, TM_MIX), 0)
        onehot = jnp.where((slot_i == q_ref[0:1, toks]) | (slot_i == q_ref[1:2, toks]), 1.0, 0.0).astype(BF16)
        srt = _dot(onehot, hn_ref[toks, :])

        @pl.when(step >= 1)
        def _():
            tile_runs(tile - TILES_PER_STEP, s, sbuf).wait()

        _store_token_tiles(sbuf, srt, SLOTS)
        _start_tile_runs(n_ref, row_ref, tile,
                         lambda loc, glob: pltpu.make_async_copy(sbuf.at[loc], xb_ref.at[glob], sem.at[s]))

    @pl.when(step == last)
    def _():
        for s, sbuf in enumerate((sb0, sb1)):
            tile_runs(step * TILES_PER_STEP + s, s, sbuf).wait()
        fills(start=False)


def _dispatch(n_flat, run_rows, tot, tail_start, tail_len, nused, hn2, ri):
    grid_spec = pltpu.PrefetchScalarGridSpec(
        num_scalar_prefetch=6,
        grid=(N_TILES // TILES_PER_STEP,),
        in_specs=[pl.BlockSpec((TILES_PER_STEP * TM_MIX, D_MODEL), lambda i, *_: (i, 0)),
                  pl.BlockSpec((SUBLANES, TILES_PER_STEP * TM_MIX), lambda i, *_: (0, i))],
        out_specs=pl.BlockSpec(memory_space=pl.ANY),
        scratch_shapes=[pltpu.VMEM((SLOTS * TOK_ROWS, LANES), jnp.uint32),
                        pltpu.VMEM((SLOTS * TOK_ROWS, LANES), jnp.uint32),
                        pltpu.VMEM((BM * TOK_ROWS, LANES), jnp.uint32),
                        pltpu.SemaphoreType.DMA((2,)), pltpu.SemaphoreType.DMA(())],
    )
    return pl.pallas_call(
        _dispatch_kernel,
        grid_spec=grid_spec,
        out_shape=jax.ShapeDtypeStruct((P_ROWS * TOK_ROWS, LANES), jnp.uint32),
        compiler_params=pltpu.CompilerParams(dimension_semantics=("arbitrary",),
                                             vmem_limit_bytes=VMEM_LIMIT),
        name="dispatch",
    )(n_flat, run_rows, tot, tail_start, tail_len, nused, hn2, ri)


def _experts_kernel(be_ref, first_ref, next_ref, next2_ref, stage_ref, nused_ref,
                    x_ref, wg_hbm, wu_hbm, wd_hbm, y_ref, wg_f, wu_f, wd_f, wg_b, wu_b, wd_b, sem):
    def weight_copies(e, st):
        return (pltpu.make_async_copy(wg_hbm.at[e], wg_f.at[st], sem.at[st, 0]),
                pltpu.make_async_copy(wu_hbm.at[e], wu_f.at[st], sem.at[st, 1]),
                pltpu.make_async_copy(wd_hbm.at[e], wd_f.at[st], sem.at[st, 2]))

    @pl.when((pl.program_id(0) == 0) & (nused_ref[0] > 0))
    def _():
        for c in weight_copies(be_ref[0], 0):
            c.start()

        @pl.when(next_ref[0] >= 0)
        def _():
            for c in weight_copies(next_ref[0], 1):
                c.start()

    for j in range(BLOCKS_PER_STEP):
        _expert_block(pl.program_id(0) * BLOCKS_PER_STEP + j, j * BM, be_ref, first_ref, next2_ref,
                      stage_ref, nused_ref, x_ref, y_ref, wg_f, wu_f, wd_f, wg_b, wu_b, wd_b, weight_copies)


def _expert_block(b, tok0, be_ref, first_ref, next2_ref, stage_ref, nused_ref, x_ref, y_ref,
                  wg_f, wu_f, wd_f, wg_b, wu_b, wd_b, weight_copies):
    @pl.when(b < nused_ref[0])
    def _():
        def new_expert(st):
            for c in weight_copies(be_ref[b], st):
                c.wait()
            wg_b[...] = wg_f[st].astype(BF16)
            wu_b[...] = wu_f[st].astype(BF16)
            wd_b[...] = wd_f[st].astype(BF16)

            @pl.when(next2_ref[b] >= 0)
            def _():
                for c in weight_copies(next2_ref[b], st):
                    c.start()

        for st in range(2):
            pl.when((first_ref[b] == 1) & (stage_ref[b] == st))(functools.partial(new_expert, st))

        x = _load_token_tiles(x_ref, BM, tok0)
        a = _dot(x, wg_b[...])
        u = _dot(x, wu_b[...])
        y = _dot((_silu(a) * u).astype(BF16), wd_b[...])
        _store_token_tiles(y_ref, y.astype(BF16).astype(F32), BM, tok0)

    @pl.when(b >= nused_ref[0])
    def _():
        y_ref[tok0 * TOK_ROWS:(tok0 + BM) * TOK_ROWS, :] = jnp.zeros((BM * TOK_ROWS, LANES), jnp.uint32)


def _experts(block_e, first, next_e, next2_e, stage, nused, xb, wg, wu, wd):
    def xmap(b, *_):
        return (b, 0)

    grid_spec = pltpu.PrefetchScalarGridSpec(
        num_scalar_prefetch=6,
        grid=(N_BLOCKS // BLOCKS_PER_STEP,),
        in_specs=[
            pl.BlockSpec((BLOCKS_PER_STEP * BM * TOK_ROWS, LANES), xmap),
            pl.BlockSpec(memory_space=pl.ANY),
            pl.BlockSpec(memory_space=pl.ANY),
            pl.BlockSpec(memory_space=pl.ANY),
        ],
        out_specs=pl.BlockSpec((BLOCKS_PER_STEP * BM * TOK_ROWS, LANES), xmap),
        scratch_shapes=[pltpu.VMEM((2, D_MODEL, D_EXPERT), F32), pltpu.VMEM((2, D_MODEL, D_EXPERT), F32),
                        pltpu.VMEM((2, D_EXPERT, D_MODEL), F32),
                        pltpu.VMEM((D_MODEL, D_EXPERT), BF16), pltpu.VMEM((D_MODEL, D_EXPERT), BF16),
                        pltpu.VMEM((D_EXPERT, D_MODEL), BF16), pltpu.SemaphoreType.DMA((2, 3))],
    )
    return pl.pallas_call(
        _experts_kernel,
        grid_spec=grid_spec,
        out_shape=jax.ShapeDtypeStruct((P_ROWS * TOK_ROWS, LANES), jnp.uint32),
        compiler_params=pltpu.CompilerParams(dimension_semantics=("arbitrary",),
                                             vmem_limit_bytes=VMEM_LIMIT),
        name="experts",
    )(block_e, first, next_e, next2_e, stage, nused, xb, wg, wu, wd)


def _combine_kernel(n_ref, row_ref, tot_ref, yb_ref, h_ref, q_ref, w_ref, g_ref, out_ref, *scratch):
    step = pl.program_id(0)
    bufs, sem = scratch[:COMBINE_TILES], scratch[COMBINE_TILES]

    def start_gather(tile, s):
        _start_tile_runs(n_ref, row_ref, tile,
                         lambda loc, glob: pltpu.make_async_copy(yb_ref.at[glob], bufs[s].at[loc], sem.at[s]))

    @pl.when(step == 0)
    def _():
        for buf in bufs:
            buf[...] = jnp.zeros_like(buf)
        for s in range(GATHER_AHEAD):
            start_gather(s, s)

    for s in range(COMBINE_TILES):
        tile = step * COMBINE_TILES + s
        ahead = s + GATHER_AHEAD
        if ahead < COMBINE_TILES:
            start_gather(tile + GATHER_AHEAD, ahead)
        else:
            pl.when(step + 1 < pl.num_programs(0))(
                functools.partial(start_gather, tile + GATHER_AHEAD, ahead - COMBINE_TILES))

        rows = pl.ds(0, tot_ref[tile] * TOK_ROWS)
        pltpu.make_async_copy(yb_ref.at[rows], bufs[s].at[rows], sem.at[s]).wait()
        ys = _load_token_tiles(bufs[s], SLOTS)
        toks = slice(s * TM_MIX, (s + 1) * TM_MIX)
        col = lax.broadcasted_iota(jnp.int32, (TM_MIX, SLOTS), 1)
        wmat = (jnp.where(col == q_ref[toks, 0:1], w_ref[toks, 0:1], 0.0)
                + jnp.where(col == q_ref[toks, 1:2], w_ref[toks, 1:2], 0.0)).astype(BF16)
        hh = h_ref[toks, :] + _dot(wmat, ys)
        out_ref[toks, :] = hh * lax.rsqrt(jnp.mean(hh * hh, axis=-1, keepdims=True) + EPS) * g_ref[...]


def _combine(n_flat, run_rows, tot, yb, h1, qt, wts, gf):
    grid_spec = pltpu.PrefetchScalarGridSpec(
        num_scalar_prefetch=3,
        grid=(N_TILES // COMBINE_TILES,),
        in_specs=[
            pl.BlockSpec(memory_space=pl.ANY),
            pl.BlockSpec((COMBINE_TILES * TM_MIX, D_MODEL), lambda i, *_: (i, 0)),
            pl.BlockSpec((COMBINE_TILES * TM_MIX, TOP_K), lambda i, *_: (i, 0)),
            pl.BlockSpec((COMBINE_TILES * TM_MIX, TOP_K), lambda i, *_: (i, 0)),
            pl.BlockSpec((1, D_MODEL), lambda i, *_: (0, 0)),
        ],
        out_specs=pl.BlockSpec((COMBINE_TILES * TM_MIX, D_MODEL), lambda i, *_: (i, 0)),
        scratch_shapes=[pltpu.VMEM((SLOTS * TOK_ROWS, LANES), jnp.uint32)] * COMBINE_TILES
                       + [pltpu.SemaphoreType.DMA((COMBINE_TILES,))],
    )
    return pl.pallas_call(
        _combine_kernel,
        grid_spec=grid_spec,
        out_shape=jax.ShapeDtypeStruct((SEQ, D_MODEL), F32),
        compiler_params=pltpu.CompilerParams(dimension_semantics=("arbitrary",),
                                             vmem_limit_bytes=VMEM_LIMIT),
        name="combine",
    )(n_flat, run_rows, tot, yb, h1, qt, wts, gf)


def kernel(x, meta, norm1_g, w_in, w_decay_up, b_decay, gla_norm_g, conv_w, conv_b, conv_ln_g,
           conv_ln_b, w_pw2, b_gate, w_out, norm2_g, w_router_group, b_router_group,
           w_router_expert, b_router_expert, w_exp_gate, w_exp_up, w_exp_down, final_norm_g):
    assert x.shape == (1, SEQ, D_MODEL) and w_in.shape[0] == 1
    x2d = x[0]
    front = jnp.concatenate([jnp.zeros((FRONT - N_META, D_MODEL), F32), meta.astype(F32)], axis=0)

    w = w_in[0]
    row2 = lambda a: a.reshape(1, -1).astype(F32)
    wq, wk, wv, wog, wdl, wglu, wgt = _wcast(w)
    up = w_decay_up[0].astype(F32)
    up_hi = up.astype(BF16)
    up_mid = (up - up_hi.astype(F32)).astype(BF16)
    up_lo = (up - up_hi.astype(F32) - up_mid.astype(F32)).astype(BF16)
    wdu = jnp.concatenate([up_hi, up_mid, up_lo, up_hi, up_mid, up_hi], axis=0)
    q, k, v, og, la, u, gates = _inproj(
        front, x2d, row2(norm1_g[0]), wq, wk, wv, wog, jnp.tile(wdl, (1, DL_COPIES)), wglu, wgt,
        wdu, row2(b_decay[0]), row2(b_gate[0]))

    o_n = _gla(q, k, v, la, row2(gla_norm_g[0]))

    wr = jnp.concatenate([w_router_group[0].T, w_router_expert[0].T,
                          jnp.zeros((LANES - N_GROUPS - N_EXPERTS, D_MODEL), F32)], axis=0).astype(F32)
    br = jnp.concatenate([b_router_group[0], b_router_expert[0],
                          jnp.zeros((LANES - N_GROUPS - N_EXPERTS,), F32)]).reshape(LANES, 1).astype(F32)
    wr_hi = wr.astype(BF16)
    wr_lo = (wr - wr_hi.astype(F32)).astype(BF16)
    h1, hn2, ri, rw, tabn, tabb = _mix(
        o_n, og, u, gates, x2d, conv_w[0].astype(F32).reshape(CONV_WIDTH, ROW_TILE, LANES),
        conv_b[0].astype(F32).reshape(ROW_TILE, LANES), row2(conv_ln_g[0]),
        row2(conv_ln_b[0]), w_pw2[0].astype(BF16), w_out[0].astype(BF16),
        row2(norm2_g[0]), wr_hi, wr_lo, br)

    experts = slice(N_GROUPS, N_GROUPS + N_EXPERTS)
    n_te = tabn[::SUBLANES, experts].astype(jnp.int32)
    seen_te = tabb[::SUBLANES, experts].astype(jnp.int32)
    counts = seen_te[-1] + n_te[-1]
    padded = (counts + BM - 1) // BM * BM
    pad_end = jnp.cumsum(padded)
    pad_start = pad_end - padded
    n_flat = n_te.reshape(-1)
    tot = jnp.sum(n_te, axis=1).astype(jnp.int32)
    run_rows = (pad_start[None, :] + seen_te).reshape(-1).astype(jnp.int32)
    tail_start = (pad_start + counts).astype(jnp.int32)
    tail_len = (padded - counts).astype(jnp.int32)
    blk = jnp.arange(N_BLOCKS, dtype=jnp.int32)
    block_e = jnp.minimum(jnp.sum((pad_end[None, :] <= blk[:, None] * BM).astype(jnp.int32), axis=1),
                          N_EXPERTS - 1).astype(jnp.int32)
    first = jnp.concatenate([jnp.ones((1,), jnp.int32), (block_e[1:] != block_e[:-1]).astype(jnp.int32)])
    nused = (pad_end[-1:] // BM).astype(jnp.int32)
    eid = jnp.arange(N_EXPERTS, dtype=jnp.int32)
    later = jnp.flip(lax.cummin(jnp.flip(jnp.where(counts > 0, eid, N_EXPERTS))))
    nxt = jnp.concatenate([later[1:], jnp.full((2,), N_EXPERTS, jnp.int32)])
    nxt2 = nxt[nxt[:N_EXPERTS]]
    or_none = lambda t: jnp.where(t < N_EXPERTS, t, -1)[block_e].astype(jnp.int32)
    next_e, next2_e = or_none(nxt[:N_EXPERTS]), or_none(nxt2)
    stage = ((jnp.cumsum(first) - 1) % 2).astype(jnp.int32)

    xb = _dispatch(n_flat, run_rows, tot, tail_start, tail_len, nused, hn2, ri)
    yb = _experts(block_e, first, next_e, next2_e, stage, nused, xb, w_exp_gate[0], w_exp_up[0], w_exp_down[0])
    out = _combine(n_flat, run_rows, tot, yb, h1, ri[0:TOP_K].T, rw[0:TOP_K].T, row2(final_norm_g))
    return out[None]
```

```python
import functools

import jax
import jax.numpy as jnp
import numpy as np
from jax import lax
from jax.experimental import pallas as pl
from jax.experimental.pallas import tpu as pltpu

F32 = jnp.float32
BF16 = jnp.bfloat16

D_MODEL = 1024
SEQ = 16384
N_META = 16
GLA_HEADS = 4
GLA_DK = 128
GLA_DV = 256
GLA_RANK = 16
GLA_TAU = 16.0
CHUNK = 64
QK_W = GLA_HEADS * GLA_DK
V_W = GLA_HEADS * GLA_DV
CONV_WIDTH = 31
N_GROUPS = 8
EXPERTS_PER_GROUP = 8
N_EXPERTS = N_GROUPS * EXPERTS_PER_GROUP
TOP_K = 2
D_EXPERT = 512
EPS = 1e-6
LOG2E = 1.4426950408889634

LANES = 128
SUBLANES = 8
ROW_TILE = D_MODEL // LANES
TOK_ROWS = D_MODEL // 2 // LANES
TOK_ALIGN = SUBLANES // TOK_ROWS

FRONT = 512
LP = FRONT + SEQ
TM_PROJ = 512
DL_COPIES = 6
TM_GLA = 512
TM_MIX = 256
MIX_TILES = 2
HIST = 32
CONV_RB = 16
N_TILES = SEQ // TM_MIX
SLOTS = TOP_K * TM_MIX + N_EXPERTS * (TOK_ALIGN - 1)
TILES_PER_STEP = 2
COMBINE_TILES = 4
GATHER_AHEAD = 2
BM = 256
BLOCKS_PER_STEP = 4
N_BLOCKS = (N_TILES * SLOTS) // BM + N_EXPERTS
P_ROWS = N_BLOCKS * BM
VMEM_LIMIT = 56 * 1024 * 1024


def _sigmoid(x):
    return 0.5 * jnp.tanh(0.5 * x) + 0.5


def _silu(x):
    return x * _sigmoid(x)


def _dot(a, b, **kw):
    return jnp.dot(a, b, preferred_element_type=F32, **kw)


def _load_row_tiles(ref, n_rows):
    return jnp.concatenate([ref[pl.ds(s, n_rows, stride=ROW_TILE), :] for s in range(ROW_TILE)], axis=-1)


def _store_row_tiles(ref, val, n_rows):
    for s in range(ROW_TILE):
        ref[pl.ds(s, n_rows, stride=ROW_TILE), :] = val[:, s * LANES:(s + 1) * LANES]


def _load_token_tiles(ref, n_tok, first_tok=0):
    r0 = first_tok * TOK_ROWS
    w = jnp.concatenate([ref[pl.ds(r0 + s, n_tok, stride=TOK_ROWS), :] for s in range(TOK_ROWS)], axis=-1)
    lo = pltpu.bitcast(w << 16, F32)
    hi = pltpu.bitcast(w & jnp.uint32(0xFFFF0000), F32)
    return jnp.concatenate([lo, hi], axis=-1).astype(BF16)


def _store_token_tiles(ref, val, n_tok, first_tok=0):
    bits = pltpu.bitcast(val, jnp.uint32)
    w = (bits[:, 0:D_MODEL // 2] >> 16) | (bits[:, D_MODEL // 2:D_MODEL] & jnp.uint32(0xFFFF0000))
    r0 = first_tok * TOK_ROWS
    for s in range(TOK_ROWS):
        ref[pl.ds(r0 + s, n_tok, stride=TOK_ROWS), :] = w[:, s * LANES:(s + 1) * LANES]


IN_SPLITS = (QK_W, QK_W, V_W, D_MODEL, GLA_RANK, 2 * D_MODEL, 2 * D_MODEL)
WCAST_ROWS = 128


def _wcast_kernel(w_ref, *outs):
    w = w_ref[...]
    off = 0
    for out, n in zip(outs, IN_SPLITS):
        out[...] = w[:, off:off + n].astype(BF16)
        off += n


def _wcast(w):
    return pl.pallas_call(
        _wcast_kernel,
        grid=(D_MODEL // WCAST_ROWS,),
        in_specs=[pl.BlockSpec((WCAST_ROWS, sum(IN_SPLITS)), lambda i: (i, 0))],
        out_specs=[pl.BlockSpec((WCAST_ROWS, n), lambda i: (i, 0)) for n in IN_SPLITS],
        out_shape=[jax.ShapeDtypeStruct((D_MODEL, n), BF16) for n in IN_SPLITS],
        compiler_params=pltpu.CompilerParams(dimension_semantics=("arbitrary",),
                                             vmem_limit_bytes=VMEM_LIMIT),
        name="wcast",
    )(w)


def _inproj_kernel(front_ref, x_ref, g_ref, wq, wk, wv, wog, wdl, wglu, wgt, wdu, bd, bg,
                   q_o, k_o, v_o, og_o, la_o, u_o, gt_o):
    i = pl.program_id(0)
    h = jnp.where(i < FRONT // TM_PROJ, front_ref[...], x_ref[...])
    ms = jnp.mean(h * h, axis=-1, keepdims=True)
    hn = (h * lax.rsqrt(ms + EPS) * g_ref[...]).astype(BF16)
    q_o[...] = _dot(hn, wq[...]).astype(BF16)
    k_o[...] = _dot(hn, wk[...]).astype(BF16)
    v_o[...] = _dot(hn, wv[...]).astype(BF16)
    og_o[...] = _silu(_dot(hn, wog[...])).astype(BF16)
    glu = _dot(hn, wglu[...])
    _store_row_tiles(u_o, glu[:, 0:D_MODEL] * _sigmoid(glu[:, D_MODEL:2 * D_MODEL]), TM_PROJ)
    gt_o[...] = _sigmoid(_dot(hn, wgt[...]) + bg[...]).astype(BF16)
    d6 = _dot(hn, wdl[...])
    hi = d6.astype(BF16).astype(F32)
    mid = (d6 - hi).astype(BF16).astype(F32)
    lane = lax.broadcasted_iota(jnp.int32, d6.shape, 1)
    pieces = jnp.where(lane < 3 * GLA_RANK, hi, jnp.where(lane < 5 * GLA_RANK, mid, d6 - hi - mid))
    z = _dot(pieces.astype(BF16), wdu[...]) + bd[...]
    la_o[...] = (jnp.minimum(z, 0.0) - jnp.log(1.0 + jnp.exp(-jnp.abs(z)))) * (LOG2E / GLA_TAU)


def _inproj(front, x2d, g1, wq, wk, wv, wog, wdl, wglu, wgt, wdu, bd, bg):
    n_front = FRONT // TM_PROJ
    grid = (LP // TM_PROJ,)
    row = lambda w: pl.BlockSpec((TM_PROJ, w), lambda i: (i, 0))
    full = lambda a: pl.BlockSpec(a.shape, lambda i: (0, 0))
    return pl.pallas_call(
        _inproj_kernel,
        grid=grid,
        in_specs=[
            pl.BlockSpec((TM_PROJ, D_MODEL), lambda i: (jnp.minimum(i, n_front - 1), 0)),
            pl.BlockSpec((TM_PROJ, D_MODEL), lambda i: (jnp.maximum(i - n_front, 0), 0)),
            full(g1), full(wq), full(wk), full(wv), full(wog), full(wdl), full(wglu), full(wgt),
            full(wdu), full(bd), full(bg),
        ],
        out_specs=[row(QK_W), row(QK_W), row(V_W), row(D_MODEL), row(QK_W),
                   pl.BlockSpec((TM_PROJ * ROW_TILE, LANES), lambda i: (i, 0)), row(2 * D_MODEL)],
        out_shape=[
            jax.ShapeDtypeStruct((LP, QK_W), BF16), jax.ShapeDtypeStruct((LP, QK_W), BF16),
            jax.ShapeDtypeStruct((LP, V_W), BF16), jax.ShapeDtypeStruct((LP, D_MODEL), BF16),
            jax.ShapeDtypeStruct((LP, QK_W), F32), jax.ShapeDtypeStruct((LP * ROW_TILE, LANES), F32),
            jax.ShapeDtypeStruct((LP, 2 * D_MODEL), BF16),
        ],
        compiler_params=pltpu.CompilerParams(dimension_semantics=("arbitrary",),
                                             vmem_limit_bytes=VMEM_LIMIT),
        name="inproj",
    )(front, x2d, g1, wq, wk, wv, wog, wdl, wglu, wgt, wdu, bd, bg)


GLA_LEVELS = (32, 16, 8, 4, 2, 1)
GLA_FINE = tuple(m for m in GLA_LEVELS if m < SUBLANES)
LA_SPLIT = 3


def _gla_tables():
    r = np.arange(CHUNK)
    t, c = r[:, None], r[None, :]
    rows = [c <= t]
    pair = []
    for m in GLA_LEVELS:
        mid = (t // (2 * m)) * (2 * m) + m - 1
        right = (t % (2 * m)) >= m
        if m in GLA_FINE:
            rows.append(np.where(right, (c > mid) & (c <= t), (c > t) & (c <= mid)))
        pair.append(((t // (2 * m)) == (c // (2 * m))) & right & ((c % (2 * m)) < m))
    pair.append(t == c)
    sums = np.concatenate(rows, axis=0).astype(np.float32)
    sums = np.concatenate([sums] * LA_SPLIT, axis=1)
    pair = np.stack([np.kron(np.eye(GLA_HEADS), p) for p in pair]).astype(np.float32)
    return jnp.asarray(sums, BF16), jnp.asarray(pair)


def _split_bf16(x):
    pieces = []
    rest = x
    for _ in range(LA_SPLIT):
        p = rest.astype(BF16)
        pieces.append(p)
        rest = rest - p.astype(F32)
    return jnp.concatenate(pieces, axis=0)


def _dot_nt(a, b):
    return lax.dot_general(a, b, (((1,), (1,)), ((), ())), preferred_element_type=F32)


def _dot_tn(a, b):
    return lax.dot_general(a, b, (((0,), (0,)), ((), ())), preferred_element_type=F32)


def _gla_exponents(a):
    b = a[0:CHUNK]
    parts = [b, b[CHUNK - 1:CHUNK] - b]
    for m in GLA_LEVELS:
        if m in GLA_FINE:
            continue
        for lo in range(0, CHUNK, 2 * m):
            ref = b[lo + m - 1:lo + m]
            parts += [ref - b[lo:lo + m], b[lo + m:lo + 2 * m] - ref]
    return jnp.concatenate(parts + [a[CHUNK:]], axis=0)


def _gla_kernel(q_ref, k_ref, v_ref, la_ref, g_ref, sums_ref, pair_ref, o_ref, s_ref):
    @pl.when(pl.program_id(0) == 0)
    def _():
        s_ref[...] = jnp.zeros_like(s_ref)

    row = lax.broadcasted_iota(jnp.int32, (CHUNK, GLA_DK), 0)
    right_rows = [(row & m) != 0 for m in GLA_LEVELS]
    sums = sums_ref[...]
    n_lvl = len(GLA_LEVELS)

    def body(c, carry):
        r0 = pl.multiple_of(c * CHUNK, CHUNK)
        rows = pl.ds(r0, CHUNK)
        states = [s_ref[h] for h in range(GLA_HEADS)]
        qs, ks, es, decs = [], [], [], []
        for hp in range(GLA_HEADS // 2):
            la3 = _split_bf16(la_ref[rows, 2 * hp * GLA_DK:2 * (hp + 1) * GLA_DK])
            a2 = _dot(sums, la3)
            e2 = jnp.exp2(_gla_exponents(a2))
            dec2 = jnp.exp2(jnp.transpose(jnp.broadcast_to(a2[CHUNK - 1:CHUNK], (LANES, 2 * GLA_DK))))
            for h in (2 * hp, 2 * hp + 1):
                half = slice((h % 2) * GLA_DK, (h % 2 + 1) * GLA_DK)
                kc = slice(h * GLA_DK, (h + 1) * GLA_DK)
                qs.append(q_ref[rows, kc].astype(F32) * (GLA_DK ** -0.5))
                ks.append(k_ref[rows, kc].astype(F32))
                es.append(e2[:, half])
                decs.append(dec2[half])
        vbs = [v_ref[rows, h * GLA_DV:(h + 1) * GLA_DV] for h in range(GLA_HEADS)]

        new_states, qes = [], []
        for h in range(GLA_HEADS):
            qes.append((qs[h] * es[h][0:CHUNK]).astype(BF16))
            kd = (ks[h] * es[h][CHUNK:2 * CHUNK]).astype(BF16)
            new_states.append(states[h] * jnp.concatenate([decs[h]] * (GLA_DV // LANES), axis=1)
                              + _dot_tn(kd, vbs[h]))

        stack = lambda parts: jnp.concatenate(parts, axis=0)
        sc = _dot_nt(stack([q.astype(BF16) for q in qs]), stack([k.astype(BF16) for k in ks])) * pair_ref[n_lvl]
        for lvl in range(n_lvl):
            x = stack([(jnp.where(right_rows[lvl], qs[h], ks[h])
                        * es[h][(2 + lvl) * CHUNK:(3 + lvl) * CHUNK]).astype(BF16) for h in range(GLA_HEADS)])
            sc = sc + _dot_nt(x, x) * pair_ref[lvl]
        o_all = _dot(sc.astype(BF16), stack(vbs))
        for h in range(GLA_HEADS):
            vc = slice(h * GLA_DV, (h + 1) * GLA_DV)
            o = o_all[h * CHUNK:(h + 1) * CHUNK] + _dot(qes[h], states[h].astype(BF16))
            o = o * lax.rsqrt(jnp.mean(o * o, axis=-1, keepdims=True) + EPS) * g_ref[:, vc]
            o_ref[rows, vc] = o.astype(BF16)
        for h in range(GLA_HEADS):
            s_ref[h] = new_states[h]
        return carry

    lax.fori_loop(0, TM_GLA // CHUNK, body, 0, unroll=4)


def _gla(q, k, v, la, g):
    sums, pair = _gla_tables()
    row = lambda w: pl.BlockSpec((TM_GLA, w), lambda t: (t, 0))
    full = lambda a: pl.BlockSpec(a.shape, lambda t: (0,) * a.ndim)
    return pl.pallas_call(
        _gla_kernel,
        grid=(LP // TM_GLA,),
        in_specs=[row(QK_W), row(QK_W), row(V_W), row(QK_W), full(g), full(sums), full(pair)],
        out_specs=row(V_W),
        out_shape=jax.ShapeDtypeStruct((LP, V_W), BF16),
        scratch_shapes=[pltpu.VMEM((GLA_HEADS, GLA_DK, GLA_DV), F32)],
        compiler_params=pltpu.CompilerParams(dimension_semantics=("arbitrary",),
                                             vmem_limit_bytes=VMEM_LIMIT),
        name="gla",
    )(q, k, v, la, g, sums, pair)


def _mix_kernel(o_ref, og_ref, u_ref, hist_ref, gt_ref, x_ref, cw_ref, cb_ref, lg_ref, lb_ref,
                wpw_ref, wo_ref, g2_ref, wrh_ref, wrl_ref, br_ref,
                h_o, hn_o, ri_o, rw_o, tabn_o, tabb_o, cbuf, seen):
    @pl.when(pl.program_id(0) == 0)
    def _():
        seen[...] = jnp.zeros_like(seen)

    for j in range(MIX_TILES):
        r = slice(j * TM_MIX, (j + 1) * TM_MIX)
        tab = slice(j * SUBLANES, (j + 1) * SUBLANES)

        def u_row(t, j=j):
            t += j * TM_MIX
            ref, t = (hist_ref, t + HIST) if t < 0 else (u_ref, t)
            return ref[t * ROW_TILE:(t + 1) * ROW_TILE, :]

        _mix_tile(o_ref.at[r], og_ref.at[r], u_row, gt_ref.at[r], x_ref.at[r], cw_ref, cb_ref,
                  lg_ref, lb_ref, wpw_ref, wo_ref, g2_ref, wrh_ref, wrl_ref, br_ref,
                  h_o.at[r], hn_o.at[r], ri_o.at[:, r], rw_o.at[:, r], tabn_o.at[tab], tabb_o.at[tab],
                  cbuf, seen)


def _mix_tile(o_ref, og_ref, u_row, gt_ref, x_ref, cw_ref, cb_ref, lg_ref, lb_ref,
              wpw_ref, wo_ref, g2_ref, wrh_ref, wrl_ref, br_ref,
              h_o, hn_o, ri_o, rw_o, tabn_o, tabb_o, cbuf, seen):
    taps = [cw_ref[jj] for jj in range(CONV_WIDTH)]
    for t0 in range(0, TM_MIX, CONV_RB):
        part = [cb_ref[...]] * CONV_RB
        for rr in range(t0 - (CONV_WIDTH - 1), t0 + CONV_RB):
            ur = u_row(rr)
            for t in range(max(t0, rr), min(t0 + CONV_RB, rr + CONV_WIDTH)):
                part[t - t0] = part[t - t0] + taps[rr - t + CONV_WIDTH - 1] * ur
        for g in range(CONV_RB):
            cbuf[(t0 + g) * ROW_TILE:(t0 + g + 1) * ROW_TILE, :] = part[g]
    acc = _load_row_tiles(cbuf, TM_MIX)

    mu = jnp.mean(acc, axis=-1, keepdims=True)
    xc = acc - mu
    ln = xc * lax.rsqrt(jnp.mean(xc * xc, axis=-1, keepdims=True) + EPS) * lg_ref[...] + lb_ref[...]
    branch_b = _dot(_silu(ln).astype(BF16), wpw_ref[...])

    branch_a = o_ref[...].astype(F32) * og_ref[...].astype(F32)
    g_a = gt_ref[:, 0:D_MODEL].astype(F32)
    g_b = gt_ref[:, D_MODEL:2 * D_MODEL].astype(F32)
    merged = (g_a * branch_a + g_b * branch_b).astype(BF16)
    h1 = x_ref[...] + _dot(merged, wo_ref[...])
    h_o[...] = h1

    hn2 = h1 * lax.rsqrt(jnp.mean(h1 * h1, axis=-1, keepdims=True) + EPS) * g2_ref[...]
    hn_hi = hn2.astype(BF16)
    hn_o[...] = hn_hi

    hn_lo = (hn2 - hn_hi.astype(F32)).astype(BF16)
    logits = (_dot_nt(wrh_ref[...], hn_hi) + _dot_nt(wrh_ref[...], hn_lo) + _dot_nt(wrl_ref[...], hn_hi)
              + br_ref[...])
    row = lax.broadcasted_iota(jnp.int32, logits.shape, 0)
    rowf = row.astype(F32)
    neg = -jnp.inf
    is_g = row < N_GROUPS
    lg = jnp.where(is_g, logits, neg)
    gmax = jnp.max(lg, axis=0, keepdims=True)
    gidx = jnp.min(jnp.where(lg == gmax, rowf, float(N_GROUPS)), axis=0, keepdims=True)
    g_w = 1.0 / jnp.sum(jnp.where(is_g, jnp.exp(lg - gmax), 0.0), axis=0, keepdims=True)
    erow = rowf - float(N_GROUPS)
    egrp = ((row - N_GROUPS) >> 3).astype(F32)
    in_grp = (row >= N_GROUPS) & (row < N_GROUPS + N_EXPERTS) & (egrp == gidx)
    le = jnp.where(in_grp, logits, neg)
    m1 = jnp.max(le, axis=0, keepdims=True)
    i1 = jnp.min(jnp.where(le == m1, erow, float(N_EXPERTS)), axis=0, keepdims=True)
    le2 = jnp.where(erow == i1, neg, le)
    m2 = jnp.max(le2, axis=0, keepdims=True)
    i2 = jnp.min(jnp.where(le2 == m2, erow, float(N_EXPERTS)), axis=0, keepdims=True)
    t = jnp.exp(m2 - m1)
    w1 = g_w / (1.0 + t)
    w2 = g_w * t / (1.0 + t)

    def run_len(n):
        return jnp.floor((n + (TOK_ALIGN - 1)) * (1.0 / TOK_ALIGN)) * TOK_ALIGN

    oh1 = erow == i1
    oh2 = erow == i2
    oh = jnp.where(oh1 | oh2, 1.0, 0.0).astype(BF16)
    tr = lax.broadcasted_iota(jnp.int32, (TM_MIX, TM_MIX), 0)
    tc = lax.broadcasted_iota(jnp.int32, (TM_MIX, TM_MIX), 1)
    before_tok = _dot(oh, (tr < tc).astype(BF16))
    n_col = jnp.sum(oh.astype(F32), axis=1, keepdims=True)
    er = lax.broadcasted_iota(jnp.int32, (LANES, LANES), 0)
    ec = lax.broadcasted_iota(jnp.int32, (LANES, LANES), 1)
    before_exp = _dot((ec < er).astype(BF16), jnp.broadcast_to(run_len(n_col), logits.shape).astype(BF16))
    where_to = before_exp + before_tok
    q1 = jnp.sum(jnp.where(oh1, where_to, 0.0), axis=0, keepdims=True)
    q2 = jnp.sum(jnp.where(oh2, where_to, 0.0), axis=0, keepdims=True)

    n_rows = run_len(_dot_nt(jnp.ones((SUBLANES, TM_MIX), BF16), oh))
    tabn_o[...] = n_rows
    tabb_o[...] = seen[...]
    seen[...] = seen[...] + n_rows

    ri_o[...] = jnp.zeros_like(ri_o)
    ri_o[0:1, :] = q1.astype(jnp.int32)
    ri_o[1:2, :] = q2.astype(jnp.int32)
    rw_o[...] = jnp.zeros_like(rw_o)
    rw_o[0:1, :] = w1
    rw_o[1:2, :] = w2


def _mix(o_n, og, u, gates, x2d, cw, cb, lg, lb, wpw, wo, g2, wrh, wrl, br):
    tm = MIX_TILES * TM_MIX
    n_front = FRONT // tm
    grid = (SEQ // tm,)
    rowp = lambda w: pl.BlockSpec((tm, w), lambda i: (i + n_front, 0))
    full = lambda a: pl.BlockSpec(a.shape, lambda i: (0,) * a.ndim)
    hist_blocks = tm // HIST
    return pl.pallas_call(
        _mix_kernel,
        grid=grid,
        in_specs=[
            rowp(V_W), rowp(D_MODEL),
            pl.BlockSpec((tm * ROW_TILE, LANES), lambda i: (i + n_front, 0)),
            pl.BlockSpec((HIST * ROW_TILE, LANES), lambda i: ((i + n_front) * hist_blocks - 1, 0)),
            rowp(2 * D_MODEL),
            pl.BlockSpec((tm, D_MODEL), lambda i: (i, 0)),
            full(cw), full(cb), full(lg), full(lb), full(wpw), full(wo), full(g2),
            full(wrh), full(wrl), full(br),
        ],
        out_specs=[
            pl.BlockSpec((tm, D_MODEL), lambda i: (i, 0)),
            pl.BlockSpec((tm, D_MODEL), lambda i: (i, 0)),
            pl.BlockSpec((SUBLANES, tm), lambda i: (0, i)),
            pl.BlockSpec((SUBLANES, tm), lambda i: (0, i)),
            pl.BlockSpec((MIX_TILES * SUBLANES, LANES), lambda i: (i, 0)),
            pl.BlockSpec((MIX_TILES * SUBLANES, LANES), lambda i: (i, 0)),
        ],
        out_shape=[
            jax.ShapeDtypeStruct((SEQ, D_MODEL), F32),
            jax.ShapeDtypeStruct((SEQ, D_MODEL), BF16),
            jax.ShapeDtypeStruct((SUBLANES, SEQ), jnp.int32),
            jax.ShapeDtypeStruct((SUBLANES, SEQ), F32),
            jax.ShapeDtypeStruct((N_TILES * SUBLANES, LANES), F32),
            jax.ShapeDtypeStruct((N_TILES * SUBLANES, LANES), F32),
        ],
        scratch_shapes=[pltpu.VMEM((TM_MIX * ROW_TILE, LANES), F32), pltpu.VMEM((SUBLANES, LANES), F32)],
        compiler_params=pltpu.CompilerParams(dimension_semantics=("arbitrary",),
                                             vmem_limit_bytes=VMEM_LIMIT),
        name="mix",
    )(o_n, og, u, u, gates, x2d, cw, cb, lg, lb, wpw, wo, g2, wrh, wrl, br)


def _tok_rows(start_tok, n_tok):
    return pl.ds(pl.multiple_of(start_tok * TOK_ROWS, SUBLANES), n_tok * TOK_ROWS)


def _start_tile_runs(n_ref, row_ref, tile, make_copy):
    def run(e, off):
        n = n_ref[tile * N_EXPERTS + e]

        @pl.when(n > 0)
        def _():
            make_copy(_tok_rows(off, n), _tok_rows(row_ref[tile * N_EXPERTS + e], n)).start()

        return off + n

    lax.fori_loop(0, N_EXPERTS, run, 0, unroll=8)


def _dispatch_kernel(n_ref, row_ref, tot_ref, ts_ref, tl_ref, nused_ref, hn_ref, q_ref, xb_ref,
                     sb0, sb1, zbuf, sem, fsem):
    step = pl.program_id(0)
    last = pl.num_programs(0) - 1

    def fills(start):
        def tail(e, c):
            n = tl_ref[e]

            @pl.when(n > 0)
            def _():
                cp = pltpu.make_async_copy(zbuf.at[pl.ds(0, n * TOK_ROWS)],
                                           xb_ref.at[_tok_rows(ts_ref[e], n)], fsem)
                cp.start() if start else cp.wait()

            return c

        lax.fori_loop(0, N_EXPERTS, tail, 0)

        def block(b, c):
            cp = pltpu.make_async_copy(zbuf, xb_ref.at[_tok_rows(b * BM, BM)], fsem)
            cp.start() if start else cp.wait()
            return c

        lax.fori_loop(nused_ref[0], N_BLOCKS, block, 0)

    @pl.when(step == 0)
    def _():
        zbuf[...] = jnp.zeros_like(zbuf)
        fills(start=True)

    def tile_runs(tile, s, sbuf):
        rows = pl.ds(0, tot_ref[tile] * TOK_ROWS)
        return pltpu.make_async_copy(sbuf.at[rows], xb_ref.at[rows], sem.at[s])

    for s, sbuf in enumerate((sb0, sb1)):
        tile = step * TILES_PER_STEP + s
        toks = slice(s * TM_MIX, (s + 1) * TM_MIX)
        slot_i = lax.broadcasted_iota(jnp.int32, (SLOTS, TM_MIX), 0)
        onehot = jnp.where((slot_i == q_ref[0:1, toks]) | (slot_i == q_ref[1:2, toks]), 1.0, 0.0).astype(BF16)
        srt = _dot(onehot, hn_ref[toks, :])

        @pl.when(step >= 1)
        def _():
            tile_runs(tile - TILES_PER_STEP, s, sbuf).wait()

        _store_token_tiles(sbuf, srt, SLOTS)
        _start_tile_runs(n_ref, row_ref, tile,
                         lambda loc, glob: pltpu.make_async_copy(sbuf.at[loc], xb_ref.at[glob], sem.at[s]))

    @pl.when(step == last)
    def _():
        for s, sbuf in enumerate((sb0, sb1)):
            tile_runs(step * TILES_PER_STEP + s, s, sbuf).wait()
        fills(start=False)


def _dispatch(n_flat, run_rows, tot, tail_start, tail_len, nused, hn2, ri):
    grid_spec = pltpu.PrefetchScalarGridSpec(
        num_scalar_prefetch=6,
        grid=(N_TILES // TILES_PER_STEP,),
        in_specs=[pl.BlockSpec((TILES_PER_STEP * TM_MIX, D_MODEL), lambda i, *_: (i, 0)),
                  pl.BlockSpec((SUBLANES, TILES_PER_STEP * TM_MIX), lambda i, *_: (0, i))],
        out_specs=pl.BlockSpec(memory_space=pl.ANY),
        scratch_shapes=[pltpu.VMEM((SLOTS * TOK_ROWS, LANES), jnp.uint32),
                        pltpu.VMEM((SLOTS * TOK_ROWS, LANES), jnp.uint32),
                        pltpu.VMEM((BM * TOK_ROWS, LANES), jnp.uint32),
                        pltpu.SemaphoreType.DMA((2,)), pltpu.SemaphoreType.DMA(())],
    )
    return pl.pallas_call(
        _dispatch_kernel,
        grid_spec=grid_spec,
        out_shape=jax.ShapeDtypeStruct((P_ROWS * TOK_ROWS, LANES), jnp.uint32),
        compiler_params=pltpu.CompilerParams(dimension_semantics=("arbitrary",),
                                             vmem_limit_bytes=VMEM_LIMIT),
        name="dispatch",
    )(n_flat, run_rows, tot, tail_start, tail_len, nused, hn2, ri)


def _experts_kernel(be_ref, first_ref, next_ref, next2_ref, stage_ref, nused_ref,
                    x_ref, wg_hbm, wu_hbm, wd_hbm, y_ref, wg_f, wu_f, wd_f, wg_b, wu_b, wd_b, sem):
    def weight_copies(e, st):
        return (pltpu.make_async_copy(wg_hbm.at[e], wg_f.at[st], sem.at[st, 0]),
                pltpu.make_async_copy(wu_hbm.at[e], wu_f.at[st], sem.at[st, 1]),
                pltpu.make_async_copy(wd_hbm.at[e], wd_f.at[st], sem.at[st, 2]))

    @pl.when((pl.program_id(0) == 0) & (nused_ref[0] > 0))
    def _():
        for c in weight_copies(be_ref[0], 0):
            c.start()

        @pl.when(next_ref[0] >= 0)
        def _():
            for c in weight_copies(next_ref[0], 1):
                c.start()

    for j in range(BLOCKS_PER_STEP):
        _expert_block(pl.program_id(0) * BLOCKS_PER_STEP + j, j * BM, be_ref, first_ref, next2_ref,
                      stage_ref, nused_ref, x_ref, y_ref, wg_f, wu_f, wd_f, wg_b, wu_b, wd_b, weight_copies)


def _expert_block(b, tok0, be_ref, first_ref, next2_ref, stage_ref, nused_ref, x_ref, y_ref,
                  wg_f, wu_f, wd_f, wg_b, wu_b, wd_b, weight_copies):
    @pl.when(b < nused_ref[0])
    def _():
        def new_expert(st):
            for c in weight_copies(be_ref[b], st):
                c.wait()
            wg_b[...] = wg_f[st].astype(BF16)
            wu_b[...] = wu_f[st].astype(BF16)
            wd_b[...] = wd_f[st].astype(BF16)

            @pl.when(next2_ref[b] >= 0)
            def _():
                for c in weight_copies(next2_ref[b], st):
                    c.start()

        for st in range(2):
            pl.when((first_ref[b] == 1) & (stage_ref[b] == st))(functools.partial(new_expert, st))

        x = _load_token_tiles(x_ref, BM, tok0)
        a = _dot(x, wg_b[...])
        u = _dot(x, wu_b[...])
        y = _dot((_silu(a) * u).astype(BF16), wd_b[...])
        _store_token_tiles(y_ref, y.astype(BF16).astype(F32), BM, tok0)

    @pl.when(b >= nused_ref[0])
    def _():
        y_ref[tok0 * TOK_ROWS:(tok0 + BM) * TOK_ROWS, :] = jnp.zeros((BM * TOK_ROWS, LANES), jnp.uint32)


def _experts(block_e, first, next_e, next2_e, stage, nused, xb, wg, wu, wd):
    def xmap(b, *_):
        return (b, 0)

    grid_spec = pltpu.PrefetchScalarGridSpec(
        num_scalar_prefetch=6,
        grid=(N_BLOCKS // BLOCKS_PER_STEP,),
        in_specs=[
            pl.BlockSpec((BLOCKS_PER_STEP * BM * TOK_ROWS, LANES), xmap),
            pl.BlockSpec(memory_space=pl.ANY),
            pl.BlockSpec(memory_space=pl.ANY),
            pl.BlockSpec(memory_space=pl.ANY),
        ],
        out_specs=pl.BlockSpec((BLOCKS_PER_STEP * BM * TOK_ROWS, LANES), xmap),
        scratch_shapes=[pltpu.VMEM((2, D_MODEL, D_EXPERT), F32), pltpu.VMEM((2, D_MODEL, D_EXPERT), F32),
                        pltpu.VMEM((2, D_EXPERT, D_MODEL), F32),
                        pltpu.VMEM((D_MODEL, D_EXPERT), BF16), pltpu.VMEM((D_MODEL, D_EXPERT), BF16),
                        pltpu.VMEM((D_EXPERT, D_MODEL), BF16), pltpu.SemaphoreType.DMA((2, 3))],
    )
    return pl.pallas_call(
        _experts_kernel,
        grid_spec=grid_spec,
        out_shape=jax.ShapeDtypeStruct((P_ROWS * TOK_ROWS, LANES), jnp.uint32),
        compiler_params=pltpu.CompilerParams(dimension_semantics=("arbitrary",),
                                             vmem_limit_bytes=VMEM_LIMIT),
        name="experts",
    )(block_e, first, next_e, next2_e, stage, nused, xb, wg, wu, wd)


def _combine_kernel(n_ref, row_ref, tot_ref, yb_ref, h_ref, q_ref, w_ref, g_ref, out_ref, *scratch):
    step = pl.program_id(0)
    bufs, sem = scratch[:COMBINE_TILES], scratch[COMBINE_TILES]

    def start_gather(tile, s):
        _start_tile_runs(n_ref, row_ref, tile,
                         lambda loc, glob: pltpu.make_async_copy(yb_ref.at[glob], bufs[s].at[loc], sem.at[s]))

    @pl.when(step == 0)
    def _():
        for buf in bufs:
            buf[...] = jnp.zeros_like(buf)
        for s in range(GATHER_AHEAD):
            start_gather(s, s)

    for s in range(COMBINE_TILES):
        tile = step * COMBINE_TILES + s
        ahead = s + GATHER_AHEAD
        if ahead < COMBINE_TILES:
            start_gather(tile + GATHER_AHEAD, ahead)
        else:
            pl.when(step + 1 < pl.num_programs(0))(
                functools.partial(start_gather, tile + GATHER_AHEAD, ahead - COMBINE_TILES))

        rows = pl.ds(0, tot_ref[tile] * TOK_ROWS)
        pltpu.make_async_copy(yb_ref.at[rows], bufs[s].at[rows], sem.at[s]).wait()
        ys = _load_token_tiles(bufs[s], SLOTS)
        toks = slice(s * TM_MIX, (s + 1) * TM_MIX)
        slot_i = lax.broadcasted_iota(jnp.int32, (SLOTS, TM_MIX), 0)
        wmat = (jnp.where(slot_i == q_ref[0:1, toks], w_ref[0:1, toks], 0.0)
                + jnp.where(slot_i == q_ref[1:2, toks], w_ref[1:2, toks], 0.0)).astype(BF16)
        hh = h_ref[toks, :] + _dot_tn(wmat, ys)
        out_ref[toks, :] = hh * lax.rsqrt(jnp.mean(hh * hh, axis=-1, keepdims=True) + EPS) * g_ref[...]


def _combine(n_flat, run_rows, tot, yb, h1, ri, rw, gf):
    grid_spec = pltpu.PrefetchScalarGridSpec(
        num_scalar_prefetch=3,
        grid=(N_TILES // COMBINE_TILES,),
        in_specs=[
            pl.BlockSpec(memory_space=pl.ANY),
            pl.BlockSpec((COMBINE_TILES * TM_MIX, D_MODEL), lambda i, *_: (i, 0)),
            pl.BlockSpec((SUBLANES, COMBINE_TILES * TM_MIX), lambda i, *_: (0, i)),
            pl.BlockSpec((SUBLANES, COMBINE_TILES * TM_MIX), lambda i, *_: (0, i)),
            pl.BlockSpec((1, D_MODEL), lambda i, *_: (0, 0)),
        ],
        out_specs=pl.BlockSpec((COMBINE_TILES * TM_MIX, D_MODEL), lambda i, *_: (i, 0)),
        scratch_shapes=[pltpu.VMEM((SLOTS * TOK_ROWS, LANES), jnp.uint32)] * COMBINE_TILES
                       + [pltpu.SemaphoreType.DMA((COMBINE_TILES,))],
    )
    return pl.pallas_call(
        _combine_kernel,
        grid_spec=grid_spec,
        out_shape=jax.ShapeDtypeStruct((SEQ, D_MODEL), F32),
        compiler_params=pltpu.CompilerParams(dimension_semantics=("arbitrary",),
                                             vmem_limit_bytes=VMEM_LIMIT),
        name="combine",
    )(n_flat, run_rows, tot, yb, h1, ri, rw, gf)


def kernel(x, meta, norm1_g, w_in, w_decay_up, b_decay, gla_norm_g, conv_w, conv_b, conv_ln_g,
           conv_ln_b, w_pw2, b_gate, w_out, norm2_g, w_router_group, b_router_group,
           w_router_expert, b_router_expert, w_exp_gate, w_exp_up, w_exp_down, final_norm_g):
    assert x.shape == (1, SEQ, D_MODEL) and w_in.shape[0] == 1
    x2d = x[0]
    front = jnp.concatenate([jnp.zeros((FRONT - N_META, D_MODEL), F32), meta.astype(F32)], axis=0)

    w = w_in[0]
    row2 = lambda a: a.reshape(1, -1).astype(F32)
    wq, wk, wv, wog, wdl, wglu, wgt = _wcast(w)
    up = w_decay_up[0].astype(F32)
    up_hi = up.astype(BF16)
    up_mid = (up - up_hi.astype(F32)).astype(BF16)
    up_lo = (up - up_hi.astype(F32) - up_mid.astype(F32)).astype(BF16)
    wdu = jnp.concatenate([up_hi, up_mid, up_lo, up_hi, up_mid, up_hi], axis=0)
    q, k, v, og, la, u, gates = _inproj(
        front, x2d, row2(norm1_g[0]), wq, wk, wv, wog, jnp.tile(wdl, (1, DL_COPIES)), wglu, wgt,
        wdu, row2(b_decay[0]), row2(b_gate[0]))

    o_n = _gla(q, k, v, la, row2(gla_norm_g[0]))

    wr = jnp.concatenate([w_router_group[0].T, w_router_expert[0].T,
                          jnp.zeros((LANES - N_GROUPS - N_EXPERTS, D_MODEL), F32)], axis=0).astype(F32)
    br = jnp.concatenate([b_router_group[0], b_router_expert[0],
                          jnp.zeros((LANES - N_GROUPS - N_EXPERTS,), F32)]).reshape(LANES, 1).astype(F32)
    wr_hi = wr.astype(BF16)
    wr_lo = (wr - wr_hi.astype(F32)).astype(BF16)
    h1, hn2, ri, rw, tabn, tabb = _mix(
        o_n, og, u, gates, x2d, conv_w[0].astype(F32).reshape(CONV_WIDTH, ROW_TILE, LANES),
        conv_b[0].astype(F32).reshape(ROW_TILE, LANES), row2(conv_ln_g[0]),
        row2(conv_ln_b[0]), w_pw2[0].astype(BF16), w_out[0].astype(BF16),
        row2(norm2_g[0]), wr_hi, wr_lo, br)

    experts = slice(N_GROUPS, N_GROUPS + N_EXPERTS)
    n_te = tabn[::SUBLANES, experts].astype(jnp.int32)
    seen_te = tabb[::SUBLANES, experts].astype(jnp.int32)
    counts = seen_te[-1] + n_te[-1]
    padded = (counts + BM - 1) // BM * BM
    pad_end = jnp.cumsum(padded)
    pad_start = pad_end - padded
    n_flat = n_te.reshape(-1)
    tot = jnp.sum(n_te, axis=1).astype(jnp.int32)
    run_rows = (pad_start[None, :] + seen_te).reshape(-1).astype(jnp.int32)
    tail_start = (pad_start + counts).astype(jnp.int32)
    tail_len = (padded - counts).astype(jnp.int32)
    blk = jnp.arange(N_BLOCKS, dtype=jnp.int32)
    block_e = jnp.minimum(jnp.sum((pad_end[None, :] <= blk[:, None] * BM).astype(jnp.int32), axis=1),
                          N_EXPERTS - 1).astype(jnp.int32)
    first = jnp.concatenate([jnp.ones((1,), jnp.int32), (block_e[1:] != block_e[:-1]).astype(jnp.int32)])
    nused = (pad_end[-1:] // BM).astype(jnp.int32)
    eid = jnp.arange(N_EXPERTS, dtype=jnp.int32)
    later = jnp.flip(lax.cummin(jnp.flip(jnp.where(counts > 0, eid, N_EXPERTS))))
    nxt = jnp.concatenate([later[1:], jnp.full((2,), N_EXPERTS, jnp.int32)])
    nxt2 = nxt[nxt[:N_EXPERTS]]
    or_none = lambda t: jnp.where(t < N_EXPERTS, t, -1)[block_e].astype(jnp.int32)
    next_e, next2_e = or_none(nxt[:N_EXPERTS]), or_none(nxt2)
    stage = ((jnp.cumsum(first) - 1) % 2).astype(jnp.int32)

    xb = _dispatch(n_flat, run_rows, tot, tail_start, tail_len, nused, hn2, ri)
    yb = _experts(block_e, first, next_e, next2_e, stage, nused, xb, w_exp_gate[0], w_exp_up[0], w_exp_down[0])
    out = _combine(n_flat, run_rows, tot, yb, h1, ri, rw, row2(final_norm_g))
    return out[None]
```

```python
import functools

import jax
import jax.numpy as jnp
import numpy as np
from jax import lax
from jax.experimental import pallas as pl
from jax.experimental.pallas import tpu as pltpu

F32 = jnp.float32
BF16 = jnp.bfloat16

D_MODEL = 1024
SEQ = 16384
N_META = 16
GLA_HEADS = 4
GLA_DK = 128
GLA_DV = 256
GLA_RANK = 16
GLA_TAU = 16.0
CHUNK = 64
QK_W = GLA_HEADS * GLA_DK
V_W = GLA_HEADS * GLA_DV
CONV_WIDTH = 31
N_GROUPS = 8
EXPERTS_PER_GROUP = 8
N_EXPERTS = N_GROUPS * EXPERTS_PER_GROUP
TOP_K = 2
D_EXPERT = 512
EPS = 1e-6
LOG2E = 1.4426950408889634

LANES = 128
SUBLANES = 8
ROW_TILE = D_MODEL // LANES
TOK_ROWS = D_MODEL // 2 // LANES
TOK_ALIGN = SUBLANES // TOK_ROWS

FRONT = 512
LP = FRONT + SEQ
TM_PROJ = 512
DL_COPIES = 6
TM_GLA = 512
TM_MIX = 256
MIX_TILES = 2
HIST = 32
CONV_RB = 16
N_TILES = SEQ // TM_MIX
SLOTS = TOP_K * TM_MIX + N_EXPERTS * (TOK_ALIGN - 1)
TILES_PER_STEP = 2
COMBINE_TILES = 4
GATHER_AHEAD = 2
BM = 256
BLOCKS_PER_STEP = 4
N_BLOCKS = (N_TILES * SLOTS) // BM + N_EXPERTS
P_ROWS = N_BLOCKS * BM
VMEM_LIMIT = 56 * 1024 * 1024


def _sigmoid(x):
    return 0.5 * jnp.tanh(0.5 * x) + 0.5


def _silu(x):
    return x * _sigmoid(x)


def _dot(a, b, **kw):
    return jnp.dot(a, b, preferred_element_type=F32, **kw)


def _load_row_tiles(ref, n_rows):
    return jnp.concatenate([ref[pl.ds(s, n_rows, stride=ROW_TILE), :] for s in range(ROW_TILE)], axis=-1)


def _store_row_tiles(ref, val, n_rows):
    for s in range(ROW_TILE):
        ref[pl.ds(s, n_rows, stride=ROW_TILE), :] = val[:, s * LANES:(s + 1) * LANES]


def _load_token_tiles(ref, n_tok, first_tok=0):
    r0 = first_tok * TOK_ROWS
    w = jnp.concatenate([ref[pl.ds(r0 + s, n_tok, stride=TOK_ROWS), :] for s in range(TOK_ROWS)], axis=-1)
    lo = pltpu.bitcast(w << 16, F32)
    hi = pltpu.bitcast(w & jnp.uint32(0xFFFF0000), F32)
    return jnp.concatenate([lo, hi], axis=-1).astype(BF16)


def _store_token_tiles(ref, val, n_tok, first_tok=0):
    bits = pltpu.bitcast(val, jnp.uint32)
    w = (bits[:, 0:D_MODEL // 2] >> 16) | (bits[:, D_MODEL // 2:D_MODEL] & jnp.uint32(0xFFFF0000))
    r0 = first_tok * TOK_ROWS
    for s in range(TOK_ROWS):
        ref[pl.ds(r0 + s, n_tok, stride=TOK_ROWS), :] = w[:, s * LANES:(s + 1) * LANES]


IN_SPLITS = (QK_W, QK_W, V_W, D_MODEL, GLA_RANK, 2 * D_MODEL, 2 * D_MODEL)
WCAST_ROWS = 128


def _wcast_chunks():
    chunks, off = [], 0
    for g, n in enumerate(IN_SPLITS):
        for c in range(0, n, WCAST_ROWS):
            chunks.append((g, off + c, c, min(WCAST_ROWS, n - c)))
        off += n
    return chunks


def _wcast_kernel(wt_hbm, *refs):
    outs, buf, sem = refs[:len(IN_SPLITS)], refs[len(IN_SPLITS)], refs[len(IN_SPLITS) + 1]
    chunks = _wcast_chunks()
    last_start = sum(IN_SPLITS) - WCAST_ROWS

    def copy(i):
        row0 = min(chunks[i][1], last_start)
        return pltpu.make_async_copy(wt_hbm.at[row0:row0 + WCAST_ROWS], buf.at[i % 2], sem.at[i % 2])

    copy(0).start()
    for i, (g, row0, col0, n) in enumerate(chunks):
        if i + 1 < len(chunks):
            copy(i + 1).start()
        copy(i).wait()
        lead = row0 - min(row0, last_start)
        t = jnp.transpose(buf[i % 2])
        outs[g][:, col0:col0 + n] = t[:, lead:lead + n].astype(BF16)


def _wcast(wt):
    return pl.pallas_call(
        _wcast_kernel,
        in_specs=[pl.BlockSpec(memory_space=pl.ANY)],
        out_specs=[pl.BlockSpec(memory_space=pltpu.VMEM) for _ in IN_SPLITS],
        out_shape=[jax.ShapeDtypeStruct((D_MODEL, n), BF16) for n in IN_SPLITS],
        scratch_shapes=[pltpu.VMEM((2, WCAST_ROWS, D_MODEL), F32), pltpu.SemaphoreType.DMA((2,))],
        compiler_params=pltpu.CompilerParams(vmem_limit_bytes=VMEM_LIMIT),
        name="wcast",
    )(wt)


def _inproj_kernel(front_ref, x_ref, g_ref, wq, wk, wv, wog, wdl, wglu, wgt, wdu, bd, bg,
                   q_o, k_o, v_o, og_o, la_o, u_o, gt_o):
    i = pl.program_id(0)
    h = jnp.where(i < FRONT // TM_PROJ, front_ref[...], x_ref[...])
    ms = jnp.mean(h * h, axis=-1, keepdims=True)
    hn = (h * lax.rsqrt(ms + EPS) * g_ref[...]).astype(BF16)
    q_o[...] = _dot(hn, wq[...]).astype(BF16)
    k_o[...] = _dot(hn, wk[...]).astype(BF16)
    v_o[...] = _dot(hn, wv[...]).astype(BF16)
    og_o[...] = _silu(_dot(hn, wog[...])).astype(BF16)
    glu = _dot(hn, wglu[...])
    _store_row_tiles(u_o, glu[:, 0:D_MODEL] * _sigmoid(glu[:, D_MODEL:2 * D_MODEL]), TM_PROJ)
    gt_o[...] = _sigmoid(_dot(hn, wgt[...]) + bg[...]).astype(BF16)
    d6 = _dot(hn, wdl[...])
    hi = d6.astype(BF16).astype(F32)
    mid = (d6 - hi).astype(BF16).astype(F32)
    lane = lax.broadcasted_iota(jnp.int32, d6.shape, 1)
    pieces = jnp.where(lane < 3 * GLA_RANK, hi, jnp.where(lane < 5 * GLA_RANK, mid, d6 - hi - mid))
    z = _dot(pieces.astype(BF16), wdu[...]) + bd[...]
    la_o[...] = (jnp.minimum(z, 0.0) - jnp.log(1.0 + jnp.exp(-jnp.abs(z)))) * (LOG2E / GLA_TAU)


def _inproj(front, x2d, g1, wq, wk, wv, wog, wdl, wglu, wgt, wdu, bd, bg):
    n_front = FRONT // TM_PROJ
    grid = (LP // TM_PROJ,)
    row = lambda w: pl.BlockSpec((TM_PROJ, w), lambda i: (i, 0))
    full = lambda a: pl.BlockSpec(a.shape, lambda i: (0, 0))
    return pl.pallas_call(
        _inproj_kernel,
        grid=grid,
        in_specs=[
            pl.BlockSpec((TM_PROJ, D_MODEL), lambda i: (jnp.minimum(i, n_front - 1), 0)),
            pl.BlockSpec((TM_PROJ, D_MODEL), lambda i: (jnp.maximum(i - n_front, 0), 0)),
            full(g1), full(wq), full(wk), full(wv), full(wog), full(wdl), full(wglu), full(wgt),
            full(wdu), full(bd), full(bg),
        ],
        out_specs=[row(QK_W), row(QK_W), row(V_W), row(D_MODEL), row(QK_W),
                   pl.BlockSpec((TM_PROJ * ROW_TILE, LANES), lambda i: (i, 0)), row(2 * D_MODEL)],
        out_shape=[
            jax.ShapeDtypeStruct((LP, QK_W), BF16), jax.ShapeDtypeStruct((LP, QK_W), BF16),
            jax.ShapeDtypeStruct((LP, V_W), BF16), jax.ShapeDtypeStruct((LP, D_MODEL), BF16),
            jax.ShapeDtypeStruct((LP, QK_W), F32), jax.ShapeDtypeStruct((LP * ROW_TILE, LANES), F32),
            jax.ShapeDtypeStruct((LP, 2 * D_MODEL), BF16),
        ],
        compiler_params=pltpu.CompilerParams(dimension_semantics=("arbitrary",),
                                             vmem_limit_bytes=VMEM_LIMIT),
        name="inproj",
    )(front, x2d, g1, wq, wk, wv, wog, wdl, wglu, wgt, wdu, bd, bg)


GLA_LEVELS = (32, 16, 8, 4, 2, 1)
GLA_FINE = tuple(m for m in GLA_LEVELS if m < SUBLANES)
LA_SPLIT = 3


def _gla_tables():
    r = np.arange(CHUNK)
    t, c = r[:, None], r[None, :]
    rows = [c <= t]
    pair = []
    for m in GLA_LEVELS:
        mid = (t // (2 * m)) * (2 * m) + m - 1
        right = (t % (2 * m)) >= m
        if m in GLA_FINE:
            rows.append(np.where(right, (c > mid) & (c <= t), (c > t) & (c <= mid)))
        pair.append(((t // (2 * m)) == (c // (2 * m))) & right & ((c % (2 * m)) < m))
    pair.append(t == c)
    sums = np.concatenate(rows, axis=0).astype(np.float32)
    sums = np.concatenate([sums] * LA_SPLIT, axis=1)
    pair = np.stack([np.kron(np.eye(GLA_HEADS), p) for p in pair]).astype(np.float32)
    return jnp.asarray(sums, BF16), jnp.asarray(pair)


def _split_bf16(x):
    pieces = []
    rest = x
    for _ in range(LA_SPLIT):
        p = rest.astype(BF16)
        pieces.append(p)
        rest = rest - p.astype(F32)
    return jnp.concatenate(pieces, axis=0)


def _dot_nt(a, b):
    return lax.dot_general(a, b, (((1,), (1,)), ((), ())), preferred_element_type=F32)


def _dot_tn(a, b):
    return lax.dot_general(a, b, (((0,), (0,)), ((), ())), preferred_element_type=F32)


def _gla_exponents(a):
    b = a[0:CHUNK]
    parts = [b, b[CHUNK - 1:CHUNK] - b]
    for m in GLA_LEVELS:
        if m in GLA_FINE:
            continue
        for lo in range(0, CHUNK, 2 * m):
            ref = b[lo + m - 1:lo + m]
            parts += [ref - b[lo:lo + m], b[lo + m:lo + 2 * m] - ref]
    return jnp.concatenate(parts + [a[CHUNK:]], axis=0)


def _gla_kernel(q_ref, k_ref, v_ref, la_ref, g_ref, sums_ref, pair_ref, o_ref, s_ref):
    @pl.when(pl.program_id(0) == 0)
    def _():
        s_ref[...] = jnp.zeros_like(s_ref)

    row = lax.broadcasted_iota(jnp.int32, (CHUNK, GLA_DK), 0)
    right_rows = [(row & m) != 0 for m in GLA_LEVELS]
    sums = sums_ref[...]
    n_lvl = len(GLA_LEVELS)

    def body(c, carry):
        r0 = pl.multiple_of(c * CHUNK, CHUNK)
        rows = pl.ds(r0, CHUNK)
        states = [s_ref[h] for h in range(GLA_HEADS)]
        qs, ks, es, decs = [], [], [], []
        for hp in range(GLA_HEADS // 2):
            la3 = _split_bf16(la_ref[rows, 2 * hp * GLA_DK:2 * (hp + 1) * GLA_DK])
            a2 = _dot(sums, la3)
            e2 = jnp.exp2(_gla_exponents(a2))
            dec2 = jnp.exp2(jnp.transpose(jnp.broadcast_to(a2[CHUNK - 1:CHUNK], (LANES, 2 * GLA_DK))))
            for h in (2 * hp, 2 * hp + 1):
                half = slice((h % 2) * GLA_DK, (h % 2 + 1) * GLA_DK)
                kc = slice(h * GLA_DK, (h + 1) * GLA_DK)
                qs.append(q_ref[rows, kc].astype(F32) * (GLA_DK ** -0.5))
                ks.append(k_ref[rows, kc].astype(F32))
                es.append(e2[:, half])
                decs.append(dec2[half])
        vbs = [v_ref[rows, h * GLA_DV:(h + 1) * GLA_DV] for h in range(GLA_HEADS)]

        new_states, qes = [], []
        for h in range(GLA_HEADS):
            qes.append((qs[h] * es[h][0:CHUNK]).astype(BF16))
            kd = (ks[h] * es[h][CHUNK:2 * CHUNK]).astype(BF16)
            new_states.append(states[h] * jnp.concatenate([decs[h]] * (GLA_DV // LANES), axis=1)
                              + _dot_tn(kd, vbs[h]))

        stack = lambda parts: jnp.concatenate(parts, axis=0)
        sc = _dot_nt(stack([q.astype(BF16) for q in qs]), stack([k.astype(BF16) for k in ks])) * pair_ref[n_lvl]
        for lvl in range(n_lvl):
            x = stack([(jnp.where(right_rows[lvl], qs[h], ks[h])
                        * es[h][(2 + lvl) * CHUNK:(3 + lvl) * CHUNK]).astype(BF16) for h in range(GLA_HEADS)])
            sc = sc + _dot_nt(x, x) * pair_ref[lvl]
        o_all = _dot(sc.astype(BF16), stack(vbs))
        for h in range(GLA_HEADS):
            vc = slice(h * GLA_DV, (h + 1) * GLA_DV)
            o = o_all[h * CHUNK:(h + 1) * CHUNK] + _dot(qes[h], states[h].astype(BF16))
            o = o * lax.rsqrt(jnp.mean(o * o, axis=-1, keepdims=True) + EPS) * g_ref[:, vc]
            o_ref[rows, vc] = o.astype(BF16)
        for h in range(GLA_HEADS):
            s_ref[h] = new_states[h]
        return carry

    lax.fori_loop(0, TM_GLA // CHUNK, body, 0, unroll=4)


def _gla(q, k, v, la, g):
    sums, pair = _gla_tables()
    row = lambda w: pl.BlockSpec((TM_GLA, w), lambda t: (t, 0))
    full = lambda a: pl.BlockSpec(a.shape, lambda t: (0,) * a.ndim)
    return pl.pallas_call(
        _gla_kernel,
        grid=(LP // TM_GLA,),
        in_specs=[row(QK_W), row(QK_W), row(V_W), row(QK_W), full(g), full(sums), full(pair)],
        out_specs=row(V_W),
        out_shape=jax.ShapeDtypeStruct((LP, V_W), BF16),
        scratch_shapes=[pltpu.VMEM((GLA_HEADS, GLA_DK, GLA_DV), F32)],
        compiler_params=pltpu.CompilerParams(dimension_semantics=("arbitrary",),
                                             vmem_limit_bytes=VMEM_LIMIT),
        name="gla",
    )(q, k, v, la, g, sums, pair)


def _mix_kernel(o_ref, og_ref, u_ref, hist_ref, gt_ref, x_ref, cw_ref, cb_ref, lg_ref, lb_ref,
                wpw_ref, wo_ref, g2_ref, wrh_ref, wrl_ref, br_ref,
                h_o, hn_o, ri_o, rw_o, tabn_o, tabb_o, cbuf, seen):
    @pl.when(pl.program_id(0) == 0)
    def _():
        seen[...] = jnp.zeros_like(seen)

    for j in range(MIX_TILES):
        r = slice(j * TM_MIX, (j + 1) * TM_MIX)
        tab = slice(j * SUBLANES, (j + 1) * SUBLANES)

        def u_row(t, j=j):
            t += j * TM_MIX
            ref, t = (hist_ref, t + HIST) if t < 0 else (u_ref, t)
            return ref[t * ROW_TILE:(t + 1) * ROW_TILE, :]

        _mix_tile(o_ref.at[r], og_ref.at[r], u_row, gt_ref.at[r], x_ref.at[r], cw_ref, cb_ref,
                  lg_ref, lb_ref, wpw_ref, wo_ref, g2_ref, wrh_ref, wrl_ref, br_ref,
                  h_o.at[r], hn_o.at[r], ri_o.at[:, r], rw_o.at[:, r], tabn_o.at[tab], tabb_o.at[tab],
                  cbuf, seen)


def _mix_tile(o_ref, og_ref, u_row, gt_ref, x_ref, cw_ref, cb_ref, lg_ref, lb_ref,
              wpw_ref, wo_ref, g2_ref, wrh_ref, wrl_ref, br_ref,
              h_o, hn_o, ri_o, rw_o, tabn_o, tabb_o, cbuf, seen):
    taps = [cw_ref[jj] for jj in range(CONV_WIDTH)]
    for t0 in range(0, TM_MIX, CONV_RB):
        part = [cb_ref[...]] * CONV_RB
        for rr in range(t0 - (CONV_WIDTH - 1), t0 + CONV_RB):
            ur = u_row(rr)
            for t in range(max(t0, rr), min(t0 + CONV_RB, rr + CONV_WIDTH)):
                part[t - t0] = part[t - t0] + taps[rr - t + CONV_WIDTH - 1] * ur
        for g in range(CONV_RB):
            cbuf[(t0 + g) * ROW_TILE:(t0 + g + 1) * ROW_TILE, :] = part[g]
    acc = _load_row_tiles(cbuf, TM_MIX)

    mu = jnp.mean(acc, axis=-1, keepdims=True)
    xc = acc - mu
    ln = xc * lax.rsqrt(jnp.mean(xc * xc, axis=-1, keepdims=True) + EPS) * lg_ref[...] + lb_ref[...]
    branch_b = _dot(_silu(ln).astype(BF16), wpw_ref[...])

    branch_a = o_ref[...].astype(F32) * og_ref[...].astype(F32)
    g_a = gt_ref[:, 0:D_MODEL].astype(F32)
    g_b = gt_ref[:, D_MODEL:2 * D_MODEL].astype(F32)
    merged = (g_a * branch_a + g_b * branch_b).astype(BF16)
    h1 = x_ref[...] + _dot(merged, wo_ref[...])
    h_o[...] = h1

    hn2 = h1 * lax.rsqrt(jnp.mean(h1 * h1, axis=-1, keepdims=True) + EPS) * g2_ref[...]
    hn_hi = hn2.astype(BF16)
    hn_o[...] = hn_hi

    hn_lo = (hn2 - hn_hi.astype(F32)).astype(BF16)
    logits = (_dot_nt(wrh_ref[...], hn_hi) + _dot_nt(wrh_ref[...], hn_lo) + _dot_nt(wrl_ref[...], hn_hi)
              + br_ref[...])
    row = lax.broadcasted_iota(jnp.int32, logits.shape, 0)
    rowf = row.astype(F32)
    neg = -jnp.inf
    is_g = row < N_GROUPS
    lg = jnp.where(is_g, logits, neg)
    gmax = jnp.max(lg, axis=0, keepdims=True)
    gidx = jnp.min(jnp.where(lg == gmax, rowf, float(N_GROUPS)), axis=0, keepdims=True)
    g_w = 1.0 / jnp.sum(jnp.where(is_g, jnp.exp(lg - gmax), 0.0), axis=0, keepdims=True)
    erow = rowf - float(N_GROUPS)
    egrp = ((row - N_GROUPS) >> 3).astype(F32)
    in_grp = (row >= N_GROUPS) & (row < N_GROUPS + N_EXPERTS) & (egrp == gidx)
    le = jnp.where(in_grp, logits, neg)
    m1 = jnp.max(le, axis=0, keepdims=True)
    i1 = jnp.min(jnp.where(le == m1, erow, float(N_EXPERTS)), axis=0, keepdims=True)
    le2 = jnp.where(erow == i1, neg, le)
    m2 = jnp.max(le2, axis=0, keepdims=True)
    i2 = jnp.min(jnp.where(le2 == m2, erow, float(N_EXPERTS)), axis=0, keepdims=True)
    t = jnp.exp(m2 - m1)
    w1 = g_w / (1.0 + t)
    w2 = g_w * t / (1.0 + t)

    def run_len(n):
        return jnp.floor((n + (TOK_ALIGN - 1)) * (1.0 / TOK_ALIGN)) * TOK_ALIGN

    oh1 = erow == i1
    oh2 = erow == i2
    oh = jnp.where(oh1 | oh2, 1.0, 0.0).astype(BF16)
    tr = lax.broadcasted_iota(jnp.int32, (TM_MIX, TM_MIX), 0)
    tc = lax.broadcasted_iota(jnp.int32, (TM_MIX, TM_MIX), 1)
    before_tok = _dot(oh, (tr < tc).astype(BF16))
    n_col = jnp.sum(oh.astype(F32), axis=1, keepdims=True)
    er = lax.broadcasted_iota(jnp.int32, (LANES, LANES), 0)
    ec = lax.broadcasted_iota(jnp.int32, (LANES, LANES), 1)
    before_exp = _dot((ec < er).astype(BF16), jnp.broadcast_to(run_len(n_col), logits.shape).astype(BF16))
    where_to = before_exp + before_tok
    q1 = jnp.sum(jnp.where(oh1, where_to, 0.0), axis=0, keepdims=True)
    q2 = jnp.sum(jnp.where(oh2, where_to, 0.0), axis=0, keepdims=True)

    n_rows = run_len(_dot_nt(jnp.ones((SUBLANES, TM_MIX), BF16), oh))
    tabn_o[...] = n_rows
    tabb_o[...] = seen[...]
    seen[...] = seen[...] + n_rows

    ri_o[...] = jnp.zeros_like(ri_o)
    ri_o[0:1, :] = q1.astype(jnp.int32)
    ri_o[1:2, :] = q2.astype(jnp.int32)
    rw_o[...] = jnp.zeros_like(rw_o)
    rw_o[0:1, :] = w1
    rw_o[1:2, :] = w2


def _mix(o_n, og, u, gates, x2d, cw, cb, lg, lb, wpw, wo, g2, wrh, wrl, br):
    tm = MIX_TILES * TM_MIX
    n_front = FRONT // tm
    grid = (SEQ // tm,)
    rowp = lambda w: pl.BlockSpec((tm, w), lambda i: (i + n_front, 0))
    full = lambda a: pl.BlockSpec(a.shape, lambda i: (0,) * a.ndim)
    hist_blocks = tm // HIST
    return pl.pallas_call(
        _mix_kernel,
        grid=grid,
        in_specs=[
            rowp(V_W), rowp(D_MODEL),
            pl.BlockSpec((tm * ROW_TILE, LANES), lambda i: (i + n_front, 0)),
            pl.BlockSpec((HIST * ROW_TILE, LANES), lambda i: ((i + n_front) * hist_blocks - 1, 0)),
            rowp(2 * D_MODEL),
            pl.BlockSpec((tm, D_MODEL), lambda i: (i, 0)),
            full(cw), full(cb), full(lg), full(lb), full(wpw), full(wo), full(g2),
            full(wrh), full(wrl), full(br),
        ],
        out_specs=[
            pl.BlockSpec((tm, D_MODEL), lambda i: (i, 0)),
            pl.BlockSpec((tm, D_MODEL), lambda i: (i, 0)),
            pl.BlockSpec((SUBLANES, tm), lambda i: (0, i)),
            pl.BlockSpec((SUBLANES, tm), lambda i: (0, i)),
            pl.BlockSpec((MIX_TILES * SUBLANES, LANES), lambda i: (i, 0)),
            pl.BlockSpec((MIX_TILES * SUBLANES, LANES), lambda i: (i, 0)),
        ],
        out_shape=[
            jax.ShapeDtypeStruct((SEQ, D_MODEL), F32),
            jax.ShapeDtypeStruct((SEQ, D_MODEL), BF16),
            jax.ShapeDtypeStruct((SUBLANES, SEQ), jnp.int32),
            jax.ShapeDtypeStruct((SUBLANES, SEQ), F32),
            jax.ShapeDtypeStruct((N_TILES * SUBLANES, LANES), F32),
            jax.ShapeDtypeStruct((N_TILES * SUBLANES, LANES), F32),
        ],
        scratch_shapes=[pltpu.VMEM((TM_MIX * ROW_TILE, LANES), F32), pltpu.VMEM((SUBLANES, LANES), F32)],
        compiler_params=pltpu.CompilerParams(dimension_semantics=("arbitrary",),
                                             vmem_limit_bytes=VMEM_LIMIT),
        name="mix",
    )(o_n, og, u, u, gates, x2d, cw, cb, lg, lb, wpw, wo, g2, wrh, wrl, br)


def _tok_rows(start_tok, n_tok):
    return pl.ds(pl.multiple_of(start_tok * TOK_ROWS, SUBLANES), n_tok * TOK_ROWS)


def _start_tile_runs(n_ref, row_ref, tile, make_copy):
    def run(e, off):
        n = n_ref[tile * N_EXPERTS + e]

        @pl.when(n > 0)
        def _():
            make_copy(_tok_rows(off, n), _tok_rows(row_ref[tile * N_EXPERTS + e], n)).start()

        return off + n

    lax.fori_loop(0, N_EXPERTS, run, 0, unroll=8)


def _dispatch_kernel(n_ref, row_ref, tot_ref, ts_ref, tl_ref, nused_ref, hn_ref, q_ref, xb_ref,
                     sb0, sb1, zbuf, sem, fsem):
    step = pl.program_id(0)
    last = pl.num_programs(0) - 1

    def fills(start):
        def tail(e, c):
            n = tl_ref[e]

            @pl.when(n > 0)
            def _():
                cp = pltpu.make_async_copy(zbuf.at[pl.ds(0, n * TOK_ROWS)],
                                           xb_ref.at[_tok_rows(ts_ref[e], n)], fsem)
                cp.start() if start else cp.wait()

            return c

        lax.fori_loop(0, N_EXPERTS, tail, 0)

        def block(b, c):
            cp = pltpu.make_async_copy(zbuf, xb_ref.at[_tok_rows(b * BM, BM)], fsem)
            cp.start() if start else cp.wait()
            return c

        lax.fori_loop(nused_ref[0], N_BLOCKS, block, 0)

    @pl.when(step == 0)
    def _():
        zbuf[...] = jnp.zeros_like(zbuf)
        fills(start=True)

    def tile_runs(tile, s, sbuf):
        rows = pl.ds(0, tot_ref[tile] * TOK_ROWS)
        return pltpu.make_async_copy(sbuf.at[rows], xb_ref.at[rows], sem.at[s])

    for s, sbuf in enumerate((sb0, sb1)):
        tile = step * TILES_PER_STEP + s
        toks = slice(s * TM_MIX, (s + 1) * TM_MIX)
        slot_i = lax.broadcasted_iota(jnp.int32, (SLOTS, TM_MIX), 0)
        onehot = jnp.where((slot_i == q_ref[0:1, toks]) | (slot_i == q_ref[1:2, toks]), 1.0, 0.0).astype(BF16)
        srt = _dot(onehot, hn_ref[toks, :])

        @pl.when(step >= 1)
        def _():
            tile_runs(tile - TILES_PER_STEP, s, sbuf).wait()

        _store_token_tiles(sbuf, srt, SLOTS)
        _start_tile_runs(n_ref, row_ref, tile,
                         lambda loc, glob: pltpu.make_async_copy(sbuf.at[loc], xb_ref.at[glob], sem.at[s]))

    @pl.when(step == last)
    def _():
        for s, sbuf in enumerate((sb0, sb1)):
            tile_runs(step * TILES_PER_STEP + s, s, sbuf).wait()
        fills(start=False)


def _dispatch(n_flat, run_rows, tot, tail_start, tail_len, nused, hn2, ri):
    grid_spec = pltpu.PrefetchScalarGridSpec(
        num_scalar_prefetch=6,
        grid=(N_TILES // TILES_PER_STEP,),
        in_specs=[pl.BlockSpec((TILES_PER_STEP * TM_MIX, D_MODEL), lambda i, *_: (i, 0)),
                  pl.BlockSpec((SUBLANES, TILES_PER_STEP * TM_MIX), lambda i, *_: (0, i))],
        out_specs=pl.BlockSpec(memory_space=pl.ANY),
        scratch_shapes=[pltpu.VMEM((SLOTS * TOK_ROWS, LANES), jnp.uint32),
                        pltpu.VMEM((SLOTS * TOK_ROWS, LANES), jnp.uint32),
                        pltpu.VMEM((BM * TOK_ROWS, LANES), jnp.uint32),
                        pltpu.SemaphoreType.DMA((2,)), pltpu.SemaphoreType.DMA(())],
    )
    return pl.pallas_call(
        _dispatch_kernel,
        grid_spec=grid_spec,
        out_shape=jax.ShapeDtypeStruct((P_ROWS * TOK_ROWS, LANES), jnp.uint32),
        compiler_params=pltpu.CompilerParams(dimension_semantics=("arbitrary",),
                                             vmem_limit_bytes=VMEM_LIMIT),
        name="dispatch",
    )(n_flat, run_rows, tot, tail_start, tail_len, nused, hn2, ri)


def _experts_kernel(be_ref, first_ref, next_ref, next2_ref, stage_ref, nused_ref,
                    x_ref, wg_hbm, wu_hbm, wd_hbm, y_ref, wg_f, wu_f, wd_f, wg_b, wu_b, wd_b, sem):
    def weight_copies(e, st):
        return (pltpu.make_async_copy(wg_hbm.at[e], wg_f.at[st], sem.at[st, 0]),
                pltpu.make_async_copy(wu_hbm.at[e], wu_f.at[st], sem.at[st, 1]),
                pltpu.make_async_copy(wd_hbm.at[e], wd_f.at[st], sem.at[st, 2]))

    @pl.when((pl.program_id(0) == 0) & (nused_ref[0] > 0))
    def _():
        for c in weight_copies(be_ref[0], 0):
            c.start()

        @pl.when(next_ref[0] >= 0)
        def _():
            for c in weight_copies(next_ref[0], 1):
                c.start()

    for j in range(BLOCKS_PER_STEP):
        _expert_block(pl.program_id(0) * BLOCKS_PER_STEP + j, j * BM, be_ref, first_ref, next2_ref,
                      stage_ref, nused_ref, x_ref, y_ref, wg_f, wu_f, wd_f, wg_b, wu_b, wd_b, weight_copies)


def _expert_block(b, tok0, be_ref, first_ref, next2_ref, stage_ref, nused_ref, x_ref, y_ref,
                  wg_f, wu_f, wd_f, wg_b, wu_b, wd_b, weight_copies):
    @pl.when(b < nused_ref[0])
    def _():
        def new_expert(st):
            for c in weight_copies(be_ref[b], st):
                c.wait()
            wg_b[...] = wg_f[st].astype(BF16)
            wu_b[...] = wu_f[st].astype(BF16)
            wd_b[...] = wd_f[st].astype(BF16)

            @pl.when(next2_ref[b] >= 0)
            def _():
                for c in weight_copies(next2_ref[b], st):
                    c.start()

        for st in range(2):
            pl.when((first_ref[b] == 1) & (stage_ref[b] == st))(functools.partial(new_expert, st))

        x = _load_token_tiles(x_ref, BM, tok0)
        a = _dot(x, wg_b[...])
        u = _dot(x, wu_b[...])
        y = _dot((_silu(a) * u).astype(BF16), wd_b[...])
        _store_token_tiles(y_ref, y.astype(BF16).astype(F32), BM, tok0)

    @pl.when(b >= nused_ref[0])
    def _():
        y_ref[tok0 * TOK_ROWS:(tok0 + BM) * TOK_ROWS, :] = jnp.zeros((BM * TOK_ROWS, LANES), jnp.uint32)


def _experts(block_e, first, next_e, next2_e, stage, nused, xb, wg, wu, wd):
    def xmap(b, *_):
        return (b, 0)

    grid_spec = pltpu.PrefetchScalarGridSpec(
        num_scalar_prefetch=6,
        grid=(N_BLOCKS // BLOCKS_PER_STEP,),
        in_specs=[
            pl.BlockSpec((BLOCKS_PER_STEP * BM * TOK_ROWS, LANES), xmap),
            pl.BlockSpec(memory_space=pl.ANY),
            pl.BlockSpec(memory_space=pl.ANY),
            pl.BlockSpec(memory_space=pl.ANY),
        ],
        out_specs=pl.BlockSpec((BLOCKS_PER_STEP * BM * TOK_ROWS, LANES), xmap),
        scratch_shapes=[pltpu.VMEM((2, D_MODEL, D_EXPERT), F32), pltpu.VMEM((2, D_MODEL, D_EXPERT), F32),
                        pltpu.VMEM((2, D_EXPERT, D_MODEL), F32),
                        pltpu.VMEM((D_MODEL, D_EXPERT), BF16), pltpu.VMEM((D_MODEL, D_EXPERT), BF16),
                        pltpu.VMEM((D_EXPERT, D_MODEL), BF16), pltpu.SemaphoreType.DMA((2, 3))],
    )
    return pl.pallas_call(
        _experts_kernel,
        grid_spec=grid_spec,
        out_shape=jax.ShapeDtypeStruct((P_ROWS * TOK_ROWS, LANES), jnp.uint32),
        compiler_params=pltpu.CompilerParams(dimension_semantics=("arbitrary",),
                                             vmem_limit_bytes=VMEM_LIMIT),
        name="experts",
    )(block_e, first, next_e, next2_e, stage, nused, xb, wg, wu, wd)


def _combine_kernel(n_ref, row_ref, tot_ref, yb_ref, h_ref, q_ref, w_ref, g_ref, out_ref, *scratch):
    step = pl.program_id(0)
    bufs, sem = scratch[:COMBINE_TILES], scratch[COMBINE_TILES]

    def start_gather(tile, s):
        _start_tile_runs(n_ref, row_ref, tile,
                         lambda loc, glob: pltpu.make_async_copy(yb_ref.at[glob], bufs[s].at[loc], sem.at[s]))

    @pl.when(step == 0)
    def _():
        for buf in bufs:
            buf[...] = jnp.zeros_like(buf)
        for s in range(GATHER_AHEAD):
            start_gather(s, s)

    for s in range(COMBINE_TILES):
        tile = step * COMBINE_TILES + s
        ahead = s + GATHER_AHEAD
        if ahead < COMBINE_TILES:
            start_gather(tile + GATHER_AHEAD, ahead)
        else:
            pl.when(step + 1 < pl.num_programs(0))(
                functools.partial(start_gather, tile + GATHER_AHEAD, ahead - COMBINE_TILES))

        rows = pl.ds(0, tot_ref[tile] * TOK_ROWS)
        pltpu.make_async_copy(yb_ref.at[rows], bufs[s].at[rows], sem.at[s]).wait()
        ys = _load_token_tiles(bufs[s], SLOTS)
        toks = slice(s * TM_MIX, (s + 1) * TM_MIX)
        slot_i = lax.broadcasted_iota(jnp.int32, (SLOTS, TM_MIX), 0)
        wmat = (jnp.where(slot_i == q_ref[0:1, toks], w_ref[0:1, toks], 0.0)
                + jnp.where(slot_i == q_ref[1:2, toks], w_ref[1:2, toks], 0.0)).astype(BF16)
        hh = h_ref[toks, :] + _dot_tn(wmat, ys)
        out_ref[toks, :] = hh * lax.rsqrt(jnp.mean(hh * hh, axis=-1, keepdims=True) + EPS) * g_ref[...]


def _combine(n_flat, run_rows, tot, yb, h1, ri, rw, gf):
    grid_spec = pltpu.PrefetchScalarGridSpec(
        num_scalar_prefetch=3,
        grid=(N_TILES // COMBINE_TILES,),
        in_specs=[
            pl.BlockSpec(memory_space=pl.ANY),
            pl.BlockSpec((COMBINE_TILES * TM_MIX, D_MODEL), lambda i, *_: (i, 0)),
            pl.BlockSpec((SUBLANES, COMBINE_TILES * TM_MIX), lambda i, *_: (0, i)),
            pl.BlockSpec((SUBLANES, COMBINE_TILES * TM_MIX), lambda i, *_: (0, i)),
            pl.BlockSpec((1, D_MODEL), lambda i, *_: (0, 0)),
        ],
        out_specs=pl.BlockSpec((COMBINE_TILES * TM_MIX, D_MODEL), lambda i, *_: (i, 0)),
        scratch_shapes=[pltpu.VMEM((SLOTS * TOK_ROWS, LANES), jnp.uint32)] * COMBINE_TILES
                       + [pltpu.SemaphoreType.DMA((COMBINE_TILES,))],
    )
    return pl.pallas_call(
        _combine_kernel,
        grid_spec=grid_spec,
        out_shape=jax.ShapeDtypeStruct((SEQ, D_MODEL), F32),
        compiler_params=pltpu.CompilerParams(dimension_semantics=("arbitrary",),
                                             vmem_limit_bytes=VMEM_LIMIT),
        name="combine",
    )(n_flat, run_rows, tot, yb, h1, ri, rw, gf)


def kernel(x, meta, norm1_g, w_in, w_decay_up, b_decay, gla_norm_g, conv_w, conv_b, conv_ln_g,
           conv_ln_b, w_pw2, b_gate, w_out, norm2_g, w_router_group, b_router_group,
           w_router_expert, b_router_expert, w_exp_gate, w_exp_up, w_exp_down, final_norm_g):
    assert x.shape == (1, SEQ, D_MODEL) and w_in.shape[0] == 1
    x2d = x[0]
    front = jnp.concatenate([jnp.zeros((FRONT - N_META, D_MODEL), F32), meta.astype(F32)], axis=0)

    w = w_in[0]
    row2 = lambda a: a.reshape(1, -1).astype(F32)
    wq, wk, wv, wog, wdl, wglu, wgt = _wcast(jnp.transpose(w))
    up = w_decay_up[0].astype(F32)
    up_hi = up.astype(BF16)
    up_mid = (up - up_hi.astype(F32)).astype(BF16)
    up_lo = (up - up_hi.astype(F32) - up_mid.astype(F32)).astype(BF16)
    wdu = jnp.concatenate([up_hi, up_mid, up_lo, up_hi, up_mid, up_hi], axis=0)
    q, k, v, og, la, u, gates = _inproj(
        front, x2d, row2(norm1_g[0]), wq, wk, wv, wog, jnp.tile(wdl, (1, DL_COPIES)), wglu, wgt,
        wdu, row2(b_decay[0]), row2(b_gate[0]))

    o_n = _gla(q, k, v, la, row2(gla_norm_g[0]))

    wr = jnp.concatenate([w_router_group[0].T, w_router_expert[0].T,
                          jnp.zeros((LANES - N_GROUPS - N_EXPERTS, D_MODEL), F32)], axis=0).astype(F32)
    br = jnp.concatenate([b_router_group[0], b_router_expert[0],
                          jnp.zeros((LANES - N_GROUPS - N_EXPERTS,), F32)]).reshape(LANES, 1).astype(F32)
    wr_hi = wr.astype(BF16)
    wr_lo = (wr - wr_hi.astype(F32)).astype(BF16)
    h1, hn2, ri, rw, tabn, tabb = _mix(
        o_n, og, u, gates, x2d, conv_w[0].astype(F32).reshape(CONV_WIDTH, ROW_TILE, LANES),
        conv_b[0].astype(F32).reshape(ROW_TILE, LANES), row2(conv_ln_g[0]),
        row2(conv_ln_b[0]), w_pw2[0].astype(BF16), w_out[0].astype(BF16),
        row2(norm2_g[0]), wr_hi, wr_lo, br)

    experts = slice(N_GROUPS, N_GROUPS + N_EXPERTS)
    n_te = tabn[::SUBLANES, experts].astype(jnp.int32)
    seen_te = tabb[::SUBLANES, experts].astype(jnp.int32)
    counts = seen_te[-1] + n_te[-1]
    padded = (counts + BM - 1) // BM * BM
    pad_end = jnp.cumsum(padded)
    pad_start = pad_end - padded
    n_flat = n_te.reshape(-1)
    tot = jnp.sum(n_te, axis=1).astype(jnp.int32)
    run_rows = (pad_start[None, :] + seen_te).reshape(-1).astype(jnp.int32)
    tail_start = (pad_start + counts).astype(jnp.int32)
    tail_len = (padded - counts).astype(jnp.int32)
    blk = jnp.arange(N_BLOCKS, dtype=jnp.int32)
    block_e = jnp.minimum(jnp.sum((pad_end[None, :] <= blk[:, None] * BM).astype(jnp.int32), axis=1),
                          N_EXPERTS - 1).astype(jnp.int32)
    first = jnp.concatenate([jnp.ones((1,), jnp.int32), (block_e[1:] != block_e[:-1]).astype(jnp.int32)])
    nused = (pad_end[-1:] // BM).astype(jnp.int32)
    eid = jnp.arange(N_EXPERTS, dtype=jnp.int32)
    later = jnp.flip(lax.cummin(jnp.flip(jnp.where(counts > 0, eid, N_EXPERTS))))
    nxt = jnp.concatenate([later[1:], jnp.full((2,), N_EXPERTS, jnp.int32)])
    nxt2 = nxt[nxt[:N_EXPERTS]]
    or_none = lambda t: jnp.where(t < N_EXPERTS, t, -1)[block_e].astype(jnp.int32)
    next_e, next2_e = or_none(nxt[:N_EXPERTS]), or_none(nxt2)
    stage = ((jnp.cumsum(first) - 1) % 2).astype(jnp.int32)

    xb = _dispatch(n_flat, run_rows, tot, tail_start, tail_len, nused, hn2, ri)
    yb = _experts(block_e, first, next_e, next2_e, stage, nused, xb, w_exp_gate[0], w_exp_up[0], w_exp_down[0])
    out = _combine(n_flat, run_rows, tot, yb, h1, ri, rw, row2(final_norm_g))
    return out[None]
```

```python
import functools

import jax
import jax.numpy as jnp
import numpy as np
from jax import lax
from jax.experimental import pallas as pl
from jax.experimental.pallas import tpu as pltpu

F32 = jnp.float32
BF16 = jnp.bfloat16

D_MODEL = 1024
SEQ = 16384
N_META = 16
GLA_HEADS = 4
GLA_DK = 128
GLA_DV = 256
GLA_RANK = 16
GLA_TAU = 16.0
CHUNK = 64
QK_W = GLA_HEADS * GLA_DK
V_W = GLA_HEADS * GLA_DV
CONV_WIDTH = 31
N_GROUPS = 8
EXPERTS_PER_GROUP = 8
N_EXPERTS = N_GROUPS * EXPERTS_PER_GROUP
TOP_K = 2
D_EXPERT = 512
EPS = 1e-6
LOG2E = 1.4426950408889634

LANES = 128
SUBLANES = 8
ROW_TILE = D_MODEL // LANES
TOK_ROWS = D_MODEL // 2 // LANES
TOK_ALIGN = SUBLANES // TOK_ROWS

FRONT = 512
LP = FRONT + SEQ
TM_PROJ = 512
DL_COPIES = 6
TM_GLA = 512
TM_MIX = 256
MIX_TILES = 2
HIST = 32
CONV_RB = 16
N_TILES = SEQ // TM_MIX
SLOTS = TOP_K * TM_MIX + N_EXPERTS * (TOK_ALIGN - 1)
TILES_PER_STEP = 2
COMBINE_TILES = 4
GATHER_AHEAD = 3
BM = 256
BLOCKS_PER_STEP = 4
N_BLOCKS = (N_TILES * SLOTS) // BM + N_EXPERTS
P_ROWS = N_BLOCKS * BM
VMEM_LIMIT = 56 * 1024 * 1024


def _sigmoid(x):
    return 0.5 * jnp.tanh(0.5 * x) + 0.5


def _silu(x):
    return x * _sigmoid(x)


def _dot(a, b, **kw):
    return jnp.dot(a, b, preferred_element_type=F32, **kw)


def _load_row_tiles(ref, n_rows):
    return jnp.concatenate([ref[pl.ds(s, n_rows, stride=ROW_TILE), :] for s in range(ROW_TILE)], axis=-1)


def _store_row_tiles(ref, val, n_rows):
    for s in range(ROW_TILE):
        ref[pl.ds(s, n_rows, stride=ROW_TILE), :] = val[:, s * LANES:(s + 1) * LANES]


def _load_token_tiles(ref, n_tok, first_tok=0):
    r0 = first_tok * TOK_ROWS
    w = jnp.concatenate([ref[pl.ds(r0 + s, n_tok, stride=TOK_ROWS), :] for s in range(TOK_ROWS)], axis=-1)
    lo = pltpu.bitcast(w << 16, F32)
    hi = pltpu.bitcast(w & jnp.uint32(0xFFFF0000), F32)
    return jnp.concatenate([lo, hi], axis=-1).astype(BF16)


def _store_token_tiles(ref, val, n_tok, first_tok=0):
    bits = pltpu.bitcast(val, jnp.uint32)
    w = (bits[:, 0:D_MODEL // 2] >> 16) | (bits[:, D_MODEL // 2:D_MODEL] & jnp.uint32(0xFFFF0000))
    r0 = first_tok * TOK_ROWS
    for s in range(TOK_ROWS):
        ref[pl.ds(r0 + s, n_tok, stride=TOK_ROWS), :] = w[:, s * LANES:(s + 1) * LANES]


IN_SPLITS = (QK_W, QK_W, V_W, D_MODEL, GLA_RANK, 2 * D_MODEL, 2 * D_MODEL)
WCAST_ROWS = 128


def _wcast_kernel(w_ref, *outs):
    w = w_ref[...]
    off = 0
    for out, n in zip(outs, IN_SPLITS):
        out[...] = w[:, off:off + n].astype(BF16)
        off += n


def _wcast(w):
    return pl.pallas_call(
        _wcast_kernel,
        grid=(D_MODEL // WCAST_ROWS,),
        in_specs=[pl.BlockSpec((WCAST_ROWS, sum(IN_SPLITS)), lambda i: (i, 0))],
        out_specs=[pl.BlockSpec((WCAST_ROWS, n), lambda i: (i, 0)) for n in IN_SPLITS],
        out_shape=[jax.ShapeDtypeStruct((D_MODEL, n), BF16) for n in IN_SPLITS],
        compiler_params=pltpu.CompilerParams(dimension_semantics=("arbitrary",),
                                             vmem_limit_bytes=VMEM_LIMIT),
        name="wcast",
    )(w)


def _inproj_kernel(front_ref, x_ref, g_ref, wq, wk, wv, wog, wdl, wglu, wgt, wdu, bd, bg,
                   q_o, k_o, v_o, og_o, la_o, u_o, gt_o):
    i = pl.program_id(0)
    h = jnp.where(i < FRONT // TM_PROJ, front_ref[...], x_ref[...])
    ms = jnp.mean(h * h, axis=-1, keepdims=True)
    hn = (h * lax.rsqrt(ms + EPS) * g_ref[...]).astype(BF16)
    q_o[...] = _dot(hn, wq[...]).astype(BF16)
    k_o[...] = _dot(hn, wk[...]).astype(BF16)
    v_o[...] = _dot(hn, wv[...]).astype(BF16)
    og_o[...] = _silu(_dot(hn, wog[...])).astype(BF16)
    glu = _dot(hn, wglu[...])
    _store_row_tiles(u_o, glu[:, 0:D_MODEL] * _sigmoid(glu[:, D_MODEL:2 * D_MODEL]), TM_PROJ)
    gt_o[...] = _sigmoid(_dot(hn, wgt[...]) + bg[...]).astype(BF16)
    d6 = _dot(hn, wdl[...])
    hi = d6.astype(BF16).astype(F32)
    mid = (d6 - hi).astype(BF16).astype(F32)
    lane = lax.broadcasted_iota(jnp.int32, d6.shape, 1)
    pieces = jnp.where(lane < 3 * GLA_RANK, hi, jnp.where(lane < 5 * GLA_RANK, mid, d6 - hi - mid))
    z = _dot(pieces.astype(BF16), wdu[...]) + bd[...]
    la_o[...] = (jnp.minimum(z, 0.0) - jnp.log(1.0 + jnp.exp(-jnp.abs(z)))) * (LOG2E / GLA_TAU)


def _inproj(front, x2d, g1, wq, wk, wv, wog, wdl, wglu, wgt, wdu, bd, bg):
    n_front = FRONT // TM_PROJ
    grid = (LP // TM_PROJ,)
    row = lambda w: pl.BlockSpec((TM_PROJ, w), lambda i: (i, 0))
    full = lambda a: pl.BlockSpec(a.shape, lambda i: (0, 0))
    return pl.pallas_call(
        _inproj_kernel,
        grid=grid,
        in_specs=[
            pl.BlockSpec((TM_PROJ, D_MODEL), lambda i: (jnp.minimum(i, n_front - 1), 0)),
            pl.BlockSpec((TM_PROJ, D_MODEL), lambda i: (jnp.maximum(i - n_front, 0), 0)),
            full(g1), full(wq), full(wk), full(wv), full(wog), full(wdl), full(wglu), full(wgt),
            full(wdu), full(bd), full(bg),
        ],
        out_specs=[row(QK_W), row(QK_W), row(V_W), row(D_MODEL), row(QK_W),
                   pl.BlockSpec((TM_PROJ * ROW_TILE, LANES), lambda i: (i, 0)), row(2 * D_MODEL)],
        out_shape=[
            jax.ShapeDtypeStruct((LP, QK_W), BF16), jax.ShapeDtypeStruct((LP, QK_W), BF16),
            jax.ShapeDtypeStruct((LP, V_W), BF16), jax.ShapeDtypeStruct((LP, D_MODEL), BF16),
            jax.ShapeDtypeStruct((LP, QK_W), F32), jax.ShapeDtypeStruct((LP * ROW_TILE, LANES), F32),
            jax.ShapeDtypeStruct((LP, 2 * D_MODEL), BF16),
        ],
        compiler_params=pltpu.CompilerParams(dimension_semantics=("arbitrary",),
                                             vmem_limit_bytes=VMEM_LIMIT),
        name="inproj",
    )(front, x2d, g1, wq, wk, wv, wog, wdl, wglu, wgt, wdu, bd, bg)


GLA_LEVELS = (32, 16, 8, 4, 2, 1)
GLA_FINE = tuple(m for m in GLA_LEVELS if m < SUBLANES)
LA_SPLIT = 3


def _gla_tables():
    r = np.arange(CHUNK)
    t, c = r[:, None], r[None, :]
    rows = [c <= t]
    pair = []
    for m in GLA_LEVELS:
        mid = (t // (2 * m)) * (2 * m) + m - 1
        right = (t % (2 * m)) >= m
        if m in GLA_FINE:
            rows.append(np.where(right, (c > mid) & (c <= t), (c > t) & (c <= mid)))
        pair.append(((t // (2 * m)) == (c // (2 * m))) & right & ((c % (2 * m)) < m))
    pair.append(t == c)
    sums = np.concatenate(rows, axis=0).astype(np.float32)
    sums = np.concatenate([sums] * LA_SPLIT, axis=1)
    pair = np.stack([np.kron(np.eye(GLA_HEADS), p) for p in pair]).astype(np.float32)
    return jnp.asarray(sums, BF16), jnp.asarray(pair)


def _split_bf16(x):
    pieces = []
    rest = x
    for _ in range(LA_SPLIT):
        p = rest.astype(BF16)
        pieces.append(p)
        rest = rest - p.astype(F32)
    return jnp.concatenate(pieces, axis=0)


def _dot_nt(a, b):
    return lax.dot_general(a, b, (((1,), (1,)), ((), ())), preferred_element_type=F32)


def _dot_tn(a, b):
    return lax.dot_general(a, b, (((0,), (0,)), ((), ())), preferred_element_type=F32)


def _gla_exponents(a):
    b = a[0:CHUNK]
    parts = [b, b[CHUNK - 1:CHUNK] - b]
    for m in GLA_LEVELS:
        if m in GLA_FINE:
            continue
        for lo in range(0, CHUNK, 2 * m):
            ref = b[lo + m - 1:lo + m]
            parts += [ref - b[lo:lo + m], b[lo + m:lo + 2 * m] - ref]
    return jnp.concatenate(parts + [a[CHUNK:]], axis=0)


def _gla_kernel(q_ref, k_ref, v_ref, la_ref, g_ref, sums_ref, pair_ref, o_ref, s_ref):
    @pl.when(pl.program_id(0) == 0)
    def _():
        s_ref[...] = jnp.zeros_like(s_ref)

    row = lax.broadcasted_iota(jnp.int32, (CHUNK, GLA_DK), 0)
    right_rows = [(row & m) != 0 for m in GLA_LEVELS]
    sums = sums_ref[...]
    n_lvl = len(GLA_LEVELS)

    def body(c, carry):
        r0 = pl.multiple_of(c * CHUNK, CHUNK)
        rows = pl.ds(r0, CHUNK)
        states = [s_ref[h] for h in range(GLA_HEADS)]
        qs, ks, es, decs = [], [], [], []
        for hp in range(GLA_HEADS // 2):
            la3 = _split_bf16(la_ref[rows, 2 * hp * GLA_DK:2 * (hp + 1) * GLA_DK])
            a2 = _dot(sums, la3)
            e2 = jnp.exp2(_gla_exponents(a2))
            dec2 = jnp.exp2(jnp.transpose(jnp.broadcast_to(a2[CHUNK - 1:CHUNK], (LANES, 2 * GLA_DK))))
            for h in (2 * hp, 2 * hp + 1):
                half = slice((h % 2) * GLA_DK, (h % 2 + 1) * GLA_DK)
                kc = slice(h * GLA_DK, (h + 1) * GLA_DK)
                qs.append(q_ref[rows, kc].astype(F32) * (GLA_DK ** -0.5))
                ks.append(k_ref[rows, kc].astype(F32))
                es.append(e2[:, half])
                decs.append(dec2[half])
        vbs = [v_ref[rows, h * GLA_DV:(h + 1) * GLA_DV] for h in range(GLA_HEADS)]

        new_states, qes = [], []
        for h in range(GLA_HEADS):
            qes.append((qs[h] * es[h][0:CHUNK]).astype(BF16))
            kd = (ks[h] * es[h][CHUNK:2 * CHUNK]).astype(BF16)
            new_states.append(states[h] * jnp.concatenate([decs[h]] * (GLA_DV // LANES), axis=1)
                              + _dot_tn(kd, vbs[h]))

        stack = lambda parts: jnp.concatenate(parts, axis=0)
        sc = _dot_nt(stack([q.astype(BF16) for q in qs]), stack([k.astype(BF16) for k in ks])) * pair_ref[n_lvl]
        for lvl in range(n_lvl):
            x = stack([(jnp.where(right_rows[lvl], qs[h], ks[h])
                        * es[h][(2 + lvl) * CHUNK:(3 + lvl) * CHUNK]).astype(BF16) for h in range(GLA_HEADS)])
            sc = sc + _dot_nt(x, x) * pair_ref[lvl]
        o_all = _dot(sc.astype(BF16), stack(vbs))
        for h in range(GLA_HEADS):
            vc = slice(h * GLA_DV, (h + 1) * GLA_DV)
            o = o_all[h * CHUNK:(h + 1) * CHUNK] + _dot(qes[h], states[h].astype(BF16))
            o = o * lax.rsqrt(jnp.mean(o * o, axis=-1, keepdims=True) + EPS) * g_ref[:, vc]
            o_ref[rows, vc] = o.astype(BF16)
        for h in range(GLA_HEADS):
            s_ref[h] = new_states[h]
        return carry

    lax.fori_loop(0, TM_GLA // CHUNK, body, 0, unroll=4)


def _gla(q, k, v, la, g):
    sums, pair = _gla_tables()
    row = lambda w: pl.BlockSpec((TM_GLA, w), lambda t: (t, 0))
    full = lambda a: pl.BlockSpec(a.shape, lambda t: (0,) * a.ndim)
    return pl.pallas_call(
        _gla_kernel,
        grid=(LP // TM_GLA,),
        in_specs=[row(QK_W), row(QK_W), row(V_W), row(QK_W), full(g), full(sums), full(pair)],
        out_specs=row(V_W),
        out_shape=jax.ShapeDtypeStruct((LP, V_W), BF16),
        scratch_shapes=[pltpu.VMEM((GLA_HEADS, GLA_DK, GLA_DV), F32)],
        compiler_params=pltpu.CompilerParams(dimension_semantics=("arbitrary",),
                                             vmem_limit_bytes=VMEM_LIMIT),
        name="gla",
    )(q, k, v, la, g, sums, pair)


def _mix_kernel(o_ref, og_ref, u_ref, hist_ref, gt_ref, x_ref, cw_ref, cb_ref, lg_ref, lb_ref,
                wpw_ref, wo_ref, g2_ref, wrh_ref, wrl_ref, br_ref,
                h_o, hn_o, ri_o, rw_o, tabn_o, tabb_o, cbuf, seen):
    @pl.when(pl.program_id(0) == 0)
    def _():
        seen[...] = jnp.zeros_like(seen)

    for j in range(MIX_TILES):
        r = slice(j * TM_MIX, (j + 1) * TM_MIX)
        tab = slice(j * SUBLANES, (j + 1) * SUBLANES)

        def u_row(t, j=j):
            t += j * TM_MIX
            ref, t = (hist_ref, t + HIST) if t < 0 else (u_ref, t)
            return ref[t * ROW_TILE:(t + 1) * ROW_TILE, :]

        _mix_tile(o_ref.at[r], og_ref.at[r], u_row, gt_ref.at[r], x_ref.at[r], cw_ref, cb_ref,
                  lg_ref, lb_ref, wpw_ref, wo_ref, g2_ref, wrh_ref, wrl_ref, br_ref,
                  h_o.at[r], hn_o.at[r], ri_o.at[:, r], rw_o.at[:, r], tabn_o.at[tab], tabb_o.at[tab],
                  cbuf, seen)


def _mix_tile(o_ref, og_ref, u_row, gt_ref, x_ref, cw_ref, cb_ref, lg_ref, lb_ref,
              wpw_ref, wo_ref, g2_ref, wrh_ref, wrl_ref, br_ref,
              h_o, hn_o, ri_o, rw_o, tabn_o, tabb_o, cbuf, seen):
    taps = [cw_ref[jj] for jj in range(CONV_WIDTH)]
    for t0 in range(0, TM_MIX, CONV_RB):
        part = [cb_ref[...]] * CONV_RB
        for rr in range(t0 - (CONV_WIDTH - 1), t0 + CONV_RB):
            ur = u_row(rr)
            for t in range(max(t0, rr), min(t0 + CONV_RB, rr + CONV_WIDTH)):
                part[t - t0] = part[t - t0] + taps[rr - t + CONV_WIDTH - 1] * ur
        for g in range(CONV_RB):
            cbuf[(t0 + g) * ROW_TILE:(t0 + g + 1) * ROW_TILE, :] = part[g]
    acc = _load_row_tiles(cbuf, TM_MIX)

    mu = jnp.mean(acc, axis=-1, keepdims=True)
    xc = acc - mu
    ln = xc * lax.rsqrt(jnp.mean(xc * xc, axis=-1, keepdims=True) + EPS) * lg_ref[...] + lb_ref[...]
    branch_b = _dot(_silu(ln).astype(BF16), wpw_ref[...])

    branch_a = o_ref[...].astype(F32) * og_ref[...].astype(F32)
    g_a = gt_ref[:, 0:D_MODEL].astype(F32)
    g_b = gt_ref[:, D_MODEL:2 * D_MODEL].astype(F32)
    merged = (g_a * branch_a + g_b * branch_b).astype(BF16)
    h1 = x_ref[...] + _dot(merged, wo_ref[...])
    h_o[...] = h1

    hn2 = h1 * lax.rsqrt(jnp.mean(h1 * h1, axis=-1, keepdims=True) + EPS) * g2_ref[...]
    hn_hi = hn2.astype(BF16)
    hn_o[...] = hn_hi

    hn_lo = (hn2 - hn_hi.astype(F32)).astype(BF16)
    logits = (_dot_nt(wrh_ref[...], hn_hi) + _dot_nt(wrh_ref[...], hn_lo) + _dot_nt(wrl_ref[...], hn_hi)
              + br_ref[...])
    row = lax.broadcasted_iota(jnp.int32, logits.shape, 0)
    rowf = row.astype(F32)
    neg = -jnp.inf
    is_g = row < N_GROUPS
    lg = jnp.where(is_g, logits, neg)
    gmax = jnp.max(lg, axis=0, keepdims=True)
    gidx = jnp.min(jnp.where(lg == gmax, rowf, float(N_GROUPS)), axis=0, keepdims=True)
    g_w = 1.0 / jnp.sum(jnp.where(is_g, jnp.exp(lg - gmax), 0.0), axis=0, keepdims=True)
    erow = rowf - float(N_GROUPS)
    egrp = ((row - N_GROUPS) >> 3).astype(F32)
    in_grp = (row >= N_GROUPS) & (row < N_GROUPS + N_EXPERTS) & (egrp == gidx)
    le = jnp.where(in_grp, logits, neg)
    m1 = jnp.max(le, axis=0, keepdims=True)
    i1 = jnp.min(jnp.where(le == m1, erow, float(N_EXPERTS)), axis=0, keepdims=True)
    le2 = jnp.where(erow == i1, neg, le)
    m2 = jnp.max(le2, axis=0, keepdims=True)
    i2 = jnp.min(jnp.where(le2 == m2, erow, float(N_EXPERTS)), axis=0, keepdims=True)
    t = jnp.exp(m2 - m1)
    w1 = g_w / (1.0 + t)
    w2 = g_w * t / (1.0 + t)

    def run_len(n):
        return jnp.floor((n + (TOK_ALIGN - 1)) * (1.0 / TOK_ALIGN)) * TOK_ALIGN

    oh1 = erow == i1
    oh2 = erow == i2
    oh = jnp.where(oh1 | oh2, 1.0, 0.0).astype(BF16)
    tr = lax.broadcasted_iota(jnp.int32, (TM_MIX, TM_MIX), 0)
    tc = lax.broadcasted_iota(jnp.int32, (TM_MIX, TM_MIX), 1)
    before_tok = _dot(oh, (tr < tc).astype(BF16))
    n_col = jnp.sum(oh.astype(F32), axis=1, keepdims=True)
    er = lax.broadcasted_iota(jnp.int32, (LANES, LANES), 0)
    ec = lax.broadcasted_iota(jnp.int32, (LANES, LANES), 1)
    before_exp = _dot((ec < er).astype(BF16), jnp.broadcast_to(run_len(n_col), logits.shape).astype(BF16))
    where_to = before_exp + before_tok
    q1 = jnp.sum(jnp.where(oh1, where_to, 0.0), axis=0, keepdims=True)
    q2 = jnp.sum(jnp.where(oh2, where_to, 0.0), axis=0, keepdims=True)

    n_rows = run_len(_dot_nt(jnp.ones((SUBLANES, TM_MIX), BF16), oh))
    tabn_o[...] = n_rows
    tabb_o[...] = seen[...]
    seen[...] = seen[...] + n_rows

    ri_o[...] = jnp.zeros_like(ri_o)
    ri_o[0:1, :] = q1.astype(jnp.int32)
    ri_o[1:2, :] = q2.astype(jnp.int32)
    rw_o[...] = jnp.zeros_like(rw_o)
    rw_o[0:1, :] = w1
    rw_o[1:2, :] = w2


def _mix(o_n, og, u, gates, x2d, cw, cb, lg, lb, wpw, wo, g2, wrh, wrl, br):
    tm = MIX_TILES * TM_MIX
    n_front = FRONT // tm
    grid = (SEQ // tm,)
    rowp = lambda w: pl.BlockSpec((tm, w), lambda i: (i + n_front, 0))
    full = lambda a: pl.BlockSpec(a.shape, lambda i: (0,) * a.ndim)
    hist_blocks = tm // HIST
    return pl.pallas_call(
        _mix_kernel,
        grid=grid,
        in_specs=[
            rowp(V_W), rowp(D_MODEL),
            pl.BlockSpec((tm * ROW_TILE, LANES), lambda i: (i + n_front, 0)),
            pl.BlockSpec((HIST * ROW_TILE, LANES), lambda i: ((i + n_front) * hist_blocks - 1, 0)),
            rowp(2 * D_MODEL),
            pl.BlockSpec((tm, D_MODEL), lambda i: (i, 0)),
            full(cw), full(cb), full(lg), full(lb), full(wpw), full(wo), full(g2),
            full(wrh), full(wrl), full(br),
        ],
        out_specs=[
            pl.BlockSpec((tm, D_MODEL), lambda i: (i, 0)),
            pl.BlockSpec((tm, D_MODEL), lambda i: (i, 0)),
            pl.BlockSpec((SUBLANES, tm), lambda i: (0, i)),
            pl.BlockSpec((SUBLANES, tm), lambda i: (0, i)),
            pl.BlockSpec((MIX_TILES * SUBLANES, LANES), lambda i: (i, 0)),
            pl.BlockSpec((MIX_TILES * SUBLANES, LANES), lambda i: (i, 0)),
        ],
        out_shape=[
            jax.ShapeDtypeStruct((SEQ, D_MODEL), F32),
            jax.ShapeDtypeStruct((SEQ, D_MODEL), BF16),
            jax.ShapeDtypeStruct((SUBLANES, SEQ), jnp.int32),
            jax.ShapeDtypeStruct((SUBLANES, SEQ), F32),
            jax.ShapeDtypeStruct((N_TILES * SUBLANES, LANES), F32),
            jax.ShapeDtypeStruct((N_TILES * SUBLANES, LANES), F32),
        ],
        scratch_shapes=[pltpu.VMEM((TM_MIX * ROW_TILE, LANES), F32), pltpu.VMEM((SUBLANES, LANES), F32)],
        compiler_params=pltpu.CompilerParams(dimension_semantics=("arbitrary",),
                                             vmem_limit_bytes=VMEM_LIMIT),
        name="mix",
    )(o_n, og, u, u, gates, x2d, cw, cb, lg, lb, wpw, wo, g2, wrh, wrl, br)


def _tok_rows(start_tok, n_tok):
    return pl.ds(pl.multiple_of(start_tok * TOK_ROWS, SUBLANES), n_tok * TOK_ROWS)


def _start_tile_runs(n_ref, row_ref, tile, make_copy):
    def run(e, off):
        n = n_ref[tile * N_EXPERTS + e]

        @pl.when(n > 0)
        def _():
            make_copy(_tok_rows(off, n), _tok_rows(row_ref[tile * N_EXPERTS + e], n)).start()

        return off + n

    lax.fori_loop(0, N_EXPERTS, run, 0, unroll=8)


def _dispatch_kernel(n_ref, row_ref, tot_ref, ts_ref, tl_ref, nused_ref, hn_ref, q_ref, xb_ref,
                     sb0, sb1, zbuf, sem, fsem):
    step = pl.program_id(0)
    last = pl.num_programs(0) - 1

    def fills(start):
        def tail(e, c):
            n = tl_ref[e]

            @pl.when(n > 0)
            def _():
                cp = pltpu.make_async_copy(zbuf.at[pl.ds(0, n * TOK_ROWS)],
                                           xb_ref.at[_tok_rows(ts_ref[e], n)], fsem)
                cp.start() if start else cp.wait()

            return c

        lax.fori_loop(0, N_EXPERTS, tail, 0)

        def block(b, c):
            cp = pltpu.make_async_copy(zbuf, xb_ref.at[_tok_rows(b * BM, BM)], fsem)
            cp.start() if start else cp.wait()
            return c

        lax.fori_loop(nused_ref[0], N_BLOCKS, block, 0)

    @pl.when(step == 0)
    def _():
        zbuf[...] = jnp.zeros_like(zbuf)
        fills(start=True)

    def tile_runs(tile, s, sbuf):
        rows = pl.ds(0, tot_ref[tile] * TOK_ROWS)
        return pltpu.make_async_copy(sbuf.at[rows], xb_ref.at[rows], sem.at[s])

    for s, sbuf in enumerate((sb0, sb1)):
        tile = step * TILES_PER_STEP + s
        toks = slice(s * TM_MIX, (s + 1) * TM_MIX)
        slot_i = lax.broadcasted_iota(jnp.int32, (SLOTS, TM_MIX), 0)
        onehot = jnp.where((slot_i == q_ref[0:1, toks]) | (slot_i == q_ref[1:2, toks]), 1.0, 0.0).astype(BF16)
        srt = _dot(onehot, hn_ref[toks, :])

        @pl.when(step >= 1)
        def _():
            tile_runs(tile - TILES_PER_STEP, s, sbuf).wait()

        _store_token_tiles(sbuf, srt, SLOTS)
        _start_tile_runs(n_ref, row_ref, tile,
                         lambda loc, glob: pltpu.make_async_copy(sbuf.at[loc], xb_ref.at[glob], sem.at[s]))

    @pl.when(step == last)
    def _():
        for s, sbuf in enumerate((sb0, sb1)):
            tile_runs(step * TILES_PER_STEP + s, s, sbuf).wait()
        fills(start=False)


def _dispatch(n_flat, run_rows, tot, tail_start, tail_len, nused, hn2, ri):
    grid_spec = pltpu.PrefetchScalarGridSpec(
        num_scalar_prefetch=6,
        grid=(N_TILES // TILES_PER_STEP,),
        in_specs=[pl.BlockSpec((TILES_PER_STEP * TM_MIX, D_MODEL), lambda i, *_: (i, 0)),
                  pl.BlockSpec((SUBLANES, TILES_PER_STEP * TM_MIX), lambda i, *_: (0, i))],
        out_specs=pl.BlockSpec(memory_space=pl.ANY),
        scratch_shapes=[pltpu.VMEM((SLOTS * TOK_ROWS, LANES), jnp.uint32),
                        pltpu.VMEM((SLOTS * TOK_ROWS, LANES), jnp.uint32),
                        pltpu.VMEM((BM * TOK_ROWS, LANES), jnp.uint32),
                        pltpu.SemaphoreType.DMA((2,)), pltpu.SemaphoreType.DMA(())],
    )
    return pl.pallas_call(
        _dispatch_kernel,
        grid_spec=grid_spec,
        out_shape=jax.ShapeDtypeStruct((P_ROWS * TOK_ROWS, LANES), jnp.uint32),
        compiler_params=pltpu.CompilerParams(dimension_semantics=("arbitrary",),
                                             vmem_limit_bytes=VMEM_LIMIT),
        name="dispatch",
    )(n_flat, run_rows, tot, tail_start, tail_len, nused, hn2, ri)


def _experts_kernel(be_ref, first_ref, next_ref, next2_ref, stage_ref, nused_ref,
                    x_ref, wg_hbm, wu_hbm, wd_hbm, y_ref, wg_f, wu_f, wd_f, wg_b, wu_b, wd_b, sem):
    def weight_copies(e, st):
        return (pltpu.make_async_copy(wg_hbm.at[e], wg_f.at[st], sem.at[st, 0]),
                pltpu.make_async_copy(wu_hbm.at[e], wu_f.at[st], sem.at[st, 1]),
                pltpu.make_async_copy(wd_hbm.at[e], wd_f.at[st], sem.at[st, 2]))

    @pl.when((pl.program_id(0) == 0) & (nused_ref[0] > 0))
    def _():
        for c in weight_copies(be_ref[0], 0):
            c.start()

        @pl.when(next_ref[0] >= 0)
        def _():
            for c in weight_copies(next_ref[0], 1):
                c.start()

    for j in range(BLOCKS_PER_STEP):
        _expert_block(pl.program_id(0) * BLOCKS_PER_STEP + j, j * BM, be_ref, first_ref, next2_ref,
                      stage_ref, nused_ref, x_ref, y_ref, wg_f, wu_f, wd_f, wg_b, wu_b, wd_b, weight_copies)


def _expert_block(b, tok0, be_ref, first_ref, next2_ref, stage_ref, nused_ref, x_ref, y_ref,
                  wg_f, wu_f, wd_f, wg_b, wu_b, wd_b, weight_copies):
    @pl.when(b < nused_ref[0])
    def _():
        def new_expert(st):
            for c in weight_copies(be_ref[b], st):
                c.wait()
            wg_b[...] = wg_f[st].astype(BF16)
            wu_b[...] = wu_f[st].astype(BF16)
            wd_b[...] = wd_f[st].astype(BF16)

            @pl.when(next2_ref[b] >= 0)
            def _():
                for c in weight_copies(next2_ref[b], st):
                    c.start()

        for st in range(2):
            pl.when((first_ref[b] == 1) & (stage_ref[b] == st))(functools.partial(new_expert, st))

        x = _load_token_tiles(x_ref, BM, tok0)
        a = _dot(x, wg_b[...])
        u = _dot(x, wu_b[...])
        y = _dot((_silu(a) * u).astype(BF16), wd_b[...])
        _store_token_tiles(y_ref, y.astype(BF16).astype(F32), BM, tok0)

    @pl.when(b >= nused_ref[0])
    def _():
        y_ref[tok0 * TOK_ROWS:(tok0 + BM) * TOK_ROWS, :] = jnp.zeros((BM * TOK_ROWS, LANES), jnp.uint32)


def _experts(block_e, first, next_e, next2_e, stage, nused, xb, wg, wu, wd):
    def xmap(b, *_):
        return (b, 0)

    grid_spec = pltpu.PrefetchScalarGridSpec(
        num_scalar_prefetch=6,
        grid=(N_BLOCKS // BLOCKS_PER_STEP,),
        in_specs=[
            pl.BlockSpec((BLOCKS_PER_STEP * BM * TOK_ROWS, LANES), xmap),
            pl.BlockSpec(memory_space=pl.ANY),
            pl.BlockSpec(memory_space=pl.ANY),
            pl.BlockSpec(memory_space=pl.ANY),
        ],
        out_specs=pl.BlockSpec((BLOCKS_PER_STEP * BM * TOK_ROWS, LANES), xmap),
        scratch_shapes=[pltpu.VMEM((2, D_MODEL, D_EXPERT), F32), pltpu.VMEM((2, D_MODEL, D_EXPERT), F32),
                        pltpu.VMEM((2, D_EXPERT, D_MODEL), F32),
                        pltpu.VMEM((D_MODEL, D_EXPERT), BF16), pltpu.VMEM((D_MODEL, D_EXPERT), BF16),
                        pltpu.VMEM((D_EXPERT, D_MODEL), BF16), pltpu.SemaphoreType.DMA((2, 3))],
    )
    return pl.pallas_call(
        _experts_kernel,
        grid_spec=grid_spec,
        out_shape=jax.ShapeDtypeStruct((P_ROWS * TOK_ROWS, LANES), jnp.uint32),
        compiler_params=pltpu.CompilerParams(dimension_semantics=("arbitrary",),
                                             vmem_limit_bytes=VMEM_LIMIT),
        name="experts",
    )(block_e, first, next_e, next2_e, stage, nused, xb, wg, wu, wd)


def _combine_kernel(n_ref, row_ref, tot_ref, yb_ref, h_ref, q_ref, w_ref, g_ref, out_ref, *scratch):
    step = pl.program_id(0)
    bufs, sem = scratch[:COMBINE_TILES], scratch[COMBINE_TILES]

    def start_gather(tile, s):
        _start_tile_runs(n_ref, row_ref, tile,
                         lambda loc, glob: pltpu.make_async_copy(yb_ref.at[glob], bufs[s].at[loc], sem.at[s]))

    @pl.when(step == 0)
    def _():
        for buf in bufs:
            buf[...] = jnp.zeros_like(buf)
        for s in range(GATHER_AHEAD):
            start_gather(s, s)

    for s in range(COMBINE_TILES):
        tile = step * COMBINE_TILES + s
        ahead = s + GATHER_AHEAD
        if ahead < COMBINE_TILES:
            start_gather(tile + GATHER_AHEAD, ahead)
        else:
            pl.when(step + 1 < pl.num_programs(0))(
                functools.partial(start_gather, tile + GATHER_AHEAD, ahead - COMBINE_TILES))

        rows = pl.ds(0, tot_ref[tile] * TOK_ROWS)
        pltpu.make_async_copy(yb_ref.at[rows], bufs[s].at[rows], sem.at[s]).wait()
        ys = _load_token_tiles(bufs[s], SLOTS)
        toks = slice(s * TM_MIX, (s + 1) * TM_MIX)
        slot_i = lax.broadcasted_iota(jnp.int32, (SLOTS, TM_MIX), 0)
        wmat = (jnp.where(slot_i == q_ref[0:1, toks], w_ref[0:1, toks], 0.0)
                + jnp.where(slot_i == q_ref[1:2, toks], w_ref[1:2, toks], 0.0)).astype(BF16)
        hh = h_ref[toks, :] + _dot_tn(wmat, ys)
        out_ref[toks, :] = hh * lax.rsqrt(jnp.mean(hh * hh, axis=-1, keepdims=True) + EPS) * g_ref[...]


def _combine(n_flat, run_rows, tot, yb, h1, ri, rw, gf):
    grid_spec = pltpu.PrefetchScalarGridSpec(
        num_scalar_prefetch=3,
        grid=(N_TILES // COMBINE_TILES,),
        in_specs=[
            pl.BlockSpec(memory_space=pl.ANY),
            pl.BlockSpec((COMBINE_TILES * TM_MIX, D_MODEL), lambda i, *_: (i, 0)),
            pl.BlockSpec((SUBLANES, COMBINE_TILES * TM_MIX), lambda i, *_: (0, i)),
            pl.BlockSpec((SUBLANES, COMBINE_TILES * TM_MIX), lambda i, *_: (0, i)),
            pl.BlockSpec((1, D_MODEL), lambda i, *_: (0, 0)),
        ],
        out_specs=pl.BlockSpec((COMBINE_TILES * TM_MIX, D_MODEL), lambda i, *_: (i, 0)),
        scratch_shapes=[pltpu.VMEM((SLOTS * TOK_ROWS, LANES), jnp.uint32)] * COMBINE_TILES
                       + [pltpu.SemaphoreType.DMA((COMBINE_TILES,))],
    )
    return pl.pallas_call(
        _combine_kernel,
        grid_spec=grid_spec,
        out_shape=jax.ShapeDtypeStruct((SEQ, D_MODEL), F32),
        compiler_params=pltpu.CompilerParams(dimension_semantics=("arbitrary",),
                                             vmem_limit_bytes=VMEM_LIMIT),
        name="combine",
    )(n_flat, run_rows, tot, yb, h1, ri, rw, gf)


def kernel(x, meta, norm1_g, w_in, w_decay_up, b_decay, gla_norm_g, conv_w, conv_b, conv_ln_g,
           conv_ln_b, w_pw2, b_gate, w_out, norm2_g, w_router_group, b_router_group,
           w_router_expert, b_router_expert, w_exp_gate, w_exp_up, w_exp_down, final_norm_g):
    assert x.shape == (1, SEQ, D_MODEL) and w_in.shape[0] == 1
    x2d = x[0]
    front = jnp.concatenate([jnp.zeros((FRONT - N_META, D_MODEL), F32), meta.astype(F32)], axis=0)

    w = w_in[0]
    row2 = lambda a: a.reshape(1, -1).astype(F32)
    wq, wk, wv, wog, wdl, wglu, wgt = _wcast(w)
    up = w_decay_up[0].astype(F32)
    up_hi = up.astype(BF16)
    up_mid = (up - up_hi.astype(F32)).astype(BF16)
    up_lo = (up - up_hi.astype(F32) - up_mid.astype(F32)).astype(BF16)
    wdu = jnp.concatenate([up_hi, up_mid, up_lo, up_hi, up_mid, up_hi], axis=0)
    q, k, v, og, la, u, gates = _inproj(
        front, x2d, row2(norm1_g[0]), wq, wk, wv, wog, jnp.tile(wdl, (1, DL_COPIES)), wglu, wgt,
        wdu, row2(b_decay[0]), row2(b_gate[0]))

    o_n = _gla(q, k, v, la, row2(gla_norm_g[0]))

    wr = jnp.concatenate([w_router_group[0].T, w_router_expert[0].T,
                          jnp.zeros((LANES - N_GROUPS - N_EXPERTS, D_MODEL), F32)], axis=0).astype(F32)
    br = jnp.concatenate([b_router_group[0], b_router_expert[0],
                          jnp.zeros((LANES - N_GROUPS - N_EXPERTS,), F32)]).reshape(LANES, 1).astype(F32)
    wr_hi = wr.astype(BF16)
    wr_lo = (wr - wr_hi.astype(F32)).astype(BF16)
    h1, hn2, ri, rw, tabn, tabb = _mix(
        o_n, og, u, gates, x2d, conv_w[0].astype(F32).reshape(CONV_WIDTH, ROW_TILE, LANES),
        conv_b[0].astype(F32).reshape(ROW_TILE, LANES), row2(conv_ln_g[0]),
        row2(conv_ln_b[0]), w_pw2[0].astype(BF16), w_out[0].astype(BF16),
        row2(norm2_g[0]), wr_hi, wr_lo, br)

    experts = slice(N_GROUPS, N_GROUPS + N_EXPERTS)
    n_te = tabn[::SUBLANES, experts].astype(jnp.int32)
    seen_te = tabb[::SUBLANES, experts].astype(jnp.int32)
    counts = seen_te[-1] + n_te[-1]
    padded = (counts + BM - 1) // BM * BM
    pad_end = jnp.cumsum(padded)
    pad_start = pad_end - padded
    n_flat = n_te.reshape(-1)
    tot = jnp.sum(n_te, axis=1).astype(jnp.int32)
    run_rows = (pad_start[None, :] + seen_te).reshape(-1).astype(jnp.int32)
    tail_start = (pad_start + counts).astype(jnp.int32)
    tail_len = (padded - counts).astype(jnp.int32)
    blk = jnp.arange(N_BLOCKS, dtype=jnp.int32)
    block_e = jnp.minimum(jnp.sum((pad_end[None, :] <= blk[:, None] * BM).astype(jnp.int32), axis=1),
                          N_EXPERTS - 1).astype(jnp.int32)
    first = jnp.concatenate([jnp.ones((1,), jnp.int32), (block_e[1:] != block_e[:-1]).astype(jnp.int32)])
    nused = (pad_end[-1:] // BM).astype(jnp.int32)
    eid = jnp.arange(N_EXPERTS, dtype=jnp.int32)
    later = jnp.flip(lax.cummin(jnp.flip(jnp.where(counts > 0, eid, N_EXPERTS))))
    nxt = jnp.concatenate([later[1:], jnp.full((2,), N_EXPERTS, jnp.int32)])
    nxt2 = nxt[nxt[:N_EXPERTS]]
    or_none = lambda t: jnp.where(t < N_EXPERTS, t, -1)[block_e].astype(jnp.int32)
    next_e, next2_e = or_none(nxt[:N_EXPERTS]), or_none(nxt2)
    stage = ((jnp.cumsum(first) - 1) % 2).astype(jnp.int32)

    xb = _dispatch(n_flat, run_rows, tot, tail_start, tail_len, nused, hn2, ri)
    yb = _experts(block_e, first, next_e, next2_e, stage, nused, xb, w_exp_gate[0], w_exp_up[0], w_exp_down[0])
    out = _combine(n_flat, run_rows, tot, yb, h1, ri, rw, row2(final_norm_g))
    return out[None]
```

```python
import functools

import jax
import jax.numpy as jnp
import numpy as np
from jax import lax
from jax.experimental import pallas as pl
from jax.experimental.pallas import tpu as pltpu

F32 = jnp.float32
BF16 = jnp.bfloat16

D_MODEL = 1024
SEQ = 16384
N_META = 16
GLA_HEADS = 4
GLA_DK = 128
GLA_DV = 256
GLA_RANK = 16
GLA_TAU = 16.0
CHUNK = 64
QK_W = GLA_HEADS * GLA_DK
V_W = GLA_HEADS * GLA_DV
CONV_WIDTH = 31
N_GROUPS = 8
EXPERTS_PER_GROUP = 8
N_EXPERTS = N_GROUPS * EXPERTS_PER_GROUP
TOP_K = 2
D_EXPERT = 512
EPS = 1e-6
LOG2E = 1.4426950408889634

LANES = 128
SUBLANES = 8
ROW_TILE = D_MODEL // LANES
TOK_ROWS = D_MODEL // 2 // LANES
TOK_ALIGN = SUBLANES // TOK_ROWS

FRONT = 512
LP = FRONT + SEQ
TM_PROJ = 512
DL_COPIES = 6
TM_GLA = 512
TM_MIX = 256
MIX_TILES = 2
HIST = 32
CONV_RB = 16
N_TILES = SEQ // TM_MIX
SLOTS = TOP_K * TM_MIX + N_EXPERTS * (TOK_ALIGN - 1)
TILES_PER_STEP = 2
COMBINE_TILES = 8
GATHER_AHEAD = 6
BM = 256
BLOCKS_PER_STEP = 4
N_BLOCKS = (N_TILES * SLOTS) // BM + N_EXPERTS
P_ROWS = N_BLOCKS * BM
VMEM_LIMIT = 56 * 1024 * 1024


def _sigmoid(x):
    return 0.5 * jnp.tanh(0.5 * x) + 0.5


def _silu(x):
    return x * _sigmoid(x)


def _dot(a, b, **kw):
    return jnp.dot(a, b, preferred_element_type=F32, **kw)


def _load_row_tiles(ref, n_rows):
    return jnp.concatenate([ref[pl.ds(s, n_rows, stride=ROW_TILE), :] for s in range(ROW_TILE)], axis=-1)


def _store_row_tiles(ref, val, n_rows):
    for s in range(ROW_TILE):
        ref[pl.ds(s, n_rows, stride=ROW_TILE), :] = val[:, s * LANES:(s + 1) * LANES]


def _load_token_tiles(ref, n_tok, first_tok=0):
    r0 = first_tok * TOK_ROWS
    w = jnp.concatenate([ref[pl.ds(r0 + s, n_tok, stride=TOK_ROWS), :] for s in range(TOK_ROWS)], axis=-1)
    lo = pltpu.bitcast(w << 16, F32)
    hi = pltpu.bitcast(w & jnp.uint32(0xFFFF0000), F32)
    return jnp.concatenate([lo, hi], axis=-1).astype(BF16)


def _store_token_tiles(ref, val, n_tok, first_tok=0):
    bits = pltpu.bitcast(val, jnp.uint32)
    w = (bits[:, 0:D_MODEL // 2] >> 16) | (bits[:, D_MODEL // 2:D_MODEL] & jnp.uint32(0xFFFF0000))
    r0 = first_tok * TOK_ROWS
    for s in range(TOK_ROWS):
        ref[pl.ds(r0 + s, n_tok, stride=TOK_ROWS), :] = w[:, s * LANES:(s + 1) * LANES]


IN_SPLITS = (QK_W, QK_W, V_W, D_MODEL, GLA_RANK, 2 * D_MODEL, 2 * D_MODEL)
WCAST_ROWS = 128


def _wcast_kernel(w_ref, *outs):
    w = w_ref[...]
    off = 0
    for out, n in zip(outs, IN_SPLITS):
        out[...] = w[:, off:off + n].astype(BF16)
        off += n


def _wcast(w):
    return pl.pallas_call(
        _wcast_kernel,
        grid=(D_MODEL // WCAST_ROWS,),
        in_specs=[pl.BlockSpec((WCAST_ROWS, sum(IN_SPLITS)), lambda i: (i, 0))],
        out_specs=[pl.BlockSpec((WCAST_ROWS, n), lambda i: (i, 0)) for n in IN_SPLITS],
        out_shape=[jax.ShapeDtypeStruct((D_MODEL, n), BF16) for n in IN_SPLITS],
        compiler_params=pltpu.CompilerParams(dimension_semantics=("arbitrary",),
                                             vmem_limit_bytes=VMEM_LIMIT),
        name="wcast",
    )(w)


def _inproj_kernel(front_ref, x_ref, g_ref, wq, wk, wv, wog, wdl, wglu, wgt, wdu, bd, bg,
                   q_o, k_o, v_o, og_o, la_o, u_o, gt_o):
    i = pl.program_id(0)
    h = jnp.where(i < FRONT // TM_PROJ, front_ref[...], x_ref[...])
    ms = jnp.mean(h * h, axis=-1, keepdims=True)
    hn = (h * lax.rsqrt(ms + EPS) * g_ref[...]).astype(BF16)
    q_o[...] = _dot(hn, wq[...]).astype(BF16)
    k_o[...] = _dot(hn, wk[...]).astype(BF16)
    v_o[...] = _dot(hn, wv[...]).astype(BF16)
    og_o[...] = _silu(_dot(hn, wog[...])).astype(BF16)
    glu = _dot(hn, wglu[...])
    _store_row_tiles(u_o, glu[:, 0:D_MODEL] * _sigmoid(glu[:, D_MODEL:2 * D_MODEL]), TM_PROJ)
    gt_o[...] = _sigmoid(_dot(hn, wgt[...]) + bg[...]).astype(BF16)
    d6 = _dot(hn, wdl[...])
    hi = d6.astype(BF16).astype(F32)
    mid = (d6 - hi).astype(BF16).astype(F32)
    lane = lax.broadcasted_iota(jnp.int32, d6.shape, 1)
    pieces = jnp.where(lane < 3 * GLA_RANK, hi, jnp.where(lane < 5 * GLA_RANK, mid, d6 - hi - mid))
    z = _dot(pieces.astype(BF16), wdu[...]) + bd[...]
    la_o[...] = (jnp.minimum(z, 0.0) - jnp.log(1.0 + jnp.exp(-jnp.abs(z)))) * (LOG2E / GLA_TAU)


def _inproj(front, x2d, g1, wq, wk, wv, wog, wdl, wglu, wgt, wdu, bd, bg):
    n_front = FRONT // TM_PROJ
    grid = (LP // TM_PROJ,)
    row = lambda w: pl.BlockSpec((TM_PROJ, w), lambda i: (i, 0))
    full = lambda a: pl.BlockSpec(a.shape, lambda i: (0, 0))
    return pl.pallas_call(
        _inproj_kernel,
        grid=grid,
        in_specs=[
            pl.BlockSpec((TM_PROJ, D_MODEL), lambda i: (jnp.minimum(i, n_front - 1), 0)),
            pl.BlockSpec((TM_PROJ, D_MODEL), lambda i: (jnp.maximum(i - n_front, 0), 0)),
            full(g1), full(wq), full(wk), full(wv), full(wog), full(wdl), full(wglu), full(wgt),
            full(wdu), full(bd), full(bg),
        ],
        out_specs=[row(QK_W), row(QK_W), row(V_W), row(D_MODEL), row(QK_W),
                   pl.BlockSpec((TM_PROJ * ROW_TILE, LANES), lambda i: (i, 0)), row(2 * D_MODEL)],
        out_shape=[
            jax.ShapeDtypeStruct((LP, QK_W), BF16), jax.ShapeDtypeStruct((LP, QK_W), BF16),
            jax.ShapeDtypeStruct((LP, V_W), BF16), jax.ShapeDtypeStruct((LP, D_MODEL), BF16),
            jax.ShapeDtypeStruct((LP, QK_W), F32), jax.ShapeDtypeStruct((LP * ROW_TILE, LANES), F32),
            jax.ShapeDtypeStruct((LP, 2 * D_MODEL), BF16),
        ],
        compiler_params=pltpu.CompilerParams(dimension_semantics=("arbitrary",),
                                             vmem_limit_bytes=VMEM_LIMIT),
        name="inproj",
    )(front, x2d, g1, wq, wk, wv, wog, wdl, wglu, wgt, wdu, bd, bg)


GLA_LEVELS = (32, 16, 8, 4, 2, 1)
GLA_FINE = tuple(m for m in GLA_LEVELS if m < SUBLANES)
LA_SPLIT = 3


def _gla_tables():
    r = np.arange(CHUNK)
    t, c = r[:, None], r[None, :]
    rows = [c <= t]
    pair = []
    for m in GLA_LEVELS:
        mid = (t // (2 * m)) * (2 * m) + m - 1
        right = (t % (2 * m)) >= m
        if m in GLA_FINE:
            rows.append(np.where(right, (c > mid) & (c <= t), (c > t) & (c <= mid)))
        pair.append(((t // (2 * m)) == (c // (2 * m))) & right & ((c % (2 * m)) < m))
    pair.append(t == c)
    sums = np.concatenate(rows, axis=0).astype(np.float32)
    sums = np.concatenate([sums] * LA_SPLIT, axis=1)
    pair = np.stack([np.kron(np.eye(GLA_HEADS), p) for p in pair]).astype(np.float32)
    return jnp.asarray(sums, BF16), jnp.asarray(pair)


def _split_bf16(x):
    pieces = []
    rest = x
    for _ in range(LA_SPLIT):
        p = rest.astype(BF16)
        pieces.append(p)
        rest = rest - p.astype(F32)
    return jnp.concatenate(pieces, axis=0)


def _dot_nt(a, b):
    return lax.dot_general(a, b, (((1,), (1,)), ((), ())), preferred_element_type=F32)


def _dot_tn(a, b):
    return lax.dot_general(a, b, (((0,), (0,)), ((), ())), preferred_element_type=F32)


def _gla_exponents(a):
    b = a[0:CHUNK]
    parts = [b, b[CHUNK - 1:CHUNK] - b]
    for m in GLA_LEVELS:
        if m in GLA_FINE:
            continue
        for lo in range(0, CHUNK, 2 * m):
            ref = b[lo + m - 1:lo + m]
            parts += [ref - b[lo:lo + m], b[lo + m:lo + 2 * m] - ref]
    return jnp.concatenate(parts + [a[CHUNK:]], axis=0)


def _gla_kernel(q_ref, k_ref, v_ref, la_ref, g_ref, sums_ref, pair_ref, o_ref, s_ref):
    @pl.when(pl.program_id(0) == 0)
    def _():
        s_ref[...] = jnp.zeros_like(s_ref)

    row = lax.broadcasted_iota(jnp.int32, (CHUNK, GLA_DK), 0)
    right_rows = [(row & m) != 0 for m in GLA_LEVELS]
    sums = sums_ref[...]
    n_lvl = len(GLA_LEVELS)

    def body(c, carry):
        r0 = pl.multiple_of(c * CHUNK, CHUNK)
        rows = pl.ds(r0, CHUNK)
        states = [s_ref[h] for h in range(GLA_HEADS)]
        qs, ks, es, decs = [], [], [], []
        for hp in range(GLA_HEADS // 2):
            la3 = _split_bf16(la_ref[rows, 2 * hp * GLA_DK:2 * (hp + 1) * GLA_DK])
            a2 = _dot(sums, la3)
            e2 = jnp.exp2(_gla_exponents(a2))
            dec2 = jnp.exp2(jnp.transpose(jnp.broadcast_to(a2[CHUNK - 1:CHUNK], (LANES, 2 * GLA_DK))))
            for h in (2 * hp, 2 * hp + 1):
                half = slice((h % 2) * GLA_DK, (h % 2 + 1) * GLA_DK)
                kc = slice(h * GLA_DK, (h + 1) * GLA_DK)
                qs.append(q_ref[rows, kc].astype(F32) * (GLA_DK ** -0.5))
                ks.append(k_ref[rows, kc].astype(F32))
                es.append(e2[:, half])
                decs.append(dec2[half])
        vbs = [v_ref[rows, h * GLA_DV:(h + 1) * GLA_DV] for h in range(GLA_HEADS)]

        new_states, qes = [], []
        for h in range(GLA_HEADS):
            qes.append((qs[h] * es[h][0:CHUNK]).astype(BF16))
            kd = (ks[h] * es[h][CHUNK:2 * CHUNK]).astype(BF16)
            new_states.append(states[h] * jnp.concatenate([decs[h]] * (GLA_DV // LANES), axis=1)
                              + _dot_tn(kd, vbs[h]))

        stack = lambda parts: jnp.concatenate(parts, axis=0)
        sc = _dot_nt(stack([q.astype(BF16) for q in qs]), stack([k.astype(BF16) for k in ks])) * pair_ref[n_lvl]
        for lvl in range(n_lvl):
            x = stack([(jnp.where(right_rows[lvl], qs[h], ks[h])
                        * es[h][(2 + lvl) * CHUNK:(3 + lvl) * CHUNK]).astype(BF16) for h in range(GLA_HEADS)])
            sc = sc + _dot_nt(x, x) * pair_ref[lvl]
        o_all = _dot(sc.astype(BF16), stack(vbs))
        for h in range(GLA_HEADS):
            vc = slice(h * GLA_DV, (h + 1) * GLA_DV)
            o = o_all[h * CHUNK:(h + 1) * CHUNK] + _dot(qes[h], states[h].astype(BF16))
            o = o * lax.rsqrt(jnp.mean(o * o, axis=-1, keepdims=True) + EPS) * g_ref[:, vc]
            o_ref[rows, vc] = o.astype(BF16)
        for h in range(GLA_HEADS):
            s_ref[h] = new_states[h]
        return carry

    lax.fori_loop(0, TM_GLA // CHUNK, body, 0, unroll=4)


def _gla(q, k, v, la, g):
    sums, pair = _gla_tables()
    row = lambda w: pl.BlockSpec((TM_GLA, w), lambda t: (t, 0))
    full = lambda a: pl.BlockSpec(a.shape, lambda t: (0,) * a.ndim)
    return pl.pallas_call(
        _gla_kernel,
        grid=(LP // TM_GLA,),
        in_specs=[row(QK_W), row(QK_W), row(V_W), row(QK_W), full(g), full(sums), full(pair)],
        out_specs=row(V_W),
        out_shape=jax.ShapeDtypeStruct((LP, V_W), BF16),
        scratch_shapes=[pltpu.VMEM((GLA_HEADS, GLA_DK, GLA_DV), F32)],
        compiler_params=pltpu.CompilerParams(dimension_semantics=("arbitrary",),
                                             vmem_limit_bytes=VMEM_LIMIT),
        name="gla",
    )(q, k, v, la, g, sums, pair)


def _mix_kernel(o_ref, og_ref, u_ref, hist_ref, gt_ref, x_ref, cw_ref, cb_ref, lg_ref, lb_ref,
                wpw_ref, wo_ref, g2_ref, wrh_ref, wrl_ref, br_ref,
                h_o, hn_o, ri_o, rw_o, tabn_o, tabb_o, cbuf, seen):
    @pl.when(pl.program_id(0) == 0)
    def _():
        seen[...] = jnp.zeros_like(seen)

    for j in range(MIX_TILES):
        r = slice(j * TM_MIX, (j + 1) * TM_MIX)
        tab = slice(j * SUBLANES, (j + 1) * SUBLANES)

        def u_row(t, j=j):
            t += j * TM_MIX
            ref, t = (hist_ref, t + HIST) if t < 0 else (u_ref, t)
            return ref[t * ROW_TILE:(t + 1) * ROW_TILE, :]

        _mix_tile(o_ref.at[r], og_ref.at[r], u_row, gt_ref.at[r], x_ref.at[r], cw_ref, cb_ref,
                  lg_ref, lb_ref, wpw_ref, wo_ref, g2_ref, wrh_ref, wrl_ref, br_ref,
                  h_o.at[r], hn_o.at[r], ri_o.at[:, r], rw_o.at[:, r], tabn_o.at[tab], tabb_o.at[tab],
                  cbuf, seen)


def _mix_tile(o_ref, og_ref, u_row, gt_ref, x_ref, cw_ref, cb_ref, lg_ref, lb_ref,
              wpw_ref, wo_ref, g2_ref, wrh_ref, wrl_ref, br_ref,
              h_o, hn_o, ri_o, rw_o, tabn_o, tabb_o, cbuf, seen):
    taps = [cw_ref[jj] for jj in range(CONV_WIDTH)]
    for t0 in range(0, TM_MIX, CONV_RB):
        part = [cb_ref[...]] * CONV_RB
        for rr in range(t0 - (CONV_WIDTH - 1), t0 + CONV_RB):
            ur = u_row(rr)
            for t in range(max(t0, rr), min(t0 + CONV_RB, rr + CONV_WIDTH)):
                part[t - t0] = part[t - t0] + taps[rr - t + CONV_WIDTH - 1] * ur
        for g in range(CONV_RB):
            cbuf[(t0 + g) * ROW_TILE:(t0 + g + 1) * ROW_TILE, :] = part[g]
    acc = _load_row_tiles(cbuf, TM_MIX)

    mu = jnp.mean(acc, axis=-1, keepdims=True)
    xc = acc - mu
    ln = xc * lax.rsqrt(jnp.mean(xc * xc, axis=-1, keepdims=True) + EPS) * lg_ref[...] + lb_ref[...]
    branch_b = _dot(_silu(ln).astype(BF16), wpw_ref[...])

    branch_a = o_ref[...].astype(F32) * og_ref[...].astype(F32)
    g_a = gt_ref[:, 0:D_MODEL].astype(F32)
    g_b = gt_ref[:, D_MODEL:2 * D_MODEL].astype(F32)
    merged = (g_a * branch_a + g_b * branch_b).astype(BF16)
    h1 = x_ref[...] + _dot(merged, wo_ref[...])
    h_o[...] = h1

    hn2 = h1 * lax.rsqrt(jnp.mean(h1 * h1, axis=-1, keepdims=True) + EPS) * g2_ref[...]
    hn_hi = hn2.astype(BF16)
    hn_o[...] = hn_hi

    hn_lo = (hn2 - hn_hi.astype(F32)).astype(BF16)
    logits = (_dot_nt(wrh_ref[...], hn_hi) + _dot_nt(wrh_ref[...], hn_lo) + _dot_nt(wrl_ref[...], hn_hi)
              + br_ref[...])
    row = lax.broadcasted_iota(jnp.int32, logits.shape, 0)
    rowf = row.astype(F32)
    neg = -jnp.inf
    is_g = row < N_GROUPS
    lg = jnp.where(is_g, logits, neg)
    gmax = jnp.max(lg, axis=0, keepdims=True)
    gidx = jnp.min(jnp.where(lg == gmax, rowf, float(N_GROUPS)), axis=0, keepdims=True)
    g_w = 1.0 / jnp.sum(jnp.where(is_g, jnp.exp(lg - gmax), 0.0), axis=0, keepdims=True)
    erow = rowf - float(N_GROUPS)
    egrp = ((row - N_GROUPS) >> 3).astype(F32)
    in_grp = (row >= N_GROUPS) & (row < N_GROUPS + N_EXPERTS) & (egrp == gidx)
    le = jnp.where(in_grp, logits, neg)
    m1 = jnp.max(le, axis=0, keepdims=True)
    i1 = jnp.min(jnp.where(le == m1, erow, float(N_EXPERTS)), axis=0, keepdims=True)
    le2 = jnp.where(erow == i1, neg, le)
    m2 = jnp.max(le2, axis=0, keepdims=True)
    i2 = jnp.min(jnp.where(le2 == m2, erow, float(N_EXPERTS)), axis=0, keepdims=True)
    t = jnp.exp(m2 - m1)
    w1 = g_w / (1.0 + t)
    w2 = g_w * t / (1.0 + t)

    def run_len(n):
        return jnp.floor((n + (TOK_ALIGN - 1)) * (1.0 / TOK_ALIGN)) * TOK_ALIGN

    oh1 = erow == i1
    oh2 = erow == i2
    oh = jnp.where(oh1 | oh2, 1.0, 0.0).astype(BF16)
    tr = lax.broadcasted_iota(jnp.int32, (TM_MIX, TM_MIX), 0)
    tc = lax.broadcasted_iota(jnp.int32, (TM_MIX, TM_MIX), 1)
    before_tok = _dot(oh, (tr < tc).astype(BF16))
    n_col = jnp.sum(oh.astype(F32), axis=1, keepdims=True)
    er = lax.broadcasted_iota(jnp.int32, (LANES, LANES), 0)
    ec = lax.broadcasted_iota(jnp.int32, (LANES, LANES), 1)
    before_exp = _dot((ec < er).astype(BF16), jnp.broadcast_to(run_len(n_col), logits.shape).astype(BF16))
    where_to = before_exp + before_tok
    q1 = jnp.sum(jnp.where(oh1, where_to, 0.0), axis=0, keepdims=True)
    q2 = jnp.sum(jnp.where(oh2, where_to, 0.0), axis=0, keepdims=True)

    n_rows = run_len(_dot_nt(jnp.ones((SUBLANES, TM_MIX), BF16), oh))
    tabn_o[...] = n_rows
    tabb_o[...] = seen[...]
    seen[...] = seen[...] + n_rows

    ri_o[...] = jnp.zeros_like(ri_o)
    ri_o[0:1, :] = q1.astype(jnp.int32)
    ri_o[1:2, :] = q2.astype(jnp.int32)
    rw_o[...] = jnp.zeros_like(rw_o)
    rw_o[0:1, :] = w1
    rw_o[1:2, :] = w2


def _mix(o_n, og, u, gates, x2d, cw, cb, lg, lb, wpw, wo, g2, wrh, wrl, br):
    tm = MIX_TILES * TM_MIX
    n_front = FRONT // tm
    grid = (SEQ // tm,)
    rowp = lambda w: pl.BlockSpec((tm, w), lambda i: (i + n_front, 0))
    full = lambda a: pl.BlockSpec(a.shape, lambda i: (0,) * a.ndim)
    hist_blocks = tm // HIST
    return pl.pallas_call(
        _mix_kernel,
        grid=grid,
        in_specs=[
            rowp(V_W), rowp(D_MODEL),
            pl.BlockSpec((tm * ROW_TILE, LANES), lambda i: (i + n_front, 0)),
            pl.BlockSpec((HIST * ROW_TILE, LANES), lambda i: ((i + n_front) * hist_blocks - 1, 0)),
            rowp(2 * D_MODEL),
            pl.BlockSpec((tm, D_MODEL), lambda i: (i, 0)),
            full(cw), full(cb), full(lg), full(lb), full(wpw), full(wo), full(g2),
            full(wrh), full(wrl), full(br),
        ],
        out_specs=[
            pl.BlockSpec((tm, D_MODEL), lambda i: (i, 0)),
            pl.BlockSpec((tm, D_MODEL), lambda i: (i, 0)),
            pl.BlockSpec((SUBLANES, tm), lambda i: (0, i)),
            pl.BlockSpec((SUBLANES, tm), lambda i: (0, i)),
            pl.BlockSpec((MIX_TILES * SUBLANES, LANES), lambda i: (i, 0)),
            pl.BlockSpec((MIX_TILES * SUBLANES, LANES), lambda i: (i, 0)),
        ],
        out_shape=[
            jax.ShapeDtypeStruct((SEQ, D_MODEL), F32),
            jax.ShapeDtypeStruct((SEQ, D_MODEL), BF16),
            jax.ShapeDtypeStruct((SUBLANES, SEQ), jnp.int32),
            jax.ShapeDtypeStruct((SUBLANES, SEQ), F32),
            jax.ShapeDtypeStruct((N_TILES * SUBLANES, LANES), F32),
            jax.ShapeDtypeStruct((N_TILES * SUBLANES, LANES), F32),
        ],
        scratch_shapes=[pltpu.VMEM((TM_MIX * ROW_TILE, LANES), F32), pltpu.VMEM((SUBLANES, LANES), F32)],
        compiler_params=pltpu.CompilerParams(dimension_semantics=("arbitrary",),
                                             vmem_limit_bytes=VMEM_LIMIT),
        name="mix",
    )(o_n, og, u, u, gates, x2d, cw, cb, lg, lb, wpw, wo, g2, wrh, wrl, br)


def _tok_rows(start_tok, n_tok):
    return pl.ds(pl.multiple_of(start_tok * TOK_ROWS, SUBLANES), n_tok * TOK_ROWS)


def _start_tile_runs(n_ref, row_ref, tile, make_copy):
    def run(e, off):
        n = n_ref[tile * N_EXPERTS + e]

        @pl.when(n > 0)
        def _():
            make_copy(_tok_rows(off, n), _tok_rows(row_ref[tile * N_EXPERTS + e], n)).start()

        return off + n

    lax.fori_loop(0, N_EXPERTS, run, 0, unroll=8)


def _dispatch_kernel(n_ref, row_ref, tot_ref, ts_ref, tl_ref, nused_ref, hn_ref, q_ref, xb_ref,
                     sb0, sb1, zbuf, sem, fsem):
    step = pl.program_id(0)
    last = pl.num_programs(0) - 1

    def fills(start):
        def tail(e, c):
            n = tl_ref[e]

            @pl.when(n > 0)
            def _():
                cp = pltpu.make_async_copy(zbuf.at[pl.ds(0, n * TOK_ROWS)],
                                           xb_ref.at[_tok_rows(ts_ref[e], n)], fsem)
                cp.start() if start else cp.wait()

            return c

        lax.fori_loop(0, N_EXPERTS, tail, 0)

        def block(b, c):
            cp = pltpu.make_async_copy(zbuf, xb_ref.at[_tok_rows(b * BM, BM)], fsem)
            cp.start() if start else cp.wait()
            return c

        lax.fori_loop(nused_ref[0], N_BLOCKS, block, 0)

    @pl.when(step == 0)
    def _():
        zbuf[...] = jnp.zeros_like(zbuf)
        fills(start=True)

    def tile_runs(tile, s, sbuf):
        rows = pl.ds(0, tot_ref[tile] * TOK_ROWS)
        return pltpu.make_async_copy(sbuf.at[rows], xb_ref.at[rows], sem.at[s])

    for s, sbuf in enumerate((sb0, sb1)):
        tile = step * TILES_PER_STEP + s
        toks = slice(s * TM_MIX, (s + 1) * TM_MIX)
        slot_i = lax.broadcasted_iota(jnp.int32, (SLOTS, TM_MIX), 0)
        onehot = jnp.where((slot_i == q_ref[0:1, toks]) | (slot_i == q_ref[1:2, toks]), 1.0, 0.0).astype(BF16)
        srt = _dot(onehot, hn_ref[toks, :])

        @pl.when(step >= 1)
        def _():
            tile_runs(tile - TILES_PER_STEP, s, sbuf).wait()

        _store_token_tiles(sbuf, srt, SLOTS)
        _start_tile_runs(n_ref, row_ref, tile,
                         lambda loc, glob: pltpu.make_async_copy(sbuf.at[loc], xb_ref.at[glob], sem.at[s]))

    @pl.when(step == last)
    def _():
        for s, sbuf in enumerate((sb0, sb1)):
            tile_runs(step * TILES_PER_STEP + s, s, sbuf).wait()
        fills(start=False)


def _dispatch(n_flat, run_rows, tot, tail_start, tail_len, nused, hn2, ri):
    grid_spec = pltpu.PrefetchScalarGridSpec(
        num_scalar_prefetch=6,
        grid=(N_TILES // TILES_PER_STEP,),
        in_specs=[pl.BlockSpec((TILES_PER_STEP * TM_MIX, D_MODEL), lambda i, *_: (i, 0)),
                  pl.BlockSpec((SUBLANES, TILES_PER_STEP * TM_MIX), lambda i, *_: (0, i))],
        out_specs=pl.BlockSpec(memory_space=pl.ANY),
        scratch_shapes=[pltpu.VMEM((SLOTS * TOK_ROWS, LANES), jnp.uint32),
                        pltpu.VMEM((SLOTS * TOK_ROWS, LANES), jnp.uint32),
                        pltpu.VMEM((BM * TOK_ROWS, LANES), jnp.uint32),
                        pltpu.SemaphoreType.DMA((2,)), pltpu.SemaphoreType.DMA(())],
    )
    return pl.pallas_call(
        _dispatch_kernel,
        grid_spec=grid_spec,
        out_shape=jax.ShapeDtypeStruct((P_ROWS * TOK_ROWS, LANES), jnp.uint32),
        compiler_params=pltpu.CompilerParams(dimension_semantics=("arbitrary",),
                                             vmem_limit_bytes=VMEM_LIMIT),
        name="dispatch",
    )(n_flat, run_rows, tot, tail_start, tail_len, nused, hn2, ri)


def _experts_kernel(be_ref, first_ref, next_ref, next2_ref, stage_ref, nused_ref,
                    x_ref, wg_hbm, wu_hbm, wd_hbm, y_ref, wg_f, wu_f, wd_f, wg_b, wu_b, wd_b, sem):
    def weight_copies(e, st):
        return (pltpu.make_async_copy(wg_hbm.at[e], wg_f.at[st], sem.at[st, 0]),
                pltpu.make_async_copy(wu_hbm.at[e], wu_f.at[st], sem.at[st, 1]),
                pltpu.make_async_copy(wd_hbm.at[e], wd_f.at[st], sem.at[st, 2]))

    @pl.when((pl.program_id(0) == 0) & (nused_ref[0] > 0))
    def _():
        for c in weight_copies(be_ref[0], 0):
            c.start()

        @pl.when(next_ref[0] >= 0)
        def _():
            for c in weight_copies(next_ref[0], 1):
                c.start()

    for j in range(BLOCKS_PER_STEP):
        _expert_block(pl.program_id(0) * BLOCKS_PER_STEP + j, j * BM, be_ref, first_ref, next2_ref,
                      stage_ref, nused_ref, x_ref, y_ref, wg_f, wu_f, wd_f, wg_b, wu_b, wd_b, weight_copies)


def _expert_block(b, tok0, be_ref, first_ref, next2_ref, stage_ref, nused_ref, x_ref, y_ref,
                  wg_f, wu_f, wd_f, wg_b, wu_b, wd_b, weight_copies):
    @pl.when(b < nused_ref[0])
    def _():
        def new_expert(st):
            for c in weight_copies(be_ref[b], st):
                c.wait()
            wg_b[...] = wg_f[st].astype(BF16)
            wu_b[...] = wu_f[st].astype(BF16)
            wd_b[...] = wd_f[st].astype(BF16)

            @pl.when(next2_ref[b] >= 0)
            def _():
                for c in weight_copies(next2_ref[b], st):
                    c.start()

        for st in range(2):
            pl.when((first_ref[b] == 1) & (stage_ref[b] == st))(functools.partial(new_expert, st))

        x = _load_token_tiles(x_ref, BM, tok0)
        a = _dot(x, wg_b[...])
        u = _dot(x, wu_b[...])
        y = _dot((_silu(a) * u).astype(BF16), wd_b[...])
        _store_token_tiles(y_ref, y.astype(BF16).astype(F32), BM, tok0)

    @pl.when(b >= nused_ref[0])
    def _():
        y_ref[tok0 * TOK_ROWS:(tok0 + BM) * TOK_ROWS, :] = jnp.zeros((BM * TOK_ROWS, LANES), jnp.uint32)


def _experts(block_e, first, next_e, next2_e, stage, nused, xb, wg, wu, wd):
    def xmap(b, *_):
        return (b, 0)

    grid_spec = pltpu.PrefetchScalarGridSpec(
        num_scalar_prefetch=6,
        grid=(N_BLOCKS // BLOCKS_PER_STEP,),
        in_specs=[
            pl.BlockSpec((BLOCKS_PER_STEP * BM * TOK_ROWS, LANES), xmap),
            pl.BlockSpec(memory_space=pl.ANY),
            pl.BlockSpec(memory_space=pl.ANY),
            pl.BlockSpec(memory_space=pl.ANY),
        ],
        out_specs=pl.BlockSpec((BLOCKS_PER_STEP * BM * TOK_ROWS, LANES), xmap),
        scratch_shapes=[pltpu.VMEM((2, D_MODEL, D_EXPERT), F32), pltpu.VMEM((2, D_MODEL, D_EXPERT), F32),
                        pltpu.VMEM((2, D_EXPERT, D_MODEL), F32),
                        pltpu.VMEM((D_MODEL, D_EXPERT), BF16), pltpu.VMEM((D_MODEL, D_EXPERT), BF16),
                        pltpu.VMEM((D_EXPERT, D_MODEL), BF16), pltpu.SemaphoreType.DMA((2, 3))],
    )
    return pl.pallas_call(
        _experts_kernel,
        grid_spec=grid_spec,
        out_shape=jax.ShapeDtypeStruct((P_ROWS * TOK_ROWS, LANES), jnp.uint32),
        compiler_params=pltpu.CompilerParams(dimension_semantics=("arbitrary",),
                                             vmem_limit_bytes=VMEM_LIMIT),
        name="experts",
    )(block_e, first, next_e, next2_e, stage, nused, xb, wg, wu, wd)


def _combine_kernel(n_ref, row_ref, tot_ref, yb_ref, h_ref, q_ref, w_ref, g_ref, out_ref, *scratch):
    step = pl.program_id(0)
    bufs, sem = scratch[:COMBINE_TILES], scratch[COMBINE_TILES]

    def start_gather(tile, s):
        _start_tile_runs(n_ref, row_ref, tile,
                         lambda loc, glob: pltpu.make_async_copy(yb_ref.at[glob], bufs[s].at[loc], sem.at[s]))

    @pl.when(step == 0)
    def _():
        for buf in bufs:
            buf[...] = jnp.zeros_like(buf)
        for s in range(GATHER_AHEAD):
            start_gather(s, s)

    for s in range(COMBINE_TILES):
        tile = step * COMBINE_TILES + s
        ahead = s + GATHER_AHEAD
        if ahead < COMBINE_TILES:
            start_gather(tile + GATHER_AHEAD, ahead)
        else:
            pl.when(step + 1 < pl.num_programs(0))(
                functools.partial(start_gather, tile + GATHER_AHEAD, ahead - COMBINE_TILES))

        rows = pl.ds(0, tot_ref[tile] * TOK_ROWS)
        pltpu.make_async_copy(yb_ref.at[rows], bufs[s].at[rows], sem.at[s]).wait()
        ys = _load_token_tiles(bufs[s], SLOTS)
        toks = slice(s * TM_MIX, (s + 1) * TM_MIX)
        slot_i = lax.broadcasted_iota(jnp.int32, (SLOTS, TM_MIX), 0)
        wmat = (jnp.where(slot_i == q_ref[0:1, toks], w_ref[0:1, toks], 0.0)
                + jnp.where(slot_i == q_ref[1:2, toks], w_ref[1:2, toks], 0.0)).astype(BF16)
        hh = h_ref[toks, :] + _dot_tn(wmat, ys)
        out_ref[toks, :] = hh * lax.rsqrt(jnp.mean(hh * hh, axis=-1, keepdims=True) + EPS) * g_ref[...]


def _combine(n_flat, run_rows, tot, yb, h1, ri, rw, gf):
    grid_spec = pltpu.PrefetchScalarGridSpec(
        num_scalar_prefetch=3,
        grid=(N_TILES // COMBINE_TILES,),
        in_specs=[
            pl.BlockSpec(memory_space=pl.ANY),
            pl.BlockSpec((COMBINE_TILES * TM_MIX, D_MODEL), lambda i, *_: (i, 0)),
            pl.BlockSpec((SUBLANES, COMBINE_TILES * TM_MIX), lambda i, *_: (0, i)),
            pl.BlockSpec((SUBLANES, COMBINE_TILES * TM_MIX), lambda i, *_: (0, i)),
            pl.BlockSpec((1, D_MODEL), lambda i, *_: (0, 0)),
        ],
        out_specs=pl.BlockSpec((COMBINE_TILES * TM_MIX, D_MODEL), lambda i, *_: (i, 0)),
        scratch_shapes=[pltpu.VMEM((SLOTS * TOK_ROWS, LANES), jnp.uint32)] * COMBINE_TILES
                       + [pltpu.SemaphoreType.DMA((COMBINE_TILES,))],
    )
    return pl.pallas_call(
        _combine_kernel,
        grid_spec=grid_spec,
        out_shape=jax.ShapeDtypeStruct((SEQ, D_MODEL), F32),
        compiler_params=pltpu.CompilerParams(dimension_semantics=("arbitrary",),
                                             vmem_limit_bytes=VMEM_LIMIT),
        name="combine",
    )(n_flat, run_rows, tot, yb, h1, ri, rw, gf)


def kernel(x, meta, norm1_g, w_in, w_decay_up, b_decay, gla_norm_g, conv_w, conv_b, conv_ln_g,
           conv_ln_b, w_pw2, b_gate, w_out, norm2_g, w_router_group, b_router_group,
           w_router_expert, b_router_expert, w_exp_gate, w_exp_up, w_exp_down, final_norm_g):
    assert x.shape == (1, SEQ, D_MODEL) and w_in.shape[0] == 1
    x2d = x[0]
    front = jnp.concatenate([jnp.zeros((FRONT - N_META, D_MODEL), F32), meta.astype(F32)], axis=0)

    w = w_in[0]
    row2 = lambda a: a.reshape(1, -1).astype(F32)
    wq, wk, wv, wog, wdl, wglu, wgt = _wcast(w)
    up = w_decay_up[0].astype(F32)
    up_hi = up.astype(BF16)
    up_mid = (up - up_hi.astype(F32)).astype(BF16)
    up_lo = (up - up_hi.astype(F32) - up_mid.astype(F32)).astype(BF16)
    wdu = jnp.concatenate([up_hi, up_mid, up_lo, up_hi, up_mid, up_hi], axis=0)
    q, k, v, og, la, u, gates = _inproj(
        front, x2d, row2(norm1_g[0]), wq, wk, wv, wog, jnp.tile(wdl, (1, DL_COPIES)), wglu, wgt,
        wdu, row2(b_decay[0]), row2(b_gate[0]))

    o_n = _gla(q, k, v, la, row2(gla_norm_g[0]))

    wr = jnp.concatenate([w_router_group[0].T, w_router_expert[0].T,
                          jnp.zeros((LANES - N_GROUPS - N_EXPERTS, D_MODEL), F32)], axis=0).astype(F32)
    br = jnp.concatenate([b_router_group[0], b_router_expert[0],
                          jnp.zeros((LANES - N_GROUPS - N_EXPERTS,), F32)]).reshape(LANES, 1).astype(F32)
    wr_hi = wr.astype(BF16)
    wr_lo = (wr - wr_hi.astype(F32)).astype(BF16)
    h1, hn2, ri, rw, tabn, tabb = _mix(
        o_n, og, u, gates, x2d, conv_w[0].astype(F32).reshape(CONV_WIDTH, ROW_TILE, LANES),
        conv_b[0].astype(F32).reshape(ROW_TILE, LANES), row2(conv_ln_g[0]),
        row2(conv_ln_b[0]), w_pw2[0].astype(BF16), w_out[0].astype(BF16),
        row2(norm2_g[0]), wr_hi, wr_lo, br)

    experts = slice(N_GROUPS, N_GROUPS + N_EXPERTS)
    n_te = tabn[::SUBLANES, experts].astype(jnp.int32)
    seen_te = tabb[::SUBLANES, experts].astype(jnp.int32)
    counts = seen_te[-1] + n_te[-1]
    padded = (counts + BM - 1) // BM * BM
    pad_end = jnp.cumsum(padded)
    pad_start = pad_end - padded
    n_flat = n_te.reshape(-1)
    tot = jnp.sum(n_te, axis=1).astype(jnp.int32)
    run_rows = (pad_start[None, :] + seen_te).reshape(-1).astype(jnp.int32)
    tail_start = (pad_start + counts).astype(jnp.int32)
    tail_len = (padded - counts).astype(jnp.int32)
    blk = jnp.arange(N_BLOCKS, dtype=jnp.int32)
    block_e = jnp.minimum(jnp.sum((pad_end[None, :] <= blk[:, None] * BM).astype(jnp.int32), axis=1),
                          N_EXPERTS - 1).astype(jnp.int32)
    first = jnp.concatenate([jnp.ones((1,), jnp.int32), (block_e[1:] != block_e[:-1]).astype(jnp.int32)])
    nused = (pad_end[-1:] // BM).astype(jnp.int32)
    eid = jnp.arange(N_EXPERTS, dtype=jnp.int32)
    later = jnp.flip(lax.cummin(jnp.flip(jnp.where(counts > 0, eid, N_EXPERTS))))
    nxt = jnp.concatenate([later[1:], jnp.full((2,), N_EXPERTS, jnp.int32)])
    nxt2 = nxt[nxt[:N_EXPERTS]]
    or_none = lambda t: jnp.where(t < N_EXPERTS, t, -1)[block_e].astype(jnp.int32)
    next_e, next2_e = or_none(nxt[:N_EXPERTS]), or_none(nxt2)
    stage = ((jnp.cumsum(first) - 1) % 2).astype(jnp.int32)

    xb = _dispatch(n_flat, run_rows, tot, tail_start, tail_len, nused, hn2, ri)
    yb = _experts(block_e, first, next_e, next2_e, stage, nused, xb, w_exp_gate[0], w_exp_up[0], w_exp_down[0])
    out = _combine(n_flat, run_rows, tot, yb, h1, ri, rw, row2(final_norm_g))
    return out[None]
```

```python
import functools

import jax
import jax.numpy as jnp
import numpy as np
from jax import lax
from jax.experimental import pallas as pl
from jax.experimental.pallas import tpu as pltpu

F32 = jnp.float32
BF16 = jnp.bfloat16

D_MODEL = 1024
SEQ = 16384
N_META = 16
GLA_HEADS = 4
GLA_DK = 128
GLA_DV = 256
GLA_RANK = 16
GLA_TAU = 16.0
CHUNK = 64
QK_W = GLA_HEADS * GLA_DK
V_W = GLA_HEADS * GLA_DV
CONV_WIDTH = 31
N_GROUPS = 8
EXPERTS_PER_GROUP = 8
N_EXPERTS = N_GROUPS * EXPERTS_PER_GROUP
TOP_K = 2
D_EXPERT = 512
EPS = 1e-6
LOG2E = 1.4426950408889634

LANES = 128
SUBLANES = 8
ROW_TILE = D_MODEL // LANES
TOK_ROWS = D_MODEL // 2 // LANES
TOK_ALIGN = SUBLANES // TOK_ROWS

FRONT = 512
LP = FRONT + SEQ
TM_PROJ = 512
DL_COPIES = 6
TM_GLA = 512
TM_MIX = 256
MIX_TILES = 2
HIST = 32
CONV_RB = 16
N_TILES = SEQ // TM_MIX
SLOTS = TOP_K * TM_MIX + N_EXPERTS * (TOK_ALIGN - 1)
TILES_PER_STEP = 2
COMBINE_TILES = 4
GATHER_AHEAD = 3
BM = 256
BLOCKS_PER_STEP = 4
X_SLOTS = 3
N_BLOCKS = (N_TILES * SLOTS) // BM + N_EXPERTS
P_ROWS = N_BLOCKS * BM
VMEM_LIMIT = 56 * 1024 * 1024


def _sigmoid(x):
    return 0.5 * jnp.tanh(0.5 * x) + 0.5


def _silu(x):
    return x * _sigmoid(x)


def _dot(a, b, **kw):
    return jnp.dot(a, b, preferred_element_type=F32, **kw)


def _dot_nt(a, b):
    return lax.dot_general(a, b, (((1,), (1,)), ((), ())), preferred_element_type=F32)


def _dot_tn(a, b):
    return lax.dot_general(a, b, (((0,), (0,)), ((), ())), preferred_element_type=F32)


def _load_row_tiles(ref, n_rows):
    return jnp.concatenate([ref[pl.ds(s, n_rows, stride=ROW_TILE), :] for s in range(ROW_TILE)], axis=-1)


def _store_row_tiles(ref, val, n_rows):
    for s in range(ROW_TILE):
        ref[pl.ds(s, n_rows, stride=ROW_TILE), :] = val[:, s * LANES:(s + 1) * LANES]


def _load_token_tiles(ref, n_tok, first_tok=0):
    r0 = first_tok * TOK_ROWS
    w = jnp.concatenate([ref[pl.ds(r0 + s, n_tok, stride=TOK_ROWS), :] for s in range(TOK_ROWS)], axis=-1)
    lo = pltpu.bitcast(w << 16, F32)
    hi = pltpu.bitcast(w & jnp.uint32(0xFFFF0000), F32)
    return jnp.concatenate([lo, hi], axis=-1).astype(BF16)


def _store_token_tiles(ref, val, n_tok, first_tok=0):
    bits = pltpu.bitcast(val, jnp.uint32)
    w = (bits[:, 0:D_MODEL // 2] >> 16) | (bits[:, D_MODEL // 2:D_MODEL] & jnp.uint32(0xFFFF0000))
    r0 = first_tok * TOK_ROWS
    for s in range(TOK_ROWS):
        ref[pl.ds(r0 + s, n_tok, stride=TOK_ROWS), :] = w[:, s * LANES:(s + 1) * LANES]


IN_SPLITS = (QK_W, QK_W, V_W, D_MODEL, GLA_RANK, 2 * D_MODEL, 2 * D_MODEL)
WCAST_ROWS = 512


def _wcast_chunks():
    chunks, off = [], 0
    for g, n in enumerate(IN_SPLITS):
        for c in range(0, n, WCAST_ROWS):
            chunks.append((g, off + c, c, min(WCAST_ROWS, n - c)))
        off += n
    return chunks


def _wcast_kernel(wt_hbm, *refs):
    outs, buf, sem = refs[:len(IN_SPLITS)], refs[len(IN_SPLITS)], refs[len(IN_SPLITS) + 1]
    chunks = _wcast_chunks()

    def copy(i):
        _, row0, _, n = chunks[i]
        return pltpu.make_async_copy(wt_hbm.at[row0:row0 + n], buf.at[i % 2, 0:n], sem.at[i % 2])

    copy(0).start()
    for i, (g, _, r0, n) in enumerate(chunks):
        if i + 1 < len(chunks):
            copy(i + 1).start()
        copy(i).wait()
        outs[g][r0:r0 + n, :] = buf[i % 2, 0:n].astype(BF16)


def _wcast(wt):
    return pl.pallas_call(
        _wcast_kernel,
        in_specs=[pl.BlockSpec(memory_space=pl.ANY)],
        out_specs=[pl.BlockSpec(memory_space=pltpu.VMEM) for _ in IN_SPLITS],
        out_shape=[jax.ShapeDtypeStruct((n, D_MODEL), BF16) for n in IN_SPLITS],
        scratch_shapes=[pltpu.VMEM((2, WCAST_ROWS, D_MODEL), F32), pltpu.SemaphoreType.DMA((2,))],
        compiler_params=pltpu.CompilerParams(vmem_limit_bytes=VMEM_LIMIT),
        name="wcast",
    )(wt)


def _inproj_kernel(front_ref, x_ref, g_ref, wq, wk, wv, wog, wdl, wglu, wgt, wdu, bd, bg,
                   q_o, k_o, v_o, og_o, la_o, u_o, gt_o):
    i = pl.program_id(0)
    h = jnp.where(i < FRONT // TM_PROJ, front_ref[...], x_ref[...])
    ms = jnp.mean(h * h, axis=-1, keepdims=True)
    hn = (h * lax.rsqrt(ms + EPS) * g_ref[...]).astype(BF16)
    q_o[...] = _dot_nt(hn, wq[...]).astype(BF16)
    k_o[...] = _dot_nt(hn, wk[...]).astype(BF16)
    v_o[...] = _dot_nt(hn, wv[...]).astype(BF16)
    og_o[...] = _silu(_dot_nt(hn, wog[...])).astype(BF16)
    glu = _dot_nt(hn, wglu[...])
    _store_row_tiles(u_o, glu[:, 0:D_MODEL] * _sigmoid(glu[:, D_MODEL:2 * D_MODEL]), TM_PROJ)
    gt_o[...] = _sigmoid(_dot_nt(hn, wgt[...]) + bg[...]).astype(BF16)
    d6 = _dot_nt(hn, wdl[...])
    hi = d6.astype(BF16).astype(F32)
    mid = (d6 - hi).astype(BF16).astype(F32)
    lane = lax.broadcasted_iota(jnp.int32, d6.shape, 1)
    pieces = jnp.where(lane < 3 * GLA_RANK, hi, jnp.where(lane < 5 * GLA_RANK, mid, d6 - hi - mid))
    z = _dot(pieces.astype(BF16), wdu[...]) + bd[...]
    la_o[...] = (jnp.minimum(z, 0.0) - jnp.log(1.0 + jnp.exp(-jnp.abs(z)))) * (LOG2E / GLA_TAU)


def _inproj(front, x2d, g1, wq, wk, wv, wog, wdl, wglu, wgt, wdu, bd, bg):
    n_front = FRONT // TM_PROJ
    grid = (LP // TM_PROJ,)
    row = lambda w: pl.BlockSpec((TM_PROJ, w), lambda i: (i, 0))
    full = lambda a: pl.BlockSpec(a.shape, lambda i: (0, 0))
    return pl.pallas_call(
        _inproj_kernel,
        grid=grid,
        in_specs=[
            pl.BlockSpec((TM_PROJ, D_MODEL), lambda i: (jnp.minimum(i, n_front - 1), 0)),
            pl.BlockSpec((TM_PROJ, D_MODEL), lambda i: (jnp.maximum(i - n_front, 0), 0)),
            full(g1), full(wq), full(wk), full(wv), full(wog), full(wdl), full(wglu), full(wgt),
            full(wdu), full(bd), full(bg),
        ],
        out_specs=[row(QK_W), row(QK_W), row(V_W), row(D_MODEL), row(QK_W),
                   pl.BlockSpec((TM_PROJ * ROW_TILE, LANES), lambda i: (i, 0)), row(2 * D_MODEL)],
        out_shape=[
            jax.ShapeDtypeStruct((LP, QK_W), BF16), jax.ShapeDtypeStruct((LP, QK_W), BF16),
            jax.ShapeDtypeStruct((LP, V_W), BF16), jax.ShapeDtypeStruct((LP, D_MODEL), BF16),
            jax.ShapeDtypeStruct((LP, QK_W), F32), jax.ShapeDtypeStruct((LP * ROW_TILE, LANES), F32),
            jax.ShapeDtypeStruct((LP, 2 * D_MODEL), BF16),
        ],
        compiler_params=pltpu.CompilerParams(dimension_semantics=("arbitrary",),
                                             vmem_limit_bytes=VMEM_LIMIT),
        name="inproj",
    )(front, x2d, g1, wq, wk, wv, wog, wdl, wglu, wgt, wdu, bd, bg)


GLA_LEVELS = (32, 16, 8, 4, 2, 1)
GLA_FINE = tuple(m for m in GLA_LEVELS if m < SUBLANES)
LA_SPLIT = 3


def _gla_tables():
    r = np.arange(CHUNK)
    t, c = r[:, None], r[None, :]
    rows = [c <= t]
    pair = []
    for m in GLA_LEVELS:
        mid = (t // (2 * m)) * (2 * m) + m - 1
        right = (t % (2 * m)) >= m
        if m in GLA_FINE:
            rows.append(np.where(right, (c > mid) & (c <= t), (c > t) & (c <= mid)))
        pair.append(((t // (2 * m)) == (c // (2 * m))) & right & ((c % (2 * m)) < m))
    pair.append(t == c)
    sums = np.concatenate(rows, axis=0).astype(np.float32)
    sums = np.concatenate([sums] * LA_SPLIT, axis=1)
    pair = np.stack([np.kron(np.eye(GLA_HEADS), p) for p in pair]).astype(np.float32)
    return jnp.asarray(sums, BF16), jnp.asarray(pair)


def _split_bf16(x):
    pieces = []
    rest = x
    for _ in range(LA_SPLIT):
        p = rest.astype(BF16)
        pieces.append(p)
        rest = rest - p.astype(F32)
    return jnp.concatenate(pieces, axis=0)


def _gla_exponents(a):
    b = a[0:CHUNK]
    parts = [b, b[CHUNK - 1:CHUNK] - b]
    for m in GLA_LEVELS:
        if m in GLA_FINE:
            continue
        for lo in range(0, CHUNK, 2 * m):
            ref = b[lo + m - 1:lo + m]
            parts += [ref - b[lo:lo + m], b[lo + m:lo + 2 * m] - ref]
    return jnp.concatenate(parts + [a[CHUNK:]], axis=0)


def _gla_kernel(q_ref, k_ref, v_ref, la_ref, g_ref, sums_ref, pair_ref, o_ref, s_ref):
    @pl.when(pl.program_id(0) == 0)
    def _():
        s_ref[...] = jnp.zeros_like(s_ref)

    row = lax.broadcasted_iota(jnp.int32, (CHUNK, GLA_DK), 0)
    right_rows = [(row & m) != 0 for m in GLA_LEVELS]
    sums = sums_ref[...]
    n_lvl = len(GLA_LEVELS)

    def body(c, carry):
        r0 = pl.multiple_of(c * CHUNK, CHUNK)
        rows = pl.ds(r0, CHUNK)
        states = [s_ref[h] for h in range(GLA_HEADS)]
        qs, ks, es, decs = [], [], [], []
        for hp in range(GLA_HEADS // 2):
            la3 = _split_bf16(la_ref[rows, 2 * hp * GLA_DK:2 * (hp + 1) * GLA_DK])
            a2 = _dot(sums, la3)
            e2 = jnp.exp2(_gla_exponents(a2))
            dec2 = jnp.exp2(jnp.transpose(jnp.broadcast_to(a2[CHUNK - 1:CHUNK], (LANES, 2 * GLA_DK))))
            for h in (2 * hp, 2 * hp + 1):
                half = slice((h % 2) * GLA_DK, (h % 2 + 1) * GLA_DK)
                kc = slice(h * GLA_DK, (h + 1) * GLA_DK)
                qs.append(q_ref[rows, kc].astype(F32) * (GLA_DK ** -0.5))
                ks.append(k_ref[rows, kc].astype(F32))
                es.append(e2[:, half])
                decs.append(dec2[half])
        vbs = [v_ref[rows, h * GLA_DV:(h + 1) * GLA_DV] for h in range(GLA_HEADS)]

        new_states, qes = [], []
        for h in range(GLA_HEADS):
            qes.append((qs[h] * es[h][0:CHUNK]).astype(BF16))
            kd = (ks[h] * es[h][CHUNK:2 * CHUNK]).astype(BF16)
            new_states.append(states[h] * jnp.concatenate([decs[h]] * (GLA_DV // LANES), axis=1)
                              + _dot_tn(kd, vbs[h]))

        stack = lambda parts: jnp.concatenate(parts, axis=0)
        sc = _dot_nt(stack([q.astype(BF16) for q in qs]), stack([k.astype(BF16) for k in ks])) * pair_ref[n_lvl]
        for lvl in range(n_lvl):
            x = stack([(jnp.where(right_rows[lvl], qs[h], ks[h])
                        * es[h][(2 + lvl) * CHUNK:(3 + lvl) * CHUNK]).astype(BF16) for h in range(GLA_HEADS)])
            sc = sc + _dot_nt(x, x) * pair_ref[lvl]
        o_all = _dot(sc.astype(BF16), stack(vbs))
        for h in range(GLA_HEADS):
            vc = slice(h * GLA_DV, (h + 1) * GLA_DV)
            o = o_all[h * CHUNK:(h + 1) * CHUNK] + _dot(qes[h], states[h].astype(BF16))
            o = o * lax.rsqrt(jnp.mean(o * o, axis=-1, keepdims=True) + EPS) * g_ref[:, vc]
            o_ref[rows, vc] = o.astype(BF16)
        for h in range(GLA_HEADS):
            s_ref[h] = new_states[h]
        return carry

    lax.fori_loop(0, TM_GLA // CHUNK, body, 0, unroll=4)


def _gla(q, k, v, la, g):
    sums, pair = _gla_tables()
    row = lambda w: pl.BlockSpec((TM_GLA, w), lambda t: (t, 0))
    full = lambda a: pl.BlockSpec(a.shape, lambda t: (0,) * a.ndim)
    return pl.pallas_call(
        _gla_kernel,
        grid=(LP // TM_GLA,),
        in_specs=[row(QK_W), row(QK_W), row(V_W), row(QK_W), full(g), full(sums), full(pair)],
        out_specs=row(V_W),
        out_shape=jax.ShapeDtypeStruct((LP, V_W), BF16),
        scratch_shapes=[pltpu.VMEM((GLA_HEADS, GLA_DK, GLA_DV), F32)],
        compiler_params=pltpu.CompilerParams(dimension_semantics=("arbitrary",),
                                             vmem_limit_bytes=VMEM_LIMIT),
        name="gla",
    )(q, k, v, la, g, sums, pair)


def _mix_kernel(o_ref, og_ref, u_ref, hist_ref, gt_ref, x_ref, cw_ref, cb_ref, lg_ref, lb_ref,
                wpw_ref, wo_ref, g2_ref, wrh_ref, wrl_ref, br_ref,
                h_o, hn_o, ri_o, rw_o, tabn_o, tabb_o, cbuf, seen):
    @pl.when(pl.program_id(0) == 0)
    def _():
        seen[...] = jnp.zeros_like(seen)

    for j in range(MIX_TILES):
        r = slice(j * TM_MIX, (j + 1) * TM_MIX)
        tab = slice(j * SUBLANES, (j + 1) * SUBLANES)

        def u_row(t, j=j):
            t += j * TM_MIX
            ref, t = (hist_ref, t + HIST) if t < 0 else (u_ref, t)
            return ref[t * ROW_TILE:(t + 1) * ROW_TILE, :]

        _mix_tile(o_ref.at[r], og_ref.at[r], u_row, gt_ref.at[r], x_ref.at[r], cw_ref, cb_ref,
                  lg_ref, lb_ref, wpw_ref, wo_ref, g2_ref, wrh_ref, wrl_ref, br_ref,
                  h_o.at[r], hn_o.at[r], ri_o.at[:, r], rw_o.at[:, r], tabn_o.at[tab], tabb_o.at[tab],
                  cbuf, seen)


def _mix_tile(o_ref, og_ref, u_row, gt_ref, x_ref, cw_ref, cb_ref, lg_ref, lb_ref,
              wpw_ref, wo_ref, g2_ref, wrh_ref, wrl_ref, br_ref,
              h_o, hn_o, ri_o, rw_o, tabn_o, tabb_o, cbuf, seen):
    taps = [cw_ref[jj] for jj in range(CONV_WIDTH)]
    for t0 in range(0, TM_MIX, CONV_RB):
        part = [cb_ref[...]] * CONV_RB
        for rr in range(t0 - (CONV_WIDTH - 1), t0 + CONV_RB):
            ur = u_row(rr)
            for t in range(max(t0, rr), min(t0 + CONV_RB, rr + CONV_WIDTH)):
                part[t - t0] = part[t - t0] + taps[rr - t + CONV_WIDTH - 1] * ur
        for g in range(CONV_RB):
            cbuf[(t0 + g) * ROW_TILE:(t0 + g + 1) * ROW_TILE, :] = part[g]
    acc = _load_row_tiles(cbuf, TM_MIX)

    mu = jnp.mean(acc, axis=-1, keepdims=True)
    xc = acc - mu
    ln = xc * lax.rsqrt(jnp.mean(xc * xc, axis=-1, keepdims=True) + EPS) * lg_ref[...] + lb_ref[...]
    branch_b = _dot(_silu(ln).astype(BF16), wpw_ref[...])

    branch_a = o_ref[...].astype(F32) * og_ref[...].astype(F32)
    g_a = gt_ref[:, 0:D_MODEL].astype(F32)
    g_b = gt_ref[:, D_MODEL:2 * D_MODEL].astype(F32)
    merged = (g_a * branch_a + g_b * branch_b).astype(BF16)
    h1 = x_ref[...] + _dot(merged, wo_ref[...])
    h_o[...] = h1

    hn2 = h1 * lax.rsqrt(jnp.mean(h1 * h1, axis=-1, keepdims=True) + EPS) * g2_ref[...]
    hn_hi = hn2.astype(BF16)
    hn_o[...] = hn_hi

    hn_lo = (hn2 - hn_hi.astype(F32)).astype(BF16)
    logits = (_dot_nt(wrh_ref[...], hn_hi) + _dot_nt(wrh_ref[...], hn_lo) + _dot_nt(wrl_ref[...], hn_hi)
              + br_ref[...])
    row = lax.broadcasted_iota(jnp.int32, logits.shape, 0)
    rowf = row.astype(F32)
    neg = -jnp.inf
    is_g = row < N_GROUPS
    lg = jnp.where(is_g, logits, neg)
    gmax = jnp.max(lg, axis=0, keepdims=True)
    gidx = jnp.min(jnp.where(lg == gmax, rowf, float(N_GROUPS)), axis=0, keepdims=True)
    g_w = 1.0 / jnp.sum(jnp.where(is_g, jnp.exp(lg - gmax), 0.0), axis=0, keepdims=True)
    erow = rowf - float(N_GROUPS)
    egrp = ((row - N_GROUPS) >> 3).astype(F32)
    in_grp = (row >= N_GROUPS) & (row < N_GROUPS + N_EXPERTS) & (egrp == gidx)
    le = jnp.where(in_grp, logits, neg)
    m1 = jnp.max(le, axis=0, keepdims=True)
    i1 = jnp.min(jnp.where(le == m1, erow, float(N_EXPERTS)), axis=0, keepdims=True)
    le2 = jnp.where(erow == i1, neg, le)
    m2 = jnp.max(le2, axis=0, keepdims=True)
    i2 = jnp.min(jnp.where(le2 == m2, erow, float(N_EXPERTS)), axis=0, keepdims=True)
    t = jnp.exp(m2 - m1)
    w1 = g_w / (1.0 + t)
    w2 = g_w * t / (1.0 + t)

    def run_len(n):
        return jnp.floor((n + (TOK_ALIGN - 1)) * (1.0 / TOK_ALIGN)) * TOK_ALIGN

    oh1 = erow == i1
    oh2 = erow == i2
    oh = jnp.where(oh1 | oh2, 1.0, 0.0).astype(BF16)
    tr = lax.broadcasted_iota(jnp.int32, (TM_MIX, TM_MIX), 0)
    tc = lax.broadcasted_iota(jnp.int32, (TM_MIX, TM_MIX), 1)
    before_tok = _dot(oh, (tr < tc).astype(BF16))
    n_col = jnp.sum(oh.astype(F32), axis=1, keepdims=True)
    er = lax.broadcasted_iota(jnp.int32, (LANES, LANES), 0)
    ec = lax.broadcasted_iota(jnp.int32, (LANES, LANES), 1)
    before_exp = _dot((ec < er).astype(BF16), jnp.broadcast_to(run_len(n_col), logits.shape).astype(BF16))
    where_to = before_exp + before_tok
    q1 = jnp.sum(jnp.where(oh1, where_to, 0.0), axis=0, keepdims=True)
    q2 = jnp.sum(jnp.where(oh2, where_to, 0.0), axis=0, keepdims=True)

    n_rows = run_len(_dot_nt(jnp.ones((SUBLANES, TM_MIX), BF16), oh))
    tabn_o[...] = n_rows
    tabb_o[...] = seen[...]
    seen[...] = seen[...] + n_rows

    ri_o[...] = jnp.zeros_like(ri_o)
    ri_o[0:1, :] = q1.astype(jnp.int32)
    ri_o[1:2, :] = q2.astype(jnp.int32)
    rw_o[...] = jnp.zeros_like(rw_o)
    rw_o[0:1, :] = w1
    rw_o[1:2, :] = w2


def _mix(o_n, og, u, gates, x2d, cw, cb, lg, lb, wpw, wo, g2, wrh, wrl, br):
    tm = MIX_TILES * TM_MIX
    n_front = FRONT // tm
    grid = (SEQ // tm,)
    rowp = lambda w: pl.BlockSpec((tm, w), lambda i: (i + n_front, 0))
    full = lambda a: pl.BlockSpec(a.shape, lambda i: (0,) * a.ndim)
    hist_blocks = tm // HIST
    return pl.pallas_call(
        _mix_kernel,
        grid=grid,
        in_specs=[
            rowp(V_W), rowp(D_MODEL),
            pl.BlockSpec((tm * ROW_TILE, LANES), lambda i: (i + n_front, 0)),
            pl.BlockSpec((HIST * ROW_TILE, LANES), lambda i: ((i + n_front) * hist_blocks - 1, 0)),
            rowp(2 * D_MODEL),
            pl.BlockSpec((tm, D_MODEL), lambda i: (i, 0)),
            full(cw), full(cb), full(lg), full(lb), full(wpw), full(wo), full(g2),
            full(wrh), full(wrl), full(br),
        ],
        out_specs=[
            pl.BlockSpec((tm, D_MODEL), lambda i: (i, 0)),
            pl.BlockSpec((tm, D_MODEL), lambda i: (i, 0)),
            pl.BlockSpec((SUBLANES, tm), lambda i: (0, i)),
            pl.BlockSpec((SUBLANES, tm), lambda i: (0, i)),
            pl.BlockSpec((MIX_TILES * SUBLANES, LANES), lambda i: (i, 0)),
            pl.BlockSpec((MIX_TILES * SUBLANES, LANES), lambda i: (i, 0)),
        ],
        out_shape=[
            jax.ShapeDtypeStruct((SEQ, D_MODEL), F32),
            jax.ShapeDtypeStruct((SEQ, D_MODEL), BF16),
            jax.ShapeDtypeStruct((SUBLANES, SEQ), jnp.int32),
            jax.ShapeDtypeStruct((SUBLANES, SEQ), F32),
            jax.ShapeDtypeStruct((N_TILES * SUBLANES, LANES), F32),
            jax.ShapeDtypeStruct((N_TILES * SUBLANES, LANES), F32),
        ],
        scratch_shapes=[pltpu.VMEM((TM_MIX * ROW_TILE, LANES), F32), pltpu.VMEM((SUBLANES, LANES), F32)],
        compiler_params=pltpu.CompilerParams(dimension_semantics=("arbitrary",),
                                             vmem_limit_bytes=VMEM_LIMIT),
        name="mix",
    )(o_n, og, u, u, gates, x2d, cw, cb, lg, lb, wpw, wo, g2, wrh, wrl, br)


def _tok_rows(start_tok, n_tok):
    return pl.ds(pl.multiple_of(start_tok * TOK_ROWS, SUBLANES), n_tok * TOK_ROWS)


def _start_tile_runs(n_ref, row_ref, tile, make_copy):
    def run(e, off):
        n = n_ref[tile * N_EXPERTS + e]

        @pl.when(n > 0)
        def _():
            make_copy(_tok_rows(off, n), _tok_rows(row_ref[tile * N_EXPERTS + e], n)).start()

        return off + n

    lax.fori_loop(0, N_EXPERTS, run, 0, unroll=8)


def _dispatch_kernel(n_ref, row_ref, tot_ref, ts_ref, tl_ref, nused_ref, hn_ref, q_ref, xb_ref,
                     sb0, sb1, zbuf, sem, fsem):
    step = pl.program_id(0)
    last = pl.num_programs(0) - 1

    def fills(start):
        def tail(e, c):
            n = tl_ref[e]

            @pl.when(n > 0)
            def _():
                cp = pltpu.make_async_copy(zbuf.at[pl.ds(0, n * TOK_ROWS)],
                                           xb_ref.at[_tok_rows(ts_ref[e], n)], fsem)
                cp.start() if start else cp.wait()

            return c

        lax.fori_loop(0, N_EXPERTS, tail, 0)

        def block(b, c):
            cp = pltpu.make_async_copy(zbuf, xb_ref.at[_tok_rows(b * BM, BM)], fsem)
            cp.start() if start else cp.wait()
            return c

        lax.fori_loop(nused_ref[0], N_BLOCKS, block, 0)

    @pl.when(step == 0)
    def _():
        zbuf[...] = jnp.zeros_like(zbuf)
        fills(start=True)

    def tile_runs(tile, s, sbuf):
        rows = pl.ds(0, tot_ref[tile] * TOK_ROWS)
        return pltpu.make_async_copy(sbuf.at[rows], xb_ref.at[rows], sem.at[s])

    for s, sbuf in enumerate((sb0, sb1)):
        tile = step * TILES_PER_STEP + s
        toks = slice(s * TM_MIX, (s + 1) * TM_MIX)
        slot_i = lax.broadcasted_iota(jnp.int32, (SLOTS, TM_MIX), 0)
        onehot = jnp.where((slot_i == q_ref[0:1, toks]) | (slot_i == q_ref[1:2, toks]), 1.0, 0.0).astype(BF16)
        srt = _dot(onehot, hn_ref[toks, :])

        @pl.when(step >= 1)
        def _():
            tile_runs(tile - TILES_PER_STEP, s, sbuf).wait()

        _store_token_tiles(sbuf, srt, SLOTS)
        _start_tile_runs(n_ref, row_ref, tile,
                         lambda loc, glob: pltpu.make_async_copy(sbuf.at[loc], xb_ref.at[glob], sem.at[s]))

    @pl.when(step == last)
    def _():
        for s, sbuf in enumerate((sb0, sb1)):
            tile_runs(step * TILES_PER_STEP + s, s, sbuf).wait()
        fills(start=False)


def _dispatch(n_flat, run_rows, tot, tail_start, tail_len, nused, hn2, ri):
    grid_spec = pltpu.PrefetchScalarGridSpec(
        num_scalar_prefetch=6,
        grid=(N_TILES // TILES_PER_STEP,),
        in_specs=[pl.BlockSpec((TILES_PER_STEP * TM_MIX, D_MODEL), lambda i, *_: (i, 0)),
                  pl.BlockSpec((SUBLANES, TILES_PER_STEP * TM_MIX), lambda i, *_: (0, i))],
        out_specs=pl.BlockSpec(memory_space=pl.ANY),
        scratch_shapes=[pltpu.VMEM((SLOTS * TOK_ROWS, LANES), jnp.uint32),
                        pltpu.VMEM((SLOTS * TOK_ROWS, LANES), jnp.uint32),
                        pltpu.VMEM((BM * TOK_ROWS, LANES), jnp.uint32),
                        pltpu.SemaphoreType.DMA((2,)), pltpu.SemaphoreType.DMA(())],
    )
    return pl.pallas_call(
        _dispatch_kernel,
        grid_spec=grid_spec,
        out_shape=jax.ShapeDtypeStruct((P_ROWS * TOK_ROWS, LANES), jnp.uint32),
        compiler_params=pltpu.CompilerParams(dimension_semantics=("arbitrary",),
                                             vmem_limit_bytes=VMEM_LIMIT),
        name="dispatch",
    )(n_flat, run_rows, tot, tail_start, tail_len, nused, hn2, ri)


def _experts_kernel(be_ref, first_ref, next_ref, next2_ref, stage_ref, nused_ref,
                    x_hbm, wg_hbm, wu_hbm, wd_hbm, y_ref, wg_f, wu_f, wd_f, wg_b, wu_b, wd_b, sem,
                    xbuf, xsem):
    step = pl.program_id(0)
    n_steps = pl.num_programs(0)
    step_rows = BLOCKS_PER_STEP * BM * TOK_ROWS

    def x_copy(st):
        rows = pl.ds(pl.multiple_of(st * step_rows, step_rows), step_rows)
        return pltpu.make_async_copy(x_hbm.at[rows], xbuf.at[st % X_SLOTS], xsem.at[st % X_SLOTS])

    @pl.when(step == 0)
    def _():
        for st in range(X_SLOTS - 1):
            x_copy(st).start()

    @pl.when(step + X_SLOTS - 1 < n_steps)
    def _():
        x_copy(step + X_SLOTS - 1).start()

    x_copy(step).wait()
    x_ref = xbuf.at[step % X_SLOTS]

    def weight_copies(e, st):
        return (pltpu.make_async_copy(wg_hbm.at[e], wg_f.at[st], sem.at[st, 0]),
                pltpu.make_async_copy(wu_hbm.at[e], wu_f.at[st], sem.at[st, 1]),
                pltpu.make_async_copy(wd_hbm.at[e], wd_f.at[st], sem.at[st, 2]))

    @pl.when((pl.program_id(0) == 0) & (nused_ref[0] > 0))
    def _():
        for c in weight_copies(be_ref[0], 0):
            c.start()

        @pl.when(next_ref[0] >= 0)
        def _():
            for c in weight_copies(next_ref[0], 1):
                c.start()

    for j in range(BLOCKS_PER_STEP):
        _expert_block(pl.program_id(0) * BLOCKS_PER_STEP + j, j * BM, be_ref, first_ref, next2_ref,
                      stage_ref, nused_ref, x_ref, y_ref, wg_f, wu_f, wd_f, wg_b, wu_b, wd_b, weight_copies)


def _expert_block(b, tok0, be_ref, first_ref, next2_ref, stage_ref, nused_ref, x_ref, y_ref,
                  wg_f, wu_f, wd_f, wg_b, wu_b, wd_b, weight_copies):
    @pl.when(b < nused_ref[0])
    def _():
        def new_expert(st):
            for c in weight_copies(be_ref[b], st):
                c.wait()
            wg_b[...] = wg_f[st].astype(BF16)
            wu_b[...] = wu_f[st].astype(BF16)
            wd_b[...] = wd_f[st].astype(BF16)

            @pl.when(next2_ref[b] >= 0)
            def _():
                for c in weight_copies(next2_ref[b], st):
                    c.start()

        for st in range(2):
            pl.when((first_ref[b] == 1) & (stage_ref[b] == st))(functools.partial(new_expert, st))

        x = _load_token_tiles(x_ref, BM, tok0)
        a = _dot(x, wg_b[...])
        u = _dot(x, wu_b[...])
        y = _dot((_silu(a) * u).astype(BF16), wd_b[...])
        _store_token_tiles(y_ref, y.astype(BF16).astype(F32), BM, tok0)

    @pl.when(b >= nused_ref[0])
    def _():
        y_ref[tok0 * TOK_ROWS:(tok0 + BM) * TOK_ROWS, :] = jnp.zeros((BM * TOK_ROWS, LANES), jnp.uint32)


def _experts(block_e, first, next_e, next2_e, stage, nused, xb, wg, wu, wd):
    def xmap(b, *_):
        return (b, 0)

    grid_spec = pltpu.PrefetchScalarGridSpec(
        num_scalar_prefetch=6,
        grid=(N_BLOCKS // BLOCKS_PER_STEP,),
        in_specs=[
            pl.BlockSpec(memory_space=pl.ANY),
            pl.BlockSpec(memory_space=pl.ANY),
            pl.BlockSpec(memory_space=pl.ANY),
            pl.BlockSpec(memory_space=pl.ANY),
        ],
        out_specs=pl.BlockSpec((BLOCKS_PER_STEP * BM * TOK_ROWS, LANES), xmap),
        scratch_shapes=[pltpu.VMEM((2, D_MODEL, D_EXPERT), F32), pltpu.VMEM((2, D_MODEL, D_EXPERT), F32),
                        pltpu.VMEM((2, D_EXPERT, D_MODEL), F32),
                        pltpu.VMEM((D_MODEL, D_EXPERT), BF16), pltpu.VMEM((D_MODEL, D_EXPERT), BF16),
                        pltpu.VMEM((D_EXPERT, D_MODEL), BF16), pltpu.SemaphoreType.DMA((2, 3)),
                        pltpu.VMEM((X_SLOTS, BLOCKS_PER_STEP * BM * TOK_ROWS, LANES), jnp.uint32),
                        pltpu.SemaphoreType.DMA((X_SLOTS,))],
    )
    assert N_BLOCKS // BLOCKS_PER_STEP >= X_SLOTS
    return pl.pallas_call(
        _experts_kernel,
        grid_spec=grid_spec,
        out_shape=jax.ShapeDtypeStruct((P_ROWS * TOK_ROWS, LANES), jnp.uint32),
        compiler_params=pltpu.CompilerParams(dimension_semantics=("arbitrary",),
                                             vmem_limit_bytes=VMEM_LIMIT),
        name="experts",
    )(block_e, first, next_e, next2_e, stage, nused, xb, wg, wu, wd)


def _combine_kernel(n_ref, row_ref, tot_ref, yb_ref, h_ref, q_ref, w_ref, g_ref, out_ref, *scratch):
    step = pl.program_id(0)
    bufs, sem = scratch[:COMBINE_TILES], scratch[COMBINE_TILES]

    def start_gather(tile, s):
        _start_tile_runs(n_ref, row_ref, tile,
                         lambda loc, glob: pltpu.make_async_copy(yb_ref.at[glob], bufs[s].at[loc], sem.at[s]))

    @pl.when(step == 0)
    def _():
        for buf in bufs:
            buf[...] = jnp.zeros_like(buf)
        for s in range(GATHER_AHEAD):
            start_gather(s, s)

    for s in range(COMBINE_TILES):
        tile = step * COMBINE_TILES + s
        ahead = s + GATHER_AHEAD
        if ahead < COMBINE_TILES:
            start_gather(tile + GATHER_AHEAD, ahead)
        else:
            pl.when(step + 1 < pl.num_programs(0))(
                functools.partial(start_gather, tile + GATHER_AHEAD, ahead - COMBINE_TILES))

        rows = pl.ds(0, tot_ref[tile] * TOK_ROWS)
        pltpu.make_async_copy(yb_ref.at[rows], bufs[s].at[rows], sem.at[s]).wait()
        ys = _load_token_tiles(bufs[s], SLOTS)
        toks = slice(s * TM_MIX, (s + 1) * TM_MIX)
        slot_i = lax.broadcasted_iota(jnp.int32, (SLOTS, TM_MIX), 0)
        wmat = (jnp.where(slot_i == q_ref[0:1, toks], w_ref[0:1, toks], 0.0)
                + jnp.where(slot_i == q_ref[1:2, toks], w_ref[1:2, toks], 0.0)).astype(BF16)
        hh = h_ref[toks, :] + _dot_tn(wmat, ys)
        out_ref[toks, :] = hh * lax.rsqrt(jnp.mean(hh * hh, axis=-1, keepdims=True) + EPS) * g_ref[...]


def _combine(n_flat, run_rows, tot, yb, h1, ri, rw, gf):
    grid_spec = pltpu.PrefetchScalarGridSpec(
        num_scalar_prefetch=3,
        grid=(N_TILES // COMBINE_TILES,),
        in_specs=[
            pl.BlockSpec(memory_space=pl.ANY),
            pl.BlockSpec((COMBINE_TILES * TM_MIX, D_MODEL), lambda i, *_: (i, 0)),
            pl.BlockSpec((SUBLANES, COMBINE_TILES * TM_MIX), lambda i, *_: (0, i)),
            pl.BlockSpec((SUBLANES, COMBINE_TILES * TM_MIX), lambda i, *_: (0, i)),
            pl.BlockSpec((1, D_MODEL), lambda i, *_: (0, 0)),
        ],
        out_specs=pl.BlockSpec((COMBINE_TILES * TM_MIX, D_MODEL), lambda i, *_: (i, 0)),
        scratch_shapes=[pltpu.VMEM((SLOTS * TOK_ROWS, LANES), jnp.uint32)] * COMBINE_TILES
                       + [pltpu.SemaphoreType.DMA((COMBINE_TILES,))],
    )
    return pl.pallas_call(
        _combine_kernel,
        grid_spec=grid_spec,
        out_shape=jax.ShapeDtypeStruct((SEQ, D_MODEL), F32),
        compiler_params=pltpu.CompilerParams(dimension_semantics=("arbitrary",),
                                             vmem_limit_bytes=VMEM_LIMIT),
        name="combine",
    )(n_flat, run_rows, tot, yb, h1, ri, rw, gf)


def kernel(x, meta, norm1_g, w_in, w_decay_up, b_decay, gla_norm_g, conv_w, conv_b, conv_ln_g,
           conv_ln_b, w_pw2, b_gate, w_out, norm2_g, w_router_group, b_router_group,
           w_router_expert, b_router_expert, w_exp_gate, w_exp_up, w_exp_down, final_norm_g):
    assert x.shape == (1, SEQ, D_MODEL) and w_in.shape[0] == 1
    x2d = x[0]
    front = jnp.concatenate([jnp.zeros((FRONT - N_META, D_MODEL), F32), meta.astype(F32)], axis=0)

    row2 = lambda a: a.reshape(1, -1).astype(F32)
    wq, wk, wv, wog, wdl, wglu, wgt = _wcast(jnp.transpose(w_in[0]))
    up = w_decay_up[0].astype(F32)
    up_hi = up.astype(BF16)
    up_mid = (up - up_hi.astype(F32)).astype(BF16)
    up_lo = (up - up_hi.astype(F32) - up_mid.astype(F32)).astype(BF16)
    wdu = jnp.concatenate([up_hi, up_mid, up_lo, up_hi, up_mid, up_hi], axis=0)
    q, k, v, og, la, u, gates = _inproj(
        front, x2d, row2(norm1_g[0]), wq, wk, wv, wog, jnp.tile(wdl, (DL_COPIES, 1)), wglu, wgt,
        wdu, row2(b_decay[0]), row2(b_gate[0]))

    o_n = _gla(q, k, v, la, row2(gla_norm_g[0]))

    wr = jnp.concatenate([w_router_group[0].T, w_router_expert[0].T,
                          jnp.zeros((LANES - N_GROUPS - N_EXPERTS, D_MODEL), F32)], axis=0).astype(F32)
    br = jnp.concatenate([b_router_group[0], b_router_expert[0],
                          jnp.zeros((LANES - N_GROUPS - N_EXPERTS,), F32)]).reshape(LANES, 1).astype(F32)
    wr_hi = wr.astype(BF16)
    wr_lo = (wr - wr_hi.astype(F32)).astype(BF16)
    h1, hn2, ri, rw, tabn, tabb = _mix(
        o_n, og, u, gates, x2d, conv_w[0].astype(F32).reshape(CONV_WIDTH, ROW_TILE, LANES),
        conv_b[0].astype(F32).reshape(ROW_TILE, LANES), row2(conv_ln_g[0]),
        row2(conv_ln_b[0]), w_pw2[0].astype(BF16), w_out[0].astype(BF16),
        row2(norm2_g[0]), wr_hi, wr_lo, br)

    experts = slice(N_GROUPS, N_GROUPS + N_EXPERTS)
    n_te = tabn[::SUBLANES, experts].astype(jnp.int32)
    seen_te = tabb[::SUBLANES, experts].astype(jnp.int32)
    counts = seen_te[-1] + n_te[-1]
    padded = (counts + BM - 1) // BM * BM
    pad_end = jnp.cumsum(padded)
    pad_start = pad_end - padded
    n_flat = n_te.reshape(-1)
    tot = jnp.sum(n_te, axis=1).astype(jnp.int32)
    run_rows = (pad_start[None, :] + seen_te).reshape(-1).astype(jnp.int32)
    tail_start = (pad_start + counts).astype(jnp.int32)
    tail_len = (padded - counts).astype(jnp.int32)
    blk = jnp.arange(N_BLOCKS, dtype=jnp.int32)
    block_e = jnp.minimum(jnp.sum((pad_end[None, :] <= blk[:, None] * BM).astype(jnp.int32), axis=1),
                          N_EXPERTS - 1).astype(jnp.int32)
    first = jnp.concatenate([jnp.ones((1,), jnp.int32), (block_e[1:] != block_e[:-1]).astype(jnp.int32)])
    nused = (pad_end[-1:] // BM).astype(jnp.int32)
    eid = jnp.arange(N_EXPERTS, dtype=jnp.int32)
    later = jnp.flip(lax.cummin(jnp.flip(jnp.where(counts > 0, eid, N_EXPERTS))))
    nxt = jnp.concatenate([later[1:], jnp.full((2,), N_EXPERTS, jnp.int32)])
    nxt2 = nxt[nxt[:N_EXPERTS]]
    or_none = lambda t: jnp.where(t < N_EXPERTS, t, -1)[block_e].astype(jnp.int32)
    next_e, next2_e = or_none(nxt[:N_EXPERTS]), or_none(nxt2)
    stage = ((jnp.cumsum(first) - 1) % 2).astype(jnp.int32)

    xb = _dispatch(n_flat, run_rows, tot, tail_start, tail_len, nused, hn2, ri)
    yb = _experts(block_e, first, next_e, next2_e, stage, nused, xb, w_exp_gate[0], w_exp_up[0], w_exp_down[0])
    out = _combine(n_flat, run_rows, tot, yb, h1, ri, rw, row2(final_norm_g))
    return out[None]
```

```python
import functools

import jax
import jax.numpy as jnp
import numpy as np
from jax import lax
from jax.experimental import pallas as pl
from jax.experimental.pallas import tpu as pltpu

F32 = jnp.float32
BF16 = jnp.bfloat16

D_MODEL = 1024
SEQ = 16384
N_META = 16
GLA_HEADS = 4
GLA_DK = 128
GLA_DV = 256
GLA_RANK = 16
GLA_TAU = 16.0
CHUNK = 64
QK_W = GLA_HEADS * GLA_DK
V_W = GLA_HEADS * GLA_DV
CONV_WIDTH = 31
N_GROUPS = 8
EXPERTS_PER_GROUP = 8
N_EXPERTS = N_GROUPS * EXPERTS_PER_GROUP
TOP_K = 2
D_EXPERT = 512
EPS = 1e-6
LOG2E = 1.4426950408889634

LANES = 128
SUBLANES = 8
ROW_TILE = D_MODEL // LANES
TOK_ROWS = D_MODEL // 2 // LANES
TOK_ALIGN = SUBLANES // TOK_ROWS

FRONT = 512
LP = FRONT + SEQ
TM_PROJ = 512
DL_COPIES = 6
TM_GLA = 512
TM_MIX = 256
MIX_TILES = 2
HIST = 32
CONV_RB = 16
N_TILES = SEQ // TM_MIX
SLOTS = TOP_K * TM_MIX + N_EXPERTS * (TOK_ALIGN - 1)
TILES_PER_STEP = 2
COMBINE_TILES = 4
GATHER_AHEAD = 3
BM = 256
BLOCKS_PER_STEP = 4
DMA_PRIORITIES = 2
X_SLOTS = 3
N_BLOCKS = (N_TILES * SLOTS) // BM + N_EXPERTS
P_ROWS = N_BLOCKS * BM
VMEM_LIMIT = 56 * 1024 * 1024


def _sigmoid(x):
    return 0.5 * jnp.tanh(0.5 * x) + 0.5


def _silu(x):
    return x * _sigmoid(x)


def _dot(a, b, **kw):
    return jnp.dot(a, b, preferred_element_type=F32, **kw)


def _dot_nt(a, b):
    return lax.dot_general(a, b, (((1,), (1,)), ((), ())), preferred_element_type=F32)


def _dot_tn(a, b):
    return lax.dot_general(a, b, (((0,), (0,)), ((), ())), preferred_element_type=F32)


def _load_row_tiles(ref, n_rows):
    return jnp.concatenate([ref[pl.ds(s, n_rows, stride=ROW_TILE), :] for s in range(ROW_TILE)], axis=-1)


def _store_row_tiles(ref, val, n_rows):
    for s in range(ROW_TILE):
        ref[pl.ds(s, n_rows, stride=ROW_TILE), :] = val[:, s * LANES:(s + 1) * LANES]


def _load_token_tiles(ref, n_tok, first_tok=0):
    r0 = first_tok * TOK_ROWS
    w = jnp.concatenate([ref[pl.ds(r0 + s, n_tok, stride=TOK_ROWS), :] for s in range(TOK_ROWS)], axis=-1)
    lo = pltpu.bitcast(w << 16, F32)
    hi = pltpu.bitcast(w & jnp.uint32(0xFFFF0000), F32)
    return jnp.concatenate([lo, hi], axis=-1).astype(BF16)


def _store_token_tiles(ref, val, n_tok, first_tok=0):
    bits = pltpu.bitcast(val, jnp.uint32)
    w = (bits[:, 0:D_MODEL // 2] >> 16) | (bits[:, D_MODEL // 2:D_MODEL] & jnp.uint32(0xFFFF0000))
    r0 = first_tok * TOK_ROWS
    for s in range(TOK_ROWS):
        ref[pl.ds(r0 + s, n_tok, stride=TOK_ROWS), :] = w[:, s * LANES:(s + 1) * LANES]


IN_SPLITS = (QK_W, QK_W, V_W, D_MODEL, GLA_RANK, 2 * D_MODEL, 2 * D_MODEL)
WCAST_ROWS = 512


def _wcast_chunks():
    chunks, off = [], 0
    for g, n in enumerate(IN_SPLITS):
        for c in range(0, n, WCAST_ROWS):
            chunks.append((g, off + c, c, min(WCAST_ROWS, n - c)))
        off += n
    return chunks


def _wcast_kernel(wt_hbm, *refs):
    outs, buf, sem = refs[:len(IN_SPLITS)], refs[len(IN_SPLITS)], refs[len(IN_SPLITS) + 1]
    chunks = _wcast_chunks()

    def copy(i):
        _, row0, _, n = chunks[i]
        return pltpu.make_async_copy(wt_hbm.at[row0:row0 + n], buf.at[i % 2, 0:n], sem.at[i % 2])

    copy(0).start()
    for i, (g, _, r0, n) in enumerate(chunks):
        if i + 1 < len(chunks):
            copy(i + 1).start()
        copy(i).wait()
        outs[g][r0:r0 + n, :] = buf[i % 2, 0:n].astype(BF16)


def _wcast(wt):
    return pl.pallas_call(
        _wcast_kernel,
        in_specs=[pl.BlockSpec(memory_space=pl.ANY)],
        out_specs=[pl.BlockSpec(memory_space=pltpu.VMEM) for _ in IN_SPLITS],
        out_shape=[jax.ShapeDtypeStruct((n, D_MODEL), BF16) for n in IN_SPLITS],
        scratch_shapes=[pltpu.VMEM((2, WCAST_ROWS, D_MODEL), F32), pltpu.SemaphoreType.DMA((2,))],
        compiler_params=pltpu.CompilerParams(vmem_limit_bytes=VMEM_LIMIT),
        name="wcast",
    )(wt)


def _inproj_kernel(front_ref, x_ref, g_ref, wq, wk, wv, wog, wdl, wglu, wgt, wdu, bd, bg,
                   q_o, k_o, v_o, og_o, la_o, u_o, gt_o):
    i = pl.program_id(0)
    h = jnp.where(i < FRONT // TM_PROJ, front_ref[...], x_ref[...])
    ms = jnp.mean(h * h, axis=-1, keepdims=True)
    hn = (h * lax.rsqrt(ms + EPS) * g_ref[...]).astype(BF16)
    q_o[...] = _dot_nt(hn, wq[...]).astype(BF16)
    k_o[...] = _dot_nt(hn, wk[...]).astype(BF16)
    v_o[...] = _dot_nt(hn, wv[...]).astype(BF16)
    og_o[...] = _silu(_dot_nt(hn, wog[...])).astype(BF16)
    glu = _dot_nt(hn, wglu[...])
    _store_row_tiles(u_o, glu[:, 0:D_MODEL] * _sigmoid(glu[:, D_MODEL:2 * D_MODEL]), TM_PROJ)
    gt_o[...] = _sigmoid(_dot_nt(hn, wgt[...]) + bg[...]).astype(BF16)
    d6 = _dot_nt(hn, wdl[...])
    hi = d6.astype(BF16).astype(F32)
    mid = (d6 - hi).astype(BF16).astype(F32)
    lane = lax.broadcasted_iota(jnp.int32, d6.shape, 1)
    pieces = jnp.where(lane < 3 * GLA_RANK, hi, jnp.where(lane < 5 * GLA_RANK, mid, d6 - hi - mid))
    z = _dot(pieces.astype(BF16), wdu[...]) + bd[...]
    la_o[...] = (jnp.minimum(z, 0.0) - jnp.log(1.0 + jnp.exp(-jnp.abs(z)))) * (LOG2E / GLA_TAU)


def _inproj(front, x2d, g1, wq, wk, wv, wog, wdl, wglu, wgt, wdu, bd, bg):
    n_front = FRONT // TM_PROJ
    grid = (LP // TM_PROJ,)
    row = lambda w: pl.BlockSpec((TM_PROJ, w), lambda i: (i, 0))
    full = lambda a: pl.BlockSpec(a.shape, lambda i: (0, 0))
    return pl.pallas_call(
        _inproj_kernel,
        grid=grid,
        in_specs=[
            pl.BlockSpec((TM_PROJ, D_MODEL), lambda i: (jnp.minimum(i, n_front - 1), 0)),
            pl.BlockSpec((TM_PROJ, D_MODEL), lambda i: (jnp.maximum(i - n_front, 0), 0)),
            full(g1), full(wq), full(wk), full(wv), full(wog), full(wdl), full(wglu), full(wgt),
            full(wdu), full(bd), full(bg),
        ],
        out_specs=[row(QK_W), row(QK_W), row(V_W), row(D_MODEL), row(QK_W),
                   pl.BlockSpec((TM_PROJ * ROW_TILE, LANES), lambda i: (i, 0)), row(2 * D_MODEL)],
        out_shape=[
            jax.ShapeDtypeStruct((LP, QK_W), BF16), jax.ShapeDtypeStruct((LP, QK_W), BF16),
            jax.ShapeDtypeStruct((LP, V_W), BF16), jax.ShapeDtypeStruct((LP, D_MODEL), BF16),
            jax.ShapeDtypeStruct((LP, QK_W), F32), jax.ShapeDtypeStruct((LP * ROW_TILE, LANES), F32),
            jax.ShapeDtypeStruct((LP, 2 * D_MODEL), BF16),
        ],
        compiler_params=pltpu.CompilerParams(dimension_semantics=("arbitrary",),
                                             vmem_limit_bytes=VMEM_LIMIT),
        name="inproj",
    )(front, x2d, g1, wq, wk, wv, wog, wdl, wglu, wgt, wdu, bd, bg)


GLA_LEVELS = (32, 16, 8, 4, 2, 1)
GLA_FINE = tuple(m for m in GLA_LEVELS if m < SUBLANES)
LA_SPLIT = 3


def _gla_tables():
    r = np.arange(CHUNK)
    t, c = r[:, None], r[None, :]
    rows = [c <= t]
    pair = []
    for m in GLA_LEVELS:
        mid = (t // (2 * m)) * (2 * m) + m - 1
        right = (t % (2 * m)) >= m
        if m in GLA_FINE:
            rows.append(np.where(right, (c > mid) & (c <= t), (c > t) & (c <= mid)))
        pair.append(((t // (2 * m)) == (c // (2 * m))) & right & ((c % (2 * m)) < m))
    pair.append(t == c)
    sums = np.concatenate(rows, axis=0).astype(np.float32)
    sums = np.concatenate([sums] * LA_SPLIT, axis=1)
    pair = np.stack([np.kron(np.eye(GLA_HEADS), p) for p in pair]).astype(np.float32)
    return jnp.asarray(sums, BF16), jnp.asarray(pair)


def _split_bf16(x):
    pieces = []
    rest = x
    for _ in range(LA_SPLIT):
        p = rest.astype(BF16)
        pieces.append(p)
        rest = rest - p.astype(F32)
    return jnp.concatenate(pieces, axis=0)


def _gla_exponents(a):
    b = a[0:CHUNK]
    parts = [b, b[CHUNK - 1:CHUNK] - b]
    for m in GLA_LEVELS:
        if m in GLA_FINE:
            continue
        for lo in range(0, CHUNK, 2 * m):
            ref = b[lo + m - 1:lo + m]
            parts += [ref - b[lo:lo + m], b[lo + m:lo + 2 * m] - ref]
    return jnp.concatenate(parts + [a[CHUNK:]], axis=0)


def _gla_kernel(q_ref, k_ref, v_ref, la_ref, g_ref, sums_ref, pair_ref, o_ref, s_ref):
    @pl.when(pl.program_id(0) == 0)
    def _():
        s_ref[...] = jnp.zeros_like(s_ref)

    row = lax.broadcasted_iota(jnp.int32, (CHUNK, GLA_DK), 0)
    right_rows = [(row & m) != 0 for m in GLA_LEVELS]
    sums = sums_ref[...]
    n_lvl = len(GLA_LEVELS)

    def body(c, carry):
        r0 = pl.multiple_of(c * CHUNK, CHUNK)
        rows = pl.ds(r0, CHUNK)
        states = [s_ref[h] for h in range(GLA_HEADS)]
        qs, ks, es, decs = [], [], [], []
        for hp in range(GLA_HEADS // 2):
            la3 = _split_bf16(la_ref[rows, 2 * hp * GLA_DK:2 * (hp + 1) * GLA_DK])
            a2 = _dot(sums, la3)
            e2 = jnp.exp2(_gla_exponents(a2))
            dec2 = jnp.exp2(jnp.transpose(jnp.broadcast_to(a2[CHUNK - 1:CHUNK], (LANES, 2 * GLA_DK))))
            for h in (2 * hp, 2 * hp + 1):
                half = slice((h % 2) * GLA_DK, (h % 2 + 1) * GLA_DK)
                kc = slice(h * GLA_DK, (h + 1) * GLA_DK)
                qs.append(q_ref[rows, kc].astype(F32) * (GLA_DK ** -0.5))
                ks.append(k_ref[rows, kc].astype(F32))
                es.append(e2[:, half])
                decs.append(dec2[half])
        vbs = [v_ref[rows, h * GLA_DV:(h + 1) * GLA_DV] for h in range(GLA_HEADS)]

        new_states, qes = [], []
        for h in range(GLA_HEADS):
            qes.append((qs[h] * es[h][0:CHUNK]).astype(BF16))
            kd = (ks[h] * es[h][CHUNK:2 * CHUNK]).astype(BF16)
            new_states.append(states[h] * jnp.concatenate([decs[h]] * (GLA_DV // LANES), axis=1)
                              + _dot_tn(kd, vbs[h]))

        stack = lambda parts: jnp.concatenate(parts, axis=0)
        sc = _dot_nt(stack([q.astype(BF16) for q in qs]), stack([k.astype(BF16) for k in ks])) * pair_ref[n_lvl]
        for lvl in range(n_lvl):
            x = stack([(jnp.where(right_rows[lvl], qs[h], ks[h])
                        * es[h][(2 + lvl) * CHUNK:(3 + lvl) * CHUNK]).astype(BF16) for h in range(GLA_HEADS)])
            sc = sc + _dot_nt(x, x) * pair_ref[lvl]
        o_all = _dot(sc.astype(BF16), stack(vbs))
        for h in range(GLA_HEADS):
            vc = slice(h * GLA_DV, (h + 1) * GLA_DV)
            o = o_all[h * CHUNK:(h + 1) * CHUNK] + _dot(qes[h], states[h].astype(BF16))
            o = o * lax.rsqrt(jnp.mean(o * o, axis=-1, keepdims=True) + EPS) * g_ref[:, vc]
            o_ref[rows, vc] = o.astype(BF16)
        for h in range(GLA_HEADS):
            s_ref[h] = new_states[h]
        return carry

    lax.fori_loop(0, TM_GLA // CHUNK, body, 0, unroll=4)


def _gla(q, k, v, la, g):
    sums, pair = _gla_tables()
    row = lambda w: pl.BlockSpec((TM_GLA, w), lambda t: (t, 0))
    full = lambda a: pl.BlockSpec(a.shape, lambda t: (0,) * a.ndim)
    return pl.pallas_call(
        _gla_kernel,
        grid=(LP // TM_GLA,),
        in_specs=[row(QK_W), row(QK_W), row(V_W), row(QK_W), full(g), full(sums), full(pair)],
        out_specs=row(V_W),
        out_shape=jax.ShapeDtypeStruct((LP, V_W), BF16),
        scratch_shapes=[pltpu.VMEM((GLA_HEADS, GLA_DK, GLA_DV), F32)],
        compiler_params=pltpu.CompilerParams(dimension_semantics=("arbitrary",),
                                             vmem_limit_bytes=VMEM_LIMIT),
        name="gla",
    )(q, k, v, la, g, sums, pair)


def _mix_kernel(o_ref, og_ref, u_ref, hist_ref, gt_ref, x_ref, cw_ref, cb_ref, lg_ref, lb_ref,
                wpw_ref, wo_ref, g2_ref, wrh_ref, wrl_ref, br_ref,
                h_o, hn_o, ri_o, rw_o, tabn_o, tabb_o, cbuf, seen):
    @pl.when(pl.program_id(0) == 0)
    def _():
        seen[...] = jnp.zeros_like(seen)

    for j in range(MIX_TILES):
        r = slice(j * TM_MIX, (j + 1) * TM_MIX)
        tab = slice(j * SUBLANES, (j + 1) * SUBLANES)

        def u_row(t, j=j):
            t += j * TM_MIX
            ref, t = (hist_ref, t + HIST) if t < 0 else (u_ref, t)
            return ref[t * ROW_TILE:(t + 1) * ROW_TILE, :]

        _mix_tile(o_ref.at[r], og_ref.at[r], u_row, gt_ref.at[r], x_ref.at[r], cw_ref, cb_ref,
                  lg_ref, lb_ref, wpw_ref, wo_ref, g2_ref, wrh_ref, wrl_ref, br_ref,
                  h_o.at[r], hn_o.at[r], ri_o.at[:, r], rw_o.at[:, r], tabn_o.at[tab], tabb_o.at[tab],
                  cbuf, seen)


def _mix_tile(o_ref, og_ref, u_row, gt_ref, x_ref, cw_ref, cb_ref, lg_ref, lb_ref,
              wpw_ref, wo_ref, g2_ref, wrh_ref, wrl_ref, br_ref,
              h_o, hn_o, ri_o, rw_o, tabn_o, tabb_o, cbuf, seen):
    taps = [cw_ref[jj] for jj in range(CONV_WIDTH)]
    for t0 in range(0, TM_MIX, CONV_RB):
        part = [cb_ref[...]] * CONV_RB
        for rr in range(t0 - (CONV_WIDTH - 1), t0 + CONV_RB):
            ur = u_row(rr)
            for t in range(max(t0, rr), min(t0 + CONV_RB, rr + CONV_WIDTH)):
                part[t - t0] = part[t - t0] + taps[rr - t + CONV_WIDTH - 1] * ur
        for g in range(CONV_RB):
            cbuf[(t0 + g) * ROW_TILE:(t0 + g + 1) * ROW_TILE, :] = part[g]
    acc = _load_row_tiles(cbuf, TM_MIX)

    mu = jnp.mean(acc, axis=-1, keepdims=True)
    xc = acc - mu
    ln = xc * lax.rsqrt(jnp.mean(xc * xc, axis=-1, keepdims=True) + EPS) * lg_ref[...] + lb_ref[...]
    branch_b = _dot(_silu(ln).astype(BF16), wpw_ref[...])

    branch_a = o_ref[...].astype(F32) * og_ref[...].astype(F32)
    g_a = gt_ref[:, 0:D_MODEL].astype(F32)
    g_b = gt_ref[:, D_MODEL:2 * D_MODEL].astype(F32)
    merged = (g_a * branch_a + g_b * branch_b).astype(BF16)
    h1 = x_ref[...] + _dot(merged, wo_ref[...])
    h_o[...] = h1

    hn2 = h1 * lax.rsqrt(jnp.mean(h1 * h1, axis=-1, keepdims=True) + EPS) * g2_ref[...]
    hn_hi = hn2.astype(BF16)
    hn_o[...] = hn_hi

    hn_lo = (hn2 - hn_hi.astype(F32)).astype(BF16)
    logits = (_dot_nt(wrh_ref[...], hn_hi) + _dot_nt(wrh_ref[...], hn_lo) + _dot_nt(wrl_ref[...], hn_hi)
              + br_ref[...])
    row = lax.broadcasted_iota(jnp.int32, logits.shape, 0)
    rowf = row.astype(F32)
    neg = -jnp.inf
    is_g = row < N_GROUPS
    lg = jnp.where(is_g, logits, neg)
    gmax = jnp.max(lg, axis=0, keepdims=True)
    gidx = jnp.min(jnp.where(lg == gmax, rowf, float(N_GROUPS)), axis=0, keepdims=True)
    g_w = 1.0 / jnp.sum(jnp.where(is_g, jnp.exp(lg - gmax), 0.0), axis=0, keepdims=True)
    erow = rowf - float(N_GROUPS)
    egrp = ((row - N_GROUPS) >> 3).astype(F32)
    in_grp = (row >= N_GROUPS) & (row < N_GROUPS + N_EXPERTS) & (egrp == gidx)
    le = jnp.where(in_grp, logits, neg)
    m1 = jnp.max(le, axis=0, keepdims=True)
    i1 = jnp.min(jnp.where(le == m1, erow, float(N_EXPERTS)), axis=0, keepdims=True)
    le2 = jnp.where(erow == i1, neg, le)
    m2 = jnp.max(le2, axis=0, keepdims=True)
    i2 = jnp.min(jnp.where(le2 == m2, erow, float(N_EXPERTS)), axis=0, keepdims=True)
    t = jnp.exp(m2 - m1)
    w1 = g_w / (1.0 + t)
    w2 = g_w * t / (1.0 + t)

    def run_len(n):
        return jnp.floor((n + (TOK_ALIGN - 1)) * (1.0 / TOK_ALIGN)) * TOK_ALIGN

    oh1 = erow == i1
    oh2 = erow == i2
    oh = jnp.where(oh1 | oh2, 1.0, 0.0).astype(BF16)
    tr = lax.broadcasted_iota(jnp.int32, (TM_MIX, TM_MIX), 0)
    tc = lax.broadcasted_iota(jnp.int32, (TM_MIX, TM_MIX), 1)
    before_tok = _dot(oh, (tr < tc).astype(BF16))
    n_col = jnp.sum(oh.astype(F32), axis=1, keepdims=True)
    er = lax.broadcasted_iota(jnp.int32, (LANES, LANES), 0)
    ec = lax.broadcasted_iota(jnp.int32, (LANES, LANES), 1)
    before_exp = _dot((ec < er).astype(BF16), jnp.broadcast_to(run_len(n_col), logits.shape).astype(BF16))
    where_to = before_exp + before_tok
    q1 = jnp.sum(jnp.where(oh1, where_to, 0.0), axis=0, keepdims=True)
    q2 = jnp.sum(jnp.where(oh2, where_to, 0.0), axis=0, keepdims=True)

    n_rows = run_len(_dot_nt(jnp.ones((SUBLANES, TM_MIX), BF16), oh))
    tabn_o[...] = n_rows
    tabb_o[...] = seen[...]
    seen[...] = seen[...] + n_rows

    ri_o[...] = jnp.zeros_like(ri_o)
    ri_o[0:1, :] = q1.astype(jnp.int32)
    ri_o[1:2, :] = q2.astype(jnp.int32)
    rw_o[...] = jnp.zeros_like(rw_o)
    rw_o[0:1, :] = w1
    rw_o[1:2, :] = w2


def _mix(o_n, og, u, gates, x2d, cw, cb, lg, lb, wpw, wo, g2, wrh, wrl, br):
    tm = MIX_TILES * TM_MIX
    n_front = FRONT // tm
    grid = (SEQ // tm,)
    rowp = lambda w: pl.BlockSpec((tm, w), lambda i: (i + n_front, 0))
    full = lambda a: pl.BlockSpec(a.shape, lambda i: (0,) * a.ndim)
    hist_blocks = tm // HIST
    return pl.pallas_call(
        _mix_kernel,
        grid=grid,
        in_specs=[
            rowp(V_W), rowp(D_MODEL),
            pl.BlockSpec((tm * ROW_TILE, LANES), lambda i: (i + n_front, 0)),
            pl.BlockSpec((HIST * ROW_TILE, LANES), lambda i: ((i + n_front) * hist_blocks - 1, 0)),
            rowp(2 * D_MODEL),
            pl.BlockSpec((tm, D_MODEL), lambda i: (i, 0)),
            full(cw), full(cb), full(lg), full(lb), full(wpw), full(wo), full(g2),
            full(wrh), full(wrl), full(br),
        ],
        out_specs=[
            pl.BlockSpec((tm, D_MODEL), lambda i: (i, 0)),
            pl.BlockSpec((tm, D_MODEL), lambda i: (i, 0)),
            pl.BlockSpec((SUBLANES, tm), lambda i: (0, i)),
            pl.BlockSpec((SUBLANES, tm), lambda i: (0, i)),
            pl.BlockSpec((MIX_TILES * SUBLANES, LANES), lambda i: (i, 0)),
            pl.BlockSpec((MIX_TILES * SUBLANES, LANES), lambda i: (i, 0)),
        ],
        out_shape=[
            jax.ShapeDtypeStruct((SEQ, D_MODEL), F32),
            jax.ShapeDtypeStruct((SEQ, D_MODEL), BF16),
            jax.ShapeDtypeStruct((SUBLANES, SEQ), jnp.int32),
            jax.ShapeDtypeStruct((SUBLANES, SEQ), F32),
            jax.ShapeDtypeStruct((N_TILES * SUBLANES, LANES), F32),
            jax.ShapeDtypeStruct((N_TILES * SUBLANES, LANES), F32),
        ],
        scratch_shapes=[pltpu.VMEM((TM_MIX * ROW_TILE, LANES), F32), pltpu.VMEM((SUBLANES, LANES), F32)],
        compiler_params=pltpu.CompilerParams(dimension_semantics=("arbitrary",),
                                             vmem_limit_bytes=VMEM_LIMIT),
        name="mix",
    )(o_n, og, u, u, gates, x2d, cw, cb, lg, lb, wpw, wo, g2, wrh, wrl, br)


def _tok_rows(start_tok, n_tok):
    return pl.ds(pl.multiple_of(start_tok * TOK_ROWS, SUBLANES), n_tok * TOK_ROWS)


def _start_tile_runs(n_ref, row_ref, tile, make_copy):
    def run_pair(i, off):
        for prio in range(DMA_PRIORITIES):
            e = i * DMA_PRIORITIES + prio
            n = n_ref[tile * N_EXPERTS + e]

            @pl.when(n > 0)
            def _(e=e, n=n, off=off, prio=prio):
                make_copy(_tok_rows(off, n), _tok_rows(row_ref[tile * N_EXPERTS + e], n)).start(priority=prio)

            off = off + n
        return off

    lax.fori_loop(0, N_EXPERTS // DMA_PRIORITIES, run_pair, 0, unroll=4)


def _dispatch_kernel(n_ref, row_ref, tot_ref, ts_ref, tl_ref, nused_ref, hn_ref, q_ref, xb_ref,
                     sb0, sb1, zbuf, sem, fsem):
    step = pl.program_id(0)
    last = pl.num_programs(0) - 1

    def fills(start):
        def tail(e, c):
            n = tl_ref[e]

            @pl.when(n > 0)
            def _():
                cp = pltpu.make_async_copy(zbuf.at[pl.ds(0, n * TOK_ROWS)],
                                           xb_ref.at[_tok_rows(ts_ref[e], n)], fsem)
                cp.start() if start else cp.wait()

            return c

        lax.fori_loop(0, N_EXPERTS, tail, 0)

        def block(b, c):
            cp = pltpu.make_async_copy(zbuf, xb_ref.at[_tok_rows(b * BM, BM)], fsem)
            cp.start() if start else cp.wait()
            return c

        lax.fori_loop(nused_ref[0], N_BLOCKS, block, 0)

    @pl.when(step == 0)
    def _():
        zbuf[...] = jnp.zeros_like(zbuf)
        fills(start=True)

    def tile_runs(tile, s, sbuf):
        rows = pl.ds(0, tot_ref[tile] * TOK_ROWS)
        return pltpu.make_async_copy(sbuf.at[rows], xb_ref.at[rows], sem.at[s])

    for s, sbuf in enumerate((sb0, sb1)):
        tile = step * TILES_PER_STEP + s
        toks = slice(s * TM_MIX, (s + 1) * TM_MIX)
        slot_i = lax.broadcasted_iota(jnp.int32, (SLOTS, TM_MIX), 0)
        onehot = jnp.where((slot_i == q_ref[0:1, toks]) | (slot_i == q_ref[1:2, toks]), 1.0, 0.0).astype(BF16)
        srt = _dot(onehot, hn_ref[toks, :])

        @pl.when(step >= 1)
        def _():
            tile_runs(tile - TILES_PER_STEP, s, sbuf).wait()

        _store_token_tiles(sbuf, srt, SLOTS)
        _start_tile_runs(n_ref, row_ref, tile,
                         lambda loc, glob: pltpu.make_async_copy(sbuf.at[loc], xb_ref.at[glob], sem.at[s]))

    @pl.when(step == last)
    def _():
        for s, sbuf in enumerate((sb0, sb1)):
            tile_runs(step * TILES_PER_STEP + s, s, sbuf).wait()
        fills(start=False)


def _dispatch(n_flat, run_rows, tot, tail_start, tail_len, nused, hn2, ri):
    grid_spec = pltpu.PrefetchScalarGridSpec(
        num_scalar_prefetch=6,
        grid=(N_TILES // TILES_PER_STEP,),
        in_specs=[pl.BlockSpec((TILES_PER_STEP * TM_MIX, D_MODEL), lambda i, *_: (i, 0)),
                  pl.BlockSpec((SUBLANES, TILES_PER_STEP * TM_MIX), lambda i, *_: (0, i))],
        out_specs=pl.BlockSpec(memory_space=pl.ANY),
        scratch_shapes=[pltpu.VMEM((SLOTS * TOK_ROWS, LANES), jnp.uint32),
                        pltpu.VMEM((SLOTS * TOK_ROWS, LANES), jnp.uint32),
                        pltpu.VMEM((BM * TOK_ROWS, LANES), jnp.uint32),
                        pltpu.SemaphoreType.DMA((2,)), pltpu.SemaphoreType.DMA(())],
    )
    return pl.pallas_call(
        _dispatch_kernel,
        grid_spec=grid_spec,
        out_shape=jax.ShapeDtypeStruct((P_ROWS * TOK_ROWS, LANES), jnp.uint32),
        compiler_params=pltpu.CompilerParams(dimension_semantics=("arbitrary",),
                                             vmem_limit_bytes=VMEM_LIMIT),
        name="dispatch",
    )(n_flat, run_rows, tot, tail_start, tail_len, nused, hn2, ri)


def _experts_kernel(be_ref, first_ref, next_ref, next2_ref, stage_ref, nused_ref,
                    x_hbm, wg_hbm, wu_hbm, wd_hbm, y_ref, wg_f, wu_f, wd_f, wg_b, wu_b, wd_b, sem,
                    xbuf, xsem):
    step = pl.program_id(0)
    n_steps = pl.num_programs(0)
    step_rows = BLOCKS_PER_STEP * BM * TOK_ROWS

    def x_copy(st):
        rows = pl.ds(pl.multiple_of(st * step_rows, step_rows), step_rows)
        return pltpu.make_async_copy(x_hbm.at[rows], xbuf.at[st % X_SLOTS], xsem.at[st % X_SLOTS])

    @pl.when(step == 0)
    def _():
        for st in range(X_SLOTS - 1):
            x_copy(st).start()

    @pl.when(step + X_SLOTS - 1 < n_steps)
    def _():
        x_copy(step + X_SLOTS - 1).start()

    x_copy(step).wait()
    x_ref = xbuf.at[step % X_SLOTS]

    def weight_copies(e, st):
        return (pltpu.make_async_copy(wg_hbm.at[e], wg_f.at[st], sem.at[st, 0]),
                pltpu.make_async_copy(wu_hbm.at[e], wu_f.at[st], sem.at[st, 1]),
                pltpu.make_async_copy(wd_hbm.at[e], wd_f.at[st], sem.at[st, 2]))

    @pl.when((pl.program_id(0) == 0) & (nused_ref[0] > 0))
    def _():
        for c in weight_copies(be_ref[0], 0):
            c.start()

        @pl.when(next_ref[0] >= 0)
        def _():
            for c in weight_copies(next_ref[0], 1):
                c.start()

    for j in range(BLOCKS_PER_STEP):
        _expert_block(pl.program_id(0) * BLOCKS_PER_STEP + j, j * BM, be_ref, first_ref, next2_ref,
                      stage_ref, nused_ref, x_ref, y_ref, wg_f, wu_f, wd_f, wg_b, wu_b, wd_b, weight_copies)


def _expert_block(b, tok0, be_ref, first_ref, next2_ref, stage_ref, nused_ref, x_ref, y_ref,
                  wg_f, wu_f, wd_f, wg_b, wu_b, wd_b, weight_copies):
    @pl.when(b < nused_ref[0])
    def _():
        def new_expert(st):
            for c in weight_copies(be_ref[b], st):
                c.wait()
            wg_b[...] = wg_f[st].astype(BF16)
            wu_b[...] = wu_f[st].astype(BF16)
            wd_b[...] = wd_f[st].astype(BF16)

            @pl.when(next2_ref[b] >= 0)
            def _():
                for c in weight_copies(next2_ref[b], st):
                    c.start()

        for st in range(2):
            pl.when((first_ref[b] == 1) & (stage_ref[b] == st))(functools.partial(new_expert, st))

        x = _load_token_tiles(x_ref, BM, tok0)
        a = _dot(x, wg_b[...])
        u = _dot(x, wu_b[...])
        y = _dot((_silu(a) * u).astype(BF16), wd_b[...])
        _store_token_tiles(y_ref, y.astype(BF16).astype(F32), BM, tok0)

    @pl.when(b >= nused_ref[0])
    def _():
        y_ref[tok0 * TOK_ROWS:(tok0 + BM) * TOK_ROWS, :] = jnp.zeros((BM * TOK_ROWS, LANES), jnp.uint32)


def _experts(block_e, first, next_e, next2_e, stage, nused, xb, wg, wu, wd):
    def xmap(b, *_):
        return (b, 0)

    grid_spec = pltpu.PrefetchScalarGridSpec(
        num_scalar_prefetch=6,
        grid=(N_BLOCKS // BLOCKS_PER_STEP,),
        in_specs=[
            pl.BlockSpec(memory_space=pl.ANY),
            pl.BlockSpec(memory_space=pl.ANY),
            pl.BlockSpec(memory_space=pl.ANY),
            pl.BlockSpec(memory_space=pl.ANY),
        ],
        out_specs=pl.BlockSpec((BLOCKS_PER_STEP * BM * TOK_ROWS, LANES), xmap),
        scratch_shapes=[pltpu.VMEM((2, D_MODEL, D_EXPERT), F32), pltpu.VMEM((2, D_MODEL, D_EXPERT), F32),
                        pltpu.VMEM((2, D_EXPERT, D_MODEL), F32),
                        pltpu.VMEM((D_MODEL, D_EXPERT), BF16), pltpu.VMEM((D_MODEL, D_EXPERT), BF16),
                        pltpu.VMEM((D_EXPERT, D_MODEL), BF16), pltpu.SemaphoreType.DMA((2, 3)),
                        pltpu.VMEM((X_SLOTS, BLOCKS_PER_STEP * BM * TOK_ROWS, LANES), jnp.uint32),
                        pltpu.SemaphoreType.DMA((X_SLOTS,))],
    )
    assert N_BLOCKS // BLOCKS_PER_STEP >= X_SLOTS
    return pl.pallas_call(
        _experts_kernel,
        grid_spec=grid_spec,
        out_shape=jax.ShapeDtypeStruct((P_ROWS * TOK_ROWS, LANES), jnp.uint32),
        compiler_params=pltpu.CompilerParams(dimension_semantics=("arbitrary",),
                                             vmem_limit_bytes=VMEM_LIMIT),
        name="experts",
    )(block_e, first, next_e, next2_e, stage, nused, xb, wg, wu, wd)


def _combine_kernel(n_ref, row_ref, tot_ref, yb_ref, h_ref, q_ref, w_ref, g_ref, out_ref, *scratch):
    step = pl.program_id(0)
    bufs, sem = scratch[:COMBINE_TILES], scratch[COMBINE_TILES]

    def start_gather(tile, s):
        _start_tile_runs(n_ref, row_ref, tile,
                         lambda loc, glob: pltpu.make_async_copy(yb_ref.at[glob], bufs[s].at[loc], sem.at[s]))

    @pl.when(step == 0)
    def _():
        for buf in bufs:
            buf[...] = jnp.zeros_like(buf)
        for s in range(GATHER_AHEAD):
            start_gather(s, s)

    for s in range(COMBINE_TILES):
        tile = step * COMBINE_TILES + s
        ahead = s + GATHER_AHEAD
        if ahead < COMBINE_TILES:
            start_gather(tile + GATHER_AHEAD, ahead)
        else:
            pl.when(step + 1 < pl.num_programs(0))(
                functools.partial(start_gather, tile + GATHER_AHEAD, ahead - COMBINE_TILES))

        rows = pl.ds(0, tot_ref[tile] * TOK_ROWS)
        pltpu.make_async_copy(yb_ref.at[rows], bufs[s].at[rows], sem.at[s]).wait()
        ys = _load_token_tiles(bufs[s], SLOTS)
        toks = slice(s * TM_MIX, (s + 1) * TM_MIX)
        slot_i = lax.broadcasted_iota(jnp.int32, (SLOTS, TM_MIX), 0)
        wmat = (jnp.where(slot_i == q_ref[0:1, toks], w_ref[0:1, toks], 0.0)
                + jnp.where(slot_i == q_ref[1:2, toks], w_ref[1:2, toks], 0.0)).astype(BF16)
        hh = h_ref[toks, :] + _dot_tn(wmat, ys)
        out_ref[toks, :] = hh * lax.rsqrt(jnp.mean(hh * hh, axis=-1, keepdims=True) + EPS) * g_ref[...]


def _combine(n_flat, run_rows, tot, yb, h1, ri, rw, gf):
    grid_spec = pltpu.PrefetchScalarGridSpec(
        num_scalar_prefetch=3,
        grid=(N_TILES // COMBINE_TILES,),
        in_specs=[
            pl.BlockSpec(memory_space=pl.ANY),
            pl.BlockSpec((COMBINE_TILES * TM_MIX, D_MODEL), lambda i, *_: (i, 0)),
            pl.BlockSpec((SUBLANES, COMBINE_TILES * TM_MIX), lambda i, *_: (0, i)),
            pl.BlockSpec((SUBLANES, COMBINE_TILES * TM_MIX), lambda i, *_: (0, i)),
            pl.BlockSpec((1, D_MODEL), lambda i, *_: (0, 0)),
        ],
        out_specs=pl.BlockSpec((COMBINE_TILES * TM_MIX, D_MODEL), lambda i, *_: (i, 0)),
        scratch_shapes=[pltpu.VMEM((SLOTS * TOK_ROWS, LANES), jnp.uint32)] * COMBINE_TILES
                       + [pltpu.SemaphoreType.DMA((COMBINE_TILES,))],
    )
    return pl.pallas_call(
        _combine_kernel,
        grid_spec=grid_spec,
        out_shape=jax.ShapeDtypeStruct((SEQ, D_MODEL), F32),
        compiler_params=pltpu.CompilerParams(dimension_semantics=("arbitrary",),
                                             vmem_limit_bytes=VMEM_LIMIT),
        name="combine",
    )(n_flat, run_rows, tot, yb, h1, ri, rw, gf)


def kernel(x, meta, norm1_g, w_in, w_decay_up, b_decay, gla_norm_g, conv_w, conv_b, conv_ln_g,
           conv_ln_b, w_pw2, b_gate, w_out, norm2_g, w_router_group, b_router_group,
           w_router_expert, b_router_expert, w_exp_gate, w_exp_up, w_exp_down, final_norm_g):
    assert x.shape == (1, SEQ, D_MODEL) and w_in.shape[0] == 1
    x2d = x[0]
    front = jnp.concatenate([jnp.zeros((FRONT - N_META, D_MODEL), F32), meta.astype(F32)], axis=0)

    row2 = lambda a: a.reshape(1, -1).astype(F32)
    wq, wk, wv, wog, wdl, wglu, wgt = _wcast(jnp.transpose(w_in[0]))
    up = w_decay_up[0].astype(F32)
    up_hi = up.astype(BF16)
    up_mid = (up - up_hi.astype(F32)).astype(BF16)
    up_lo = (up - up_hi.astype(F32) - up_mid.astype(F32)).astype(BF16)
    wdu = jnp.concatenate([up_hi, up_mid, up_lo, up_hi, up_mid, up_hi], axis=0)
    q, k, v, og, la, u, gates = _inproj(
        front, x2d, row2(norm1_g[0]), wq, wk, wv, wog, jnp.tile(wdl, (DL_COPIES, 1)), wglu, wgt,
        wdu, row2(b_decay[0]), row2(b_gate[0]))

    o_n = _gla(q, k, v, la, row2(gla_norm_g[0]))

    wr = jnp.concatenate([w_router_group[0].T, w_router_expert[0].T,
                          jnp.zeros((LANES - N_GROUPS - N_EXPERTS, D_MODEL), F32)], axis=0).astype(F32)
    br = jnp.concatenate([b_router_group[0], b_router_expert[0],
                          jnp.zeros((LANES - N_GROUPS - N_EXPERTS,), F32)]).reshape(LANES, 1).astype(F32)
    wr_hi = wr.astype(BF16)
    wr_lo = (wr - wr_hi.astype(F32)).astype(BF16)
    h1, hn2, ri, rw, tabn, tabb = _mix(
        o_n, og, u, gates, x2d, conv_w[0].astype(F32).reshape(CONV_WIDTH, ROW_TILE, LANES),
        conv_b[0].astype(F32).reshape(ROW_TILE, LANES), row2(conv_ln_g[0]),
        row2(conv_ln_b[0]), w_pw2[0].astype(BF16), w_out[0].astype(BF16),
        row2(norm2_g[0]), wr_hi, wr_lo, br)

    experts = slice(N_GROUPS, N_GROUPS + N_EXPERTS)
    n_te = tabn[::SUBLANES, experts].astype(jnp.int32)
    seen_te = tabb[::SUBLANES, experts].astype(jnp.int32)
    counts = seen_te[-1] + n_te[-1]
    padded = (counts + BM - 1) // BM * BM
    pad_end = jnp.cumsum(padded)
    pad_start = pad_end - padded
    n_flat = n_te.reshape(-1)
    tot = jnp.sum(n_te, axis=1).astype(jnp.int32)
    run_rows = (pad_start[None, :] + seen_te).reshape(-1).astype(jnp.int32)
    tail_start = (pad_start + counts).astype(jnp.int32)
    tail_len = (padded - counts).astype(jnp.int32)
    blk = jnp.arange(N_BLOCKS, dtype=jnp.int32)
    block_e = jnp.minimum(jnp.sum((pad_end[None, :] <= blk[:, None] * BM).astype(jnp.int32), axis=1),
                          N_EXPERTS - 1).astype(jnp.int32)
    first = jnp.concatenate([jnp.ones((1,), jnp.int32), (block_e[1:] != block_e[:-1]).astype(jnp.int32)])
    nused = (pad_end[-1:] // BM).astype(jnp.int32)
    eid = jnp.arange(N_EXPERTS, dtype=jnp.int32)
    later = jnp.flip(lax.cummin(jnp.flip(jnp.where(counts > 0, eid, N_EXPERTS))))
    nxt = jnp.concatenate([later[1:], jnp.full((2,), N_EXPERTS, jnp.int32)])
    nxt2 = nxt[nxt[:N_EXPERTS]]
    or_none = lambda t: jnp.where(t < N_EXPERTS, t, -1)[block_e].astype(jnp.int32)
    next_e, next2_e = or_none(nxt[:N_EXPERTS]), or_none(nxt2)
    stage = ((jnp.cumsum(first) - 1) % 2).astype(jnp.int32)

    xb = _dispatch(n_flat, run_rows, tot, tail_start, tail_len, nused, hn2, ri)
    yb = _experts(block_e, first, next_e, next2_e, stage, nused, xb, w_exp_gate[0], w_exp_up[0], w_exp_down[0])
    out = _combine(n_flat, run_rows, tot, yb, h1, ri, rw, row2(final_norm_g))
    return out[None]
```
